```python
import math
import jax
import jax.numpy as jnp
from jax import lax
import numpy as np

D_MODEL = 2048
BATCH = 2
SEQ = 8192
DEPTH = 2

N_META = 16
EPS = 1e-6
NEG_INF = -1e30
CONV_K = 4

GDN_HEADS = 4
GDN_DK = 128
GDN_DV = 128
GDN_CHUNK = 64
GDN_QK = GDN_HEADS * GDN_DK
GDN_VW = GDN_HEADS * GDN_DV

MLA_HEADS = 4
MLA_Q_RANK = 512
MLA_KV_RANK = 256
MLA_NOPE = 128
MLA_ROPE = 64
MLA_V = 128
ROPE_BASE = 10000.0

FOX_HEADS = 4
FOX_DH = 128
FOX_W = FOX_HEADS * FOX_DH
ATTN_BLOCK = 128

SSD_HEADS = 8
SSD_HEADDIM = 64
SSD_GROUPS = 2
SSD_STATE = 128
SSD_CHUNK = 256
SSD_INNER = SSD_HEADS * SSD_HEADDIM

N_BRANCH = 4
BRANCH_W = 512

D_FF = 5632
N_EXPERTS = 8
TOP_K = 2
D_FF_EXPERT = 1408
N_DENSE = (DEPTH + 1) // 2
N_MOE = DEPTH // 2

IN_WIDTHS = (
    GDN_QK, GDN_QK, GDN_VW, GDN_VW, GDN_HEADS, GDN_HEADS,
    MLA_Q_RANK, MLA_KV_RANK, MLA_ROPE,
    FOX_W, FOX_W, FOX_W, FOX_HEADS,
    SSD_INNER, SSD_INNER, SSD_GROUPS * SSD_STATE, SSD_GROUPS * SSD_STATE, SSD_HEADS,
)
D_IN = sum(IN_WIDTHS)

kernel_name = 'hybrid_gated_parallel_mixer_moe'


def rms_norm(x, g):
    xf = x.astype(jnp.float32)
    y = xf * lax.rsqrt(jnp.mean(xf * xf, axis=-1, keepdims=True) + EPS)
    return (y * g.astype(jnp.float32)).astype(x.dtype)


def l2norm(t):
    return t * lax.rsqrt(jnp.sum(t * t, axis=-1, keepdims=True) + EPS)


def causal_dwconv(x, w):
    ch = x.shape[-1]
    return lax.conv_general_dilated(
        x, w.astype(x.dtype)[:, None, :], window_strides=(1,), padding=[(w.shape[0] - 1, 0)],
        dimension_numbers=('NWC', 'WIO', 'NWC'), feature_group_count=ch)


def pad_to_chunks(t, chunk):
    front = chunk - N_META
    back = (-(t.shape[1] + front)) % chunk
    pad = [(0, 0), (front, back)] + [(0, 0)] * (t.ndim - 2)
    return jnp.pad(t, pad), front


def rope_tables(L):
    inv = 1.0 / (ROPE_BASE ** (jnp.arange(0, MLA_ROPE, 2, dtype=jnp.float32) / MLA_ROPE))
    ang = jnp.arange(L, dtype=jnp.float32)[:, None] * inv[None, :]
    ang = jnp.concatenate([ang, ang], axis=-1)
    return jnp.cos(ang), jnp.sin(ang)


def apply_rope(t, cos, sin):
    half = t.shape[-1] // 2
    rot = jnp.concatenate([-t[..., half:], t[..., :half]], axis=-1)
    return t * cos[None, :, None, :] + rot * sin[None, :, None, :]


def block_causal_attention(q, k, v, log_f=None):
    Bsz, L, H, dq = q.shape
    dv = v.shape[-1]
    (q, front), (k, _), (v, _) = pad_to_chunks(q, ATTN_BLOCK), pad_to_chunks(k, ATTN_BLOCK), pad_to_chunks(v, ATTN_BLOCK)
    T = q.shape[1]
    nb = T // ATTN_BLOCK
    q = jnp.moveaxis(q, 2, 1).astype(jnp.float32) * dq ** -0.5
    k = jnp.moveaxis(k, 2, 1).astype(jnp.float32)
    v = jnp.moveaxis(v, 2, 1).astype(jnp.float32)
    qb = jnp.moveaxis(q.reshape(Bsz, H, nb, ATTN_BLOCK, dq), 2, 0)
    if log_f is None:
        cum = None
        cb = None
    else:
        lf, _ = pad_to_chunks(log_f.astype(jnp.float32), ATTN_BLOCK)
        cum = jnp.cumsum(jnp.moveaxis(lf, 2, 1), axis=-1)
        cb = jnp.moveaxis(cum.reshape(Bsz, H, nb, ATTN_BLOCK), 2, 0)
    kpos = jnp.arange(T)

    def one_block(args):
        i, q_i, c_i = args
        s = jnp.einsum('bhqd,bhkd->bhqk', q_i, k)
        if c_i is not None:
            s = s + c_i[..., :, None] - cum[..., None, :]
        qpos = i * ATTN_BLOCK + jnp.arange(ATTN_BLOCK)
        mask = (kpos[None, :] <= qpos[:, None]) & (kpos[None, :] >= front)
        p = jax.nn.softmax(jnp.where(mask, s, NEG_INF), axis=-1)
        return jnp.einsum('bhqk,bhkd->bhqd', p, v)

    o = lax.map(one_block, (jnp.arange(nb), qb, cb))
    o = jnp.moveaxis(o, 0, 2).reshape(Bsz, H, T, dv)[:, :, front:front + L]
    return jnp.moveaxis(o, 1, 2)


def gated_deltanet(q, k, v, z, b_logit, a_logit, conv_w, A_log, dt_bias, norm_g):
    Bsz, L, _ = q.shape
    C = GDN_CHUNK
    f32 = jnp.float32
    qkv = jax.nn.silu(causal_dwconv(jnp.concatenate([q, k, v], axis=-1), conv_w))
    q, k, v = jnp.split(qkv, [GDN_QK, 2 * GDN_QK], axis=-1)
    q = l2norm(q.reshape(Bsz, L, GDN_HEADS, GDN_DK).astype(f32)) * GDN_DK ** -0.5
    k = l2norm(k.reshape(Bsz, L, GDN_HEADS, GDN_DK).astype(f32))
    v = v.reshape(Bsz, L, GDN_HEADS, GDN_DV).astype(f32)
    beta = jax.nn.sigmoid(b_logit.astype(f32))
    g = -jnp.exp(A_log.astype(f32)) * jax.nn.softplus(a_logit.astype(f32) + dt_bias.astype(f32))

    def to_chunks(t):
        t, _ = pad_to_chunks(t, C)
        t = jnp.moveaxis(t, 2, 1)
        return t.reshape(t.shape[0], t.shape[1], -1, C, *t.shape[3:])

    front = C - N_META
    qc, kc, vc, bc, gc = (to_chunks(t) for t in (q, k, v, beta, g))
    gcs = jnp.cumsum(gc, axis=-1)
    causal = jnp.tril(jnp.ones((C, C), dtype=bool))
    strict = jnp.tril(jnp.ones((C, C), dtype=bool), -1)
    decay = jnp.exp(jnp.where(causal, gcs[..., :, None] - gcs[..., None, :], -jnp.inf))
    kb = kc * bc[..., None]
    a_strict = jnp.where(strict, jnp.einsum('bhncd,bhnsd->bhncs', kb, kc) * decay, 0.0)
    eye = jnp.eye(C, dtype=f32)
    tinv = lax.linalg.triangular_solve(eye + a_strict, jnp.broadcast_to(eye, a_strict.shape),
                                       left_side=True, lower=True)
    u = tinv @ (vc * bc[..., None])
    w = tinv @ (kb * jnp.exp(gcs)[..., None])
    attn_intra = jnp.einsum('bhncd,bhnsd->bhncs', qc, kc) * decay
    q_dec = qc * jnp.exp(gcs)[..., None]
    k_to_end = kc * jnp.exp(gcs[..., -1:] - gcs)[..., None]
    chunk_decay = jnp.exp(gcs[..., -1])

    def step(S, inp):
        u_i, w_i, qd_i, kd_i, att_i, cd_i = inp
        v_new = u_i - jnp.einsum('bhcd,bhde->bhce', w_i, S)
        o = jnp.einsum('bhcd,bhde->bhce', qd_i, S) + jnp.einsum('bhcs,bhse->bhce', att_i, v_new)
        S = S * cd_i[..., None, None] + jnp.einsum('bhcd,bhce->bhde', kd_i, v_new)
        return S, o

    xs = tuple(jnp.moveaxis(t, 2, 0) for t in (u, w, q_dec, k_to_end, attn_intra, chunk_decay))
    S0 = jnp.zeros((Bsz, GDN_HEADS, GDN_DK, GDN_DV), f32)
    _, o = lax.scan(step, S0, xs)
    o = jnp.moveaxis(o, 0, 2).reshape(Bsz, GDN_HEADS, -1, GDN_DV)[:, :, front:front + L]
    o = jnp.moveaxis(o, 1, 2)
    o = rms_norm(o, norm_g) * jax.nn.silu(z.reshape(Bsz, L, GDN_HEADS, GDN_DV).astype(f32))
    return o.reshape(Bsz, L, GDN_VW)


def mla_attention(qa, kva, kpe, cos, sin, qa_g, wq_b, kva_g, wkv_b, qn_g, kn_g):
    Bsz, L, _ = qa.shape
    q = (rms_norm(qa, qa_g) @ wq_b).reshape(Bsz, L, MLA_HEADS, MLA_NOPE + MLA_ROPE)
    kv = (rms_norm(kva, kva_g) @ wkv_b).reshape(Bsz, L, MLA_HEADS, MLA_NOPE + MLA_V)
    k_nope, v = kv[..., :MLA_NOPE], kv[..., MLA_NOPE:]
    k = jnp.concatenate([k_nope, jnp.broadcast_to(kpe[:, :, None, :], (Bsz, L, MLA_HEADS, MLA_ROPE))], axis=-1)
    q = rms_norm(q, qn_g)
    k = rms_norm(k, kn_g)
    q = jnp.concatenate([q[..., :MLA_NOPE], apply_rope(q[..., MLA_NOPE:], cos, sin)], axis=-1)
    k = jnp.concatenate([k[..., :MLA_NOPE], apply_rope(k[..., MLA_NOPE:], cos, sin)], axis=-1)
    o = block_causal_attention(q, k, v)
    return o.reshape(Bsz, L, MLA_HEADS * MLA_V)


def forgetting_attention(q, k, v, f_logit, qn_g, kn_g, b_f):
    Bsz, L, _ = q.shape
    q = rms_norm(q.reshape(Bsz, L, FOX_HEADS, FOX_DH), qn_g)
    k = rms_norm(k.reshape(Bsz, L, FOX_HEADS, FOX_DH), kn_g)
    v = v.reshape(Bsz, L, FOX_HEADS, FOX_DH)
    log_f = jax.nn.log_sigmoid(f_logit.astype(jnp.float32) + b_f.astype(jnp.float32))
    o = block_causal_attention(q, k, v, log_f)
    return o.reshape(Bsz, L, FOX_W)


def mamba2_ssd(z, xs, Bm, Cm, dt_raw, conv_w, conv_b, dt_bias, A_log, D, norm_g):
    Bsz, L, _ = z.shape
    C, G, HG, P, N = SSD_CHUNK, SSD_GROUPS, SSD_HEADS // SSD_GROUPS, SSD_HEADDIM, SSD_STATE
    f32 = jnp.float32
    xBC = jax.nn.silu(causal_dwconv(jnp.concatenate([xs, Bm, Cm], axis=-1), conv_w) + conv_b)
    xs, Bm, Cm = jnp.split(xBC, [SSD_INNER, SSD_INNER + G * N], axis=-1)
    X = xs.reshape(Bsz, L, G, HG, P).astype(f32)
    Bm = Bm.reshape(Bsz, L, G, N).astype(f32)
    Cm = Cm.reshape(Bsz, L, G, N).astype(f32)
    dt = jax.nn.softplus(dt_raw.astype(f32) + dt_bias.astype(f32)).reshape(Bsz, L, G, HG)
    a = dt * (-jnp.exp(A_log.astype(f32))).reshape(G, HG)
    Xdt = X * dt[..., None]

    def to_chunks(t):
        t, _ = pad_to_chunks(t, C)
        return t.reshape(t.shape[0], -1, C, *t.shape[2:])

    front = C - N_META
    Xc, Bc, Cc, ac = (to_chunks(t) for t in (Xdt, Bm, Cm, a))
    acs = jnp.cumsum(ac, axis=2)
    causal = jnp.tril(jnp.ones((C, C), dtype=bool))
    seg = acs[:, :, :, None] - acs[:, :, None, :]
    Lmat = jnp.exp(jnp.where(causal[:, :, None, None], seg, -jnp.inf))
    CB = jnp.einsum('bclgn,bcsgn->bclsg', Cc, Bc)
    y_diag = jnp.einsum('bclsgh,bcsghp->bclghp', CB[..., None] * Lmat, Xc)
    decay_to_end = jnp.exp(acs[:, :, -1:] - acs)
    states = jnp.einsum('bclgn,bclghp->bcghpn', Bc, Xc * decay_to_end[..., None])
    chunk_decay = jnp.exp(acs[:, :, -1])

    def step(hs, inp):
        st, cd = inp
        return hs * cd[..., None, None] + st, hs

    h0 = jnp.zeros((Bsz, G, HG, P, N), f32)
    _, h_in = lax.scan(step, h0, (jnp.moveaxis(states, 1, 0), jnp.moveaxis(chunk_decay, 1, 0)))
    h_in = jnp.moveaxis(h_in, 0, 1)
    y_off = jnp.einsum('bclgn,bcghpn->bclghp', Cc, h_in) * jnp.exp(acs)[..., None]
    y = (y_diag + y_off).reshape(Bsz, -1, G, HG, P)[:, front:front + L]
    y = y + X * D.astype(f32).reshape(G, HG)[..., None]
    y = y.reshape(Bsz, L, G, HG * P) * jax.nn.silu(z.reshape(Bsz, L, G, HG * P).astype(f32))
    y = rms_norm(y, norm_g.reshape(G, HG * P))
    return y.reshape(Bsz, L, SSD_INNER)


def hybrid_mixer(h, cos, sin, w_in,
                 gdn_conv_w, gdn_A_log, gdn_dt_bias, gdn_norm_g,
                 mla_qa_g, mla_wq_b, mla_kva_g, mla_wkv_b, mla_qn_g, mla_kn_g,
                 fox_qn_g, fox_kn_g, fox_b_f,
                 ssd_conv_w, ssd_conv_b, ssd_dt_bias, ssd_A_log, ssd_D, ssd_norm_g,
                 w_gate, w_branch, w_o):
    proj = h @ w_in
    (g_q, g_k, g_v, g_z, g_b, g_a, m_qa, m_kva, m_kpe, f_q, f_k, f_v, f_f,
     s_z, s_x, s_B, s_C, s_dt) = jnp.split(proj, np.cumsum(IN_WIDTHS)[:-1].tolist(), axis=-1)
    branches = (
        gated_deltanet(g_q, g_k, g_v, g_z, g_b, g_a, gdn_conv_w, gdn_A_log, gdn_dt_bias, gdn_norm_g),
        mla_attention(m_qa, m_kva, m_kpe, cos, sin, mla_qa_g, mla_wq_b, mla_kva_g, mla_wkv_b, mla_qn_g, mla_kn_g),
        forgetting_attention(f_q, f_k, f_v, f_f, fox_qn_g, fox_kn_g, fox_b_f),
        mamba2_ssd(s_z, s_x, s_B, s_C, s_dt, ssd_conv_w, ssd_conv_b, ssd_dt_bias, ssd_A_log, ssd_D, ssd_norm_g),
    )
    merged = sum(jax.nn.sigmoid(h @ w_gate[b]) * (branches[b] @ w_branch[b]) for b in range(N_BRANCH))
    return merged @ w_o


def swiglu(h, w_g, w_u, w_d):
    return (jax.nn.silu(h @ w_g) * (h @ w_u)) @ w_d


def moe_swiglu(h, router_w, w_g, w_u, w_d):
    logits = (h @ router_w).astype(jnp.float32)
    top_v, top_i = lax.top_k(logits, TOP_K)
    top_p = jax.nn.softmax(top_v, axis=-1)
    combine = jnp.sum(jax.nn.one_hot(top_i, N_EXPERTS, dtype=jnp.float32) * top_p[..., None], axis=-2)
    return sum(combine[..., e:e + 1] * swiglu(h, w_g[e], w_u[e], w_d[e]) for e in range(N_EXPERTS))


def setup_inputs(seed: int = 0) -> dict:
    key = jax.random.key(seed)
    keys = iter(jax.random.split(key, 48))
    f32 = jnp.float32

    def nrm(shape, scale):
        return jax.random.normal(next(keys), shape, f32) * scale

    def gain(shape):
        return 1.0 + nrm(shape, 0.02)

    def a_log(shape):
        return jnp.log(jax.random.uniform(next(keys), shape, f32, 1.0, 16.0))

    def dt_bias(shape):
        dt = jnp.exp(jax.random.uniform(next(keys), shape, f32, math.log(1e-3), math.log(1e-1)))
        return dt + jnp.log(-jnp.expm1(-dt))

    Dm = D_MODEL
    return {
        'x': nrm((BATCH, SEQ, Dm), 1.0),
        'meta_tokens': nrm((N_META, Dm), 1.0),
        'mix_norm_g': gain((DEPTH, Dm)),
        'w_in': nrm((DEPTH, Dm, D_IN), Dm ** -0.5),
        'gdn_conv_w': nrm((DEPTH, CONV_K, 2 * GDN_QK + GDN_VW), CONV_K ** -0.5),
        'gdn_A_log': a_log((DEPTH, GDN_HEADS)),
        'gdn_dt_bias': dt_bias((DEPTH, GDN_HEADS)),
        'gdn_norm_g': gain((DEPTH, GDN_DV)),
        'mla_qa_g': gain((DEPTH, MLA_Q_RANK)),
        'mla_wq_b': nrm((DEPTH, MLA_Q_RANK, MLA_HEADS * (MLA_NOPE + MLA_ROPE)), MLA_Q_RANK ** -0.5),
        'mla_kva_g': gain((DEPTH, MLA_KV_RANK)),
        'mla_wkv_b': nrm((DEPTH, MLA_KV_RANK, MLA_HEADS * (MLA_NOPE + MLA_V)), MLA_KV_RANK ** -0.5),
        'mla_qn_g': gain((DEPTH, MLA_NOPE + MLA_ROPE)),
        'mla_kn_g': gain((DEPTH, MLA_NOPE + MLA_ROPE)),
        'fox_qn_g': gain((DEPTH, FOX_DH)),
        'fox_kn_g': gain((DEPTH, FOX_DH)),
        'fox_b_f': 2.0 + nrm((DEPTH, FOX_HEADS), 0.1),
        'ssd_conv_w': nrm((DEPTH, CONV_K, SSD_INNER + 2 * SSD_GROUPS * SSD_STATE), CONV_K ** -0.5),
        'ssd_conv_b': nrm((DEPTH, SSD_INNER + 2 * SSD_GROUPS * SSD_STATE), 0.02),
        'ssd_dt_bias': dt_bias((DEPTH, SSD_HEADS)),
        'ssd_A_log': a_log((DEPTH, SSD_HEADS)),
        'ssd_D': gain((DEPTH, SSD_HEADS)),
        'ssd_norm_g': gain((DEPTH, SSD_INNER)),
        'w_gate': nrm((DEPTH, N_BRANCH, Dm, Dm), Dm ** -0.5),
        'w_branch': nrm((DEPTH, N_BRANCH, BRANCH_W, Dm), BRANCH_W ** -0.5),
        'w_o': nrm((DEPTH, Dm, Dm), Dm ** -0.5),
        'ffn_norm_g': gain((DEPTH, Dm)),
        'dense_w_gate': nrm((N_DENSE, Dm, D_FF), Dm ** -0.5),
        'dense_w_up': nrm((N_DENSE, Dm, D_FF), Dm ** -0.5),
        'dense_w_down': nrm((N_DENSE, D_FF, Dm), D_FF ** -0.5),
        'router_w': nrm((N_MOE, Dm, N_EXPERTS), Dm ** -0.5),
        'moe_w_gate': nrm((N_MOE, N_EXPERTS, Dm, D_FF_EXPERT), Dm ** -0.5),
        'moe_w_up': nrm((N_MOE, N_EXPERTS, Dm, D_FF_EXPERT), Dm ** -0.5),
        'moe_w_down': nrm((N_MOE, N_EXPERTS, D_FF_EXPERT, Dm), D_FF_EXPERT ** -0.5),
    }


def reference(x, meta_tokens, mix_norm_g, w_in,
              gdn_conv_w, gdn_A_log, gdn_dt_bias, gdn_norm_g,
              mla_qa_g, mla_wq_b, mla_kva_g, mla_wkv_b, mla_qn_g, mla_kn_g,
              fox_qn_g, fox_kn_g, fox_b_f,
              ssd_conv_w, ssd_conv_b, ssd_dt_bias, ssd_A_log, ssd_D, ssd_norm_g,
              w_gate, w_branch, w_o, ffn_norm_g,
              dense_w_gate, dense_w_up, dense_w_down,
              router_w, moe_w_gate, moe_w_up, moe_w_down):
    Bsz = x.shape[0]
    meta = jnp.broadcast_to(meta_tokens[None].astype(x.dtype), (Bsz, N_META, D_MODEL))
    h = jnp.concatenate([meta, x], axis=1)
    cos, sin = rope_tables(h.shape[1])
    for layer in range(DEPTH):
        hn = rms_norm(h, mix_norm_g[layer])
        h = h + hybrid_mixer(
            hn, cos, sin, w_in[layer],
            gdn_conv_w[layer], gdn_A_log[layer], gdn_dt_bias[layer], gdn_norm_g[layer],
            mla_qa_g[layer], mla_wq_b[layer], mla_kva_g[layer], mla_wkv_b[layer], mla_qn_g[layer], mla_kn_g[layer],
            fox_qn_g[layer], fox_kn_g[layer], fox_b_f[layer],
            ssd_conv_w[layer], ssd_conv_b[layer], ssd_dt_bias[layer], ssd_A_log[layer], ssd_D[layer], ssd_norm_g[layer],
            w_gate[layer], w_branch[layer], w_o[layer])
        hn = rms_norm(h, ffn_norm_g[layer])
        i = layer // 2
        if layer % 2 == 0:
            h = h + swiglu(hn, dense_w_gate[i], dense_w_up[i], dense_w_down[i])
        else:
            h = h + moe_swiglu(hn, router_w[i], moe_w_gate[i], moe_w_up[i], moe_w_down[i])
    return h[:, N_META:].astype(x.dtype)
```

```python
import functools
import math

import numpy as np
import jax
import jax.numpy as jnp
from jax import lax
from jax.experimental import pallas as pl
from jax.experimental.pallas import tpu as pltpu

F32 = jnp.float32
BF16 = jnp.bfloat16
HI = lax.Precision.HIGHEST
NT_DIMS = (((1,), (1,)), ((), ()))

D_MODEL = 2048
N_META = 16
EPS = 1e-6
NEG = -1e30
CONV_K = 4

GDN_HEADS, GDN_DK, GDN_DV, GDN_CHUNK = 4, 128, 128, 64
MLA_HEADS, MLA_Q_RANK, MLA_KV_RANK, MLA_NOPE, MLA_ROPE, MLA_V = 4, 512, 256, 128, 64, 128
MLA_DQK = MLA_NOPE + MLA_ROPE
ROPE_BASE = 10000.0
FOX_HEADS, FOX_DH = 4, 128
SSD_HEADS, SSD_HEADDIM, SSD_GROUPS, SSD_STATE = 8, 64, 2, 128
SSD_HG = SSD_HEADS // SSD_GROUPS
SSD_INNER = SSD_HEADS * SSD_HEADDIM
N_BRANCH, BRANCH_W = 4, 512
N_EXPERTS, TOP_K = 8, 2

LANES = 128
ROW_TILE = 256
MM_TM = 512
VMEM_LIMIT = 56 * 1024 * 1024

PROJ_W = 6144
COL_GDN_QKV, COL_GDN_Z = 0, 1536
COL_SSD_XBC, COL_SSD_Z = 2048, 3072
COL_MLA_QA, COL_MLA_KVA = 3584, 4096
COL_SMALL = 4352
COL_FOX_QKV = 4608
L_BETA, L_GA, L_FF, L_DT, L_KPE = 0, 4, 8, 12, 20


def _cparams(sem, vmem=VMEM_LIMIT):
    return pltpu.CompilerParams(dimension_semantics=sem, vmem_limit_bytes=vmem)


def _softplus(x):
    return jnp.maximum(x, 0.0) + jnp.log1p(jnp.exp(-jnp.abs(x)))


def _silu(x):
    return x * jax.nn.sigmoid(x)


def _dot(a, b, precision=None):
    return jnp.dot(a, b, preferred_element_type=F32, precision=precision)


def _dot_nt(a, b):
    return lax.dot_general(a, b, NT_DIMS, preferred_element_type=F32)


def _rmsnorm_kernel(h_ref, g_ref, o_ref):
    x = h_ref[...]
    y = x * lax.rsqrt(jnp.mean(x * x, axis=-1, keepdims=True) + EPS) * g_ref[...]
    o_ref[...] = y.astype(o_ref.dtype)


def _rmsnorm(h, g):
    M, D = h.shape
    return pl.pallas_call(
        _rmsnorm_kernel,
        out_shape=jax.ShapeDtypeStruct((M, D), BF16),
        grid=(M // MM_TM,),
        in_specs=[pl.BlockSpec((MM_TM, D), lambda i: (i, 0)),
                  pl.BlockSpec((1, D), lambda i: (0, 0))],
        out_specs=pl.BlockSpec((MM_TM, D), lambda i: (i, 0)),
        compiler_params=_cparams(("parallel",)),
        name="rmsnorm",
    )(h, g.reshape(1, D).astype(F32))


def _rmsnorm_router_kernel(h_ref, g_ref, rw_ref, o_ref, comb_ref):
    x = h_ref[...]
    y = x * lax.rsqrt(jnp.mean(x * x, axis=-1, keepdims=True) + EPS) * g_ref[...]
    o_ref[...] = y.astype(o_ref.dtype)
    logits = _dot(y, rw_ref[...], HI)
    lane = lax.broadcasted_iota(jnp.int32, logits.shape, 1)
    logits = jnp.where(lane < N_EXPERTS, logits, NEG)
    m1 = jnp.max(logits, axis=-1, keepdims=True)
    i1 = jnp.min(jnp.where(logits == m1, lane, LANES), axis=-1, keepdims=True)
    rest = jnp.where(lane == i1, NEG, logits)
    m2 = jnp.max(rest, axis=-1, keepdims=True)
    i2 = jnp.min(jnp.where(rest == m2, lane, LANES), axis=-1, keepdims=True)
    e2 = jnp.exp(m2 - m1)
    p1 = 1.0 / (1.0 + e2)
    p2 = e2 * p1
    comb_ref[...] = jnp.where(lane == i1, p1, 0.0) + jnp.where(lane == i2, p2, 0.0)


def _rmsnorm_router(h, g, router_w):
    M, D = h.shape
    rw = jnp.zeros((D, LANES), F32).at[:, :N_EXPERTS].set(router_w.astype(F32))
    return pl.pallas_call(
        _rmsnorm_router_kernel,
        out_shape=(jax.ShapeDtypeStruct((M, D), BF16), jax.ShapeDtypeStruct((M, LANES), F32)),
        grid=(M // MM_TM,),
        in_specs=[pl.BlockSpec((MM_TM, D), lambda i: (i, 0)),
                  pl.BlockSpec((1, D), lambda i: (0, 0)),
                  pl.BlockSpec((D, LANES), lambda i: (0, 0))],
        out_specs=(pl.BlockSpec((MM_TM, D), lambda i: (i, 0)),
                   pl.BlockSpec((MM_TM, LANES), lambda i: (i, 0))),
        compiler_params=_cparams(("parallel",)),
        name="rmsnorm_router",
    )(h, g.reshape(1, D).astype(F32), rw)


def _mm_kernel(a_ref, w_ref, o_ref):
    o_ref[...] = _dot(a_ref[...], w_ref[...]).astype(o_ref.dtype)


def _mm_res_kernel(a_ref, w_ref, r_ref, o_ref):
    o_ref[...] = r_ref[...] + _dot(a_ref[...], w_ref[...])


def _matmul(a, w, *, tn, tm=MM_TM, residual=None, out_dtype=F32, name="matmul"):
    M, K = a.shape
    N = w.shape[1]
    in_specs = [pl.BlockSpec((tm, K), lambda j, i: (i, 0)),
                pl.BlockSpec((K, tn), lambda j, i: (0, j))]
    args = [a, w]
    kern = _mm_kernel
    if residual is not None:
        in_specs.append(pl.BlockSpec((tm, tn), lambda j, i: (i, j)))
        args.append(residual)
        kern = _mm_res_kernel
    return pl.pallas_call(
        kern,
        out_shape=jax.ShapeDtypeStruct((M, N), out_dtype),
        grid=(N // tn, M // tm),
        in_specs=in_specs,
        out_specs=pl.BlockSpec((tm, tn), lambda j, i: (i, j)),
        compiler_params=_cparams(("parallel", "parallel")),
        name=name,
    )(*args)


def _swiglu_kernel(a_ref, wg_ref, wu_ref, o_ref):
    a = a_ref[...]
    g = _dot(a, wg_ref[...])
    u = _dot(a, wu_ref[...])
    o_ref[...] = (_silu(g) * u).astype(o_ref.dtype)


def _swiglu_up(a, wg, wu, *, tn, tm=MM_TM):
    M, K = a.shape
    F = wg.shape[1]
    return pl.pallas_call(
        _swiglu_kernel,
        out_shape=jax.ShapeDtypeStruct((M, F), BF16),
        grid=(F // tn, M // tm),
        in_specs=[pl.BlockSpec((tm, K), lambda j, i: (i, 0)),
                  pl.BlockSpec((K, tn), lambda j, i: (0, j)),
                  pl.BlockSpec((K, tn), lambda j, i: (0, j))],
        out_specs=pl.BlockSpec((tm, tn), lambda j, i: (i, j)),
        compiler_params=_cparams(("parallel", "parallel")),
        name="swiglu_up",
    )(a, wg, wu)


def _moe_up_kernel(a_ref, wg_ref, wu_ref, c_ref, o_ref):
    a = a_ref[...]
    g = _dot(a, wg_ref[0])
    u = _dot(a, wu_ref[0])
    o_ref[...] = (_silu(g) * u * c_ref[0]).astype(o_ref.dtype)


def _moe_up(a, wg, wu, comb_cols, *, tm=MM_TM):
    M, K = a.shape
    E, _, F = wg.shape
    return pl.pallas_call(
        _moe_up_kernel,
        out_shape=jax.ShapeDtypeStruct((M, E * F), BF16),
        grid=(E, M // tm),
        in_specs=[pl.BlockSpec((tm, K), lambda e, i: (i, 0)),
                  pl.BlockSpec((1, K, F), lambda e, i: (e, 0, 0)),
                  pl.BlockSpec((1, K, F), lambda e, i: (e, 0, 0)),
                  pl.BlockSpec((1, tm, 1), lambda e, i: (e, i, 0))],
        out_specs=pl.BlockSpec((tm, F), lambda e, i: (i, e)),
        compiler_params=_cparams(("parallel", "parallel")),
        name="moe_up",
    )(a, wg, wu, comb_cols)


def _merge_kernel(hn_ref, b0_ref, b1_ref, b2_ref, b3_ref, wg_ref, wb_ref, o_ref):
    hn = hn_ref[...]
    acc = None
    for b, br_ref in enumerate((b0_ref, b1_ref, b2_ref, b3_ref)):
        gate = jax.nn.sigmoid(_dot(hn, wg_ref[b]))
        term = gate * _dot(br_ref[...], wb_ref[b])
        acc = term if acc is None else acc + term
    o_ref[...] = acc.astype(o_ref.dtype)


def _merge(hn, branches, wg, wb, *, tn=512, tm=MM_TM):
    M, D = hn.shape
    N = wg.shape[2]
    bspec = pl.BlockSpec((tm, BRANCH_W), lambda j, i: (i, 0))
    return pl.pallas_call(
        _merge_kernel,
        out_shape=jax.ShapeDtypeStruct((M, N), BF16),
        grid=(N // tn, M // tm),
        in_specs=[pl.BlockSpec((tm, D), lambda j, i: (i, 0)), bspec, bspec, bspec, bspec,
                  pl.BlockSpec((N_BRANCH, D, tn), lambda j, i: (0, 0, j)),
                  pl.BlockSpec((N_BRANCH, BRANCH_W, tn), lambda j, i: (0, 0, j))],
        out_specs=pl.BlockSpec((tm, tn), lambda j, i: (i, j)),
        compiler_params=_cparams(("parallel", "parallel")),
        name="gate_merge",
    )(hn, *branches, wg, wb)


def _causal_conv(x, carry_ref, cw):
    n = x.shape[0]
    xext = jnp.concatenate([carry_ref[...], x], axis=0)
    y = cw[0:1] * xext[5:5 + n]
    for i in range(1, CONV_K):
        y = y + cw[i:i + 1] * xext[5 + i:5 + i + n]
    carry_ref[...] = x[n - 8:n]
    return y


def _tile_masks(n, chunk):
    row = lax.broadcasted_iota(jnp.int32, (n, n), 0)
    col = lax.broadcasted_iota(jnp.int32, (n, n), 1)
    if chunk == n:
        return col <= row, col < row
    in_chunk = col >= (row // chunk) * chunk
    return in_chunk & (col <= row), in_chunk & (col < row)


def _gdn_kernel(qkv_ref, z_ref, sm_ref, cw_ref, alog_ref, dtb_ref, ng_ref, o_ref, s_ref, carry_ref):
    @pl.when(pl.program_id(1) == 0)
    def _():
        s_ref[...] = jnp.zeros_like(s_ref)
        carry_ref[...] = jnp.zeros_like(carry_ref)

    n, C = ROW_TILE, GDN_CHUNK
    y = _silu(_causal_conv(qkv_ref[0], carry_ref, cw_ref[...]))
    sm = sm_ref[0]
    z = z_ref[0]
    beta_all = jax.nn.sigmoid(sm)
    g_all = -jnp.exp(alog_ref[...]) * _softplus(sm + dtb_ref[...])
    causal, strict = _tile_masks(n, C)
    gcs_all = _dot(causal.astype(F32), g_all, HI)
    gcs_t = gcs_all.T
    outs = []
    for h in range(GDN_HEADS):
        q = y[:, h * GDN_DK:(h + 1) * GDN_DK]
        k = y[:, GDN_HEADS * GDN_DK + h * GDN_DK:GDN_HEADS * GDN_DK + (h + 1) * GDN_DK]
        v = y[:, 2 * GDN_HEADS * GDN_DK + h * GDN_DV:2 * GDN_HEADS * GDN_DK + (h + 1) * GDN_DV]
        q = q * lax.rsqrt(jnp.sum(q * q, axis=-1, keepdims=True) + EPS) * GDN_DK ** -0.5
        k = k * lax.rsqrt(jnp.sum(k * k, axis=-1, keepdims=True) + EPS)
        beta = beta_all[:, L_BETA + h:L_BETA + h + 1]
        gc = gcs_all[:, L_GA + h:L_GA + h + 1]
        gr = gcs_t[L_GA + h:L_GA + h + 1, :]
        decay = jnp.exp(jnp.where(causal, gc - gr, NEG))
        kb = k * beta
        k16 = k.astype(BF16)
        x = jnp.where(strict, -(_dot_nt(kb.astype(BF16), k16) * decay), 0.0)
        r = x
        p = _dot(x, x, HI)
        for _ in range(int(math.log2(C)) - 2):
            rp = _dot(jnp.concatenate([r, p], axis=0), p, HI)
            r = r + p + rp[:n]
            p = rp[n:]
        r = r + p + _dot(r, p, HI)
        egc = jnp.exp(gc)
        rhs = jnp.concatenate([v * beta, kb * egc], axis=1)
        uw = rhs + _dot(r.astype(BF16), rhs.astype(BF16))
        u, w = uw[:, :GDN_DV], uw[:, GDN_DV:]
        att = _dot_nt(q.astype(BF16), k16) * decay
        qd = q * egc
        s = s_ref[h]
        o_chunks = []
        for c in range(n // C):
            rows = slice(c * C, (c + 1) * C)
            g_last = gc[(c + 1) * C - 1:(c + 1) * C, :]
            kd = k[rows] * jnp.exp(g_last - gc[rows])
            s16 = s.astype(BF16)
            v_new = u[rows] - _dot(w[rows].astype(BF16), s16)
            o_chunks.append(_dot(qd[rows].astype(BF16), s16)
                            + _dot(att[rows, rows].astype(BF16), v_new.astype(BF16)))
            s = s * jnp.exp(g_last) + _dot(kd.T.astype(BF16), v_new.astype(BF16))
        s_ref[h] = s
        o = jnp.concatenate(o_chunks, axis=0)
        o = o * lax.rsqrt(jnp.mean(o * o, axis=-1, keepdims=True) + EPS) * ng_ref[...]
        outs.append(o * _silu(z[:, h * GDN_DV:(h + 1) * GDN_DV]))
    o_ref[0] = jnp.concatenate(outs, axis=1).astype(o_ref.dtype)


def _lane_vec(vals, lane0):
    v = jnp.zeros((1, LANES), F32)
    return v.at[0, lane0:lane0 + vals.shape[0]].set(vals.astype(F32))


def _gdn(proj, conv_w, a_log, dt_bias, norm_g):
    B, T, _ = proj.shape
    n = ROW_TILE
    W = 2 * GDN_HEADS * GDN_DK + GDN_HEADS * GDN_DV
    ZW = GDN_HEADS * GDN_DV
    return pl.pallas_call(
        _gdn_kernel,
        out_shape=jax.ShapeDtypeStruct((B, T, ZW), BF16),
        grid=(B, T // n),
        in_specs=[pl.BlockSpec((1, n, W), lambda b, t: (b, t, COL_GDN_QKV // W)),
                  pl.BlockSpec((1, n, ZW), lambda b, t: (b, t, COL_GDN_Z // ZW)),
                  pl.BlockSpec((1, n, LANES), lambda b, t: (b, t, COL_SMALL // LANES)),
                  pl.BlockSpec((CONV_K, W), lambda b, t: (0, 0)),
                  pl.BlockSpec((1, LANES), lambda b, t: (0, 0)),
                  pl.BlockSpec((1, LANES), lambda b, t: (0, 0)),
                  pl.BlockSpec((1, GDN_DV), lambda b, t: (0, 0))],
        out_specs=pl.BlockSpec((1, n, ZW), lambda b, t: (b, t, 0)),
        scratch_shapes=[pltpu.VMEM((GDN_HEADS, GDN_DK, GDN_DV), F32),
                        pltpu.VMEM((8, W), F32)],
        compiler_params=_cparams(("arbitrary", "arbitrary")),
        name="gdn",
    )(proj, proj, proj, conv_w.astype(F32), _lane_vec(a_log, L_GA), _lane_vec(dt_bias, L_GA),
      norm_g.reshape(1, GDN_DV).astype(F32))


def _ssd_kernel(xbc_ref, z_ref, sm_ref, cw_ref, cb_ref, alog_ref, dtb_ref, dvec_ref, ng_ref,
                o_ref, hs_ref, carry_ref):
    @pl.when(pl.program_id(1) == 0)
    def _():
        hs_ref[...] = jnp.zeros_like(hs_ref)
        carry_ref[...] = jnp.zeros_like(carry_ref)

    n, P, N = ROW_TILE, SSD_HEADDIM, SSD_STATE
    y = _silu(_causal_conv(xbc_ref[0], carry_ref, cw_ref[...]) + cb_ref[...])
    xs = y[:, :SSD_INNER]
    bm = y[:, SSD_INNER:SSD_INNER + SSD_GROUPS * N]
    cm = y[:, SSD_INNER + SSD_GROUPS * N:]
    sm = sm_ref[0]
    dt_all = _softplus(sm + dtb_ref[...])
    a_all = dt_all * (-jnp.exp(alog_ref[...]))
    causal, _ = _tile_masks(n, n)
    acs_all = _dot(causal.astype(F32), a_all, HI)
    acs_t = acs_all.T
    ys = []
    for g in range(SSD_GROUPS):
        bg = bm[:, g * N:(g + 1) * N]
        cg16 = cm[:, g * N:(g + 1) * N].astype(BF16)
        cb = _dot_nt(cg16, bg.astype(BF16))
        bgt16 = bg.T.astype(BF16)
        for j in range(SSD_HG):
            hh = g * SSD_HG + j
            ac = acs_all[:, L_DT + hh:L_DT + hh + 1]
            ar = acs_t[L_DT + hh:L_DT + hh + 1, :]
            lmat = jnp.exp(jnp.where(causal, ac - ar, NEG))
            xdt = xs[:, hh * P:(hh + 1) * P] * dt_all[:, L_DT + hh:L_DT + hh + 1]
            y_diag = _dot((cb * lmat).astype(BF16), xdt.astype(BF16))
            a_last = ac[n - 1:n, :]
            st = _dot(bgt16, (xdt * jnp.exp(a_last - ac)).astype(BF16))
            h_prev = hs_ref[hh]
            y_off = _dot(cg16, h_prev.astype(BF16)) * jnp.exp(ac)
            hs_ref[hh] = h_prev * jnp.exp(a_last) + st
            ys.append(y_diag + y_off)
    yy = jnp.concatenate(ys, axis=1) + xs * dvec_ref[...]
    yy = yy * _silu(z_ref[0])
    gw = SSD_HG * P
    outs = []
    for g in range(SSD_GROUPS):
        seg = yy[:, g * gw:(g + 1) * gw]
        outs.append(seg * lax.rsqrt(jnp.mean(seg * seg, axis=-1, keepdims=True) + EPS)
                    * ng_ref[:, g * gw:(g + 1) * gw])
    o_ref[0] = jnp.concatenate(outs, axis=1).astype(o_ref.dtype)


def _ssd(proj, conv_w, conv_b, dt_bias, a_log, d_skip, norm_g):
    B, T, _ = proj.shape
    n = ROW_TILE
    W = SSD_INNER + 2 * SSD_GROUPS * SSD_STATE
    dvec = jnp.repeat(d_skip.astype(F32), SSD_HEADDIM).reshape(1, SSD_INNER)
    return pl.pallas_call(
        _ssd_kernel,
        out_shape=jax.ShapeDtypeStruct((B, T, SSD_INNER), BF16),
        grid=(B, T // n),
        in_specs=[pl.BlockSpec((1, n, W), lambda b, t: (b, t, COL_SSD_XBC // W)),
                  pl.BlockSpec((1, n, SSD_INNER), lambda b, t: (b, t, COL_SSD_Z // SSD_INNER)),
                  pl.BlockSpec((1, n, LANES), lambda b, t: (b, t, COL_SMALL // LANES)),
                  pl.BlockSpec((CONV_K, W), lambda b, t: (0, 0)),
                  pl.BlockSpec((1, W), lambda b, t: (0, 0)),
                  pl.BlockSpec((1, LANES), lambda b, t: (0, 0)),
                  pl.BlockSpec((1, LANES), lambda b, t: (0, 0)),
                  pl.BlockSpec((1, SSD_INNER), lambda b, t: (0, 0)),
                  pl.BlockSpec((1, SSD_INNER), lambda b, t: (0, 0))],
        out_specs=pl.BlockSpec((1, n, SSD_INNER), lambda b, t: (b, t, 0)),
        scratch_shapes=[pltpu.VMEM((SSD_HEADS, SSD_STATE, SSD_HEADDIM), F32),
                        pltpu.VMEM((8, W), F32)],
        compiler_params=_cparams(("arbitrary", "arbitrary")),
        name="ssd",
    )(proj, proj, proj, conv_w.astype(F32), conv_b.reshape(1, W).astype(F32),
      _lane_vec(a_log, L_DT), _lane_vec(dt_bias, L_DT), dvec,
      norm_g.reshape(1, SSD_INNER).astype(F32))


def _fox_prep_kernel(qkv_ref, sm_ref, qg_ref, kg_ref, bf_ref, q_ref, k_ref, v_ref, cum_ref, run_ref):
    @pl.when(pl.program_id(1) == 0)
    def _():
        run_ref[...] = jnp.zeros_like(run_ref)

    n, dh, H = ROW_TILE, FOX_DH, FOX_HEADS
    x = qkv_ref[0]
    qs, ks = [], []
    for h in range(H):
        q = x[:, h * dh:(h + 1) * dh]
        k = x[:, H * dh + h * dh:H * dh + (h + 1) * dh]
        qs.append(q * lax.rsqrt(jnp.mean(q * q, axis=-1, keepdims=True) + EPS) * (qg_ref[...] * dh ** -0.5))
        ks.append(k * lax.rsqrt(jnp.mean(k * k, axis=-1, keepdims=True) + EPS) * kg_ref[...])
    q_ref[0] = jnp.concatenate(qs, axis=1).astype(q_ref.dtype)
    k_ref[0] = jnp.concatenate(ks, axis=1).astype(k_ref.dtype)
    v_ref[0] = x[:, 2 * H * dh:].astype(v_ref.dtype)
    log_f = -_softplus(-(sm_ref[0] + bf_ref[...]))
    causal, _ = _tile_masks(n, n)
    cum = _dot(causal.astype(F32), log_f, HI) + run_ref[0:1, :]
    cum_ref[0] = cum
    run_ref[...] = jnp.broadcast_to(cum[n - 1:n, :], run_ref.shape)


def _fox_prep(proj, qn_g, kn_g, b_f):
    B, T, _ = proj.shape
    n = ROW_TILE
    W = 3 * FOX_HEADS * FOX_DH
    HW = FOX_HEADS * FOX_DH
    ospec = pl.BlockSpec((1, n, HW), lambda b, t: (b, t, 0))
    return pl.pallas_call(
        _fox_prep_kernel,
        out_shape=(jax.ShapeDtypeStruct((B, T, HW), BF16),) * 3 + (jax.ShapeDtypeStruct((B, T, LANES), F32),),
        grid=(B, T // n),
        in_specs=[pl.BlockSpec((1, n, W), lambda b, t: (b, t, COL_FOX_QKV // W)),
                  pl.BlockSpec((1, n, LANES), lambda b, t: (b, t, COL_SMALL // LANES)),
                  pl.BlockSpec((1, FOX_DH), lambda b, t: (0, 0)),
                  pl.BlockSpec((1, FOX_DH), lambda b, t: (0, 0)),
                  pl.BlockSpec((1, LANES), lambda b, t: (0, 0))],
        out_specs=(ospec, ospec, ospec, pl.BlockSpec((1, n, LANES), lambda b, t: (b, t, 0))),
        scratch_shapes=[pltpu.VMEM((8, LANES), F32)],
        compiler_params=_cparams(("arbitrary", "arbitrary")),
        name="fox_prep",
    )(proj, proj, qn_g.reshape(1, FOX_DH).astype(F32), kn_g.reshape(1, FOX_DH).astype(F32),
      _lane_vec(b_f, L_FF))


def _mla_prep_kernel(qa_ref, kva_ref, sm_ref, cos_ref, sin_ref, qag_ref, wq_ref, kvag_ref, wkv_ref,
                     qgn_ref, qgr_ref, kgn_ref, kgr_ref, rot_ref, exp_ref, q_ref, k_ref, v_ref):
    n, H, dn, dr = ROW_TILE, MLA_HEADS, MLA_NOPE, MLA_ROPE
    qa = qa_ref[0]
    qa = qa * lax.rsqrt(jnp.mean(qa * qa, axis=-1, keepdims=True) + EPS) * qag_ref[...]
    qq = _dot(qa.astype(BF16), wq_ref[...])
    kva = kva_ref[0]
    kva = kva * lax.rsqrt(jnp.mean(kva * kva, axis=-1, keepdims=True) + EPS) * kvag_ref[...]
    kv = _dot(kva.astype(BF16), wkv_ref[...])
    sm = sm_ref[0]
    lane = lax.broadcasted_iota(jnp.int32, (n, LANES), 1)
    is_kpe = (lane >= L_KPE) & (lane < L_KPE + dr)
    kpe_ss = jnp.sum(jnp.where(is_kpe, sm * sm, 0.0), axis=-1, keepdims=True)
    kpe4 = _dot(sm, exp_ref[...], HI)
    qr = qq[:, H * dn:]
    grp = lax.broadcasted_iota(jnp.int32, (n, H * dr), 1) // dr
    q_rs, k_rs = [], []
    for h in range(H):
        qn = qq[:, h * dn:(h + 1) * dn]
        ssr = jnp.sum(jnp.where(grp == h, qr * qr, 0.0), axis=-1, keepdims=True)
        q_rs.append(lax.rsqrt((jnp.sum(qn * qn, axis=-1, keepdims=True) + ssr) / MLA_DQK + EPS))
        kn = kv[:, h * dn:(h + 1) * dn]
        k_rs.append(lax.rsqrt((jnp.sum(kn * kn, axis=-1, keepdims=True) + kpe_ss) / MLA_DQK + EPS))

    def per_group(vals):
        out = vals[H - 1]
        for h in range(H - 2, -1, -1):
            out = jnp.where(grp == h, vals[h], out)
        return out

    cos, sin, rot = cos_ref[...], sin_ref[...], rot_ref[...]
    tq = qr * per_group(q_rs) * qgr_ref[...]
    tq = tq * cos + _dot(tq, rot, HI) * sin
    tk = kpe4 * per_group(k_rs) * kgr_ref[...]
    tk = tk * cos + _dot(tk, rot, HI) * sin
    scale = MLA_DQK ** -0.5
    half_id = lane // dr
    q_parts, k_parts = [], []
    for h in range(H):
        blk = slice((h // 2) * LANES, (h // 2 + 1) * LANES)
        q_parts.append(qq[:, h * dn:(h + 1) * dn] * q_rs[h] * (qgn_ref[...] * scale))
        q_parts.append(jnp.where(half_id == h % 2, tq[:, blk] * scale, 0.0))
        k_parts.append(kv[:, h * dn:(h + 1) * dn] * k_rs[h] * kgn_ref[...])
        k_parts.append(tk[:, blk])
    q_ref[0] = jnp.concatenate(q_parts, axis=1).astype(q_ref.dtype)
    k_ref[0] = jnp.concatenate(k_parts, axis=1).astype(k_ref.dtype)
    v_ref[0] = kv[:, H * dn:].astype(v_ref.dtype)


def _rope_consts(T):
    H, dr = MLA_HEADS, MLA_ROPE
    inv = 1.0 / (ROPE_BASE ** (jnp.arange(0, dr, 2, dtype=F32) / dr))
    ang = jnp.arange(T, dtype=F32)[:, None] * inv[None, :]
    ang = jnp.concatenate([ang, ang], axis=-1)
    cos4 = jnp.tile(jnp.cos(ang), (1, H))
    sin4 = jnp.tile(jnp.sin(ang), (1, H))
    rot = np.zeros((H * dr, H * dr), np.float32)
    for h in range(H):
        for c in range(dr // 2):
            rot[h * dr + c + dr // 2, h * dr + c] = -1.0
            rot[h * dr + c, h * dr + c + dr // 2] = 1.0
    expand = np.zeros((LANES, H * dr), np.float32)
    for h in range(H):
        for c in range(dr):
            expand[L_KPE + c, h * dr + c] = 1.0
    return cos4, sin4, jnp.asarray(rot), jnp.asarray(expand)


def _mla_prep(proj, rope, qa_g, wq_b, kva_g, wkv_b, qn_g, kn_g):
    B, T, _ = proj.shape
    n, H, dn, dr, dv = ROW_TILE, MLA_HEADS, MLA_NOPE, MLA_ROPE, MLA_V
    cos4, sin4, rot, expand = rope
    wq = wq_b.reshape(MLA_Q_RANK, H, MLA_DQK)
    wq = jnp.concatenate([wq[:, :, :dn].reshape(MLA_Q_RANK, H * dn),
                          wq[:, :, dn:].reshape(MLA_Q_RANK, H * dr)], axis=1).astype(BF16)
    wkv = wkv_b.reshape(MLA_KV_RANK, H, dn + dv)
    wkv = jnp.concatenate([wkv[:, :, :dn].reshape(MLA_KV_RANK, H * dn),
                           wkv[:, :, dn:].reshape(MLA_KV_RANK, H * dv)], axis=1).astype(BF16)
    qg, kg = qn_g.astype(F32), kn_g.astype(F32)
    const = lambda shape: pl.BlockSpec(shape, lambda b, t: (0,) * len(shape))
    QW = H * 2 * LANES
    return pl.pallas_call(
        _mla_prep_kernel,
        out_shape=(jax.ShapeDtypeStruct((B, T, QW), BF16), jax.ShapeDtypeStruct((B, T, QW), BF16),
                   jax.ShapeDtypeStruct((B, T, H * dv), BF16)),
        grid=(B, T // n),
        in_specs=[pl.BlockSpec((1, n, MLA_Q_RANK), lambda b, t: (b, t, COL_MLA_QA // MLA_Q_RANK)),
                  pl.BlockSpec((1, n, MLA_KV_RANK), lambda b, t: (b, t, COL_MLA_KVA // MLA_KV_RANK)),
                  pl.BlockSpec((1, n, LANES), lambda b, t: (b, t, COL_SMALL // LANES)),
                  pl.BlockSpec((n, H * dr), lambda b, t: (t, 0)),
                  pl.BlockSpec((n, H * dr), lambda b, t: (t, 0)),
                  const((1, MLA_Q_RANK)), const(wq.shape), const((1, MLA_KV_RANK)), const(wkv.shape),
                  const((1, dn)), const((1, H * dr)), const((1, dn)), const((1, H * dr)),
                  const(rot.shape), const(expand.shape)],
        out_specs=(pl.BlockSpec((1, n, QW), lambda b, t: (b, t, 0)),
                   pl.BlockSpec((1, n, QW), lambda b, t: (b, t, 0)),
                   pl.BlockSpec((1, n, H * dv), lambda b, t: (b, t, 0))),
        compiler_params=_cparams(("parallel", "parallel")),
        name="mla_prep",
    )(proj, proj, proj, cos4, sin4,
      qa_g.reshape(1, -1).astype(F32), wq, kva_g.reshape(1, -1).astype(F32), wkv,
      qg[:dn].reshape(1, dn), jnp.tile(qg[dn:], H).reshape(1, H * dr),
      kg[:dn].reshape(1, dn), jnp.tile(kg[dn:], H).reshape(1, H * dr), rot, expand)


def _flash_kernel(*refs, tq, has_bias):
    if has_bias:
        q_ref, k_ref, v_ref, cum_ref, o_ref = refs
    else:
        q_ref, k_ref, v_ref, o_ref = refs
    qi = pl.program_id(2)
    q = q_ref[0]
    dv = v_ref.shape[-1]

    def step(j, carry, masked):
        m, l, acc = carry
        start = pl.multiple_of(j * tq, tq)
        s = _dot_nt(q, k_ref[0, pl.ds(start, tq), :])
        if has_bias:
            s = s - cum_ref[0, 0, :, pl.ds(start, tq)]
        if masked:
            row = lax.broadcasted_iota(jnp.int32, s.shape, 0)
            col = lax.broadcasted_iota(jnp.int32, s.shape, 1)
            s = jnp.where(col <= row, s, NEG)
        m_new = jnp.maximum(m, jnp.max(s, axis=-1, keepdims=True))
        alpha = jnp.exp(m - m_new)
        p = jnp.exp(s - m_new)
        l = alpha * l + jnp.sum(p, axis=-1, keepdims=True)
        acc = alpha * acc + _dot(p.astype(BF16), v_ref[0, pl.ds(start, tq), :])
        return m_new, l, acc

    init = (jnp.full((tq, 1), NEG, F32), jnp.zeros((tq, 1), F32), jnp.zeros((tq, dv), F32))
    carry = lax.fori_loop(0, qi, lambda j, c: step(j, c, False), init)
    _, l, acc = step(qi, carry, True)
    o_ref[0] = (acc / l).astype(o_ref.dtype)


def _flash(q, k, v, cum_rows=None, *, tq=ROW_TILE):
    B, T, QW = q.shape
    H = MLA_HEADS
    dqk, dv = QW // H, v.shape[-1] // H
    in_specs = [pl.BlockSpec((1, tq, dqk), lambda b, h, i: (b, i, h)),
                pl.BlockSpec((1, T, dqk), lambda b, h, i: (b, 0, h)),
                pl.BlockSpec((1, T, dv), lambda b, h, i: (b, 0, h))]
    args = [q, k, v]
    if cum_rows is not None:
        in_specs.append(pl.BlockSpec((1, 1, 1, T), lambda b, h, i: (b, h, 0, 0)))
        args.append(cum_rows)
    return pl.pallas_call(
        functools.partial(_flash_kernel, tq=tq, has_bias=cum_rows is not None),
        out_shape=jax.ShapeDtypeStruct((B, T, H * dv), BF16),
        grid=(B, H, T // tq),
        in_specs=in_specs,
        out_specs=pl.BlockSpec((1, tq, dv), lambda b, h, i: (b, i, h)),
        compiler_params=_cparams(("parallel", "parallel", "parallel")),
        name="flash_bias" if cum_rows is not None else "flash",
    )(*args)


def _reorder_w_in(w_in):
    D = w_in.shape[0]
    w = w_in.astype(BF16)
    z = lambda n: jnp.zeros((D, n), BF16)
    small = [w[:, 2048:2052], w[:, 2052:2056], w[:, 4424:4428], w[:, 5964:5972], w[:, 2824:2888]]
    n_small = sum(s.shape[1] for s in small)
    cols = [w[:, 0:2048],
            w[:, 4940:5964],
            w[:, 4428:4940],
            w[:, 2056:2824],
            *small, z(LANES - n_small), z(COL_FOX_QKV - COL_SMALL - LANES),
            w[:, 2888:4424]]
    out = jnp.concatenate(cols, axis=1)
    assert out.shape[1] == PROJ_W
    return out


def _mixer(h, hn, B, T, rope, w_in, gdn_conv_w, gdn_A_log, gdn_dt_bias, gdn_norm_g,
           mla_qa_g, mla_wq_b, mla_kva_g, mla_wkv_b, mla_qn_g, mla_kn_g,
           fox_qn_g, fox_kn_g, fox_b_f,
           ssd_conv_w, ssd_conv_b, ssd_dt_bias, ssd_A_log, ssd_D, ssd_norm_g,
           w_gate, w_branch, w_o):
    M = B * T
    proj = _matmul(hn, _reorder_w_in(w_in), tn=1024, name="in_proj").reshape(B, T, PROJ_W)
    o_gdn = _gdn(proj, gdn_conv_w, gdn_A_log, gdn_dt_bias, gdn_norm_g)
    mq, mk, mv = _mla_prep(proj, rope, mla_qa_g, mla_wq_b, mla_kva_g, mla_wkv_b, mla_qn_g, mla_kn_g)
    o_mla = _flash(mq, mk, mv)
    fq, fk, fv, cum = _fox_prep(proj, fox_qn_g, fox_kn_g, fox_b_f)
    cum_rows = jnp.swapaxes(cum[:, :, L_FF:L_FF + FOX_HEADS], 1, 2)[:, :, None, :]
    o_fox = _flash(fq, fk, fv, cum_rows)
    o_ssd = _ssd(proj, ssd_conv_w, ssd_conv_b, ssd_dt_bias, ssd_A_log, ssd_D, ssd_norm_g)
    branches = [o.reshape(M, BRANCH_W) for o in (o_gdn, o_mla, o_fox, o_ssd)]
    merged = _merge(hn, branches, w_gate.astype(BF16), w_branch.astype(BF16))
    return _matmul(merged, w_o.astype(BF16), tn=512, residual=h, name="out_proj")


def kernel(x, meta_tokens, mix_norm_g, w_in, gdn_conv_w, gdn_A_log, gdn_dt_bias, gdn_norm_g, mla_qa_g, mla_wq_b, mla_kva_g, mla_wkv_b, mla_qn_g, mla_kn_g, fox_qn_g, fox_kn_g, fox_b_f, ssd_conv_w, ssd_conv_b, ssd_dt_bias, ssd_A_log, ssd_D, ssd_norm_g, w_gate, w_branch, w_o, ffn_norm_g, dense_w_gate, dense_w_up, dense_w_down, router_w, moe_w_gate, moe_w_up, moe_w_down):
    B, S, D = x.shape
    L = N_META + S
    T = -(-L // ROW_TILE) * ROW_TILE
    assert (B * T) % MM_TM == 0
    depth = w_in.shape[0]
    meta = jnp.broadcast_to(meta_tokens[None].astype(x.dtype), (B, N_META, D))
    h = jnp.concatenate([meta, x, jnp.zeros((B, T - L, D), x.dtype)], axis=1).reshape(B * T, D)
    rope = _rope_consts(T)
    for layer in range(depth):
        hn = _rmsnorm(h, mix_norm_g[layer])
        h = _mixer(h, hn, B, T, rope, w_in[layer],
                   gdn_conv_w[layer], gdn_A_log[layer], gdn_dt_bias[layer], gdn_norm_g[layer],
                   mla_qa_g[layer], mla_wq_b[layer], mla_kva_g[layer], mla_wkv_b[layer],
                   mla_qn_g[layer], mla_kn_g[layer],
                   fox_qn_g[layer], fox_kn_g[layer], fox_b_f[layer],
                   ssd_conv_w[layer], ssd_conv_b[layer], ssd_dt_bias[layer], ssd_A_log[layer],
                   ssd_D[layer], ssd_norm_g[layer],
                   w_gate[layer], w_branch[layer], w_o[layer])
        i = layer // 2
        if layer % 2 == 0:
            hn = _rmsnorm(h, ffn_norm_g[layer])
            act = _swiglu_up(hn, dense_w_gate[i].astype(BF16), dense_w_up[i].astype(BF16), tn=512)
            h = _matmul(act, dense_w_down[i].astype(BF16), tn=512, residual=h, name="ffn_down")
        else:
            hn, comb = _rmsnorm_router(h, ffn_norm_g[layer], router_w[i])
            comb_cols = jnp.swapaxes(comb[:, :N_EXPERTS], 0, 1)[:, :, None]
            act = _moe_up(hn, moe_w_gate[i].astype(BF16), moe_w_up[i].astype(BF16), comb_cols)
            E, F, _ = moe_w_down[i].shape
            h = _matmul(act, moe_w_down[i].reshape(E * F, D).astype(BF16), tn=512, tm=256,
                        residual=h, name="moe_down")
    return h.reshape(B, T, D)[:, N_META:L].astype(x.dtype)
```

```python
import functools
import math

import numpy as np
import jax
import jax.numpy as jnp
from jax import lax
from jax.experimental import pallas as pl
from jax.experimental.pallas import tpu as pltpu

F32 = jnp.float32
BF16 = jnp.bfloat16
HI = lax.Precision.HIGHEST
NT_DIMS = (((1,), (1,)), ((), ()))

D_MODEL = 2048
N_META = 16
EPS = 1e-6
NEG = -1e30
CONV_K = 4

GDN_HEADS, GDN_DK, GDN_DV = 4, 128, 128
MLA_HEADS, MLA_Q_RANK, MLA_KV_RANK, MLA_NOPE, MLA_ROPE, MLA_V = 4, 512, 256, 128, 64, 128
MLA_DQK = MLA_NOPE + MLA_ROPE
ROPE_BASE = 10000.0
FOX_HEADS, FOX_DH = 4, 128
SSD_HEADS, SSD_HEADDIM, SSD_GROUPS, SSD_STATE = 8, 64, 2, 128
SSD_HG = SSD_HEADS // SSD_GROUPS
SSD_INNER = SSD_HEADS * SSD_HEADDIM
N_BRANCH, BRANCH_W = 4, 512
N_EXPERTS, TOP_K = 8, 2

LANES = 128
ROW_TILE = 256
FLASH_TILES = (768, 256)
FLASH_HEADS_PER_STEP = 2
LOG2E = 1.4426950408889634
MM_TM = 512
VMEM_LIMIT = 56 * 1024 * 1024

PROJ_W = 6144
COL_GDN_QKV, COL_GDN_Z = 0, 1536
COL_SSD_XBC, COL_SSD_Z = 2048, 3072
COL_MLA_QA, COL_MLA_KVA = 3584, 4096
COL_SMALL = 4352
COL_FOX_QKV = 4608
L_BETA, L_GA, L_FF, L_DT, L_KPE = 0, 4, 8, 12, 20


def _cparams(sem, vmem=VMEM_LIMIT):
    return pltpu.CompilerParams(dimension_semantics=sem, vmem_limit_bytes=vmem)


def _softplus(x):
    return jnp.maximum(x, 0.0) + jnp.log1p(jnp.exp(-jnp.abs(x)))


def _silu(x):
    return x * jax.nn.sigmoid(x)


def _dot(a, b, precision=None):
    return jnp.dot(a, b, preferred_element_type=F32, precision=precision)


def _dot_nt(a, b):
    return lax.dot_general(a, b, NT_DIMS, preferred_element_type=F32)


def _rmsnorm_kernel(h_ref, g_ref, o_ref):
    x = h_ref[...]
    y = x * lax.rsqrt(jnp.mean(x * x, axis=-1, keepdims=True) + EPS) * g_ref[...]
    o_ref[...] = y.astype(o_ref.dtype)


def _rmsnorm(h, g):
    M, D = h.shape
    return pl.pallas_call(
        _rmsnorm_kernel,
        out_shape=jax.ShapeDtypeStruct((M, D), BF16),
        grid=(M // MM_TM,),
        in_specs=[pl.BlockSpec((MM_TM, D), lambda i: (i, 0)),
                  pl.BlockSpec((1, D), lambda i: (0, 0))],
        out_specs=pl.BlockSpec((MM_TM, D), lambda i: (i, 0)),
        compiler_params=_cparams(("parallel",)),
        name="rmsnorm",
    )(h, g.reshape(1, D).astype(F32))


def _rmsnorm_router_kernel(h_ref, g_ref, rw_ref, o_ref, comb_ref):
    x = h_ref[...]
    y = x * lax.rsqrt(jnp.mean(x * x, axis=-1, keepdims=True) + EPS) * g_ref[...]
    o_ref[...] = y.astype(o_ref.dtype)
    logits = _dot(y, rw_ref[...], HI)
    lane = lax.broadcasted_iota(jnp.int32, logits.shape, 1)
    logits = jnp.where(lane < N_EXPERTS, logits, NEG)
    m1 = jnp.max(logits, axis=-1, keepdims=True)
    i1 = jnp.min(jnp.where(logits == m1, lane, LANES), axis=-1, keepdims=True)
    rest = jnp.where(lane == i1, NEG, logits)
    m2 = jnp.max(rest, axis=-1, keepdims=True)
    i2 = jnp.min(jnp.where(rest == m2, lane, LANES), axis=-1, keepdims=True)
    e2 = jnp.exp(m2 - m1)
    p1 = 1.0 / (1.0 + e2)
    p2 = e2 * p1
    comb_ref[...] = jnp.where(lane == i1, p1, 0.0) + jnp.where(lane == i2, p2, 0.0)


def _rmsnorm_router(h, g, router_w):
    M, D = h.shape
    rw = jnp.zeros((D, LANES), F32).at[:, :N_EXPERTS].set(router_w.astype(F32))
    return pl.pallas_call(
        _rmsnorm_router_kernel,
        out_shape=(jax.ShapeDtypeStruct((M, D), BF16), jax.ShapeDtypeStruct((M, LANES), F32)),
        grid=(M // MM_TM,),
        in_specs=[pl.BlockSpec((MM_TM, D), lambda i: (i, 0)),
                  pl.BlockSpec((1, D), lambda i: (0, 0)),
                  pl.BlockSpec((D, LANES), lambda i: (0, 0))],
        out_specs=(pl.BlockSpec((MM_TM, D), lambda i: (i, 0)),
                   pl.BlockSpec((MM_TM, LANES), lambda i: (i, 0))),
        compiler_params=_cparams(("parallel",)),
        name="rmsnorm_router",
    )(h, g.reshape(1, D).astype(F32), rw)


def _mm_kernel(a_ref, w_ref, o_ref):
    o_ref[...] = _dot(a_ref[...], w_ref[...]).astype(o_ref.dtype)


def _mm_res_kernel(a_ref, w_ref, r_ref, o_ref):
    o_ref[...] = r_ref[...] + _dot(a_ref[...], w_ref[...])


def _matmul(a, w, *, tn, tm=MM_TM, residual=None, out_dtype=F32, name="matmul"):
    M, K = a.shape
    N = w.shape[1]
    in_specs = [pl.BlockSpec((tm, K), lambda j, i: (i, 0)),
                pl.BlockSpec((K, tn), lambda j, i: (0, j))]
    args = [a, w]
    kern = _mm_kernel
    if residual is not None:
        in_specs.append(pl.BlockSpec((tm, tn), lambda j, i: (i, j)))
        args.append(residual)
        kern = _mm_res_kernel
    return pl.pallas_call(
        kern,
        out_shape=jax.ShapeDtypeStruct((M, N), out_dtype),
        grid=(N // tn, M // tm),
        in_specs=in_specs,
        out_specs=pl.BlockSpec((tm, tn), lambda j, i: (i, j)),
        compiler_params=_cparams(("parallel", "parallel")),
        name=name,
    )(*args)


def _swiglu_kernel(a_ref, wg_ref, wu_ref, o_ref):
    a = a_ref[...]
    g = _dot(a, wg_ref[...])
    u = _dot(a, wu_ref[...])
    o_ref[...] = (_silu(g) * u).astype(o_ref.dtype)


def _swiglu_up(a, wg, wu, *, tn, tm=MM_TM):
    M, K = a.shape
    F = wg.shape[1]
    return pl.pallas_call(
        _swiglu_kernel,
        out_shape=jax.ShapeDtypeStruct((M, F), BF16),
        grid=(F // tn, M // tm),
        in_specs=[pl.BlockSpec((tm, K), lambda j, i: (i, 0)),
                  pl.BlockSpec((K, tn), lambda j, i: (0, j)),
                  pl.BlockSpec((K, tn), lambda j, i: (0, j))],
        out_specs=pl.BlockSpec((tm, tn), lambda j, i: (i, j)),
        compiler_params=_cparams(("parallel", "parallel")),
        name="swiglu_up",
    )(a, wg, wu)


def _moe_up_kernel(a_ref, wg_ref, wu_ref, c_ref, o_ref):
    a = a_ref[...]
    g = _dot(a, wg_ref[0])
    u = _dot(a, wu_ref[0])
    o_ref[...] = (_silu(g) * u * c_ref[0]).astype(o_ref.dtype)


def _moe_up(a, wg, wu, comb_cols, *, tm=MM_TM):
    M, K = a.shape
    E, _, F = wg.shape
    return pl.pallas_call(
        _moe_up_kernel,
        out_shape=jax.ShapeDtypeStruct((M, E * F), BF16),
        grid=(E, M // tm),
        in_specs=[pl.BlockSpec((tm, K), lambda e, i: (i, 0)),
                  pl.BlockSpec((1, K, F), lambda e, i: (e, 0, 0)),
                  pl.BlockSpec((1, K, F), lambda e, i: (e, 0, 0)),
                  pl.BlockSpec((1, tm, 1), lambda e, i: (e, i, 0))],
        out_specs=pl.BlockSpec((tm, F), lambda e, i: (i, e)),
        compiler_params=_cparams(("parallel", "parallel")),
        name="moe_up",
    )(a, wg, wu, comb_cols)


def _merge_kernel(hn_ref, b0_ref, b1_ref, b2_ref, b3_ref, wg_ref, wb_ref, o_ref):
    hn = hn_ref[...]
    acc = None
    for b, br_ref in enumerate((b0_ref, b1_ref, b2_ref, b3_ref)):
        gate = jax.nn.sigmoid(_dot(hn, wg_ref[b]))
        term = gate * _dot(br_ref[...], wb_ref[b])
        acc = term if acc is None else acc + term
    o_ref[...] = acc.astype(o_ref.dtype)


def _merge(hn, branches, wg, wb, *, tn=512, tm=MM_TM):
    M, D = hn.shape
    N = wg.shape[2]
    bspec = pl.BlockSpec((tm, BRANCH_W), lambda j, i: (i, 0))
    return pl.pallas_call(
        _merge_kernel,
        out_shape=jax.ShapeDtypeStruct((M, N), BF16),
        grid=(N // tn, M // tm),
        in_specs=[pl.BlockSpec((tm, D), lambda j, i: (i, 0)), bspec, bspec, bspec, bspec,
                  pl.BlockSpec((N_BRANCH, D, tn), lambda j, i: (0, 0, j)),
                  pl.BlockSpec((N_BRANCH, BRANCH_W, tn), lambda j, i: (0, 0, j))],
        out_specs=pl.BlockSpec((tm, tn), lambda j, i: (i, j)),
        compiler_params=_cparams(("parallel", "parallel")),
        name="gate_merge",
    )(hn, *branches, wg, wb)


def _causal_conv(x, carry_ref, cw):
    n = x.shape[0]
    xext = jnp.concatenate([carry_ref[...], x], axis=0)
    y = cw[0:1] * xext[5:5 + n]
    for i in range(1, CONV_K):
        y = y + cw[i:i + 1] * xext[5 + i:5 + i + n]
    carry_ref[...] = x[n - 8:n]
    return y


def _tile_masks(n, chunk):
    row = lax.broadcasted_iota(jnp.int32, (n, n), 0)
    col = lax.broadcasted_iota(jnp.int32, (n, n), 1)
    if chunk == n:
        return col <= row, col < row
    in_chunk = col >= (row // chunk) * chunk
    return in_chunk & (col <= row), in_chunk & (col < row)


def _gdn_kernel(qkv_ref, z_ref, sm_ref, cw_ref, alog_ref, dtb_ref, ng_ref, o_ref, s_ref, carry_ref):
    @pl.when(pl.program_id(1) == 0)
    def _():
        s_ref[...] = jnp.zeros_like(s_ref)
        carry_ref[...] = jnp.zeros_like(carry_ref)

    n = ROW_TILE
    y = _silu(_causal_conv(qkv_ref[0], carry_ref, cw_ref[...]))
    sm = sm_ref[0]
    z = z_ref[0]
    beta_all = jax.nn.sigmoid(sm)
    g_all = -jnp.exp(alog_ref[...]) * _softplus(sm + dtb_ref[...])
    row = lax.broadcasted_iota(jnp.int32, (n, n), 0)
    col = lax.broadcasted_iota(jnp.int32, (n, n), 1)
    causal, strict = col <= row, col < row
    diff_bits = row ^ col
    levels = int(math.log2(n))
    level_masks = [(diff_bits >= (1 << l)) & (diff_bits < (2 << l)) for l in range(levels)]
    eye = jnp.where(row == col, 1.0, 0.0)
    gcs_all = _dot(causal.astype(F32), g_all, HI)
    gcs_t = gcs_all.T
    outs = []
    for h in range(GDN_HEADS):
        q = y[:, h * GDN_DK:(h + 1) * GDN_DK]
        k = y[:, GDN_HEADS * GDN_DK + h * GDN_DK:GDN_HEADS * GDN_DK + (h + 1) * GDN_DK]
        v = y[:, 2 * GDN_HEADS * GDN_DK + h * GDN_DV:2 * GDN_HEADS * GDN_DK + (h + 1) * GDN_DV]
        q = q * lax.rsqrt(jnp.sum(q * q, axis=-1, keepdims=True) + EPS) * GDN_DK ** -0.5
        k = k * lax.rsqrt(jnp.sum(k * k, axis=-1, keepdims=True) + EPS)
        beta = beta_all[:, L_BETA + h:L_BETA + h + 1]
        gc = gcs_all[:, L_GA + h:L_GA + h + 1]
        gr = gcs_t[L_GA + h:L_GA + h + 1, :]
        decay = jnp.exp(jnp.where(causal, gc - gr, NEG))
        kb = k * beta
        k16 = k.astype(BF16)
        a = jnp.where(strict, _dot_nt(kb.astype(BF16), k16) * decay, 0.0)
        t = eye - jnp.where(level_masks[0], a, 0.0)
        for l in range(1, levels):
            t16 = t.astype(BF16)
            a_l = jnp.where(level_masks[l], a, 0.0).astype(BF16)
            t = t - _dot(_dot(t16, a_l).astype(BF16), t16)
        egc = jnp.exp(gc)
        rhs = jnp.concatenate([v * beta, kb * egc], axis=1)
        uw = _dot(t.astype(BF16), rhs.astype(BF16))
        u, w = uw[:, :GDN_DV], uw[:, GDN_DV:]
        att = _dot_nt(q.astype(BF16), k16) * decay
        g_last = gc[n - 1:n, :]
        kd = k * jnp.exp(g_last - gc)
        s = s_ref[h]
        ws = _dot(jnp.concatenate([w, q * egc], axis=0).astype(BF16), s.astype(BF16))
        v_new = (u - ws[:n]).astype(BF16)
        o = ws[n:] + _dot(att.astype(BF16), v_new)
        s_ref[h] = s * jnp.exp(g_last) + _dot(kd.T.astype(BF16), v_new)
        o = o * lax.rsqrt(jnp.mean(o * o, axis=-1, keepdims=True) + EPS) * ng_ref[...]
        outs.append(o * _silu(z[:, h * GDN_DV:(h + 1) * GDN_DV]))
    o_ref[0] = jnp.concatenate(outs, axis=1).astype(o_ref.dtype)


def _lane_vec(vals, lane0):
    v = jnp.zeros((1, LANES), F32)
    return v.at[0, lane0:lane0 + vals.shape[0]].set(vals.astype(F32))


def _gdn(proj, conv_w, a_log, dt_bias, norm_g):
    B, T, _ = proj.shape
    n = ROW_TILE
    W = 2 * GDN_HEADS * GDN_DK + GDN_HEADS * GDN_DV
    ZW = GDN_HEADS * GDN_DV
    return pl.pallas_call(
        _gdn_kernel,
        out_shape=jax.ShapeDtypeStruct((B, T, ZW), BF16),
        grid=(B, T // n),
        in_specs=[pl.BlockSpec((1, n, W), lambda b, t: (b, t, COL_GDN_QKV // W)),
                  pl.BlockSpec((1, n, ZW), lambda b, t: (b, t, COL_GDN_Z // ZW)),
                  pl.BlockSpec((1, n, LANES), lambda b, t: (b, t, COL_SMALL // LANES)),
                  pl.BlockSpec((CONV_K, W), lambda b, t: (0, 0)),
                  pl.BlockSpec((1, LANES), lambda b, t: (0, 0)),
                  pl.BlockSpec((1, LANES), lambda b, t: (0, 0)),
                  pl.BlockSpec((1, GDN_DV), lambda b, t: (0, 0))],
        out_specs=pl.BlockSpec((1, n, ZW), lambda b, t: (b, t, 0)),
        scratch_shapes=[pltpu.VMEM((GDN_HEADS, GDN_DK, GDN_DV), F32),
                        pltpu.VMEM((8, W), F32)],
        compiler_params=_cparams(("arbitrary", "arbitrary")),
        name="gdn",
    )(proj, proj, proj, conv_w.astype(F32), _lane_vec(a_log, L_GA), _lane_vec(dt_bias, L_GA),
      norm_g.reshape(1, GDN_DV).astype(F32))


def _ssd_kernel(xbc_ref, z_ref, sm_ref, cw_ref, cb_ref, alog_ref, dtb_ref, dvec_ref, ng_ref,
                o_ref, hs_ref, carry_ref):
    @pl.when(pl.program_id(1) == 0)
    def _():
        hs_ref[...] = jnp.zeros_like(hs_ref)
        carry_ref[...] = jnp.zeros_like(carry_ref)

    n, P, N = ROW_TILE, SSD_HEADDIM, SSD_STATE
    y = _silu(_causal_conv(xbc_ref[0], carry_ref, cw_ref[...]) + cb_ref[...])
    xs = y[:, :SSD_INNER]
    bm = y[:, SSD_INNER:SSD_INNER + SSD_GROUPS * N]
    cm = y[:, SSD_INNER + SSD_GROUPS * N:]
    sm = sm_ref[0]
    dt_all = _softplus(sm + dtb_ref[...])
    a_all = dt_all * (-jnp.exp(alog_ref[...]))
    causal, _ = _tile_masks(n, n)
    acs_all = _dot(causal.astype(F32), a_all, HI)
    acs_t = acs_all.T
    ys = []
    for g in range(SSD_GROUPS):
        bg = bm[:, g * N:(g + 1) * N]
        cg16 = cm[:, g * N:(g + 1) * N].astype(BF16)
        cb = _dot_nt(cg16, bg.astype(BF16))
        bgt16 = bg.T.astype(BF16)
        for j in range(SSD_HG):
            hh = g * SSD_HG + j
            ac = acs_all[:, L_DT + hh:L_DT + hh + 1]
            ar = acs_t[L_DT + hh:L_DT + hh + 1, :]
            lmat = jnp.exp(jnp.where(causal, ac - ar, NEG))
            xdt = xs[:, hh * P:(hh + 1) * P] * dt_all[:, L_DT + hh:L_DT + hh + 1]
            y_diag = _dot((cb * lmat).astype(BF16), xdt.astype(BF16))
            a_last = ac[n - 1:n, :]
            st = _dot(bgt16, (xdt * jnp.exp(a_last - ac)).astype(BF16))
            h_prev = hs_ref[hh]
            y_off = _dot(cg16, h_prev.astype(BF16)) * jnp.exp(ac)
            hs_ref[hh] = h_prev * jnp.exp(a_last) + st
            ys.append(y_diag + y_off)
    yy = jnp.concatenate(ys, axis=1) + xs * dvec_ref[...]
    yy = yy * _silu(z_ref[0])
    gw = SSD_HG * P
    outs = []
    for g in range(SSD_GROUPS):
        seg = yy[:, g * gw:(g + 1) * gw]
        outs.append(seg * lax.rsqrt(jnp.mean(seg * seg, axis=-1, keepdims=True) + EPS)
                    * ng_ref[:, g * gw:(g + 1) * gw])
    o_ref[0] = jnp.concatenate(outs, axis=1).astype(o_ref.dtype)


def _ssd(proj, conv_w, conv_b, dt_bias, a_log, d_skip, norm_g):
    B, T, _ = proj.shape
    n = ROW_TILE
    W = SSD_INNER + 2 * SSD_GROUPS * SSD_STATE
    dvec = jnp.repeat(d_skip.astype(F32), SSD_HEADDIM).reshape(1, SSD_INNER)
    return pl.pallas_call(
        _ssd_kernel,
        out_shape=jax.ShapeDtypeStruct((B, T, SSD_INNER), BF16),
        grid=(B, T // n),
        in_specs=[pl.BlockSpec((1, n, W), lambda b, t: (b, t, COL_SSD_XBC // W)),
                  pl.BlockSpec((1, n, SSD_INNER), lambda b, t: (b, t, COL_SSD_Z // SSD_INNER)),
                  pl.BlockSpec((1, n, LANES), lambda b, t: (b, t, COL_SMALL // LANES)),
                  pl.BlockSpec((CONV_K, W), lambda b, t: (0, 0)),
                  pl.BlockSpec((1, W), lambda b, t: (0, 0)),
                  pl.BlockSpec((1, LANES), lambda b, t: (0, 0)),
                  pl.BlockSpec((1, LANES), lambda b, t: (0, 0)),
                  pl.BlockSpec((1, SSD_INNER), lambda b, t: (0, 0)),
                  pl.BlockSpec((1, SSD_INNER), lambda b, t: (0, 0))],
        out_specs=pl.BlockSpec((1, n, SSD_INNER), lambda b, t: (b, t, 0)),
        scratch_shapes=[pltpu.VMEM((SSD_HEADS, SSD_STATE, SSD_HEADDIM), F32),
                        pltpu.VMEM((8, W), F32)],
        compiler_params=_cparams(("arbitrary", "arbitrary")),
        name="ssd",
    )(proj, proj, proj, conv_w.astype(F32), conv_b.reshape(1, W).astype(F32),
      _lane_vec(a_log, L_DT), _lane_vec(dt_bias, L_DT), dvec,
      norm_g.reshape(1, SSD_INNER).astype(F32))


def _ones_lane0(n):
    lane = lax.broadcasted_iota(jnp.int32, (n, LANES), 1)
    return jnp.where(lane == 0, 1.0, 0.0)


def _fox_prep_kernel(qkv_ref, sm_ref, qg_ref, kg_ref, bf_ref, q_ref, k_ref, v_ref, run_ref):
    @pl.when(pl.program_id(1) == 0)
    def _():
        run_ref[...] = jnp.zeros_like(run_ref)

    n, dh, H = ROW_TILE, FOX_DH, FOX_HEADS
    x = qkv_ref[0]
    log_f = -_softplus(-(sm_ref[0] + bf_ref[...]))
    causal, _ = _tile_masks(n, n)
    cum = _dot(causal.astype(F32), log_f, HI) + run_ref[0:1, :]
    run_ref[...] = jnp.broadcast_to(cum[n - 1:n, :], run_ref.shape)
    lane = lax.broadcasted_iota(jnp.int32, (n, LANES), 1)
    ones = _ones_lane0(n)
    q_bias = jnp.where(lane < 3, 1.0, 0.0)
    qs, ks, vs = [], [], []
    for h in range(H):
        q = x[:, h * dh:(h + 1) * dh]
        k = x[:, H * dh + h * dh:H * dh + (h + 1) * dh]
        qs.append(q * lax.rsqrt(jnp.mean(q * q, axis=-1, keepdims=True) + EPS)
                  * (qg_ref[...] * (dh ** -0.5 * LOG2E)))
        qs.append(q_bias)
        ks.append(k * lax.rsqrt(jnp.mean(k * k, axis=-1, keepdims=True) + EPS) * kg_ref[...])
        c = cum[:, L_FF + h:L_FF + h + 1] * (-LOG2E)
        c_hi = c.astype(BF16).astype(F32)
        c_mid = (c - c_hi).astype(BF16).astype(F32)
        c_lo = c - c_hi - c_mid
        ks.append(jnp.where(lane == 0, c_hi, jnp.where(lane == 1, c_mid, jnp.where(lane == 2, c_lo, 0.0))))
        vs.append(x[:, 2 * H * dh + h * dh:2 * H * dh + (h + 1) * dh])
        vs.append(ones)
    q_ref[0] = jnp.concatenate(qs, axis=1).astype(q_ref.dtype)
    k_ref[0] = jnp.concatenate(ks, axis=1).astype(k_ref.dtype)
    v_ref[0] = jnp.concatenate(vs, axis=1).astype(v_ref.dtype)


def _fox_prep(proj, qn_g, kn_g, b_f):
    B, T, _ = proj.shape
    n = ROW_TILE
    W = 3 * FOX_HEADS * FOX_DH
    HW = FOX_HEADS * 2 * LANES
    ospec = pl.BlockSpec((1, n, HW), lambda b, t: (b, t, 0))
    return pl.pallas_call(
        _fox_prep_kernel,
        out_shape=(jax.ShapeDtypeStruct((B, T, HW), BF16),) * 3,
        grid=(B, T // n),
        in_specs=[pl.BlockSpec((1, n, W), lambda b, t: (b, t, COL_FOX_QKV // W)),
                  pl.BlockSpec((1, n, LANES), lambda b, t: (b, t, COL_SMALL // LANES)),
                  pl.BlockSpec((1, FOX_DH), lambda b, t: (0, 0)),
                  pl.BlockSpec((1, FOX_DH), lambda b, t: (0, 0)),
                  pl.BlockSpec((1, LANES), lambda b, t: (0, 0))],
        out_specs=(ospec, ospec, ospec),
        scratch_shapes=[pltpu.VMEM((8, LANES), F32)],
        compiler_params=_cparams(("arbitrary", "arbitrary")),
        name="fox_prep",
    )(proj, proj, qn_g.reshape(1, FOX_DH).astype(F32), kn_g.reshape(1, FOX_DH).astype(F32),
      _lane_vec(b_f, L_FF))


def _mla_prep_kernel(qa_ref, kva_ref, sm_ref, cos_ref, sin_ref, qag_ref, wq_ref, kvag_ref, wkv_ref,
                     qgn_ref, qgr_ref, kgn_ref, kgr_ref, rot_ref, exp_ref, q_ref, k_ref, v_ref):
    n, H, dn, dr = ROW_TILE, MLA_HEADS, MLA_NOPE, MLA_ROPE
    qa = qa_ref[0]
    qa = qa * lax.rsqrt(jnp.mean(qa * qa, axis=-1, keepdims=True) + EPS) * qag_ref[...]
    qq = _dot(qa.astype(BF16), wq_ref[...])
    kva = kva_ref[0]
    kva = kva * lax.rsqrt(jnp.mean(kva * kva, axis=-1, keepdims=True) + EPS) * kvag_ref[...]
    kv = _dot(kva.astype(BF16), wkv_ref[...])
    sm = sm_ref[0]
    lane = lax.broadcasted_iota(jnp.int32, (n, LANES), 1)
    is_kpe = (lane >= L_KPE) & (lane < L_KPE + dr)
    kpe_ss = jnp.sum(jnp.where(is_kpe, sm * sm, 0.0), axis=-1, keepdims=True)
    kpe4 = _dot(sm, exp_ref[...], HI)
    qr = qq[:, H * dn:]
    grp = lax.broadcasted_iota(jnp.int32, (n, H * dr), 1) // dr
    q_rs, k_rs = [], []
    for h in range(H):
        qn = qq[:, h * dn:(h + 1) * dn]
        ssr = jnp.sum(jnp.where(grp == h, qr * qr, 0.0), axis=-1, keepdims=True)
        q_rs.append(lax.rsqrt((jnp.sum(qn * qn, axis=-1, keepdims=True) + ssr) / MLA_DQK + EPS))
        kn = kv[:, h * dn:(h + 1) * dn]
        k_rs.append(lax.rsqrt((jnp.sum(kn * kn, axis=-1, keepdims=True) + kpe_ss) / MLA_DQK + EPS))

    def per_group(vals):
        out = vals[H - 1]
        for h in range(H - 2, -1, -1):
            out = jnp.where(grp == h, vals[h], out)
        return out

    cos, sin, rot = cos_ref[...], sin_ref[...], rot_ref[...]
    tq = qr * per_group(q_rs) * qgr_ref[...]
    tq = tq * cos + _dot(tq, rot, HI) * sin
    tk = kpe4 * per_group(k_rs) * kgr_ref[...]
    tk = tk * cos + _dot(tk, rot, HI) * sin
    scale = MLA_DQK ** -0.5 * LOG2E
    half_id = lane // dr
    ones = _ones_lane0(n)
    q_parts, k_parts, v_parts = [], [], []
    for h in range(H):
        blk = slice((h // 2) * LANES, (h // 2 + 1) * LANES)
        q_parts.append(qq[:, h * dn:(h + 1) * dn] * q_rs[h] * (qgn_ref[...] * scale))
        q_parts.append(jnp.where(half_id == h % 2, tq[:, blk] * scale, 0.0))
        k_parts.append(kv[:, h * dn:(h + 1) * dn] * k_rs[h] * kgn_ref[...])
        k_parts.append(tk[:, blk])
        v_parts.append(kv[:, H * dn + h * MLA_V:H * dn + (h + 1) * MLA_V])
        v_parts.append(ones)
    q_ref[0] = jnp.concatenate(q_parts, axis=1).astype(q_ref.dtype)
    k_ref[0] = jnp.concatenate(k_parts, axis=1).astype(k_ref.dtype)
    v_ref[0] = jnp.concatenate(v_parts, axis=1).astype(v_ref.dtype)


def _rope_consts(T):
    H, dr = MLA_HEADS, MLA_ROPE
    inv = 1.0 / (ROPE_BASE ** (jnp.arange(0, dr, 2, dtype=F32) / dr))
    ang = jnp.arange(T, dtype=F32)[:, None] * inv[None, :]
    ang = jnp.concatenate([ang, ang], axis=-1)
    cos4 = jnp.tile(jnp.cos(ang), (1, H))
    sin4 = jnp.tile(jnp.sin(ang), (1, H))
    rot = np.zeros((H * dr, H * dr), np.float32)
    for h in range(H):
        for c in range(dr // 2):
            rot[h * dr + c + dr // 2, h * dr + c] = -1.0
            rot[h * dr + c, h * dr + c + dr // 2] = 1.0
    expand = np.zeros((LANES, H * dr), np.float32)
    for h in range(H):
        for c in range(dr):
            expand[L_KPE + c, h * dr + c] = 1.0
    return cos4, sin4, jnp.asarray(rot), jnp.asarray(expand)


def _mla_prep(proj, rope, qa_g, wq_b, kva_g, wkv_b, qn_g, kn_g):
    B, T, _ = proj.shape
    n, H, dn, dr, dv = ROW_TILE, MLA_HEADS, MLA_NOPE, MLA_ROPE, MLA_V
    cos4, sin4, rot, expand = rope
    wq = wq_b.reshape(MLA_Q_RANK, H, MLA_DQK)
    wq = jnp.concatenate([wq[:, :, :dn].reshape(MLA_Q_RANK, H * dn),
                          wq[:, :, dn:].reshape(MLA_Q_RANK, H * dr)], axis=1).astype(BF16)
    wkv = wkv_b.reshape(MLA_KV_RANK, H, dn + dv)
    wkv = jnp.concatenate([wkv[:, :, :dn].reshape(MLA_KV_RANK, H * dn),
                           wkv[:, :, dn:].reshape(MLA_KV_RANK, H * dv)], axis=1).astype(BF16)
    qg, kg = qn_g.astype(F32), kn_g.astype(F32)
    const = lambda shape: pl.BlockSpec(shape, lambda b, t: (0,) * len(shape))
    QW = H * 2 * LANES
    return pl.pallas_call(
        _mla_prep_kernel,
        out_shape=(jax.ShapeDtypeStruct((B, T, QW), BF16), jax.ShapeDtypeStruct((B, T, QW), BF16),
                   jax.ShapeDtypeStruct((B, T, QW), BF16)),
        grid=(B, T // n),
        in_specs=[pl.BlockSpec((1, n, MLA_Q_RANK), lambda b, t: (b, t, COL_MLA_QA // MLA_Q_RANK)),
                  pl.BlockSpec((1, n, MLA_KV_RANK), lambda b, t: (b, t, COL_MLA_KVA // MLA_KV_RANK)),
                  pl.BlockSpec((1, n, LANES), lambda b, t: (b, t, COL_SMALL // LANES)),
                  pl.BlockSpec((n, H * dr), lambda b, t: (t, 0)),
                  pl.BlockSpec((n, H * dr), lambda b, t: (t, 0)),
                  const((1, MLA_Q_RANK)), const(wq.shape), const((1, MLA_KV_RANK)), const(wkv.shape),
                  const((1, dn)), const((1, H * dr)), const((1, dn)), const((1, H * dr)),
                  const(rot.shape), const(expand.shape)],
        out_specs=(pl.BlockSpec((1, n, QW), lambda b, t: (b, t, 0)),
                   pl.BlockSpec((1, n, QW), lambda b, t: (b, t, 0)),
                   pl.BlockSpec((1, n, QW), lambda b, t: (b, t, 0))),
        compiler_params=_cparams(("parallel", "parallel")),
        name="mla_prep",
    )(proj, proj, proj, cos4, sin4,
      qa_g.reshape(1, -1).astype(F32), wq, kva_g.reshape(1, -1).astype(F32), wkv,
      qg[:dn].reshape(1, dn), jnp.tile(qg[dn:], H).reshape(1, H * dr),
      kg[:dn].reshape(1, dn), jnp.tile(kg[dn:], H).reshape(1, H * dr), rot, expand)


def _flash_kernel(q_ref, k_ref, v_ref, o_ref, m_ref, acc_ref, *, tq):
    qi = pl.program_id(2)
    nh = m_ref.shape[0]
    dqk, dv2 = q_ref.shape[-1] // nh, v_ref.shape[-1] // nh
    dv = dv2 // 2
    rb = ROW_TILE
    m_ref[...] = jnp.full(m_ref.shape, NEG, F32)
    acc_ref[...] = jnp.zeros(acc_ref.shape, F32)

    def step(j, masked):
        start = pl.multiple_of(j * tq, tq)
        for h in range(nh):
            for r in range(0, tq, rb):
                s = _dot_nt(q_ref[0, r:r + rb, h * dqk:(h + 1) * dqk],
                            k_ref[0, pl.ds(start, tq), h * dqk:(h + 1) * dqk])
                if masked:
                    row = lax.broadcasted_iota(jnp.int32, s.shape, 0) + r
                    col = lax.broadcasted_iota(jnp.int32, s.shape, 1)
                    s = jnp.where(col <= row, s, NEG)
                lane_max = s[:, :LANES]
                for c in range(1, tq // LANES):
                    lane_max = jnp.maximum(lane_max, s[:, c * LANES:(c + 1) * LANES])
                m_old = m_ref[h, r:r + rb]
                m_new = jnp.maximum(m_old, jnp.max(lane_max, axis=-1, keepdims=True))
                m_ref[h, r:r + rb] = m_new
                p = jnp.exp2(s - m_new).astype(BF16)
                acc_ref[h, r:r + rb] = (jnp.exp2(m_old - m_new) * acc_ref[h, r:r + rb]
                                        + _dot(p, v_ref[0, pl.ds(start, tq), h * dv2:(h + 1) * dv2]))

    def body(j, carry):
        step(j, False)
        return carry

    lax.fori_loop(0, qi, body, 0)
    step(qi, True)
    outs = []
    for h in range(nh):
        acc = acc_ref[h]
        outs.append(acc[:, :dv] / acc[:, dv:dv + 1])
    o_ref[0] = jnp.concatenate(outs, axis=1).astype(o_ref.dtype)


def _flash(q, k, v):
    B, T, QW = q.shape
    H, nh = MLA_HEADS, FLASH_HEADS_PER_STEP
    dqk, dv2 = QW // H, v.shape[-1] // H
    dv = dv2 // 2
    tq = next(t for t in FLASH_TILES if T % t == 0)
    return pl.pallas_call(
        functools.partial(_flash_kernel, tq=tq),
        out_shape=jax.ShapeDtypeStruct((B, T, H * dv), BF16),
        grid=(B, H // nh, T // tq),
        in_specs=[pl.BlockSpec((1, tq, nh * dqk), lambda b, h, i: (b, i, h)),
                  pl.BlockSpec((1, T, nh * dqk), lambda b, h, i: (b, 0, h)),
                  pl.BlockSpec((1, T, nh * dv2), lambda b, h, i: (b, 0, h))],
        out_specs=pl.BlockSpec((1, tq, nh * dv), lambda b, h, i: (b, i, h)),
        scratch_shapes=[pltpu.VMEM((nh, tq, 1), F32), pltpu.VMEM((nh, tq, dv2), F32)],
        compiler_params=_cparams(("parallel", "parallel", "arbitrary")),
        name="flash",
    )(q, k, v)


def _reorder_w_in(w_in):
    D = w_in.shape[0]
    w = w_in.astype(BF16)
    z = lambda n: jnp.zeros((D, n), BF16)
    small = [w[:, 2048:2052], w[:, 2052:2056], w[:, 4424:4428], w[:, 5964:5972], w[:, 2824:2888]]
    n_small = sum(s.shape[1] for s in small)
    cols = [w[:, 0:2048],
            w[:, 4940:5964],
            w[:, 4428:4940],
            w[:, 2056:2824],
            *small, z(LANES - n_small), z(COL_FOX_QKV - COL_SMALL - LANES),
            w[:, 2888:4424]]
    out = jnp.concatenate(cols, axis=1)
    assert out.shape[1] == PROJ_W
    return out


def _mixer(h, hn, B, T, rope, w_in, gdn_conv_w, gdn_A_log, gdn_dt_bias, gdn_norm_g,
           mla_qa_g, mla_wq_b, mla_kva_g, mla_wkv_b, mla_qn_g, mla_kn_g,
           fox_qn_g, fox_kn_g, fox_b_f,
           ssd_conv_w, ssd_conv_b, ssd_dt_bias, ssd_A_log, ssd_D, ssd_norm_g,
           w_gate, w_branch, w_o):
    M = B * T
    proj = _matmul(hn, _reorder_w_in(w_in), tn=1024, name="in_proj").reshape(B, T, PROJ_W)
    o_gdn = _gdn(proj, gdn_conv_w, gdn_A_log, gdn_dt_bias, gdn_norm_g)
    mq, mk, mv = _mla_prep(proj, rope, mla_qa_g, mla_wq_b, mla_kva_g, mla_wkv_b, mla_qn_g, mla_kn_g)
    o_mla = _flash(mq, mk, mv)
    o_fox = _flash(*_fox_prep(proj, fox_qn_g, fox_kn_g, fox_b_f))
    o_ssd = _ssd(proj, ssd_conv_w, ssd_conv_b, ssd_dt_bias, ssd_A_log, ssd_D, ssd_norm_g)
    branches = [o.reshape(M, BRANCH_W) for o in (o_gdn, o_mla, o_fox, o_ssd)]
    merged = _merge(hn, branches, w_gate.astype(BF16), w_branch.astype(BF16))
    return _matmul(merged, w_o.astype(BF16), tn=512, residual=h, name="out_proj")


def kernel(x, meta_tokens, mix_norm_g, w_in, gdn_conv_w, gdn_A_log, gdn_dt_bias, gdn_norm_g, mla_qa_g, mla_wq_b, mla_kva_g, mla_wkv_b, mla_qn_g, mla_kn_g, fox_qn_g, fox_kn_g, fox_b_f, ssd_conv_w, ssd_conv_b, ssd_dt_bias, ssd_A_log, ssd_D, ssd_norm_g, w_gate, w_branch, w_o, ffn_norm_g, dense_w_gate, dense_w_up, dense_w_down, router_w, moe_w_gate, moe_w_up, moe_w_down):
    B, S, D = x.shape
    L = N_META + S
    T = -(-L // ROW_TILE) * ROW_TILE
    assert (B * T) % MM_TM == 0
    depth = w_in.shape[0]
    meta = jnp.broadcast_to(meta_tokens[None].astype(x.dtype), (B, N_META, D))
    h = jnp.concatenate([meta, x, jnp.zeros((B, T - L, D), x.dtype)], axis=1).reshape(B * T, D)
    rope = _rope_consts(T)
    for layer in range(depth):
        hn = _rmsnorm(h, mix_norm_g[layer])
        h = _mixer(h, hn, B, T, rope, w_in[layer],
                   gdn_conv_w[layer], gdn_A_log[layer], gdn_dt_bias[layer], gdn_norm_g[layer],
                   mla_qa_g[layer], mla_wq_b[layer], mla_kva_g[layer], mla_wkv_b[layer],
                   mla_qn_g[layer], mla_kn_g[layer],
                   fox_qn_g[layer], fox_kn_g[layer], fox_b_f[layer],
                   ssd_conv_w[layer], ssd_conv_b[layer], ssd_dt_bias[layer], ssd_A_log[layer],
                   ssd_D[layer], ssd_norm_g[layer],
                   w_gate[layer], w_branch[layer], w_o[layer])
        i = layer // 2
        if layer % 2 == 0:
            hn = _rmsnorm(h, ffn_norm_g[layer])
            act = _swiglu_up(hn, dense_w_gate[i].astype(BF16), dense_w_up[i].astype(BF16), tn=512)
            h = _matmul(act, dense_w_down[i].astype(BF16), tn=512, residual=h, name="ffn_down")
        else:
            hn, comb = _rmsnorm_router(h, ffn_norm_g[layer], router_w[i])
            comb_cols = jnp.swapaxes(comb[:, :N_EXPERTS], 0, 1)[:, :, None]
            act = _moe_up(hn, moe_w_gate[i].astype(BF16), moe_w_up[i].astype(BF16), comb_cols)
            E, F, _ = moe_w_down[i].shape
            h = _matmul(act, moe_w_down[i].reshape(E * F, D).astype(BF16), tn=512, tm=256,
                        residual=h, name="moe_down")
    return h.reshape(B, T, D)[:, N_META:L].astype(x.dtype)
```

```python
import functools
import math

import numpy as np
import jax
import jax.numpy as jnp
from jax import lax
from jax.experimental import pallas as pl
from jax.experimental.pallas import tpu as pltpu

F32 = jnp.float32
BF16 = jnp.bfloat16
HI = lax.Precision.HIGHEST
NT_DIMS = (((1,), (1,)), ((), ()))

D_MODEL = 2048
N_META = 16
EPS = 1e-6
NEG = -1e30
CONV_K = 4

GDN_HEADS, GDN_DK, GDN_DV = 4, 128, 128
MLA_HEADS, MLA_Q_RANK, MLA_KV_RANK, MLA_NOPE, MLA_ROPE, MLA_V = 4, 512, 256, 128, 64, 128
MLA_DQK = MLA_NOPE + MLA_ROPE
ROPE_BASE = 10000.0
FOX_HEADS, FOX_DH = 4, 128
SSD_HEADS, SSD_HEADDIM, SSD_GROUPS, SSD_STATE = 8, 64, 2, 128
SSD_HG = SSD_HEADS // SSD_GROUPS
SSD_INNER = SSD_HEADS * SSD_HEADDIM
N_BRANCH, BRANCH_W = 4, 512
N_EXPERTS, TOP_K = 8, 2

LANES = 128
ROW_TILE = 256
FLASH_TILES = (768, 256)
FLASH_HEADS_PER_STEP = 2
LOG2E = 1.4426950408889634
MM_TM = 512
VMEM_LIMIT = 56 * 1024 * 1024

PROJ_W = 6144
COL_GDN_QKV, COL_GDN_Z = 0, 1536
COL_SSD_XBC, COL_SSD_Z = 2048, 3072
COL_MLA_QA, COL_MLA_KVA = 3584, 4096
COL_SMALL = 4352
COL_FOX_QKV = 4608
L_BETA, L_GA, L_FF, L_DT, L_KPE = 0, 4, 8, 12, 20


def _cparams(sem, vmem=VMEM_LIMIT):
    return pltpu.CompilerParams(dimension_semantics=sem, vmem_limit_bytes=vmem)


def _softplus(x):
    return jnp.maximum(x, 0.0) + jnp.log1p(jnp.exp(-jnp.abs(x)))


def _silu(x):
    return x * jax.nn.sigmoid(x)


def _dot(a, b, precision=None):
    return jnp.dot(a, b, preferred_element_type=F32, precision=precision)


def _dot_nt(a, b):
    return lax.dot_general(a, b, NT_DIMS, preferred_element_type=F32)


def _rmsnorm_kernel(h_ref, g_ref, o_ref):
    x = h_ref[...]
    y = x * lax.rsqrt(jnp.mean(x * x, axis=-1, keepdims=True) + EPS) * g_ref[...]
    o_ref[...] = y.astype(o_ref.dtype)


def _rmsnorm(h, g):
    M, D = h.shape
    return pl.pallas_call(
        _rmsnorm_kernel,
        out_shape=jax.ShapeDtypeStruct((M, D), BF16),
        grid=(M // MM_TM,),
        in_specs=[pl.BlockSpec((MM_TM, D), lambda i: (i, 0)),
                  pl.BlockSpec((1, D), lambda i: (0, 0))],
        out_specs=pl.BlockSpec((MM_TM, D), lambda i: (i, 0)),
        compiler_params=_cparams(("parallel",)),
        name="rmsnorm",
    )(h, g.reshape(1, D).astype(F32))


def _rmsnorm_router_kernel(h_ref, g_ref, rw_ref, o_ref, comb_ref):
    x = h_ref[...]
    y = x * lax.rsqrt(jnp.mean(x * x, axis=-1, keepdims=True) + EPS) * g_ref[...]
    o_ref[...] = y.astype(o_ref.dtype)
    logits = _dot(y, rw_ref[...], HI)
    lane = lax.broadcasted_iota(jnp.int32, logits.shape, 1)
    logits = jnp.where(lane < N_EXPERTS, logits, NEG)
    m1 = jnp.max(logits, axis=-1, keepdims=True)
    i1 = jnp.min(jnp.where(logits == m1, lane, LANES), axis=-1, keepdims=True)
    rest = jnp.where(lane == i1, NEG, logits)
    m2 = jnp.max(rest, axis=-1, keepdims=True)
    i2 = jnp.min(jnp.where(rest == m2, lane, LANES), axis=-1, keepdims=True)
    e2 = jnp.exp(m2 - m1)
    p1 = 1.0 / (1.0 + e2)
    p2 = e2 * p1
    comb_ref[...] = jnp.where(lane == i1, p1, 0.0) + jnp.where(lane == i2, p2, 0.0)


def _rmsnorm_router(h, g, router_w):
    M, D = h.shape
    rw = jnp.zeros((D, LANES), F32).at[:, :N_EXPERTS].set(router_w.astype(F32))
    return pl.pallas_call(
        _rmsnorm_router_kernel,
        out_shape=(jax.ShapeDtypeStruct((M, D), BF16), jax.ShapeDtypeStruct((M, LANES), F32)),
        grid=(M // MM_TM,),
        in_specs=[pl.BlockSpec((MM_TM, D), lambda i: (i, 0)),
                  pl.BlockSpec((1, D), lambda i: (0, 0)),
                  pl.BlockSpec((D, LANES), lambda i: (0, 0))],
        out_specs=(pl.BlockSpec((MM_TM, D), lambda i: (i, 0)),
                   pl.BlockSpec((MM_TM, LANES), lambda i: (i, 0))),
        compiler_params=_cparams(("parallel",)),
        name="rmsnorm_router",
    )(h, g.reshape(1, D).astype(F32), rw)


def _resident_bf16(w_ref, w16_ref):
    @pl.when(pl.program_id(1) == 0)
    def _():
        w16_ref[...] = w_ref[...].astype(BF16)
    return w16_ref


def _wspec(shape, index_map, buffers):
    return pl.BlockSpec(shape, index_map, pipeline_mode=pl.Buffered(buffers))


def _mm_kernel(*refs, has_res, cast):
    a_ref, w_ref = refs[:2]
    o_ref = refs[2 + has_res]
    w = _resident_bf16(w_ref, refs[3 + has_res])[...] if cast else w_ref[...]
    y = _dot(a_ref[...], w)
    if has_res:
        y = y + refs[2][...]
    o_ref[...] = y.astype(o_ref.dtype)


def _matmul(a, w, *, tn, tm=MM_TM, residual=None, out_dtype=F32, w_buffers=2, name="matmul"):
    M, K = a.shape
    N = w.shape[1]
    cast = w.dtype != BF16
    in_specs = [pl.BlockSpec((tm, K), lambda j, i: (i, 0)),
                _wspec((K, tn), lambda j, i: (0, j), w_buffers)]
    args = [a, w]
    if residual is not None:
        in_specs.append(pl.BlockSpec((tm, tn), lambda j, i: (i, j)))
        args.append(residual)
    return pl.pallas_call(
        functools.partial(_mm_kernel, has_res=residual is not None, cast=cast),
        out_shape=jax.ShapeDtypeStruct((M, N), out_dtype),
        grid=(N // tn, M // tm),
        in_specs=in_specs,
        out_specs=pl.BlockSpec((tm, tn), lambda j, i: (i, j)),
        scratch_shapes=[pltpu.VMEM((K, tn), BF16)] if cast else [],
        compiler_params=_cparams(("parallel", "arbitrary")),
        name=name,
    )(*args)


def _swiglu_kernel(a_ref, wg_ref, wu_ref, o_ref, wg16_ref, wu16_ref):
    a = a_ref[...]
    g = _dot(a, _resident_bf16(wg_ref, wg16_ref)[...])
    u = _dot(a, _resident_bf16(wu_ref, wu16_ref)[...])
    o_ref[...] = (_silu(g) * u).astype(o_ref.dtype)


def _swiglu_up(a, wg, wu, *, tn, tm=MM_TM):
    M, K = a.shape
    F = wg.shape[1]
    return pl.pallas_call(
        _swiglu_kernel,
        out_shape=jax.ShapeDtypeStruct((M, F), BF16),
        grid=(F // tn, M // tm),
        in_specs=[pl.BlockSpec((tm, K), lambda j, i: (i, 0)),
                  pl.BlockSpec((K, tn), lambda j, i: (0, j)),
                  pl.BlockSpec((K, tn), lambda j, i: (0, j))],
        out_specs=pl.BlockSpec((tm, tn), lambda j, i: (i, j)),
        scratch_shapes=[pltpu.VMEM((K, tn), BF16), pltpu.VMEM((K, tn), BF16)],
        compiler_params=_cparams(("parallel", "arbitrary")),
        name="swiglu_up",
    )(a, wg, wu)


def _moe_up_kernel(a_ref, wg_ref, wu_ref, c_ref, o_ref, wg16_ref, wu16_ref):
    a = a_ref[...]
    g = _dot(a, _resident_bf16(wg_ref, wg16_ref)[0])
    u = _dot(a, _resident_bf16(wu_ref, wu16_ref)[0])
    o_ref[...] = (_silu(g) * u * c_ref[0]).astype(o_ref.dtype)


def _moe_up(a, wg, wu, comb_cols, *, tm=MM_TM):
    M, K = a.shape
    E, _, F = wg.shape
    return pl.pallas_call(
        _moe_up_kernel,
        out_shape=jax.ShapeDtypeStruct((M, E * F), BF16),
        grid=(E, M // tm),
        in_specs=[pl.BlockSpec((tm, K), lambda e, i: (i, 0)),
                  _wspec((1, K, F), lambda e, i: (e, 0, 0), 1),
                  _wspec((1, K, F), lambda e, i: (e, 0, 0), 1),
                  pl.BlockSpec((1, tm, 1), lambda e, i: (e, i, 0))],
        out_specs=pl.BlockSpec((tm, F), lambda e, i: (i, e)),
        scratch_shapes=[pltpu.VMEM((1, K, F), BF16), pltpu.VMEM((1, K, F), BF16)],
        compiler_params=_cparams(("parallel", "arbitrary")),
        name="moe_up",
    )(a, wg, wu, comb_cols)


def _merge_kernel(hn_ref, b0_ref, b1_ref, b2_ref, b3_ref, wg_ref, wb_ref, o_ref, wg16_ref, wb16_ref):
    hn = hn_ref[...]
    wg16 = _resident_bf16(wg_ref, wg16_ref)
    wb16 = _resident_bf16(wb_ref, wb16_ref)
    acc = None
    for b, br_ref in enumerate((b0_ref, b1_ref, b2_ref, b3_ref)):
        gate = jax.nn.sigmoid(_dot(hn, wg16[b]))
        term = gate * _dot(br_ref[...], wb16[b])
        acc = term if acc is None else acc + term
    o_ref[...] = acc.astype(o_ref.dtype)


def _merge(hn, branches, wg, wb, *, tn=512, tm=MM_TM):
    M, D = hn.shape
    N = wg.shape[2]
    bspec = pl.BlockSpec((tm, BRANCH_W), lambda j, i: (i, 0))
    return pl.pallas_call(
        _merge_kernel,
        out_shape=jax.ShapeDtypeStruct((M, N), BF16),
        grid=(N // tn, M // tm),
        in_specs=[pl.BlockSpec((tm, D), lambda j, i: (i, 0)), bspec, bspec, bspec, bspec,
                  _wspec((N_BRANCH, D, tn), lambda j, i: (0, 0, j), 1),
                  _wspec((N_BRANCH, BRANCH_W, tn), lambda j, i: (0, 0, j), 1)],
        out_specs=pl.BlockSpec((tm, tn), lambda j, i: (i, j)),
        scratch_shapes=[pltpu.VMEM((N_BRANCH, D, tn), BF16), pltpu.VMEM((N_BRANCH, BRANCH_W, tn), BF16)],
        compiler_params=_cparams(("parallel", "arbitrary")),
        name="gate_merge",
    )(hn, *branches, wg, wb)


def _causal_conv(x, carry_ref, cw):
    n = x.shape[0]
    xext = jnp.concatenate([carry_ref[...], x], axis=0)
    y = cw[0:1] * xext[5:5 + n]
    for i in range(1, CONV_K):
        y = y + cw[i:i + 1] * xext[5 + i:5 + i + n]
    carry_ref[...] = x[n - 8:n]
    return y


def _tile_masks(n, chunk):
    row = lax.broadcasted_iota(jnp.int32, (n, n), 0)
    col = lax.broadcasted_iota(jnp.int32, (n, n), 1)
    if chunk == n:
        return col <= row, col < row
    in_chunk = col >= (row // chunk) * chunk
    return in_chunk & (col <= row), in_chunk & (col < row)


def _gdn_kernel(qkv_ref, z_ref, sm_ref, cw_ref, alog_ref, dtb_ref, ng_ref, o_ref, s_ref, carry_ref):
    @pl.when(pl.program_id(1) == 0)
    def _():
        s_ref[...] = jnp.zeros_like(s_ref)
        carry_ref[...] = jnp.zeros_like(carry_ref)

    n = ROW_TILE
    y = _silu(_causal_conv(qkv_ref[0], carry_ref, cw_ref[...]))
    sm = sm_ref[0]
    z = z_ref[0]
    beta_all = jax.nn.sigmoid(sm)
    g_all = -jnp.exp(alog_ref[...]) * _softplus(sm + dtb_ref[...])
    row = lax.broadcasted_iota(jnp.int32, (n, n), 0)
    col = lax.broadcasted_iota(jnp.int32, (n, n), 1)
    causal, strict = col <= row, col < row
    diff_bits = row ^ col
    levels = int(math.log2(n))
    level_masks = [(diff_bits >= (1 << l)) & (diff_bits < (2 << l)) for l in range(levels)]
    eye = jnp.where(row == col, 1.0, 0.0)
    gcs_all = _dot(causal.astype(F32), g_all, HI)
    gcs_t = gcs_all.T
    outs = []
    for h in range(GDN_HEADS):
        q = y[:, h * GDN_DK:(h + 1) * GDN_DK]
        k = y[:, GDN_HEADS * GDN_DK + h * GDN_DK:GDN_HEADS * GDN_DK + (h + 1) * GDN_DK]
        v = y[:, 2 * GDN_HEADS * GDN_DK + h * GDN_DV:2 * GDN_HEADS * GDN_DK + (h + 1) * GDN_DV]
        q = q * lax.rsqrt(jnp.sum(q * q, axis=-1, keepdims=True) + EPS) * GDN_DK ** -0.5
        k = k * lax.rsqrt(jnp.sum(k * k, axis=-1, keepdims=True) + EPS)
        beta = beta_all[:, L_BETA + h:L_BETA + h + 1]
        gc = gcs_all[:, L_GA + h:L_GA + h + 1]
        gr = gcs_t[L_GA + h:L_GA + h + 1, :]
        decay = jnp.exp(jnp.where(causal, gc - gr, NEG))
        kb = k * beta
        k16 = k.astype(BF16)
        a = jnp.where(strict, _dot_nt(kb.astype(BF16), k16) * decay, 0.0)
        t = eye - jnp.where(level_masks[0], a, 0.0)
        for l in range(1, levels):
            t16 = t.astype(BF16)
            a_l = jnp.where(level_masks[l], a, 0.0).astype(BF16)
            t = t - _dot(_dot(t16, a_l).astype(BF16), t16)
        egc = jnp.exp(gc)
        rhs = jnp.concatenate([v * beta, kb * egc], axis=1)
        uw = _dot(t.astype(BF16), rhs.astype(BF16))
        u, w = uw[:, :GDN_DV], uw[:, GDN_DV:]
        att = _dot_nt(q.astype(BF16), k16) * decay
        g_last = gc[n - 1:n, :]
        kd = k * jnp.exp(g_last - gc)
        s = s_ref[h]
        ws = _dot(jnp.concatenate([w, q * egc], axis=0).astype(BF16), s.astype(BF16))
        v_new = (u - ws[:n]).astype(BF16)
        o = ws[n:] + _dot(att.astype(BF16), v_new)
        s_ref[h] = s * jnp.exp(g_last) + _dot(kd.T.astype(BF16), v_new)
        o = o * lax.rsqrt(jnp.mean(o * o, axis=-1, keepdims=True) + EPS) * ng_ref[...]
        outs.append(o * _silu(z[:, h * GDN_DV:(h + 1) * GDN_DV]))
    o_ref[0] = jnp.concatenate(outs, axis=1).astype(o_ref.dtype)


def _lane_vec(vals, lane0):
    v = jnp.zeros((1, LANES), F32)
    return v.at[0, lane0:lane0 + vals.shape[0]].set(vals.astype(F32))


def _gdn(proj, conv_w, a_log, dt_bias, norm_g):
    B, T, _ = proj.shape
    n = ROW_TILE
    W = 2 * GDN_HEADS * GDN_DK + GDN_HEADS * GDN_DV
    ZW = GDN_HEADS * GDN_DV
    return pl.pallas_call(
        _gdn_kernel,
        out_shape=jax.ShapeDtypeStruct((B, T, ZW), BF16),
        grid=(B, T // n),
        in_specs=[pl.BlockSpec((1, n, W), lambda b, t: (b, t, COL_GDN_QKV // W)),
                  pl.BlockSpec((1, n, ZW), lambda b, t: (b, t, COL_GDN_Z // ZW)),
                  pl.BlockSpec((1, n, LANES), lambda b, t: (b, t, COL_SMALL // LANES)),
                  pl.BlockSpec((CONV_K, W), lambda b, t: (0, 0)),
                  pl.BlockSpec((1, LANES), lambda b, t: (0, 0)),
                  pl.BlockSpec((1, LANES), lambda b, t: (0, 0)),
                  pl.BlockSpec((1, GDN_DV), lambda b, t: (0, 0))],
        out_specs=pl.BlockSpec((1, n, ZW), lambda b, t: (b, t, 0)),
        scratch_shapes=[pltpu.VMEM((GDN_HEADS, GDN_DK, GDN_DV), F32),
                        pltpu.VMEM((8, W), F32)],
        compiler_params=_cparams(("arbitrary", "arbitrary")),
        name="gdn",
    )(proj, proj, proj, conv_w.astype(F32), _lane_vec(a_log, L_GA), _lane_vec(dt_bias, L_GA),
      norm_g.reshape(1, GDN_DV).astype(F32))


def _ssd_kernel(xbc_ref, z_ref, sm_ref, cw_ref, cb_ref, alog_ref, dtb_ref, dvec_ref, ng_ref,
                o_ref, hs_ref, carry_ref):
    @pl.when(pl.program_id(1) == 0)
    def _():
        hs_ref[...] = jnp.zeros_like(hs_ref)
        carry_ref[...] = jnp.zeros_like(carry_ref)

    n, P, N = ROW_TILE, SSD_HEADDIM, SSD_STATE
    y = _silu(_causal_conv(xbc_ref[0], carry_ref, cw_ref[...]) + cb_ref[...])
    xs = y[:, :SSD_INNER]
    bm = y[:, SSD_INNER:SSD_INNER + SSD_GROUPS * N]
    cm = y[:, SSD_INNER + SSD_GROUPS * N:]
    sm = sm_ref[0]
    dt_all = _softplus(sm + dtb_ref[...])
    a_all = dt_all * (-jnp.exp(alog_ref[...]))
    causal, _ = _tile_masks(n, n)
    acs_all = _dot(causal.astype(F32), a_all, HI)
    acs_t = acs_all.T
    ys = []
    for g in range(SSD_GROUPS):
        bg = bm[:, g * N:(g + 1) * N]
        cg16 = cm[:, g * N:(g + 1) * N].astype(BF16)
        cb = _dot_nt(cg16, bg.astype(BF16))
        bgt16 = bg.T.astype(BF16)
        for j in range(SSD_HG):
            hh = g * SSD_HG + j
            ac = acs_all[:, L_DT + hh:L_DT + hh + 1]
            ar = acs_t[L_DT + hh:L_DT + hh + 1, :]
            lmat = jnp.exp(jnp.where(causal, ac - ar, NEG))
            xdt = xs[:, hh * P:(hh + 1) * P] * dt_all[:, L_DT + hh:L_DT + hh + 1]
            y_diag = _dot((cb * lmat).astype(BF16), xdt.astype(BF16))
            a_last = ac[n - 1:n, :]
            st = _dot(bgt16, (xdt * jnp.exp(a_last - ac)).astype(BF16))
            h_prev = hs_ref[hh]
            y_off = _dot(cg16, h_prev.astype(BF16)) * jnp.exp(ac)
            hs_ref[hh] = h_prev * jnp.exp(a_last) + st
            ys.append(y_diag + y_off)
    yy = jnp.concatenate(ys, axis=1) + xs * dvec_ref[...]
    yy = yy * _silu(z_ref[0])
    gw = SSD_HG * P
    outs = []
    for g in range(SSD_GROUPS):
        seg = yy[:, g * gw:(g + 1) * gw]
        outs.append(seg * lax.rsqrt(jnp.mean(seg * seg, axis=-1, keepdims=True) + EPS)
                    * ng_ref[:, g * gw:(g + 1) * gw])
    o_ref[0] = jnp.concatenate(outs, axis=1).astype(o_ref.dtype)


def _ssd(proj, conv_w, conv_b, dt_bias, a_log, d_skip, norm_g):
    B, T, _ = proj.shape
    n = ROW_TILE
    W = SSD_INNER + 2 * SSD_GROUPS * SSD_STATE
    dvec = jnp.repeat(d_skip.astype(F32), SSD_HEADDIM).reshape(1, SSD_INNER)
    return pl.pallas_call(
        _ssd_kernel,
        out_shape=jax.ShapeDtypeStruct((B, T, SSD_INNER), BF16),
        grid=(B, T // n),
        in_specs=[pl.BlockSpec((1, n, W), lambda b, t: (b, t, COL_SSD_XBC // W)),
                  pl.BlockSpec((1, n, SSD_INNER), lambda b, t: (b, t, COL_SSD_Z // SSD_INNER)),
                  pl.BlockSpec((1, n, LANES), lambda b, t: (b, t, COL_SMALL // LANES)),
                  pl.BlockSpec((CONV_K, W), lambda b, t: (0, 0)),
                  pl.BlockSpec((1, W), lambda b, t: (0, 0)),
                  pl.BlockSpec((1, LANES), lambda b, t: (0, 0)),
                  pl.BlockSpec((1, LANES), lambda b, t: (0, 0)),
                  pl.BlockSpec((1, SSD_INNER), lambda b, t: (0, 0)),
                  pl.BlockSpec((1, SSD_INNER), lambda b, t: (0, 0))],
        out_specs=pl.BlockSpec((1, n, SSD_INNER), lambda b, t: (b, t, 0)),
        scratch_shapes=[pltpu.VMEM((SSD_HEADS, SSD_STATE, SSD_HEADDIM), F32),
                        pltpu.VMEM((8, W), F32)],
        compiler_params=_cparams(("arbitrary", "arbitrary")),
        name="ssd",
    )(proj, proj, proj, conv_w.astype(F32), conv_b.reshape(1, W).astype(F32),
      _lane_vec(a_log, L_DT), _lane_vec(dt_bias, L_DT), dvec,
      norm_g.reshape(1, SSD_INNER).astype(F32))


def _ones_lane0(n):
    lane = lax.broadcasted_iota(jnp.int32, (n, LANES), 1)
    return jnp.where(lane == 0, 1.0, 0.0)


def _fox_prep_kernel(qkv_ref, sm_ref, qg_ref, kg_ref, bf_ref, q_ref, k_ref, v_ref, run_ref):
    @pl.when(pl.program_id(1) == 0)
    def _():
        run_ref[...] = jnp.zeros_like(run_ref)

    n, dh, H = ROW_TILE, FOX_DH, FOX_HEADS
    x = qkv_ref[0]
    log_f = -_softplus(-(sm_ref[0] + bf_ref[...]))
    causal, _ = _tile_masks(n, n)
    cum = _dot(causal.astype(F32), log_f, HI) + run_ref[0:1, :]
    run_ref[...] = jnp.broadcast_to(cum[n - 1:n, :], run_ref.shape)
    lane = lax.broadcasted_iota(jnp.int32, (n, LANES), 1)
    ones = _ones_lane0(n)
    q_bias = jnp.where(lane < 3, 1.0, 0.0)
    qs, ks, vs = [], [], []
    for h in range(H):
        q = x[:, h * dh:(h + 1) * dh]
        k = x[:, H * dh + h * dh:H * dh + (h + 1) * dh]
        qs.append(q * lax.rsqrt(jnp.mean(q * q, axis=-1, keepdims=True) + EPS)
                  * (qg_ref[...] * (dh ** -0.5 * LOG2E)))
        qs.append(q_bias)
        ks.append(k * lax.rsqrt(jnp.mean(k * k, axis=-1, keepdims=True) + EPS) * kg_ref[...])
        c = cum[:, L_FF + h:L_FF + h + 1] * (-LOG2E)
        c_hi = c.astype(BF16).astype(F32)
        c_mid = (c - c_hi).astype(BF16).astype(F32)
        c_lo = c - c_hi - c_mid
        ks.append(jnp.where(lane == 0, c_hi, jnp.where(lane == 1, c_mid, jnp.where(lane == 2, c_lo, 0.0))))
        vs.append(x[:, 2 * H * dh + h * dh:2 * H * dh + (h + 1) * dh])
        vs.append(ones)
    q_ref[0] = jnp.concatenate(qs, axis=1).astype(q_ref.dtype)
    k_ref[0] = jnp.concatenate(ks, axis=1).astype(k_ref.dtype)
    v_ref[0] = jnp.concatenate(vs, axis=1).astype(v_ref.dtype)


def _fox_prep(proj, qn_g, kn_g, b_f):
    B, T, _ = proj.shape
    n = ROW_TILE
    W = 3 * FOX_HEADS * FOX_DH
    HW = FOX_HEADS * 2 * LANES
    ospec = pl.BlockSpec((1, n, HW), lambda b, t: (b, t, 0))
    return pl.pallas_call(
        _fox_prep_kernel,
        out_shape=(jax.ShapeDtypeStruct((B, T, HW), BF16),) * 3,
        grid=(B, T // n),
        in_specs=[pl.BlockSpec((1, n, W), lambda b, t: (b, t, COL_FOX_QKV // W)),
                  pl.BlockSpec((1, n, LANES), lambda b, t: (b, t, COL_SMALL // LANES)),
                  pl.BlockSpec((1, FOX_DH), lambda b, t: (0, 0)),
                  pl.BlockSpec((1, FOX_DH), lambda b, t: (0, 0)),
                  pl.BlockSpec((1, LANES), lambda b, t: (0, 0))],
        out_specs=(ospec, ospec, ospec),
        scratch_shapes=[pltpu.VMEM((8, LANES), F32)],
        compiler_params=_cparams(("arbitrary", "arbitrary")),
        name="fox_prep",
    )(proj, proj, qn_g.reshape(1, FOX_DH).astype(F32), kn_g.reshape(1, FOX_DH).astype(F32),
      _lane_vec(b_f, L_FF))


def _mla_prep_kernel(qa_ref, kva_ref, sm_ref, cos_ref, sin_ref, qag_ref, wq_ref, kvag_ref, wkv_ref,
                     qgn_ref, qgr_ref, kgn_ref, kgr_ref, rot_ref, exp_ref, q_ref, k_ref, v_ref):
    n, H, dn, dr = ROW_TILE, MLA_HEADS, MLA_NOPE, MLA_ROPE
    qa = qa_ref[0]
    qa = qa * lax.rsqrt(jnp.mean(qa * qa, axis=-1, keepdims=True) + EPS) * qag_ref[...]
    qq = _dot(qa.astype(BF16), wq_ref[...])
    kva = kva_ref[0]
    kva = kva * lax.rsqrt(jnp.mean(kva * kva, axis=-1, keepdims=True) + EPS) * kvag_ref[...]
    kv = _dot(kva.astype(BF16), wkv_ref[...])
    sm = sm_ref[0]
    lane = lax.broadcasted_iota(jnp.int32, (n, LANES), 1)
    is_kpe = (lane >= L_KPE) & (lane < L_KPE + dr)
    kpe_ss = jnp.sum(jnp.where(is_kpe, sm * sm, 0.0), axis=-1, keepdims=True)
    kpe4 = _dot(sm, exp_ref[...], HI)
    qr = qq[:, H * dn:]
    grp = lax.broadcasted_iota(jnp.int32, (n, H * dr), 1) // dr
    q_rs, k_rs = [], []
    for h in range(H):
        qn = qq[:, h * dn:(h + 1) * dn]
        ssr = jnp.sum(jnp.where(grp == h, qr * qr, 0.0), axis=-1, keepdims=True)
        q_rs.append(lax.rsqrt((jnp.sum(qn * qn, axis=-1, keepdims=True) + ssr) / MLA_DQK + EPS))
        kn = kv[:, h * dn:(h + 1) * dn]
        k_rs.append(lax.rsqrt((jnp.sum(kn * kn, axis=-1, keepdims=True) + kpe_ss) / MLA_DQK + EPS))

    def per_group(vals):
        out = vals[H - 1]
        for h in range(H - 2, -1, -1):
            out = jnp.where(grp == h, vals[h], out)
        return out

    cos, sin, rot = cos_ref[...], sin_ref[...], rot_ref[...]
    tq = qr * per_group(q_rs) * qgr_ref[...]
    tq = tq * cos + _dot(tq, rot, HI) * sin
    tk = kpe4 * per_group(k_rs) * kgr_ref[...]
    tk = tk * cos + _dot(tk, rot, HI) * sin
    scale = MLA_DQK ** -0.5 * LOG2E
    half_id = lane // dr
    ones = _ones_lane0(n)
    q_parts, k_parts, v_parts = [], [], []
    for h in range(H):
        blk = slice((h // 2) * LANES, (h // 2 + 1) * LANES)
        q_parts.append(qq[:, h * dn:(h + 1) * dn] * q_rs[h] * (qgn_ref[...] * scale))
        q_parts.append(jnp.where(half_id == h % 2, tq[:, blk] * scale, 0.0))
        k_parts.append(kv[:, h * dn:(h + 1) * dn] * k_rs[h] * kgn_ref[...])
        k_parts.append(tk[:, blk])
        v_parts.append(kv[:, H * dn + h * MLA_V:H * dn + (h + 1) * MLA_V])
        v_parts.append(ones)
    q_ref[0] = jnp.concatenate(q_parts, axis=1).astype(q_ref.dtype)
    k_ref[0] = jnp.concatenate(k_parts, axis=1).astype(k_ref.dtype)
    v_ref[0] = jnp.concatenate(v_parts, axis=1).astype(v_ref.dtype)


def _rope_consts(T):
    H, dr = MLA_HEADS, MLA_ROPE
    inv = 1.0 / (ROPE_BASE ** (jnp.arange(0, dr, 2, dtype=F32) / dr))
    ang = jnp.arange(T, dtype=F32)[:, None] * inv[None, :]
    ang = jnp.concatenate([ang, ang], axis=-1)
    cos4 = jnp.tile(jnp.cos(ang), (1, H))
    sin4 = jnp.tile(jnp.sin(ang), (1, H))
    rot = np.zeros((H * dr, H * dr), np.float32)
    for h in range(H):
        for c in range(dr // 2):
            rot[h * dr + c + dr // 2, h * dr + c] = -1.0
            rot[h * dr + c, h * dr + c + dr // 2] = 1.0
    expand = np.zeros((LANES, H * dr), np.float32)
    for h in range(H):
        for c in range(dr):
            expand[L_KPE + c, h * dr + c] = 1.0
    return cos4, sin4, jnp.asarray(rot), jnp.asarray(expand)


def _mla_prep(proj, rope, qa_g, wq_b, kva_g, wkv_b, qn_g, kn_g):
    B, T, _ = proj.shape
    n, H, dn, dr, dv = ROW_TILE, MLA_HEADS, MLA_NOPE, MLA_ROPE, MLA_V
    cos4, sin4, rot, expand = rope
    wq = wq_b.reshape(MLA_Q_RANK, H, MLA_DQK)
    wq = jnp.concatenate([wq[:, :, :dn].reshape(MLA_Q_RANK, H * dn),
                          wq[:, :, dn:].reshape(MLA_Q_RANK, H * dr)], axis=1).astype(BF16)
    wkv = wkv_b.reshape(MLA_KV_RANK, H, dn + dv)
    wkv = jnp.concatenate([wkv[:, :, :dn].reshape(MLA_KV_RANK, H * dn),
                           wkv[:, :, dn:].reshape(MLA_KV_RANK, H * dv)], axis=1).astype(BF16)
    qg, kg = qn_g.astype(F32), kn_g.astype(F32)
    const = lambda shape: pl.BlockSpec(shape, lambda b, t: (0,) * len(shape))
    QW = H * 2 * LANES
    return pl.pallas_call(
        _mla_prep_kernel,
        out_shape=(jax.ShapeDtypeStruct((B, T, QW), BF16), jax.ShapeDtypeStruct((B, T, QW), BF16),
                   jax.ShapeDtypeStruct((B, T, QW), BF16)),
        grid=(B, T // n),
        in_specs=[pl.BlockSpec((1, n, MLA_Q_RANK), lambda b, t: (b, t, COL_MLA_QA // MLA_Q_RANK)),
                  pl.BlockSpec((1, n, MLA_KV_RANK), lambda b, t: (b, t, COL_MLA_KVA // MLA_KV_RANK)),
                  pl.BlockSpec((1, n, LANES), lambda b, t: (b, t, COL_SMALL // LANES)),
                  pl.BlockSpec((n, H * dr), lambda b, t: (t, 0)),
                  pl.BlockSpec((n, H * dr), lambda b, t: (t, 0)),
                  const((1, MLA_Q_RANK)), const(wq.shape), const((1, MLA_KV_RANK)), const(wkv.shape),
                  const((1, dn)), const((1, H * dr)), const((1, dn)), const((1, H * dr)),
                  const(rot.shape), const(expand.shape)],
        out_specs=(pl.BlockSpec((1, n, QW), lambda b, t: (b, t, 0)),
                   pl.BlockSpec((1, n, QW), lambda b, t: (b, t, 0)),
                   pl.BlockSpec((1, n, QW), lambda b, t: (b, t, 0))),
        compiler_params=_cparams(("parallel", "parallel")),
        name="mla_prep",
    )(proj, proj, proj, cos4, sin4,
      qa_g.reshape(1, -1).astype(F32), wq, kva_g.reshape(1, -1).astype(F32), wkv,
      qg[:dn].reshape(1, dn), jnp.tile(qg[dn:], H).reshape(1, H * dr),
      kg[:dn].reshape(1, dn), jnp.tile(kg[dn:], H).reshape(1, H * dr), rot, expand)


def _flash_kernel(q_ref, k_ref, v_ref, o_ref, m_ref, acc_ref, *, tq):
    qi = pl.program_id(2)
    nh = m_ref.shape[0]
    dqk, dv2 = q_ref.shape[-1] // nh, v_ref.shape[-1] // nh
    dv = dv2 // 2
    rb = ROW_TILE
    m_ref[...] = jnp.full(m_ref.shape, NEG, F32)
    acc_ref[...] = jnp.zeros(acc_ref.shape, F32)

    def step(j, masked):
        start = pl.multiple_of(j * tq, tq)
        for h in range(nh):
            for r in range(0, tq, rb):
                s = _dot_nt(q_ref[0, r:r + rb, h * dqk:(h + 1) * dqk],
                            k_ref[0, pl.ds(start, tq), h * dqk:(h + 1) * dqk])
                if masked:
                    row = lax.broadcasted_iota(jnp.int32, s.shape, 0) + r
                    col = lax.broadcasted_iota(jnp.int32, s.shape, 1)
                    s = jnp.where(col <= row, s, NEG)
                lane_max = s[:, :LANES]
                for c in range(1, tq // LANES):
                    lane_max = jnp.maximum(lane_max, s[:, c * LANES:(c + 1) * LANES])
                m_old = m_ref[h, r:r + rb]
                m_new = jnp.maximum(m_old, jnp.max(lane_max, axis=-1, keepdims=True))
                m_ref[h, r:r + rb] = m_new
                p = jnp.exp2(s - m_new).astype(BF16)
                acc_ref[h, r:r + rb] = (jnp.exp2(m_old - m_new) * acc_ref[h, r:r + rb]
                                        + _dot(p, v_ref[0, pl.ds(start, tq), h * dv2:(h + 1) * dv2]))

    def body(jj, carry):
        step(2 * jj, False)
        step(2 * jj + 1, False)
        return carry

    lax.fori_loop(0, qi // 2, body, 0)

    @pl.when(qi % 2 == 1)
    def _():
        step(qi - 1, False)

    step(qi, True)
    outs = []
    for h in range(nh):
        acc = acc_ref[h]
        outs.append(acc[:, :dv] / acc[:, dv:dv + 1])
    o_ref[0] = jnp.concatenate(outs, axis=1).astype(o_ref.dtype)


def _flash(q, k, v):
    B, T, QW = q.shape
    H, nh = MLA_HEADS, FLASH_HEADS_PER_STEP
    dqk, dv2 = QW // H, v.shape[-1] // H
    dv = dv2 // 2
    tq = next(t for t in FLASH_TILES if T % t == 0)
    return pl.pallas_call(
        functools.partial(_flash_kernel, tq=tq),
        out_shape=jax.ShapeDtypeStruct((B, T, H * dv), BF16),
        grid=(B, H // nh, T // tq),
        in_specs=[pl.BlockSpec((1, tq, nh * dqk), lambda b, h, i: (b, i, h)),
                  pl.BlockSpec((1, T, nh * dqk), lambda b, h, i: (b, 0, h)),
                  pl.BlockSpec((1, T, nh * dv2), lambda b, h, i: (b, 0, h))],
        out_specs=pl.BlockSpec((1, tq, nh * dv), lambda b, h, i: (b, i, h)),
        scratch_shapes=[pltpu.VMEM((nh, tq, 1), F32), pltpu.VMEM((nh, tq, dv2), F32)],
        compiler_params=_cparams(("parallel", "parallel", "arbitrary")),
        name="flash",
    )(q, k, v)


def _reorder_w_in(w_in):
    D = w_in.shape[0]
    w = w_in.astype(BF16)
    z = lambda n: jnp.zeros((D, n), BF16)
    small = [w[:, 2048:2052], w[:, 2052:2056], w[:, 4424:4428], w[:, 5964:5972], w[:, 2824:2888]]
    n_small = sum(s.shape[1] for s in small)
    cols = [w[:, 0:2048],
            w[:, 4940:5964],
            w[:, 4428:4940],
            w[:, 2056:2824],
            *small, z(LANES - n_small), z(COL_FOX_QKV - COL_SMALL - LANES),
            w[:, 2888:4424]]
    out = jnp.concatenate(cols, axis=1)
    assert out.shape[1] == PROJ_W
    return out


def _mixer(h, hn, B, T, rope, w_in, gdn_conv_w, gdn_A_log, gdn_dt_bias, gdn_norm_g,
           mla_qa_g, mla_wq_b, mla_kva_g, mla_wkv_b, mla_qn_g, mla_kn_g,
           fox_qn_g, fox_kn_g, fox_b_f,
           ssd_conv_w, ssd_conv_b, ssd_dt_bias, ssd_A_log, ssd_D, ssd_norm_g,
           w_gate, w_branch, w_o):
    M = B * T
    proj = _matmul(hn, _reorder_w_in(w_in), tn=1024, name="in_proj").reshape(B, T, PROJ_W)
    o_gdn = _gdn(proj, gdn_conv_w, gdn_A_log, gdn_dt_bias, gdn_norm_g)
    mq, mk, mv = _mla_prep(proj, rope, mla_qa_g, mla_wq_b, mla_kva_g, mla_wkv_b, mla_qn_g, mla_kn_g)
    o_mla = _flash(mq, mk, mv)
    o_fox = _flash(*_fox_prep(proj, fox_qn_g, fox_kn_g, fox_b_f))
    o_ssd = _ssd(proj, ssd_conv_w, ssd_conv_b, ssd_dt_bias, ssd_A_log, ssd_D, ssd_norm_g)
    branches = [o.reshape(M, BRANCH_W) for o in (o_gdn, o_mla, o_fox, o_ssd)]
    merged = _merge(hn, branches, w_gate, w_branch)
    return _matmul(merged, w_o, tn=w_o.shape[1], residual=h, w_buffers=1, name="out_proj")


def kernel(x, meta_tokens, mix_norm_g, w_in, gdn_conv_w, gdn_A_log, gdn_dt_bias, gdn_norm_g, mla_qa_g, mla_wq_b, mla_kva_g, mla_wkv_b, mla_qn_g, mla_kn_g, fox_qn_g, fox_kn_g, fox_b_f, ssd_conv_w, ssd_conv_b, ssd_dt_bias, ssd_A_log, ssd_D, ssd_norm_g, w_gate, w_branch, w_o, ffn_norm_g, dense_w_gate, dense_w_up, dense_w_down, router_w, moe_w_gate, moe_w_up, moe_w_down):
    B, S, D = x.shape
    L = N_META + S
    T = -(-L // ROW_TILE) * ROW_TILE
    assert (B * T) % MM_TM == 0
    depth = w_in.shape[0]
    meta = jnp.broadcast_to(meta_tokens[None].astype(x.dtype), (B, N_META, D))
    h = jnp.concatenate([meta, x, jnp.zeros((B, T - L, D), x.dtype)], axis=1).reshape(B * T, D)
    rope = _rope_consts(T)
    for layer in range(depth):
        hn = _rmsnorm(h, mix_norm_g[layer])
        h = _mixer(h, hn, B, T, rope, w_in[layer],
                   gdn_conv_w[layer], gdn_A_log[layer], gdn_dt_bias[layer], gdn_norm_g[layer],
                   mla_qa_g[layer], mla_wq_b[layer], mla_kva_g[layer], mla_wkv_b[layer],
                   mla_qn_g[layer], mla_kn_g[layer],
                   fox_qn_g[layer], fox_kn_g[layer], fox_b_f[layer],
                   ssd_conv_w[layer], ssd_conv_b[layer], ssd_dt_bias[layer], ssd_A_log[layer],
                   ssd_D[layer], ssd_norm_g[layer],
                   w_gate[layer], w_branch[layer], w_o[layer])
        i = layer // 2
        if layer % 2 == 0:
            hn = _rmsnorm(h, ffn_norm_g[layer])
            act = _swiglu_up(hn, dense_w_gate[i], dense_w_up[i], tn=512)
            h = _matmul(act, dense_w_down[i], tn=1024, tm=256, residual=h, w_buffers=1, name="ffn_down")
        else:
            hn, comb = _rmsnorm_router(h, ffn_norm_g[layer], router_w[i])
            comb_cols = jnp.swapaxes(comb[:, :N_EXPERTS], 0, 1)[:, :, None]
            act = _moe_up(hn, moe_w_gate[i], moe_w_up[i], comb_cols)
            E, F, _ = moe_w_down[i].shape
            h = _matmul(act, moe_w_down[i].reshape(E * F, D), tn=512, tm=256,
                        residual=h, w_buffers=1, name="moe_down")
    return h.reshape(B, T, D)[:, N_META:L].astype(x.dtype)
```

```python
import functools
import math

import numpy as np
import jax
import jax.numpy as jnp
from jax import lax
from jax.experimental import pallas as pl
from jax.experimental.pallas import tpu as pltpu

F32 = jnp.float32
BF16 = jnp.bfloat16
HI = lax.Precision.HIGHEST
NT_DIMS = (((1,), (1,)), ((), ()))

D_MODEL = 2048
N_META = 16
EPS = 1e-6
NEG = -1e30
CONV_K = 4

GDN_HEADS, GDN_DK, GDN_DV = 4, 128, 128
MLA_HEADS, MLA_Q_RANK, MLA_KV_RANK, MLA_NOPE, MLA_ROPE, MLA_V = 4, 512, 256, 128, 64, 128
MLA_DQK = MLA_NOPE + MLA_ROPE
ROPE_BASE = 10000.0
FOX_HEADS, FOX_DH = 4, 128
SSD_HEADS, SSD_HEADDIM, SSD_GROUPS, SSD_STATE = 8, 64, 2, 128
SSD_HG = SSD_HEADS // SSD_GROUPS
SSD_INNER = SSD_HEADS * SSD_HEADDIM
N_BRANCH, BRANCH_W = 4, 512
N_EXPERTS, TOP_K = 8, 2

LANES = 128
ROW_TILE = 256
FLASH_TILES = (768, 256)
FLASH_HEADS_PER_STEP = 2
LOG2E = 1.4426950408889634
MM_TM = 512
VMEM_LIMIT = 56 * 1024 * 1024

PROJ_W = 6144
COL_GDN_QKV, COL_GDN_Z = 0, 1536
COL_SSD_XBC, COL_SSD_Z = 2048, 3072
COL_MLA_QA, COL_MLA_KVA = 3584, 4096
COL_SMALL = 4352
COL_FOX_QKV = 4608
L_BETA, L_GA, L_FF, L_DT, L_KPE = 0, 4, 8, 12, 20


def _cparams(sem, vmem=VMEM_LIMIT):
    return pltpu.CompilerParams(dimension_semantics=sem, vmem_limit_bytes=vmem)


def _softplus(x):
    return jnp.maximum(x, 0.0) + jnp.log1p(jnp.exp(-jnp.abs(x)))


def _silu(x):
    return x * jax.nn.sigmoid(x)


def _dot(a, b, precision=None):
    return jnp.dot(a, b, preferred_element_type=F32, precision=precision)


def _dot_nt(a, b):
    return lax.dot_general(a, b, NT_DIMS, preferred_element_type=F32)


def _rmsnorm_kernel(h_ref, g_ref, o_ref):
    x = h_ref[...]
    y = x * lax.rsqrt(jnp.mean(x * x, axis=-1, keepdims=True) + EPS) * g_ref[...]
    o_ref[...] = y.astype(o_ref.dtype)


def _rmsnorm(h, g):
    M, D = h.shape
    return pl.pallas_call(
        _rmsnorm_kernel,
        out_shape=jax.ShapeDtypeStruct((M, D), BF16),
        grid=(M // MM_TM,),
        in_specs=[pl.BlockSpec((MM_TM, D), lambda i: (i, 0)),
                  pl.BlockSpec((1, D), lambda i: (0, 0))],
        out_specs=pl.BlockSpec((MM_TM, D), lambda i: (i, 0)),
        compiler_params=_cparams(("parallel",)),
        name="rmsnorm",
    )(h, g.reshape(1, D).astype(F32))


def _rmsnorm_router_kernel(h_ref, g_ref, rw_ref, o_ref, comb_ref):
    x = h_ref[...]
    y = x * lax.rsqrt(jnp.mean(x * x, axis=-1, keepdims=True) + EPS) * g_ref[...]
    o_ref[...] = y.astype(o_ref.dtype)
    logits = _dot(y, rw_ref[...], HI)
    lane = lax.broadcasted_iota(jnp.int32, logits.shape, 1)
    logits = jnp.where(lane < N_EXPERTS, logits, NEG)
    m1 = jnp.max(logits, axis=-1, keepdims=True)
    i1 = jnp.min(jnp.where(logits == m1, lane, LANES), axis=-1, keepdims=True)
    rest = jnp.where(lane == i1, NEG, logits)
    m2 = jnp.max(rest, axis=-1, keepdims=True)
    i2 = jnp.min(jnp.where(rest == m2, lane, LANES), axis=-1, keepdims=True)
    e2 = jnp.exp(m2 - m1)
    p1 = 1.0 / (1.0 + e2)
    p2 = e2 * p1
    comb_ref[...] = jnp.where(lane == i1, p1, 0.0) + jnp.where(lane == i2, p2, 0.0)


def _rmsnorm_router(h, g, router_w):
    M, D = h.shape
    rw = jnp.zeros((D, LANES), F32).at[:, :N_EXPERTS].set(router_w.astype(F32))
    return pl.pallas_call(
        _rmsnorm_router_kernel,
        out_shape=(jax.ShapeDtypeStruct((M, D), BF16), jax.ShapeDtypeStruct((M, LANES), F32)),
        grid=(M // MM_TM,),
        in_specs=[pl.BlockSpec((MM_TM, D), lambda i: (i, 0)),
                  pl.BlockSpec((1, D), lambda i: (0, 0)),
                  pl.BlockSpec((D, LANES), lambda i: (0, 0))],
        out_specs=(pl.BlockSpec((MM_TM, D), lambda i: (i, 0)),
                   pl.BlockSpec((MM_TM, LANES), lambda i: (i, 0))),
        compiler_params=_cparams(("parallel",)),
        name="rmsnorm_router",
    )(h, g.reshape(1, D).astype(F32), rw)


def _mm_kernel(*refs, has_res):
    a_ref, w_ref = refs[:2]
    y = _dot(a_ref[...], w_ref[...])
    if has_res:
        y = y + refs[2][...]
    refs[-1][...] = y.astype(refs[-1].dtype)


def _matmul(a, w, *, tn, tm=MM_TM, residual=None, out_dtype=F32, name="matmul"):
    M, K = a.shape
    N = w.shape[1]
    in_specs = [pl.BlockSpec((tm, K), lambda j, i: (i, 0)),
                pl.BlockSpec((K, tn), lambda j, i: (0, j))]
    args = [a, w]
    if residual is not None:
        in_specs.append(pl.BlockSpec((tm, tn), lambda j, i: (i, j)))
        args.append(residual)
    return pl.pallas_call(
        functools.partial(_mm_kernel, has_res=residual is not None),
        out_shape=jax.ShapeDtypeStruct((M, N), out_dtype),
        grid=(N // tn, M // tm),
        in_specs=in_specs,
        out_specs=pl.BlockSpec((tm, tn), lambda j, i: (i, j)),
        compiler_params=_cparams(("parallel", "parallel")),
        name=name,
    )(*args)


def _swiglu_kernel(a_ref, wg_ref, wu_ref, o_ref):
    a = a_ref[...]
    g = _dot(a, wg_ref[...])
    u = _dot(a, wu_ref[...])
    o_ref[...] = (_silu(g) * u).astype(o_ref.dtype)


def _swiglu_up(a, wg, wu, *, tn, tm=MM_TM):
    M, K = a.shape
    F = wg.shape[1]
    return pl.pallas_call(
        _swiglu_kernel,
        out_shape=jax.ShapeDtypeStruct((M, F), BF16),
        grid=(F // tn, M // tm),
        in_specs=[pl.BlockSpec((tm, K), lambda j, i: (i, 0)),
                  pl.BlockSpec((K, tn), lambda j, i: (0, j)),
                  pl.BlockSpec((K, tn), lambda j, i: (0, j))],
        out_specs=pl.BlockSpec((tm, tn), lambda j, i: (i, j)),
        compiler_params=_cparams(("parallel", "parallel")),
        name="swiglu_up",
    )(a, wg, wu)


def _moe_up_kernel(a_ref, wg_ref, wu_ref, c_ref, o_ref):
    a = a_ref[...]
    g = _dot(a, wg_ref[0])
    u = _dot(a, wu_ref[0])
    o_ref[...] = (_silu(g) * u * c_ref[0]).astype(o_ref.dtype)


def _moe_up(a, wg, wu, comb_cols, *, tm=MM_TM):
    M, K = a.shape
    E, _, F = wg.shape
    return pl.pallas_call(
        _moe_up_kernel,
        out_shape=jax.ShapeDtypeStruct((M, E * F), BF16),
        grid=(E, M // tm),
        in_specs=[pl.BlockSpec((tm, K), lambda e, i: (i, 0)),
                  pl.BlockSpec((1, K, F), lambda e, i: (e, 0, 0)),
                  pl.BlockSpec((1, K, F), lambda e, i: (e, 0, 0)),
                  pl.BlockSpec((1, tm, 1), lambda e, i: (e, i, 0))],
        out_specs=pl.BlockSpec((tm, F), lambda e, i: (i, e)),
        compiler_params=_cparams(("parallel", "parallel")),
        name="moe_up",
    )(a, wg, wu, comb_cols)


def _merge_kernel(hn_ref, b0_ref, b1_ref, b2_ref, b3_ref, wg_ref, wb_ref, o_ref):
    hn = hn_ref[...]
    acc = None
    for b, br_ref in enumerate((b0_ref, b1_ref, b2_ref, b3_ref)):
        gate = jax.nn.sigmoid(_dot(hn, wg_ref[b]))
        term = gate * _dot(br_ref[...], wb_ref[b])
        acc = term if acc is None else acc + term
    o_ref[...] = acc.astype(o_ref.dtype)


def _merge(hn, branches, wg, wb, *, tn=512, tm=MM_TM):
    M, D = hn.shape
    N = wg.shape[2]
    bspec = pl.BlockSpec((tm, BRANCH_W), lambda j, i: (i, 0))
    return pl.pallas_call(
        _merge_kernel,
        out_shape=jax.ShapeDtypeStruct((M, N), BF16),
        grid=(N // tn, M // tm),
        in_specs=[pl.BlockSpec((tm, D), lambda j, i: (i, 0)), bspec, bspec, bspec, bspec,
                  pl.BlockSpec((N_BRANCH, D, tn), lambda j, i: (0, 0, j)),
                  pl.BlockSpec((N_BRANCH, BRANCH_W, tn), lambda j, i: (0, 0, j))],
        out_specs=pl.BlockSpec((tm, tn), lambda j, i: (i, j)),
        compiler_params=_cparams(("parallel", "parallel")),
        name="gate_merge",
    )(hn, *branches, wg, wb)


def _causal_conv(x, carry_ref, cw):
    n = x.shape[0]
    xext = jnp.concatenate([carry_ref[...], x], axis=0)
    y = cw[0:1] * xext[5:5 + n]
    for i in range(1, CONV_K):
        y = y + cw[i:i + 1] * xext[5 + i:5 + i + n]
    carry_ref[...] = x[n - 8:n]
    return y


def _tile_masks(n, chunk):
    row = lax.broadcasted_iota(jnp.int32, (n, n), 0)
    col = lax.broadcasted_iota(jnp.int32, (n, n), 1)
    if chunk == n:
        return col <= row, col < row
    in_chunk = col >= (row // chunk) * chunk
    return in_chunk & (col <= row), in_chunk & (col < row)


def _gdn_kernel(qkv_ref, z_ref, sm_ref, cw_ref, alog_ref, dtb_ref, ng_ref, o_ref, s_ref, carry_ref):
    @pl.when(pl.program_id(1) == 0)
    def _():
        s_ref[...] = jnp.zeros_like(s_ref)
        carry_ref[...] = jnp.zeros_like(carry_ref)

    n = ROW_TILE
    y = _silu(_causal_conv(qkv_ref[0], carry_ref, cw_ref[...]))
    sm = sm_ref[0]
    z = z_ref[0]
    beta_all = jax.nn.sigmoid(sm)
    g_all = -jnp.exp(alog_ref[...]) * _softplus(sm + dtb_ref[...])
    row = lax.broadcasted_iota(jnp.int32, (n, n), 0)
    col = lax.broadcasted_iota(jnp.int32, (n, n), 1)
    causal, strict = col <= row, col < row
    diff_bits = row ^ col
    levels = int(math.log2(n))
    level_masks = [(diff_bits >= (1 << l)) & (diff_bits < (2 << l)) for l in range(levels)]
    eye = jnp.where(row == col, 1.0, 0.0)
    gcs_all = _dot(causal.astype(F32), g_all, HI)
    gcs_t = gcs_all.T
    outs = []
    for h in range(GDN_HEADS):
        q = y[:, h * GDN_DK:(h + 1) * GDN_DK]
        k = y[:, GDN_HEADS * GDN_DK + h * GDN_DK:GDN_HEADS * GDN_DK + (h + 1) * GDN_DK]
        v = y[:, 2 * GDN_HEADS * GDN_DK + h * GDN_DV:2 * GDN_HEADS * GDN_DK + (h + 1) * GDN_DV]
        q = q * lax.rsqrt(jnp.sum(q * q, axis=-1, keepdims=True) + EPS) * GDN_DK ** -0.5
        k = k * lax.rsqrt(jnp.sum(k * k, axis=-1, keepdims=True) + EPS)
        beta = beta_all[:, L_BETA + h:L_BETA + h + 1]
        gc = gcs_all[:, L_GA + h:L_GA + h + 1]
        gr = gcs_t[L_GA + h:L_GA + h + 1, :]
        decay = jnp.exp(jnp.where(causal, gc - gr, NEG))
        kb = k * beta
        k16 = k.astype(BF16)
        a = jnp.where(strict, _dot_nt(kb.astype(BF16), k16) * decay, 0.0)
        t = eye - jnp.where(level_masks[0], a, 0.0)
        for l in range(1, levels):
            t16 = t.astype(BF16)
            a_l = jnp.where(level_masks[l], a, 0.0).astype(BF16)
            t = t - _dot(_dot(t16, a_l).astype(BF16), t16)
        egc = jnp.exp(gc)
        rhs = jnp.concatenate([v * beta, kb * egc], axis=1)
        uw = _dot(t.astype(BF16), rhs.astype(BF16))
        u, w = uw[:, :GDN_DV], uw[:, GDN_DV:]
        att = _dot_nt(q.astype(BF16), k16) * decay
        g_last = gc[n - 1:n, :]
        kd = k * jnp.exp(g_last - gc)
        s = s_ref[h]
        ws = _dot(jnp.concatenate([w, q * egc], axis=0).astype(BF16), s.astype(BF16))
        v_new = (u - ws[:n]).astype(BF16)
        o = ws[n:] + _dot(att.astype(BF16), v_new)
        s_ref[h] = s * jnp.exp(g_last) + _dot(kd.T.astype(BF16), v_new)
        o = o * lax.rsqrt(jnp.mean(o * o, axis=-1, keepdims=True) + EPS) * ng_ref[...]
        outs.append(o * _silu(z[:, h * GDN_DV:(h + 1) * GDN_DV]))
    o_ref[0] = jnp.concatenate(outs, axis=1).astype(o_ref.dtype)


def _lane_vec(vals, lane0):
    v = jnp.zeros((1, LANES), F32)
    return v.at[0, lane0:lane0 + vals.shape[0]].set(vals.astype(F32))


def _gdn(proj, conv_w, a_log, dt_bias, norm_g):
    B, T, _ = proj.shape
    n = ROW_TILE
    W = 2 * GDN_HEADS * GDN_DK + GDN_HEADS * GDN_DV
    ZW = GDN_HEADS * GDN_DV
    return pl.pallas_call(
        _gdn_kernel,
        out_shape=jax.ShapeDtypeStruct((B, T, ZW), BF16),
        grid=(B, T // n),
        in_specs=[pl.BlockSpec((1, n, W), lambda b, t: (b, t, COL_GDN_QKV // W)),
                  pl.BlockSpec((1, n, ZW), lambda b, t: (b, t, COL_GDN_Z // ZW)),
                  pl.BlockSpec((1, n, LANES), lambda b, t: (b, t, COL_SMALL // LANES)),
                  pl.BlockSpec((CONV_K, W), lambda b, t: (0, 0)),
                  pl.BlockSpec((1, LANES), lambda b, t: (0, 0)),
                  pl.BlockSpec((1, LANES), lambda b, t: (0, 0)),
                  pl.BlockSpec((1, GDN_DV), lambda b, t: (0, 0))],
        out_specs=pl.BlockSpec((1, n, ZW), lambda b, t: (b, t, 0)),
        scratch_shapes=[pltpu.VMEM((GDN_HEADS, GDN_DK, GDN_DV), F32),
                        pltpu.VMEM((8, W), F32)],
        compiler_params=_cparams(("arbitrary", "arbitrary")),
        name="gdn",
    )(proj, proj, proj, conv_w.astype(F32), _lane_vec(a_log, L_GA), _lane_vec(dt_bias, L_GA),
      norm_g.reshape(1, GDN_DV).astype(F32))


def _ssd_kernel(xbc_ref, z_ref, sm_ref, cw_ref, cb_ref, alog_ref, dtb_ref, dvec_ref, ng_ref,
                o_ref, hs_ref, carry_ref):
    @pl.when(pl.program_id(1) == 0)
    def _():
        hs_ref[...] = jnp.zeros_like(hs_ref)
        carry_ref[...] = jnp.zeros_like(carry_ref)

    n, P, N = ROW_TILE, SSD_HEADDIM, SSD_STATE
    y = _silu(_causal_conv(xbc_ref[0], carry_ref, cw_ref[...]) + cb_ref[...])
    xs = y[:, :SSD_INNER]
    bm = y[:, SSD_INNER:SSD_INNER + SSD_GROUPS * N]
    cm = y[:, SSD_INNER + SSD_GROUPS * N:]
    sm = sm_ref[0]
    dt_all = _softplus(sm + dtb_ref[...])
    a_all = dt_all * (-jnp.exp(alog_ref[...]))
    causal, _ = _tile_masks(n, n)
    acs_all = _dot(causal.astype(F32), a_all, HI)
    acs_t = acs_all.T
    ys = []
    for g in range(SSD_GROUPS):
        bg = bm[:, g * N:(g + 1) * N]
        cg16 = cm[:, g * N:(g + 1) * N].astype(BF16)
        cb = _dot_nt(cg16, bg.astype(BF16))
        bgt16 = bg.T.astype(BF16)
        for j in range(SSD_HG):
            hh = g * SSD_HG + j
            ac = acs_all[:, L_DT + hh:L_DT + hh + 1]
            ar = acs_t[L_DT + hh:L_DT + hh + 1, :]
            lmat = jnp.exp(jnp.where(causal, ac - ar, NEG))
            xdt = xs[:, hh * P:(hh + 1) * P] * dt_all[:, L_DT + hh:L_DT + hh + 1]
            y_diag = _dot((cb * lmat).astype(BF16), xdt.astype(BF16))
            a_last = ac[n - 1:n, :]
            st = _dot(bgt16, (xdt * jnp.exp(a_last - ac)).astype(BF16))
            h_prev = hs_ref[hh]
            y_off = _dot(cg16, h_prev.astype(BF16)) * jnp.exp(ac)
            hs_ref[hh] = h_prev * jnp.exp(a_last) + st
            ys.append(y_diag + y_off)
    yy = jnp.concatenate(ys, axis=1) + xs * dvec_ref[...]
    yy = yy * _silu(z_ref[0])
    gw = SSD_HG * P
    outs = []
    for g in range(SSD_GROUPS):
        seg = yy[:, g * gw:(g + 1) * gw]
        outs.append(seg * lax.rsqrt(jnp.mean(seg * seg, axis=-1, keepdims=True) + EPS)
                    * ng_ref[:, g * gw:(g + 1) * gw])
    o_ref[0] = jnp.concatenate(outs, axis=1).astype(o_ref.dtype)


def _ssd(proj, conv_w, conv_b, dt_bias, a_log, d_skip, norm_g):
    B, T, _ = proj.shape
    n = ROW_TILE
    W = SSD_INNER + 2 * SSD_GROUPS * SSD_STATE
    dvec = jnp.repeat(d_skip.astype(F32), SSD_HEADDIM).reshape(1, SSD_INNER)
    return pl.pallas_call(
        _ssd_kernel,
        out_shape=jax.ShapeDtypeStruct((B, T, SSD_INNER), BF16),
        grid=(B, T // n),
        in_specs=[pl.BlockSpec((1, n, W), lambda b, t: (b, t, COL_SSD_XBC // W)),
                  pl.BlockSpec((1, n, SSD_INNER), lambda b, t: (b, t, COL_SSD_Z // SSD_INNER)),
                  pl.BlockSpec((1, n, LANES), lambda b, t: (b, t, COL_SMALL // LANES)),
                  pl.BlockSpec((CONV_K, W), lambda b, t: (0, 0)),
                  pl.BlockSpec((1, W), lambda b, t: (0, 0)),
                  pl.BlockSpec((1, LANES), lambda b, t: (0, 0)),
                  pl.BlockSpec((1, LANES), lambda b, t: (0, 0)),
                  pl.BlockSpec((1, SSD_INNER), lambda b, t: (0, 0)),
                  pl.BlockSpec((1, SSD_INNER), lambda b, t: (0, 0))],
        out_specs=pl.BlockSpec((1, n, SSD_INNER), lambda b, t: (b, t, 0)),
        scratch_shapes=[pltpu.VMEM((SSD_HEADS, SSD_STATE, SSD_HEADDIM), F32),
                        pltpu.VMEM((8, W), F32)],
        compiler_params=_cparams(("arbitrary", "arbitrary")),
        name="ssd",
    )(proj, proj, proj, conv_w.astype(F32), conv_b.reshape(1, W).astype(F32),
      _lane_vec(a_log, L_DT), _lane_vec(dt_bias, L_DT), dvec,
      norm_g.reshape(1, SSD_INNER).astype(F32))


def _ones_lane0(n):
    lane = lax.broadcasted_iota(jnp.int32, (n, LANES), 1)
    return jnp.where(lane == 0, 1.0, 0.0)


def _fox_prep_kernel(qkv_ref, sm_ref, qg_ref, kg_ref, bf_ref, q_ref, k_ref, v_ref, run_ref):
    @pl.when(pl.program_id(1) == 0)
    def _():
        run_ref[...] = jnp.zeros_like(run_ref)

    n, dh, H = ROW_TILE, FOX_DH, FOX_HEADS
    x = qkv_ref[0]
    log_f = -_softplus(-(sm_ref[0] + bf_ref[...]))
    causal, _ = _tile_masks(n, n)
    cum = _dot(causal.astype(F32), log_f, HI) + run_ref[0:1, :]
    run_ref[...] = jnp.broadcast_to(cum[n - 1:n, :], run_ref.shape)
    lane = lax.broadcasted_iota(jnp.int32, (n, LANES), 1)
    ones = _ones_lane0(n)
    q_bias = jnp.where(lane < 3, 1.0, 0.0)
    qs, ks, vs = [], [], []
    for h in range(H):
        q = x[:, h * dh:(h + 1) * dh]
        k = x[:, H * dh + h * dh:H * dh + (h + 1) * dh]
        qs.append(q * lax.rsqrt(jnp.mean(q * q, axis=-1, keepdims=True) + EPS)
                  * (qg_ref[...] * (dh ** -0.5 * LOG2E)))
        qs.append(q_bias)
        ks.append(k * lax.rsqrt(jnp.mean(k * k, axis=-1, keepdims=True) + EPS) * kg_ref[...])
        c = cum[:, L_FF + h:L_FF + h + 1] * (-LOG2E)
        c_hi = c.astype(BF16).astype(F32)
        c_mid = (c - c_hi).astype(BF16).astype(F32)
        c_lo = c - c_hi - c_mid
        ks.append(jnp.where(lane == 0, c_hi, jnp.where(lane == 1, c_mid, jnp.where(lane == 2, c_lo, 0.0))))
        vs.append(x[:, 2 * H * dh + h * dh:2 * H * dh + (h + 1) * dh])
        vs.append(ones)
    q_ref[0] = jnp.concatenate(qs, axis=1).astype(q_ref.dtype)
    k_ref[0] = jnp.concatenate(ks, axis=1).astype(k_ref.dtype)
    v_ref[0] = jnp.concatenate(vs, axis=1).astype(v_ref.dtype)


def _fox_prep(proj, qn_g, kn_g, b_f):
    B, T, _ = proj.shape
    n = ROW_TILE
    W = 3 * FOX_HEADS * FOX_DH
    HW = FOX_HEADS * 2 * LANES
    ospec = pl.BlockSpec((1, n, HW), lambda b, t: (b, t, 0))
    return pl.pallas_call(
        _fox_prep_kernel,
        out_shape=(jax.ShapeDtypeStruct((B, T, HW), BF16),) * 3,
        grid=(B, T // n),
        in_specs=[pl.BlockSpec((1, n, W), lambda b, t: (b, t, COL_FOX_QKV // W)),
                  pl.BlockSpec((1, n, LANES), lambda b, t: (b, t, COL_SMALL // LANES)),
                  pl.BlockSpec((1, FOX_DH), lambda b, t: (0, 0)),
                  pl.BlockSpec((1, FOX_DH), lambda b, t: (0, 0)),
                  pl.BlockSpec((1, LANES), lambda b, t: (0, 0))],
        out_specs=(ospec, ospec, ospec),
        scratch_shapes=[pltpu.VMEM((8, LANES), F32)],
        compiler_params=_cparams(("arbitrary", "arbitrary")),
        name="fox_prep",
    )(proj, proj, qn_g.reshape(1, FOX_DH).astype(F32), kn_g.reshape(1, FOX_DH).astype(F32),
      _lane_vec(b_f, L_FF))


def _mla_prep_kernel(qa_ref, kva_ref, sm_ref, cos_ref, sin_ref, qag_ref, wq_ref, kvag_ref, wkv_ref,
                     qgn_ref, qgr_ref, kgn_ref, kgr_ref, rot_ref, exp_ref, q_ref, k_ref, v_ref):
    n, H, dn, dr = ROW_TILE, MLA_HEADS, MLA_NOPE, MLA_ROPE
    qa = qa_ref[0]
    qa = qa * lax.rsqrt(jnp.mean(qa * qa, axis=-1, keepdims=True) + EPS) * qag_ref[...]
    qq = _dot(qa.astype(BF16), wq_ref[...])
    kva = kva_ref[0]
    kva = kva * lax.rsqrt(jnp.mean(kva * kva, axis=-1, keepdims=True) + EPS) * kvag_ref[...]
    kv = _dot(kva.astype(BF16), wkv_ref[...])
    sm = sm_ref[0]
    lane = lax.broadcasted_iota(jnp.int32, (n, LANES), 1)
    is_kpe = (lane >= L_KPE) & (lane < L_KPE + dr)
    kpe_ss = jnp.sum(jnp.where(is_kpe, sm * sm, 0.0), axis=-1, keepdims=True)
    kpe4 = _dot(sm, exp_ref[...], HI)
    qr = qq[:, H * dn:]
    grp = lax.broadcasted_iota(jnp.int32, (n, H * dr), 1) // dr
    q_rs, k_rs = [], []
    for h in range(H):
        qn = qq[:, h * dn:(h + 1) * dn]
        ssr = jnp.sum(jnp.where(grp == h, qr * qr, 0.0), axis=-1, keepdims=True)
        q_rs.append(lax.rsqrt((jnp.sum(qn * qn, axis=-1, keepdims=True) + ssr) / MLA_DQK + EPS))
        kn = kv[:, h * dn:(h + 1) * dn]
        k_rs.append(lax.rsqrt((jnp.sum(kn * kn, axis=-1, keepdims=True) + kpe_ss) / MLA_DQK + EPS))

    def per_group(vals):
        out = vals[H - 1]
        for h in range(H - 2, -1, -1):
            out = jnp.where(grp == h, vals[h], out)
        return out

    cos, sin, rot = cos_ref[...], sin_ref[...], rot_ref[...]
    tq = qr * per_group(q_rs) * qgr_ref[...]
    tq = tq * cos + _dot(tq, rot, HI) * sin
    tk = kpe4 * per_group(k_rs) * kgr_ref[...]
    tk = tk * cos + _dot(tk, rot, HI) * sin
    scale = MLA_DQK ** -0.5 * LOG2E
    half_id = lane // dr
    ones = _ones_lane0(n)
    q_parts, k_parts, v_parts = [], [], []
    for h in range(H):
        blk = slice((h // 2) * LANES, (h // 2 + 1) * LANES)
        q_parts.append(qq[:, h * dn:(h + 1) * dn] * q_rs[h] * (qgn_ref[...] * scale))
        q_parts.append(jnp.where(half_id == h % 2, tq[:, blk] * scale, 0.0))
        k_parts.append(kv[:, h * dn:(h + 1) * dn] * k_rs[h] * kgn_ref[...])
        k_parts.append(tk[:, blk])
        v_parts.append(kv[:, H * dn + h * MLA_V:H * dn + (h + 1) * MLA_V])
        v_parts.append(ones)
    q_ref[0] = jnp.concatenate(q_parts, axis=1).astype(q_ref.dtype)
    k_ref[0] = jnp.concatenate(k_parts, axis=1).astype(k_ref.dtype)
    v_ref[0] = jnp.concatenate(v_parts, axis=1).astype(v_ref.dtype)


def _rope_consts(T):
    H, dr = MLA_HEADS, MLA_ROPE
    inv = 1.0 / (ROPE_BASE ** (jnp.arange(0, dr, 2, dtype=F32) / dr))
    ang = jnp.arange(T, dtype=F32)[:, None] * inv[None, :]
    ang = jnp.concatenate([ang, ang], axis=-1)
    cos4 = jnp.tile(jnp.cos(ang), (1, H))
    sin4 = jnp.tile(jnp.sin(ang), (1, H))
    rot = np.zeros((H * dr, H * dr), np.float32)
    for h in range(H):
        for c in range(dr // 2):
            rot[h * dr + c + dr // 2, h * dr + c] = -1.0
            rot[h * dr + c, h * dr + c + dr // 2] = 1.0
    expand = np.zeros((LANES, H * dr), np.float32)
    for h in range(H):
        for c in range(dr):
            expand[L_KPE + c, h * dr + c] = 1.0
    return cos4, sin4, jnp.asarray(rot), jnp.asarray(expand)


def _mla_prep(proj, rope, qa_g, wq_b, kva_g, wkv_b, qn_g, kn_g):
    B, T, _ = proj.shape
    n, H, dn, dr, dv = ROW_TILE, MLA_HEADS, MLA_NOPE, MLA_ROPE, MLA_V
    cos4, sin4, rot, expand = rope
    wq = wq_b.reshape(MLA_Q_RANK, H, MLA_DQK)
    wq = jnp.concatenate([wq[:, :, :dn].reshape(MLA_Q_RANK, H * dn),
                          wq[:, :, dn:].reshape(MLA_Q_RANK, H * dr)], axis=1).astype(BF16)
    wkv = wkv_b.reshape(MLA_KV_RANK, H, dn + dv)
    wkv = jnp.concatenate([wkv[:, :, :dn].reshape(MLA_KV_RANK, H * dn),
                           wkv[:, :, dn:].reshape(MLA_KV_RANK, H * dv)], axis=1).astype(BF16)
    qg, kg = qn_g.astype(F32), kn_g.astype(F32)
    const = lambda shape: pl.BlockSpec(shape, lambda b, t: (0,) * len(shape))
    QW = H * 2 * LANES
    return pl.pallas_call(
        _mla_prep_kernel,
        out_shape=(jax.ShapeDtypeStruct((B, T, QW), BF16), jax.ShapeDtypeStruct((B, T, QW), BF16),
                   jax.ShapeDtypeStruct((B, T, QW), BF16)),
        grid=(B, T // n),
        in_specs=[pl.BlockSpec((1, n, MLA_Q_RANK), lambda b, t: (b, t, COL_MLA_QA // MLA_Q_RANK)),
                  pl.BlockSpec((1, n, MLA_KV_RANK), lambda b, t: (b, t, COL_MLA_KVA // MLA_KV_RANK)),
                  pl.BlockSpec((1, n, LANES), lambda b, t: (b, t, COL_SMALL // LANES)),
                  pl.BlockSpec((n, H * dr), lambda b, t: (t, 0)),
                  pl.BlockSpec((n, H * dr), lambda b, t: (t, 0)),
                  const((1, MLA_Q_RANK)), const(wq.shape), const((1, MLA_KV_RANK)), const(wkv.shape),
                  const((1, dn)), const((1, H * dr)), const((1, dn)), const((1, H * dr)),
                  const(rot.shape), const(expand.shape)],
        out_specs=(pl.BlockSpec((1, n, QW), lambda b, t: (b, t, 0)),
                   pl.BlockSpec((1, n, QW), lambda b, t: (b, t, 0)),
                   pl.BlockSpec((1, n, QW), lambda b, t: (b, t, 0))),
        compiler_params=_cparams(("parallel", "parallel")),
        name="mla_prep",
    )(proj, proj, proj, cos4, sin4,
      qa_g.reshape(1, -1).astype(F32), wq, kva_g.reshape(1, -1).astype(F32), wkv,
      qg[:dn].reshape(1, dn), jnp.tile(qg[dn:], H).reshape(1, H * dr),
      kg[:dn].reshape(1, dn), jnp.tile(kg[dn:], H).reshape(1, H * dr), rot, expand)


def _flash_kernel(q_ref, k_ref, v_ref, o_ref, m_ref, acc_ref, *, tq):
    qi = pl.program_id(2)
    nh = m_ref.shape[0]
    dqk, dv2 = q_ref.shape[-1] // nh, v_ref.shape[-1] // nh
    dv = dv2 // 2
    rb = ROW_TILE
    m_ref[...] = jnp.full(m_ref.shape, NEG, F32)
    acc_ref[...] = jnp.zeros(acc_ref.shape, F32)

    def step(j, masked):
        start = pl.multiple_of(j * tq, tq)
        for h in range(nh):
            for r in range(0, tq, rb):
                kw = r + rb if masked else tq
                s = _dot_nt(q_ref[0, r:r + rb, h * dqk:(h + 1) * dqk],
                            k_ref[0, pl.ds(start, kw), h * dqk:(h + 1) * dqk])
                if masked:
                    row = lax.broadcasted_iota(jnp.int32, s.shape, 0) + r
                    col = lax.broadcasted_iota(jnp.int32, s.shape, 1)
                    s = jnp.where(col <= row, s, NEG)
                lane_max = s[:, :LANES]
                for c in range(1, kw // LANES):
                    lane_max = jnp.maximum(lane_max, s[:, c * LANES:(c + 1) * LANES])
                m_old = m_ref[h, r:r + rb]
                m_new = jnp.maximum(m_old, jnp.max(lane_max, axis=-1, keepdims=True))
                m_ref[h, r:r + rb] = m_new
                p = jnp.exp2(s - m_new).astype(BF16)
                acc_ref[h, r:r + rb] = (jnp.exp2(m_old - m_new) * acc_ref[h, r:r + rb]
                                        + _dot(p, v_ref[0, pl.ds(start, kw), h * dv2:(h + 1) * dv2]))

    def body(jj, carry):
        step(2 * jj, False)
        step(2 * jj + 1, False)
        return carry

    lax.fori_loop(0, qi // 2, body, 0)

    @pl.when(qi % 2 == 1)
    def _():
        step(qi - 1, False)

    step(qi, True)
    outs = []
    for h in range(nh):
        acc = acc_ref[h]
        outs.append(acc[:, :dv] / acc[:, dv:dv + 1])
    o_ref[0] = jnp.concatenate(outs, axis=1).astype(o_ref.dtype)


def _flash(q, k, v):
    B, T, QW = q.shape
    H, nh = MLA_HEADS, FLASH_HEADS_PER_STEP
    dqk, dv2 = QW // H, v.shape[-1] // H
    dv = dv2 // 2
    tq = next(t for t in FLASH_TILES if T % t == 0)
    return pl.pallas_call(
        functools.partial(_flash_kernel, tq=tq),
        out_shape=jax.ShapeDtypeStruct((B, T, H * dv), BF16),
        grid=(B, H // nh, T // tq),
        in_specs=[pl.BlockSpec((1, tq, nh * dqk), lambda b, h, i: (b, i, h)),
                  pl.BlockSpec((1, T, nh * dqk), lambda b, h, i: (b, 0, h)),
                  pl.BlockSpec((1, T, nh * dv2), lambda b, h, i: (b, 0, h))],
        out_specs=pl.BlockSpec((1, tq, nh * dv), lambda b, h, i: (b, i, h)),
        scratch_shapes=[pltpu.VMEM((nh, tq, 1), F32), pltpu.VMEM((nh, tq, dv2), F32)],
        compiler_params=_cparams(("parallel", "parallel", "arbitrary")),
        name="flash",
    )(q, k, v)


def _reorder_w_in(w_in):
    D = w_in.shape[0]
    w = w_in.astype(BF16)
    z = lambda n: jnp.zeros((D, n), BF16)
    small = [w[:, 2048:2052], w[:, 2052:2056], w[:, 4424:4428], w[:, 5964:5972], w[:, 2824:2888]]
    n_small = sum(s.shape[1] for s in small)
    cols = [w[:, 0:2048],
            w[:, 4940:5964],
            w[:, 4428:4940],
            w[:, 2056:2824],
            *small, z(LANES - n_small), z(COL_FOX_QKV - COL_SMALL - LANES),
            w[:, 2888:4424]]
    out = jnp.concatenate(cols, axis=1)
    assert out.shape[1] == PROJ_W
    return out


def _mixer(h, hn, B, T, rope, w_in, gdn_conv_w, gdn_A_log, gdn_dt_bias, gdn_norm_g,
           mla_qa_g, mla_wq_b, mla_kva_g, mla_wkv_b, mla_qn_g, mla_kn_g,
           fox_qn_g, fox_kn_g, fox_b_f,
           ssd_conv_w, ssd_conv_b, ssd_dt_bias, ssd_A_log, ssd_D, ssd_norm_g,
           w_gate, w_branch, w_o):
    M = B * T
    proj = _matmul(hn, _reorder_w_in(w_in), tn=1024, name="in_proj").reshape(B, T, PROJ_W)
    o_gdn = _gdn(proj, gdn_conv_w, gdn_A_log, gdn_dt_bias, gdn_norm_g)
    mq, mk, mv = _mla_prep(proj, rope, mla_qa_g, mla_wq_b, mla_kva_g, mla_wkv_b, mla_qn_g, mla_kn_g)
    o_mla = _flash(mq, mk, mv)
    o_fox = _flash(*_fox_prep(proj, fox_qn_g, fox_kn_g, fox_b_f))
    o_ssd = _ssd(proj, ssd_conv_w, ssd_conv_b, ssd_dt_bias, ssd_A_log, ssd_D, ssd_norm_g)
    branches = [o.reshape(M, BRANCH_W) for o in (o_gdn, o_mla, o_fox, o_ssd)]
    merged = _merge(hn, branches, w_gate.astype(BF16), w_branch.astype(BF16))
    return _matmul(merged, w_o.astype(BF16), tn=w_o.shape[1], residual=h, name="out_proj")


def kernel(x, meta_tokens, mix_norm_g, w_in, gdn_conv_w, gdn_A_log, gdn_dt_bias, gdn_norm_g, mla_qa_g, mla_wq_b, mla_kva_g, mla_wkv_b, mla_qn_g, mla_kn_g, fox_qn_g, fox_kn_g, fox_b_f, ssd_conv_w, ssd_conv_b, ssd_dt_bias, ssd_A_log, ssd_D, ssd_norm_g, w_gate, w_branch, w_o, ffn_norm_g, dense_w_gate, dense_w_up, dense_w_down, router_w, moe_w_gate, moe_w_up, moe_w_down):
    B, S, D = x.shape
    L = N_META + S
    T = -(-L // ROW_TILE) * ROW_TILE
    assert (B * T) % MM_TM == 0
    depth = w_in.shape[0]
    meta = jnp.broadcast_to(meta_tokens[None].astype(x.dtype), (B, N_META, D))
    h = jnp.concatenate([meta, x, jnp.zeros((B, T - L, D), x.dtype)], axis=1).reshape(B * T, D)
    rope = _rope_consts(T)
    for layer in range(depth):
        hn = _rmsnorm(h, mix_norm_g[layer])
        h = _mixer(h, hn, B, T, rope, w_in[layer],
                   gdn_conv_w[layer], gdn_A_log[layer], gdn_dt_bias[layer], gdn_norm_g[layer],
                   mla_qa_g[layer], mla_wq_b[layer], mla_kva_g[layer], mla_wkv_b[layer],
                   mla_qn_g[layer], mla_kn_g[layer],
                   fox_qn_g[layer], fox_kn_g[layer], fox_b_f[layer],
                   ssd_conv_w[layer], ssd_conv_b[layer], ssd_dt_bias[layer], ssd_A_log[layer],
                   ssd_D[layer], ssd_norm_g[layer],
                   w_gate[layer], w_branch[layer], w_o[layer])
        i = layer // 2
        if layer % 2 == 0:
            hn = _rmsnorm(h, ffn_norm_g[layer])
            act = _swiglu_up(hn, dense_w_gate[i].astype(BF16), dense_w_up[i].astype(BF16), tn=512)
            h = _matmul(act, dense_w_down[i].astype(BF16), tn=512, residual=h, name="ffn_down")
        else:
            hn, comb = _rmsnorm_router(h, ffn_norm_g[layer], router_w[i])
            comb_cols = jnp.swapaxes(comb[:, :N_EXPERTS], 0, 1)[:, :, None]
            act = _moe_up(hn, moe_w_gate[i].astype(BF16), moe_w_up[i].astype(BF16), comb_cols)
            E, F, _ = moe_w_down[i].shape
            h = _matmul(act, moe_w_down[i].reshape(E * F, D).astype(BF16), tn=512, tm=256,
                        residual=h, name="moe_down")
    return h.reshape(B, T, D)[:, N_META:L].astype(x.dtype)
```

```python
import functools
import math

import numpy as np
import jax
import jax.numpy as jnp
from jax import lax
from jax.experimental import pallas as pl
from jax.experimental.pallas import tpu as pltpu

F32 = jnp.float32
BF16 = jnp.bfloat16
HI = lax.Precision.HIGHEST
NT_DIMS = (((1,), (1,)), ((), ()))

D_MODEL = 2048
N_META = 16
EPS = 1e-6
NEG = -1e30
CONV_K = 4

GDN_HEADS, GDN_DK, GDN_DV = 4, 128, 128
MLA_HEADS, MLA_Q_RANK, MLA_KV_RANK, MLA_NOPE, MLA_ROPE, MLA_V = 4, 512, 256, 128, 64, 128
MLA_DQK = MLA_NOPE + MLA_ROPE
ROPE_BASE = 10000.0
FOX_HEADS, FOX_DH = 4, 128
SSD_HEADS, SSD_HEADDIM, SSD_GROUPS, SSD_STATE = 8, 64, 2, 128
SSD_HG = SSD_HEADS // SSD_GROUPS
SSD_INNER = SSD_HEADS * SSD_HEADDIM
N_BRANCH, BRANCH_W = 4, 512
N_EXPERTS, TOP_K = 8, 2

LANES = 128
ROW_TILE = 256
FLASH_TILES = (768, 256)
FLASH_HEADS_PER_STEP = 2
LOG2E = 1.4426950408889634
MM_TM = 512
MOE_TM = 256
R_E1, R_E2, R_P1, R_P2, R_R1, R_R2 = range(6)
VMEM_LIMIT = 56 * 1024 * 1024

PROJ_W = 6144
COL_GDN_QKV, COL_GDN_Z = 0, 1536
COL_SSD_XBC, COL_SSD_Z = 2048, 3072
COL_MLA_QA, COL_MLA_KVA = 3584, 4096
COL_SMALL = 4352
COL_FOX_QKV = 4608
L_BETA, L_GA, L_FF, L_DT, L_KPE = 0, 4, 8, 12, 20


def _cparams(sem, vmem=VMEM_LIMIT):
    return pltpu.CompilerParams(dimension_semantics=sem, vmem_limit_bytes=vmem)


def _softplus(x):
    return jnp.maximum(x, 0.0) + jnp.log1p(jnp.exp(-jnp.abs(x)))


def _silu(x):
    return x * jax.nn.sigmoid(x)


def _dot(a, b, precision=None):
    return jnp.dot(a, b, preferred_element_type=F32, precision=precision)


def _dot_nt(a, b):
    return lax.dot_general(a, b, NT_DIMS, preferred_element_type=F32)


def _rmsnorm_kernel(h_ref, g_ref, o_ref):
    x = h_ref[...]
    y = x * lax.rsqrt(jnp.mean(x * x, axis=-1, keepdims=True) + EPS) * g_ref[...]
    o_ref[...] = y.astype(o_ref.dtype)


def _rmsnorm(h, g):
    M, D = h.shape
    return pl.pallas_call(
        _rmsnorm_kernel,
        out_shape=jax.ShapeDtypeStruct((M, D), BF16),
        grid=(M // MM_TM,),
        in_specs=[pl.BlockSpec((MM_TM, D), lambda i: (i, 0)),
                  pl.BlockSpec((1, D), lambda i: (0, 0))],
        out_specs=pl.BlockSpec((MM_TM, D), lambda i: (i, 0)),
        compiler_params=_cparams(("parallel",)),
        name="rmsnorm",
    )(h, g.reshape(1, D).astype(F32))


def _rmsnorm_router_kernel(h_ref, g_ref, rw_ref, o_ref, rec_ref, cnt_ref, run_ref):
    @pl.when(pl.program_id(1) == 0)
    def _():
        run_ref[...] = jnp.zeros_like(run_ref)

    x = h_ref[...]
    n = x.shape[0]
    y = x * lax.rsqrt(jnp.mean(x * x, axis=-1, keepdims=True) + EPS) * g_ref[...]
    o_ref[...] = y.astype(o_ref.dtype)
    logits = _dot(y, rw_ref[...], HI)
    lane = lax.broadcasted_iota(jnp.int32, logits.shape, 1)
    logits = jnp.where(lane < N_EXPERTS, logits, NEG)
    m1 = jnp.max(logits, axis=-1, keepdims=True)
    i1 = jnp.min(jnp.where(logits == m1, lane, LANES), axis=-1, keepdims=True)
    rest = jnp.where(lane == i1, NEG, logits)
    m2 = jnp.max(rest, axis=-1, keepdims=True)
    i2 = jnp.min(jnp.where(rest == m2, lane, LANES), axis=-1, keepdims=True)
    e2 = jnp.exp(m2 - m1)
    p1 = 1.0 / (1.0 + e2)
    p2 = e2 * p1
    sel = jnp.where(lane == i1, 1.0, 0.0) + jnp.where(lane == i2, 1.0, 0.0)
    row = lax.broadcasted_iota(jnp.int32, (n, n), 0)
    col = lax.broadcasted_iota(jnp.int32, (n, n), 1)
    earlier = jnp.where(col < row, 1.0, 0.0).astype(BF16)
    run = run_ref[0:1, :]
    before = _dot(earlier, sel.astype(BF16)) + run
    r1 = jnp.sum(jnp.where(lane == i1, before, 0.0), axis=-1, keepdims=True)
    r2 = jnp.sum(jnp.where(lane == i2, before, 0.0), axis=-1, keepdims=True)
    cnt_ref[...] = jnp.broadcast_to(run, cnt_ref.shape)
    run_ref[...] = jnp.broadcast_to(before[n - 1:n] + sel[n - 1:n], run_ref.shape)
    rec = jnp.zeros_like(logits)
    for ln, val in ((R_E1, i1.astype(F32)), (R_E2, i2.astype(F32)), (R_P1, p1), (R_P2, p2), (R_R1, r1), (R_R2, r2)):
        rec = jnp.where(lane == ln, val, rec)
    rec_ref[...] = rec


def _rmsnorm_router(h, g, router_w, B, T):
    M, D = h.shape
    n = ROW_TILE
    NT = T // n
    rw = jnp.zeros((D, LANES), F32).at[:, :N_EXPERTS].set(router_w.astype(F32))
    return pl.pallas_call(
        _rmsnorm_router_kernel,
        out_shape=(jax.ShapeDtypeStruct((M, D), BF16), jax.ShapeDtypeStruct((M, LANES), F32),
                   jax.ShapeDtypeStruct((B * NT * 8, LANES), F32)),
        grid=(B, NT),
        in_specs=[pl.BlockSpec((n, D), lambda b, t: (b * NT + t, 0)),
                  pl.BlockSpec((1, D), lambda b, t: (0, 0)),
                  pl.BlockSpec((D, LANES), lambda b, t: (0, 0))],
        out_specs=(pl.BlockSpec((n, D), lambda b, t: (b * NT + t, 0)),
                   pl.BlockSpec((n, LANES), lambda b, t: (b * NT + t, 0)),
                   pl.BlockSpec((8, LANES), lambda b, t: (b * NT + t, 0))),
        scratch_shapes=[pltpu.VMEM((8, LANES), F32)],
        compiler_params=_cparams(("arbitrary", "arbitrary")),
        name="rmsnorm_router",
    )(h, g.reshape(1, D).astype(F32), rw)


def _mm_kernel(*refs, has_res):
    a_ref, w_ref = refs[:2]
    y = _dot(a_ref[...], w_ref[...])
    if has_res:
        y = y + refs[2][...]
    refs[-1][...] = y.astype(refs[-1].dtype)


def _matmul(a, w, *, tn, tm=MM_TM, residual=None, out_dtype=F32, name="matmul"):
    M, K = a.shape
    N = w.shape[1]
    in_specs = [pl.BlockSpec((tm, K), lambda j, i: (i, 0)),
                pl.BlockSpec((K, tn), lambda j, i: (0, j))]
    args = [a, w]
    if residual is not None:
        in_specs.append(pl.BlockSpec((tm, tn), lambda j, i: (i, j)))
        args.append(residual)
    return pl.pallas_call(
        functools.partial(_mm_kernel, has_res=residual is not None),
        out_shape=jax.ShapeDtypeStruct((M, N), out_dtype),
        grid=(N // tn, M // tm),
        in_specs=in_specs,
        out_specs=pl.BlockSpec((tm, tn), lambda j, i: (i, j)),
        compiler_params=_cparams(("parallel", "parallel")),
        name=name,
    )(*args)


def _swiglu_kernel(a_ref, wg_ref, wu_ref, o_ref):
    a = a_ref[...]
    g = _dot(a, wg_ref[...])
    u = _dot(a, wu_ref[...])
    o_ref[...] = (_silu(g) * u).astype(o_ref.dtype)


def _swiglu_up(a, wg, wu, *, tn, tm=MM_TM):
    M, K = a.shape
    F = wg.shape[1]
    return pl.pallas_call(
        _swiglu_kernel,
        out_shape=jax.ShapeDtypeStruct((M, F), BF16),
        grid=(F // tn, M // tm),
        in_specs=[pl.BlockSpec((tm, K), lambda j, i: (i, 0)),
                  pl.BlockSpec((K, tn), lambda j, i: (0, j)),
                  pl.BlockSpec((K, tn), lambda j, i: (0, j))],
        out_specs=pl.BlockSpec((tm, tn), lambda j, i: (i, j)),
        compiler_params=_cparams(("parallel", "parallel")),
        name="swiglu_up",
    )(a, wg, wu)


def _dispatch_kernel(lo_ref, nb_ref, src_ref, hn_ref, o_ref, acc_ref, *, wb):
    idx = pl.program_id(0) * pl.num_programs(1) + pl.program_id(1)
    tm = o_ref.shape[0]
    src = src_ref[0]
    lane = lax.broadcasted_iota(jnp.int32, (tm, wb), 1)
    acc_ref[...] = jnp.zeros_like(acc_ref)

    def body(k, carry):
        start = pl.multiple_of((lo_ref[idx] + k) * wb, wb)
        onehot = jnp.where(src == start + lane, 1.0, 0.0).astype(BF16)
        acc_ref[...] += _dot(onehot, hn_ref[0, pl.ds(start, wb), :])
        return carry

    lax.fori_loop(0, nb_ref[idx], body, 0)
    o_ref[...] = acc_ref[...].astype(o_ref.dtype)


def _dispatch(hn3, src, lo_blk, n_blk, *, tm, wb=ROW_TILE):
    B, T, D = hn3.shape
    nti = src.shape[1] // tm
    grid_spec = pltpu.PrefetchScalarGridSpec(
        num_scalar_prefetch=2,
        grid=(B, nti),
        in_specs=[pl.BlockSpec((1, tm, 1), lambda b, i, lo, nb: (b * nti + i, 0, 0)),
                  pl.BlockSpec((1, T, D), lambda b, i, lo, nb: (b, 0, 0), pipeline_mode=pl.Buffered(1))],
        out_specs=pl.BlockSpec((tm, D), lambda b, i, lo, nb: (b * nti + i, 0)),
        scratch_shapes=[pltpu.VMEM((tm, D), F32)])
    return pl.pallas_call(
        functools.partial(_dispatch_kernel, wb=wb),
        out_shape=jax.ShapeDtypeStruct((B * nti * tm, D), BF16),
        grid_spec=grid_spec,
        compiler_params=_cparams(("arbitrary", "arbitrary")),
        name="moe_dispatch",
    )(lo_blk, n_blk, src.reshape(B * nti, tm, 1), hn3)


def _expert_up_kernel(te_ref, nb_ref, x_ref, wg_ref, wu_ref, p_ref, o_ref):
    i = pl.program_id(0)

    @pl.when(nb_ref[i] > 0)
    def _():
        x = x_ref[...]
        o_ref[...] = (_silu(_dot(x, wg_ref[0])) * _dot(x, wu_ref[0]) * p_ref[...]).astype(o_ref.dtype)

    @pl.when(nb_ref[i] == 0)
    def _():
        o_ref[...] = jnp.zeros_like(o_ref)


def _expert_down_kernel(te_ref, nb_ref, a_ref, wd_ref, o_ref):
    i = pl.program_id(0)

    @pl.when(nb_ref[i] > 0)
    def _():
        o_ref[...] = _dot(a_ref[...], wd_ref[0]).astype(o_ref.dtype)

    @pl.when(nb_ref[i] == 0)
    def _():
        o_ref[...] = jnp.zeros_like(o_ref)


def _expert_ffn(x, tile_expert, n_blk, p_rows, wg, wu, wd, *, tm):
    R, D = x.shape
    E, _, F = wg.shape
    up_spec = pltpu.PrefetchScalarGridSpec(
        num_scalar_prefetch=2,
        grid=(R // tm,),
        in_specs=[pl.BlockSpec((tm, D), lambda i, te, nb: (i, 0)),
                  pl.BlockSpec((1, D, F), lambda i, te, nb: (te[i], 0, 0)),
                  pl.BlockSpec((1, D, F), lambda i, te, nb: (te[i], 0, 0)),
                  pl.BlockSpec((tm, 1), lambda i, te, nb: (i, 0))],
        out_specs=pl.BlockSpec((tm, F), lambda i, te, nb: (i, 0)))
    act = pl.pallas_call(
        _expert_up_kernel,
        out_shape=jax.ShapeDtypeStruct((R, F), BF16),
        grid_spec=up_spec,
        compiler_params=_cparams(("arbitrary",)),
        name="moe_up",
    )(tile_expert, n_blk, x, wg, wu, p_rows)
    down_spec = pltpu.PrefetchScalarGridSpec(
        num_scalar_prefetch=2,
        grid=(R // tm,),
        in_specs=[pl.BlockSpec((tm, F), lambda i, te, nb: (i, 0)),
                  pl.BlockSpec((1, F, D), lambda i, te, nb: (te[i], 0, 0))],
        out_specs=pl.BlockSpec((tm, D), lambda i, te, nb: (i, 0)))
    return pl.pallas_call(
        _expert_down_kernel,
        out_shape=jax.ShapeDtypeStruct((R, D), BF16),
        grid_spec=down_spec,
        compiler_params=_cparams(("arbitrary",)),
        name="moe_down",
    )(tile_expert, n_blk, act, wd)


def _combine_kernel(yb_ref, yv_ref, h_ref, d1_ref, d2_ref, *refs, tm):
    y_refs, o_ref = refs[:-1], refs[-1]
    n = o_ref.shape[0]
    base = (pl.program_id(0) * pl.num_programs(1) + pl.program_id(1)) * len(y_refs)
    d1, d2 = d1_ref[...], d2_ref[...]
    lane = lax.broadcasted_iota(jnp.int32, (n, tm), 1)
    o_ref[...] = h_ref[...]
    for k, y_ref in enumerate(y_refs):
        @pl.when(yv_ref[base + k] > 0)
        def _():
            rows = yb_ref[base + k] * tm + lane
            hit = jnp.where(d1 == rows, 1.0, 0.0) + jnp.where(d2 == rows, 1.0, 0.0)
            o_ref[...] += _dot(hit.astype(BF16), y_ref[...])


def _combine(h, y, d1, d2, y_blk, y_valid, B, T, *, tm):
    M, D = h.shape
    n = ROW_TILE
    NT = T // n
    slots = y_blk.shape[0] // (B * NT)
    y_specs = [pl.BlockSpec((tm, D), lambda b, t, yb, yv, k=k: (yb[(b * NT + t) * slots + k], 0))
               for k in range(slots)]
    tok = lambda w: pl.BlockSpec((n, w), lambda b, t, yb, yv: (b * NT + t, 0))
    grid_spec = pltpu.PrefetchScalarGridSpec(
        num_scalar_prefetch=2,
        grid=(B, NT),
        in_specs=[tok(D), tok(1), tok(1)] + y_specs,
        out_specs=tok(D))
    return pl.pallas_call(
        functools.partial(_combine_kernel, tm=tm),
        out_shape=jax.ShapeDtypeStruct((M, D), F32),
        grid_spec=grid_spec,
        compiler_params=_cparams(("arbitrary", "arbitrary")),
        name="moe_combine",
    )(y_blk, y_valid, h, d1, d2, *([y] * slots))


def _moe(h, hn, rec, cnt_before, wg, wu, wd, B, T):
    M, D = h.shape
    E, tm, n = N_EXPERTS, MOE_TM, ROW_TILE
    NT = T // n
    rb = TOP_K * T + E * tm
    nti = rb // tm
    i32 = jnp.int32
    col = lambda ln: rec[:, ln].reshape(B, T)
    e1, e2 = col(R_E1).astype(i32), col(R_E2).astype(i32)
    r1, r2 = col(R_R1).astype(i32), col(R_R2).astype(i32)
    cntb = cnt_before.reshape(B, NT, 8, LANES)[:, :, 0, :E].astype(i32)
    counts = (jax.nn.one_hot(e1, E, dtype=i32) + jax.nn.one_hot(e2, E, dtype=i32)).sum(axis=1)
    padded = (counts + tm - 1) // tm * tm
    ends = jnp.cumsum(padded, axis=1)
    off = ends - padded
    d1 = jnp.take_along_axis(off, e1, axis=1) + r1
    d2 = jnp.take_along_axis(off, e2, axis=1) + r2
    bidx = jnp.arange(B, dtype=i32)[:, None]
    tok = jnp.broadcast_to(jnp.arange(T, dtype=i32)[None], (B, T))
    src = jnp.full((B, rb), -1, i32).at[bidx, d1].set(tok).at[bidx, d2].set(tok)
    p_rows = jnp.zeros((B, rb), F32).at[bidx, d1].set(col(R_P1)).at[bidx, d2].set(col(R_P2))
    tile_start = jnp.arange(nti, dtype=i32) * tm
    tile_expert = jnp.minimum((tile_start[None, :, None] >= ends[:, None, :]).sum(-1), E - 1).astype(i32)
    src_t = src.reshape(B, nti, tm)
    lo = jnp.where(src_t >= 0, src_t, T).min(axis=-1)
    hi = src_t.max(axis=-1)
    lo_blk = jnp.where(hi >= 0, lo // n, 0).astype(i32)
    n_blk = jnp.where(hi >= 0, hi // n - lo // n + 1, 0).astype(i32)
    x = _dispatch(hn.reshape(B, T, D), src, lo_blk.reshape(-1), n_blk.reshape(-1), tm=tm)
    y = _expert_ffn(x, tile_expert.reshape(-1), n_blk.reshape(-1), p_rows.reshape(B * rb, 1),
                    wg, wu, wd, tm=tm)
    cnt_end = jnp.concatenate([cntb[:, 1:], counts[:, None, :]], axis=1)
    first = off[:, None, :] + cntb
    last = off[:, None, :] + cnt_end - 1
    used = cnt_end > cntb
    fb = jnp.minimum(first // tm, nti - 1)
    lb = jnp.minimum(jnp.maximum(last, first) // tm, nti - 1)
    blk = jnp.stack([fb, lb], axis=-1) + (jnp.arange(B, dtype=i32) * nti)[:, None, None, None]
    valid = jnp.stack([used, used & (lb > fb)], axis=-1)
    row0 = (jnp.arange(B, dtype=i32) * rb)[:, None]
    return _combine(h, y, (d1 + row0).reshape(M, 1), (d2 + row0).reshape(M, 1),
                    blk.reshape(-1).astype(i32), valid.reshape(-1).astype(i32), B, T, tm=tm)


def _merge_kernel(hn_ref, b0_ref, b1_ref, b2_ref, b3_ref, wg_ref, wb_ref, o_ref):
    hn = hn_ref[...]
    acc = None
    for b, br_ref in enumerate((b0_ref, b1_ref, b2_ref, b3_ref)):
        gate = jax.nn.sigmoid(_dot(hn, wg_ref[b]))
        term = gate * _dot(br_ref[...], wb_ref[b])
        acc = term if acc is None else acc + term
    o_ref[...] = acc.astype(o_ref.dtype)


def _merge(hn, branches, wg, wb, *, tn=512, tm=MM_TM):
    M, D = hn.shape
    N = wg.shape[2]
    bspec = pl.BlockSpec((tm, BRANCH_W), lambda j, i: (i, 0))
    return pl.pallas_call(
        _merge_kernel,
        out_shape=jax.ShapeDtypeStruct((M, N), BF16),
        grid=(N // tn, M // tm),
        in_specs=[pl.BlockSpec((tm, D), lambda j, i: (i, 0)), bspec, bspec, bspec, bspec,
                  pl.BlockSpec((N_BRANCH, D, tn), lambda j, i: (0, 0, j)),
                  pl.BlockSpec((N_BRANCH, BRANCH_W, tn), lambda j, i: (0, 0, j))],
        out_specs=pl.BlockSpec((tm, tn), lambda j, i: (i, j)),
        compiler_params=_cparams(("parallel", "parallel")),
        name="gate_merge",
    )(hn, *branches, wg, wb)


def _causal_conv(x, carry_ref, cw):
    n = x.shape[0]
    xext = jnp.concatenate([carry_ref[...], x], axis=0)
    y = cw[0:1] * xext[5:5 + n]
    for i in range(1, CONV_K):
        y = y + cw[i:i + 1] * xext[5 + i:5 + i + n]
    carry_ref[...] = x[n - 8:n]
    return y


def _tile_masks(n, chunk):
    row = lax.broadcasted_iota(jnp.int32, (n, n), 0)
    col = lax.broadcasted_iota(jnp.int32, (n, n), 1)
    if chunk == n:
        return col <= row, col < row
    in_chunk = col >= (row // chunk) * chunk
    return in_chunk & (col <= row), in_chunk & (col < row)


def _gdn_kernel(qkv_ref, z_ref, sm_ref, cw_ref, alog_ref, dtb_ref, ng_ref, o_ref, s_ref, carry_ref):
    @pl.when(pl.program_id(1) == 0)
    def _():
        s_ref[...] = jnp.zeros_like(s_ref)
        carry_ref[...] = jnp.zeros_like(carry_ref)

    n = ROW_TILE
    y = _silu(_causal_conv(qkv_ref[0], carry_ref, cw_ref[...]))
    sm = sm_ref[0]
    z = z_ref[0]
    beta_all = jax.nn.sigmoid(sm)
    g_all = -jnp.exp(alog_ref[...]) * _softplus(sm + dtb_ref[...])
    row = lax.broadcasted_iota(jnp.int32, (n, n), 0)
    col = lax.broadcasted_iota(jnp.int32, (n, n), 1)
    causal, strict = col <= row, col < row
    diff_bits = row ^ col
    levels = int(math.log2(n))
    level_masks = [(diff_bits >= (1 << l)) & (diff_bits < (2 << l)) for l in range(levels)]
    eye = jnp.where(row == col, 1.0, 0.0)
    gcs_all = _dot(causal.astype(F32), g_all, HI)
    gcs_t = gcs_all.T
    outs = []
    for h in range(GDN_HEADS):
        q = y[:, h * GDN_DK:(h + 1) * GDN_DK]
        k = y[:, GDN_HEADS * GDN_DK + h * GDN_DK:GDN_HEADS * GDN_DK + (h + 1) * GDN_DK]
        v = y[:, 2 * GDN_HEADS * GDN_DK + h * GDN_DV:2 * GDN_HEADS * GDN_DK + (h + 1) * GDN_DV]
        q = q * lax.rsqrt(jnp.sum(q * q, axis=-1, keepdims=True) + EPS) * GDN_DK ** -0.5
        k = k * lax.rsqrt(jnp.sum(k * k, axis=-1, keepdims=True) + EPS)
        beta = beta_all[:, L_BETA + h:L_BETA + h + 1]
        gc = gcs_all[:, L_GA + h:L_GA + h + 1]
        gr = gcs_t[L_GA + h:L_GA + h + 1, :]
        decay = jnp.exp(jnp.where(causal, gc - gr, NEG))
        kb = k * beta
        k16 = k.astype(BF16)
        a = jnp.where(strict, _dot_nt(kb.astype(BF16), k16) * decay, 0.0)
        t = eye - jnp.where(level_masks[0], a, 0.0)
        for l in range(1, levels):
            t16 = t.astype(BF16)
            a_l = jnp.where(level_masks[l], a, 0.0).astype(BF16)
            t = t - _dot(_dot(t16, a_l).astype(BF16), t16)
        egc = jnp.exp(gc)
        rhs = jnp.concatenate([v * beta, kb * egc], axis=1)
        uw = _dot(t.astype(BF16), rhs.astype(BF16))
        u, w = uw[:, :GDN_DV], uw[:, GDN_DV:]
        att = _dot_nt(q.astype(BF16), k16) * decay
        g_last = gc[n - 1:n, :]
        kd = k * jnp.exp(g_last - gc)
        s = s_ref[h]
        ws = _dot(jnp.concatenate([w, q * egc], axis=0).astype(BF16), s.astype(BF16))
        v_new = (u - ws[:n]).astype(BF16)
        o = ws[n:] + _dot(att.astype(BF16), v_new)
        s_ref[h] = s * jnp.exp(g_last) + _dot(kd.T.astype(BF16), v_new)
        o = o * lax.rsqrt(jnp.mean(o * o, axis=-1, keepdims=True) + EPS) * ng_ref[...]
        outs.append(o * _silu(z[:, h * GDN_DV:(h + 1) * GDN_DV]))
    o_ref[0] = jnp.concatenate(outs, axis=1).astype(o_ref.dtype)


def _lane_vec(vals, lane0):
    v = jnp.zeros((1, LANES), F32)
    return v.at[0, lane0:lane0 + vals.shape[0]].set(vals.astype(F32))


def _gdn(proj, conv_w, a_log, dt_bias, norm_g):
    B, T, _ = proj.shape
    n = ROW_TILE
    W = 2 * GDN_HEADS * GDN_DK + GDN_HEADS * GDN_DV
    ZW = GDN_HEADS * GDN_DV
    return pl.pallas_call(
        _gdn_kernel,
        out_shape=jax.ShapeDtypeStruct((B, T, ZW), BF16),
        grid=(B, T // n),
        in_specs=[pl.BlockSpec((1, n, W), lambda b, t: (b, t, COL_GDN_QKV // W)),
                  pl.BlockSpec((1, n, ZW), lambda b, t: (b, t, COL_GDN_Z // ZW)),
                  pl.BlockSpec((1, n, LANES), lambda b, t: (b, t, COL_SMALL // LANES)),
                  pl.BlockSpec((CONV_K, W), lambda b, t: (0, 0)),
                  pl.BlockSpec((1, LANES), lambda b, t: (0, 0)),
                  pl.BlockSpec((1, LANES), lambda b, t: (0, 0)),
                  pl.BlockSpec((1, GDN_DV), lambda b, t: (0, 0))],
        out_specs=pl.BlockSpec((1, n, ZW), lambda b, t: (b, t, 0)),
        scratch_shapes=[pltpu.VMEM((GDN_HEADS, GDN_DK, GDN_DV), F32),
                        pltpu.VMEM((8, W), F32)],
        compiler_params=_cparams(("arbitrary", "arbitrary")),
        name="gdn",
    )(proj, proj, proj, conv_w.astype(F32), _lane_vec(a_log, L_GA), _lane_vec(dt_bias, L_GA),
      norm_g.reshape(1, GDN_DV).astype(F32))


def _ssd_kernel(xbc_ref, z_ref, sm_ref, cw_ref, cb_ref, alog_ref, dtb_ref, dvec_ref, ng_ref,
                o_ref, hs_ref, carry_ref):
    @pl.when(pl.program_id(1) == 0)
    def _():
        hs_ref[...] = jnp.zeros_like(hs_ref)
        carry_ref[...] = jnp.zeros_like(carry_ref)

    n, P, N = ROW_TILE, SSD_HEADDIM, SSD_STATE
    y = _silu(_causal_conv(xbc_ref[0], carry_ref, cw_ref[...]) + cb_ref[...])
    xs = y[:, :SSD_INNER]
    bm = y[:, SSD_INNER:SSD_INNER + SSD_GROUPS * N]
    cm = y[:, SSD_INNER + SSD_GROUPS * N:]
    sm = sm_ref[0]
    dt_all = _softplus(sm + dtb_ref[...])
    a_all = dt_all * (-jnp.exp(alog_ref[...]))
    causal, _ = _tile_masks(n, n)
    acs_all = _dot(causal.astype(F32), a_all, HI)
    acs_t = acs_all.T
    ys = []
    for g in range(SSD_GROUPS):
        bg = bm[:, g * N:(g + 1) * N]
        cg16 = cm[:, g * N:(g + 1) * N].astype(BF16)
        cb = _dot_nt(cg16, bg.astype(BF16))
        bgt16 = bg.T.astype(BF16)
        for j in range(SSD_HG):
            hh = g * SSD_HG + j
            ac = acs_all[:, L_DT + hh:L_DT + hh + 1]
            ar = acs_t[L_DT + hh:L_DT + hh + 1, :]
            lmat = jnp.exp(jnp.where(causal, ac - ar, NEG))
            xdt = xs[:, hh * P:(hh + 1) * P] * dt_all[:, L_DT + hh:L_DT + hh + 1]
            y_diag = _dot((cb * lmat).astype(BF16), xdt.astype(BF16))
            a_last = ac[n - 1:n, :]
            st = _dot(bgt16, (xdt * jnp.exp(a_last - ac)).astype(BF16))
            h_prev = hs_ref[hh]
            y_off = _dot(cg16, h_prev.astype(BF16)) * jnp.exp(ac)
            hs_ref[hh] = h_prev * jnp.exp(a_last) + st
            ys.append(y_diag + y_off)
    yy = jnp.concatenate(ys, axis=1) + xs * dvec_ref[...]
    yy = yy * _silu(z_ref[0])
    gw = SSD_HG * P
    outs = []
    for g in range(SSD_GROUPS):
        seg = yy[:, g * gw:(g + 1) * gw]
        outs.append(seg * lax.rsqrt(jnp.mean(seg * seg, axis=-1, keepdims=True) + EPS)
                    * ng_ref[:, g * gw:(g + 1) * gw])
    o_ref[0] = jnp.concatenate(outs, axis=1).astype(o_ref.dtype)


def _ssd(proj, conv_w, conv_b, dt_bias, a_log, d_skip, norm_g):
    B, T, _ = proj.shape
    n = ROW_TILE
    W = SSD_INNER + 2 * SSD_GROUPS * SSD_STATE
    dvec = jnp.repeat(d_skip.astype(F32), SSD_HEADDIM).reshape(1, SSD_INNER)
    return pl.pallas_call(
        _ssd_kernel,
        out_shape=jax.ShapeDtypeStruct((B, T, SSD_INNER), BF16),
        grid=(B, T // n),
        in_specs=[pl.BlockSpec((1, n, W), lambda b, t: (b, t, COL_SSD_XBC // W)),
                  pl.BlockSpec((1, n, SSD_INNER), lambda b, t: (b, t, COL_SSD_Z // SSD_INNER)),
                  pl.BlockSpec((1, n, LANES), lambda b, t: (b, t, COL_SMALL // LANES)),
                  pl.BlockSpec((CONV_K, W), lambda b, t: (0, 0)),
                  pl.BlockSpec((1, W), lambda b, t: (0, 0)),
                  pl.BlockSpec((1, LANES), lambda b, t: (0, 0)),
                  pl.BlockSpec((1, LANES), lambda b, t: (0, 0)),
                  pl.BlockSpec((1, SSD_INNER), lambda b, t: (0, 0)),
                  pl.BlockSpec((1, SSD_INNER), lambda b, t: (0, 0))],
        out_specs=pl.BlockSpec((1, n, SSD_INNER), lambda b, t: (b, t, 0)),
        scratch_shapes=[pltpu.VMEM((SSD_HEADS, SSD_STATE, SSD_HEADDIM), F32),
                        pltpu.VMEM((8, W), F32)],
        compiler_params=_cparams(("arbitrary", "arbitrary")),
        name="ssd",
    )(proj, proj, proj, conv_w.astype(F32), conv_b.reshape(1, W).astype(F32),
      _lane_vec(a_log, L_DT), _lane_vec(dt_bias, L_DT), dvec,
      norm_g.reshape(1, SSD_INNER).astype(F32))


def _ones_lane0(n):
    lane = lax.broadcasted_iota(jnp.int32, (n, LANES), 1)
    return jnp.where(lane == 0, 1.0, 0.0)


def _fox_prep_kernel(qkv_ref, sm_ref, qg_ref, kg_ref, bf_ref, q_ref, k_ref, v_ref, run_ref):
    @pl.when(pl.program_id(1) == 0)
    def _():
        run_ref[...] = jnp.zeros_like(run_ref)

    n, dh, H = ROW_TILE, FOX_DH, FOX_HEADS
    x = qkv_ref[0]
    log_f = -_softplus(-(sm_ref[0] + bf_ref[...]))
    causal, _ = _tile_masks(n, n)
    cum = _dot(causal.astype(F32), log_f, HI) + run_ref[0:1, :]
    run_ref[...] = jnp.broadcast_to(cum[n - 1:n, :], run_ref.shape)
    lane = lax.broadcasted_iota(jnp.int32, (n, LANES), 1)
    ones = _ones_lane0(n)
    q_bias = jnp.where(lane < 3, 1.0, 0.0)
    qs, ks, vs = [], [], []
    for h in range(H):
        q = x[:, h * dh:(h + 1) * dh]
        k = x[:, H * dh + h * dh:H * dh + (h + 1) * dh]
        qs.append(q * lax.rsqrt(jnp.mean(q * q, axis=-1, keepdims=True) + EPS)
                  * (qg_ref[...] * (dh ** -0.5 * LOG2E)))
        qs.append(q_bias)
        ks.append(k * lax.rsqrt(jnp.mean(k * k, axis=-1, keepdims=True) + EPS) * kg_ref[...])
        c = cum[:, L_FF + h:L_FF + h + 1] * (-LOG2E)
        c_hi = c.astype(BF16).astype(F32)
        c_mid = (c - c_hi).astype(BF16).astype(F32)
        c_lo = c - c_hi - c_mid
        ks.append(jnp.where(lane == 0, c_hi, jnp.where(lane == 1, c_mid, jnp.where(lane == 2, c_lo, 0.0))))
        vs.append(x[:, 2 * H * dh + h * dh:2 * H * dh + (h + 1) * dh])
        vs.append(ones)
    q_ref[0] = jnp.concatenate(qs, axis=1).astype(q_ref.dtype)
    k_ref[0] = jnp.concatenate(ks, axis=1).astype(k_ref.dtype)
    v_ref[0] = jnp.concatenate(vs, axis=1).astype(v_ref.dtype)


def _fox_prep(proj, qn_g, kn_g, b_f):
    B, T, _ = proj.shape
    n = ROW_TILE
    W = 3 * FOX_HEADS * FOX_DH
    HW = FOX_HEADS * 2 * LANES
    ospec = pl.BlockSpec((1, n, HW), lambda b, t: (b, t, 0))
    return pl.pallas_call(
        _fox_prep_kernel,
        out_shape=(jax.ShapeDtypeStruct((B, T, HW), BF16),) * 3,
        grid=(B, T // n),
        in_specs=[pl.BlockSpec((1, n, W), lambda b, t: (b, t, COL_FOX_QKV // W)),
                  pl.BlockSpec((1, n, LANES), lambda b, t: (b, t, COL_SMALL // LANES)),
                  pl.BlockSpec((1, FOX_DH), lambda b, t: (0, 0)),
                  pl.BlockSpec((1, FOX_DH), lambda b, t: (0, 0)),
                  pl.BlockSpec((1, LANES), lambda b, t: (0, 0))],
        out_specs=(ospec, ospec, ospec),
        scratch_shapes=[pltpu.VMEM((8, LANES), F32)],
        compiler_params=_cparams(("arbitrary", "arbitrary")),
        name="fox_prep",
    )(proj, proj, qn_g.reshape(1, FOX_DH).astype(F32), kn_g.reshape(1, FOX_DH).astype(F32),
      _lane_vec(b_f, L_FF))


def _mla_prep_kernel(qa_ref, kva_ref, sm_ref, cos_ref, sin_ref, qag_ref, wq_ref, kvag_ref, wkv_ref,
                     qgn_ref, qgr_ref, kgn_ref, kgr_ref, rot_ref, exp_ref, q_ref, k_ref, v_ref):
    n, H, dn, dr = ROW_TILE, MLA_HEADS, MLA_NOPE, MLA_ROPE
    qa = qa_ref[0]
    qa = qa * lax.rsqrt(jnp.mean(qa * qa, axis=-1, keepdims=True) + EPS) * qag_ref[...]
    qq = _dot(qa.astype(BF16), wq_ref[...])
    kva = kva_ref[0]
    kva = kva * lax.rsqrt(jnp.mean(kva * kva, axis=-1, keepdims=True) + EPS) * kvag_ref[...]
    kv = _dot(kva.astype(BF16), wkv_ref[...])
    sm = sm_ref[0]
    lane = lax.broadcasted_iota(jnp.int32, (n, LANES), 1)
    is_kpe = (lane >= L_KPE) & (lane < L_KPE + dr)
    kpe_ss = jnp.sum(jnp.where(is_kpe, sm * sm, 0.0), axis=-1, keepdims=True)
    kpe4 = _dot(sm, exp_ref[...], HI)
    qr = qq[:, H * dn:]
    grp = lax.broadcasted_iota(jnp.int32, (n, H * dr), 1) // dr
    q_rs, k_rs = [], []
    for h in range(H):
        qn = qq[:, h * dn:(h + 1) * dn]
        ssr = jnp.sum(jnp.where(grp == h, qr * qr, 0.0), axis=-1, keepdims=True)
        q_rs.append(lax.rsqrt((jnp.sum(qn * qn, axis=-1, keepdims=True) + ssr) / MLA_DQK + EPS))
        kn = kv[:, h * dn:(h + 1) * dn]
        k_rs.append(lax.rsqrt((jnp.sum(kn * kn, axis=-1, keepdims=True) + kpe_ss) / MLA_DQK + EPS))

    def per_group(vals):
        out = vals[H - 1]
        for h in range(H - 2, -1, -1):
            out = jnp.where(grp == h, vals[h], out)
        return out

    cos, sin, rot = cos_ref[...], sin_ref[...], rot_ref[...]
    tq = qr * per_group(q_rs) * qgr_ref[...]
    tq = tq * cos + _dot(tq, rot, HI) * sin
    tk = kpe4 * per_group(k_rs) * kgr_ref[...]
    tk = tk * cos + _dot(tk, rot, HI) * sin
    scale = MLA_DQK ** -0.5 * LOG2E
    half_id = lane // dr
    ones = _ones_lane0(n)
    q_parts, k_parts, v_parts = [], [], []
    for h in range(H):
        blk = slice((h // 2) * LANES, (h // 2 + 1) * LANES)
        q_parts.append(qq[:, h * dn:(h + 1) * dn] * q_rs[h] * (qgn_ref[...] * scale))
        q_parts.append(jnp.where(half_id == h % 2, tq[:, blk] * scale, 0.0))
        k_parts.append(kv[:, h * dn:(h + 1) * dn] * k_rs[h] * kgn_ref[...])
        k_parts.append(tk[:, blk])
        v_parts.append(kv[:, H * dn + h * MLA_V:H * dn + (h + 1) * MLA_V])
        v_parts.append(ones)
    q_ref[0] = jnp.concatenate(q_parts, axis=1).astype(q_ref.dtype)
    k_ref[0] = jnp.concatenate(k_parts, axis=1).astype(k_ref.dtype)
    v_ref[0] = jnp.concatenate(v_parts, axis=1).astype(v_ref.dtype)


def _rope_consts(T):
    H, dr = MLA_HEADS, MLA_ROPE
    inv = 1.0 / (ROPE_BASE ** (jnp.arange(0, dr, 2, dtype=F32) / dr))
    ang = jnp.arange(T, dtype=F32)[:, None] * inv[None, :]
    ang = jnp.concatenate([ang, ang], axis=-1)
    cos4 = jnp.tile(jnp.cos(ang), (1, H))
    sin4 = jnp.tile(jnp.sin(ang), (1, H))
    rot = np.zeros((H * dr, H * dr), np.float32)
    for h in range(H):
        for c in range(dr // 2):
            rot[h * dr + c + dr // 2, h * dr + c] = -1.0
            rot[h * dr + c, h * dr + c + dr // 2] = 1.0
    expand = np.zeros((LANES, H * dr), np.float32)
    for h in range(H):
        for c in range(dr):
            expand[L_KPE + c, h * dr + c] = 1.0
    return cos4, sin4, jnp.asarray(rot), jnp.asarray(expand)


def _mla_prep(proj, rope, qa_g, wq_b, kva_g, wkv_b, qn_g, kn_g):
    B, T, _ = proj.shape
    n, H, dn, dr, dv = ROW_TILE, MLA_HEADS, MLA_NOPE, MLA_ROPE, MLA_V
    cos4, sin4, rot, expand = rope
    wq = wq_b.reshape(MLA_Q_RANK, H, MLA_DQK)
    wq = jnp.concatenate([wq[:, :, :dn].reshape(MLA_Q_RANK, H * dn),
                          wq[:, :, dn:].reshape(MLA_Q_RANK, H * dr)], axis=1).astype(BF16)
    wkv = wkv_b.reshape(MLA_KV_RANK, H, dn + dv)
    wkv = jnp.concatenate([wkv[:, :, :dn].reshape(MLA_KV_RANK, H * dn),
                           wkv[:, :, dn:].reshape(MLA_KV_RANK, H * dv)], axis=1).astype(BF16)
    qg, kg = qn_g.astype(F32), kn_g.astype(F32)
    const = lambda shape: pl.BlockSpec(shape, lambda b, t: (0,) * len(shape))
    QW = H * 2 * LANES
    return pl.pallas_call(
        _mla_prep_kernel,
        out_shape=(jax.ShapeDtypeStruct((B, T, QW), BF16), jax.ShapeDtypeStruct((B, T, QW), BF16),
                   jax.ShapeDtypeStruct((B, T, QW), BF16)),
        grid=(B, T // n),
        in_specs=[pl.BlockSpec((1, n, MLA_Q_RANK), lambda b, t: (b, t, COL_MLA_QA // MLA_Q_RANK)),
                  pl.BlockSpec((1, n, MLA_KV_RANK), lambda b, t: (b, t, COL_MLA_KVA // MLA_KV_RANK)),
                  pl.BlockSpec((1, n, LANES), lambda b, t: (b, t, COL_SMALL // LANES)),
                  pl.BlockSpec((n, H * dr), lambda b, t: (t, 0)),
                  pl.BlockSpec((n, H * dr), lambda b, t: (t, 0)),
                  const((1, MLA_Q_RANK)), const(wq.shape), const((1, MLA_KV_RANK)), const(wkv.shape),
                  const((1, dn)), const((1, H * dr)), const((1, dn)), const((1, H * dr)),
                  const(rot.shape), const(expand.shape)],
        out_specs=(pl.BlockSpec((1, n, QW), lambda b, t: (b, t, 0)),
                   pl.BlockSpec((1, n, QW), lambda b, t: (b, t, 0)),
                   pl.BlockSpec((1, n, QW), lambda b, t: (b, t, 0))),
        compiler_params=_cparams(("parallel", "parallel")),
        name="mla_prep",
    )(proj, proj, proj, cos4, sin4,
      qa_g.reshape(1, -1).astype(F32), wq, kva_g.reshape(1, -1).astype(F32), wkv,
      qg[:dn].reshape(1, dn), jnp.tile(qg[dn:], H).reshape(1, H * dr),
      kg[:dn].reshape(1, dn), jnp.tile(kg[dn:], H).reshape(1, H * dr), rot, expand)


def _flash_kernel(q_ref, k_ref, v_ref, o_ref, m_ref, acc_ref, *, tq):
    qi = pl.program_id(2)
    nh = m_ref.shape[0]
    dqk, dv2 = q_ref.shape[-1] // nh, v_ref.shape[-1] // nh
    dv = dv2 // 2
    rb = ROW_TILE
    m_ref[...] = jnp.full(m_ref.shape, NEG, F32)
    acc_ref[...] = jnp.zeros(acc_ref.shape, F32)

    def step(j, masked):
        start = pl.multiple_of(j * tq, tq)
        for h in range(nh):
            for r in range(0, tq, rb):
                kw = tq
                s = _dot_nt(q_ref[0, r:r + rb, h * dqk:(h + 1) * dqk],
                            k_ref[0, pl.ds(start, kw), h * dqk:(h + 1) * dqk])
                if masked:
                    row = lax.broadcasted_iota(jnp.int32, s.shape, 0) + r
                    col = lax.broadcasted_iota(jnp.int32, s.shape, 1)
                    s = jnp.where(col <= row, s, NEG)
                lane_max = s[:, :LANES]
                for c in range(1, kw // LANES):
                    lane_max = jnp.maximum(lane_max, s[:, c * LANES:(c + 1) * LANES])
                m_old = m_ref[h, r:r + rb]
                m_new = jnp.maximum(m_old, jnp.max(lane_max, axis=-1, keepdims=True))
                m_ref[h, r:r + rb] = m_new
                p = jnp.exp2(s - m_new).astype(BF16)
                acc_ref[h, r:r + rb] = (jnp.exp2(m_old - m_new) * acc_ref[h, r:r + rb]
                                        + _dot(p, v_ref[0, pl.ds(start, kw), h * dv2:(h + 1) * dv2]))

    def body(jj, carry):
        step(2 * jj, False)
        step(2 * jj + 1, False)
        return carry

    lax.fori_loop(0, qi // 2, body, 0)

    @pl.when(qi % 2 == 1)
    def _():
        step(qi - 1, False)

    step(qi, True)
    outs = []
    for h in range(nh):
        acc = acc_ref[h]
        outs.append(acc[:, :dv] / acc[:, dv:dv + 1])
    o_ref[0] = jnp.concatenate(outs, axis=1).astype(o_ref.dtype)


def _flash(q, k, v):
    B, T, QW = q.shape
    H, nh = MLA_HEADS, FLASH_HEADS_PER_STEP
    dqk, dv2 = QW // H, v.shape[-1] // H
    dv = dv2 // 2
    tq = next(t for t in FLASH_TILES if T % t == 0)
    return pl.pallas_call(
        functools.partial(_flash_kernel, tq=tq),
        out_shape=jax.ShapeDtypeStruct((B, T, H * dv), BF16),
        grid=(B, H // nh, T // tq),
        in_specs=[pl.BlockSpec((1, tq, nh * dqk), lambda b, h, i: (b, i, h)),
                  pl.BlockSpec((1, T, nh * dqk), lambda b, h, i: (b, 0, h)),
                  pl.BlockSpec((1, T, nh * dv2), lambda b, h, i: (b, 0, h))],
        out_specs=pl.BlockSpec((1, tq, nh * dv), lambda b, h, i: (b, i, h)),
        scratch_shapes=[pltpu.VMEM((nh, tq, 1), F32), pltpu.VMEM((nh, tq, dv2), F32)],
        compiler_params=_cparams(("parallel", "parallel", "arbitrary")),
        name="flash",
    )(q, k, v)


def _reorder_w_in(w_in):
    D = w_in.shape[0]
    w = w_in.astype(BF16)
    z = lambda n: jnp.zeros((D, n), BF16)
    small = [w[:, 2048:2052], w[:, 2052:2056], w[:, 4424:4428], w[:, 5964:5972], w[:, 2824:2888]]
    n_small = sum(s.shape[1] for s in small)
    cols = [w[:, 0:2048],
            w[:, 4940:5964],
            w[:, 4428:4940],
            w[:, 2056:2824],
            *small, z(LANES - n_small), z(COL_FOX_QKV - COL_SMALL - LANES),
            w[:, 2888:4424]]
    out = jnp.concatenate(cols, axis=1)
    assert out.shape[1] == PROJ_W
    return out


def _mixer(h, hn, B, T, rope, w_in, gdn_conv_w, gdn_A_log, gdn_dt_bias, gdn_norm_g,
           mla_qa_g, mla_wq_b, mla_kva_g, mla_wkv_b, mla_qn_g, mla_kn_g,
           fox_qn_g, fox_kn_g, fox_b_f,
           ssd_conv_w, ssd_conv_b, ssd_dt_bias, ssd_A_log, ssd_D, ssd_norm_g,
           w_gate, w_branch, w_o):
    M = B * T
    proj = _matmul(hn, _reorder_w_in(w_in), tn=1024, name="in_proj").reshape(B, T, PROJ_W)
    o_gdn = _gdn(proj, gdn_conv_w, gdn_A_log, gdn_dt_bias, gdn_norm_g)
    mq, mk, mv = _mla_prep(proj, rope, mla_qa_g, mla_wq_b, mla_kva_g, mla_wkv_b, mla_qn_g, mla_kn_g)
    o_mla = _flash(mq, mk, mv)
    o_fox = _flash(*_fox_prep(proj, fox_qn_g, fox_kn_g, fox_b_f))
    o_ssd = _ssd(proj, ssd_conv_w, ssd_conv_b, ssd_dt_bias, ssd_A_log, ssd_D, ssd_norm_g)
    branches = [o.reshape(M, BRANCH_W) for o in (o_gdn, o_mla, o_fox, o_ssd)]
    merged = _merge(hn, branches, w_gate.astype(BF16), w_branch.astype(BF16))
    return _matmul(merged, w_o.astype(BF16), tn=w_o.shape[1], residual=h, name="out_proj")


def kernel(x, meta_tokens, mix_norm_g, w_in, gdn_conv_w, gdn_A_log, gdn_dt_bias, gdn_norm_g, mla_qa_g, mla_wq_b, mla_kva_g, mla_wkv_b, mla_qn_g, mla_kn_g, fox_qn_g, fox_kn_g, fox_b_f, ssd_conv_w, ssd_conv_b, ssd_dt_bias, ssd_A_log, ssd_D, ssd_norm_g, w_gate, w_branch, w_o, ffn_norm_g, dense_w_gate, dense_w_up, dense_w_down, router_w, moe_w_gate, moe_w_up, moe_w_down):
    B, S, D = x.shape
    L = N_META + S
    T = -(-L // ROW_TILE) * ROW_TILE
    assert (B * T) % MM_TM == 0
    depth = w_in.shape[0]
    meta = jnp.broadcast_to(meta_tokens[None].astype(x.dtype), (B, N_META, D))
    h = jnp.concatenate([meta, x, jnp.zeros((B, T - L, D), x.dtype)], axis=1).reshape(B * T, D)
    rope = _rope_consts(T)
    for layer in range(depth):
        hn = _rmsnorm(h, mix_norm_g[layer])
        h = _mixer(h, hn, B, T, rope, w_in[layer],
                   gdn_conv_w[layer], gdn_A_log[layer], gdn_dt_bias[layer], gdn_norm_g[layer],
                   mla_qa_g[layer], mla_wq_b[layer], mla_kva_g[layer], mla_wkv_b[layer],
                   mla_qn_g[layer], mla_kn_g[layer],
                   fox_qn_g[layer], fox_kn_g[layer], fox_b_f[layer],
                   ssd_conv_w[layer], ssd_conv_b[layer], ssd_dt_bias[layer], ssd_A_log[layer],
                   ssd_D[layer], ssd_norm_g[layer],
                   w_gate[layer], w_branch[layer], w_o[layer])
        i = layer // 2
        if layer % 2 == 0:
            hn = _rmsnorm(h, ffn_norm_g[layer])
            act = _swiglu_up(hn, dense_w_gate[i].astype(BF16), dense_w_up[i].astype(BF16), tn=512)
            h = _matmul(act, dense_w_down[i].astype(BF16), tn=512, residual=h, name="ffn_down")
        else:
            hn, rec, cnt_before = _rmsnorm_router(h, ffn_norm_g[layer], router_w[i], B, T)
            h = _moe(h, hn, rec, cnt_before, moe_w_gate[i].astype(BF16), moe_w_up[i].astype(BF16),
                     moe_w_down[i].astype(BF16), B, T)
    return h.reshape(B, T, D)[:, N_META:L].astype(x.dtype)
```

```python
import functools
import math

import numpy as np
import jax
import jax.numpy as jnp
from jax import lax
from jax.experimental import pallas as pl
from jax.experimental.pallas import tpu as pltpu

F32 = jnp.float32
BF16 = jnp.bfloat16
HI = lax.Precision.HIGHEST
NT_DIMS = (((1,), (1,)), ((), ()))

D_MODEL = 2048
N_META = 16
EPS = 1e-6
NEG = -1e30
CONV_K = 4

GDN_HEADS, GDN_DK, GDN_DV = 4, 128, 128
MLA_HEADS, MLA_Q_RANK, MLA_KV_RANK, MLA_NOPE, MLA_ROPE, MLA_V = 4, 512, 256, 128, 64, 128
MLA_DQK = MLA_NOPE + MLA_ROPE
ROPE_BASE = 10000.0
FOX_HEADS, FOX_DH = 4, 128
SSD_HEADS, SSD_HEADDIM, SSD_GROUPS, SSD_STATE = 8, 64, 2, 128
SSD_HG = SSD_HEADS // SSD_GROUPS
SSD_INNER = SSD_HEADS * SSD_HEADDIM
N_BRANCH, BRANCH_W = 4, 512
N_EXPERTS, TOP_K = 8, 2

LANES = 128
ROW_TILE = 256
FLASH_TILES = (768, 256)
FLASH_HEADS_PER_STEP = 2
LOG2E = 1.4426950408889634
MM_TM = 512
MOE_TM = 256
R_E1, R_E2, R_P1, R_P2, R_R1, R_R2 = range(6)
VMEM_LIMIT = 56 * 1024 * 1024

PROJ_W = 6144
COL_GDN_QKV, COL_GDN_Z = 0, 1536
COL_SSD_XBC, COL_SSD_Z = 2048, 3072
COL_MLA_QA, COL_MLA_KVA = 3584, 4096
COL_SMALL = 4352
COL_FOX_QKV = 4608
L_BETA, L_GA, L_FF, L_DT, L_KPE = 0, 4, 8, 12, 20


def _cparams(sem, vmem=VMEM_LIMIT):
    return pltpu.CompilerParams(dimension_semantics=sem, vmem_limit_bytes=vmem)


def _softplus(x):
    return jnp.maximum(x, 0.0) + jnp.log1p(jnp.exp(-jnp.abs(x)))


def _silu(x):
    return x * jax.nn.sigmoid(x)


def _dot(a, b, precision=None):
    return jnp.dot(a, b, preferred_element_type=F32, precision=precision)


def _dot_nt(a, b):
    return lax.dot_general(a, b, NT_DIMS, preferred_element_type=F32)


def _rmsnorm_kernel(h_ref, g_ref, o_ref):
    x = h_ref[...]
    y = x * lax.rsqrt(jnp.mean(x * x, axis=-1, keepdims=True) + EPS) * g_ref[...]
    o_ref[...] = y.astype(o_ref.dtype)


def _rmsnorm(h, g):
    M, D = h.shape
    return pl.pallas_call(
        _rmsnorm_kernel,
        out_shape=jax.ShapeDtypeStruct((M, D), BF16),
        grid=(M // MM_TM,),
        in_specs=[pl.BlockSpec((MM_TM, D), lambda i: (i, 0)),
                  pl.BlockSpec((1, D), lambda i: (0, 0))],
        out_specs=pl.BlockSpec((MM_TM, D), lambda i: (i, 0)),
        compiler_params=_cparams(("parallel",)),
        name="rmsnorm",
    )(h, g.reshape(1, D).astype(F32))


def _rmsnorm_router_kernel(h_ref, g_ref, rw_ref, o_ref, rec_ref, cnt_ref, run_ref):
    @pl.when(pl.program_id(1) == 0)
    def _():
        run_ref[...] = jnp.zeros_like(run_ref)

    x = h_ref[...]
    n = x.shape[0]
    y = x * lax.rsqrt(jnp.mean(x * x, axis=-1, keepdims=True) + EPS) * g_ref[...]
    o_ref[...] = y.astype(o_ref.dtype)
    logits = _dot(y, rw_ref[...], HI)
    lane = lax.broadcasted_iota(jnp.int32, logits.shape, 1)
    logits = jnp.where(lane < N_EXPERTS, logits, NEG)
    m1 = jnp.max(logits, axis=-1, keepdims=True)
    i1 = jnp.min(jnp.where(logits == m1, lane, LANES), axis=-1, keepdims=True)
    rest = jnp.where(lane == i1, NEG, logits)
    m2 = jnp.max(rest, axis=-1, keepdims=True)
    i2 = jnp.min(jnp.where(rest == m2, lane, LANES), axis=-1, keepdims=True)
    e2 = jnp.exp(m2 - m1)
    p1 = 1.0 / (1.0 + e2)
    p2 = e2 * p1
    sel = jnp.where(lane == i1, 1.0, 0.0) + jnp.where(lane == i2, 1.0, 0.0)
    row = lax.broadcasted_iota(jnp.int32, (n, n), 0)
    col = lax.broadcasted_iota(jnp.int32, (n, n), 1)
    earlier = jnp.where(col < row, 1.0, 0.0).astype(BF16)
    run = run_ref[0:1, :]
    before = _dot(earlier, sel.astype(BF16)) + run
    r1 = jnp.sum(jnp.where(lane == i1, before, 0.0), axis=-1, keepdims=True)
    r2 = jnp.sum(jnp.where(lane == i2, before, 0.0), axis=-1, keepdims=True)
    cnt_ref[...] = jnp.broadcast_to(run, cnt_ref.shape)
    run_ref[...] = jnp.broadcast_to(before[n - 1:n] + sel[n - 1:n], run_ref.shape)
    rec = jnp.zeros_like(logits)
    for ln, val in ((R_E1, i1.astype(F32)), (R_E2, i2.astype(F32)), (R_P1, p1), (R_P2, p2), (R_R1, r1), (R_R2, r2)):
        rec = jnp.where(lane == ln, val, rec)
    rec_ref[...] = rec


def _rmsnorm_router(h, g, router_w, B, T):
    M, D = h.shape
    n = ROW_TILE
    NT = T // n
    rw = jnp.zeros((D, LANES), F32).at[:, :N_EXPERTS].set(router_w.astype(F32))
    return pl.pallas_call(
        _rmsnorm_router_kernel,
        out_shape=(jax.ShapeDtypeStruct((M, D), BF16), jax.ShapeDtypeStruct((M, LANES), F32),
                   jax.ShapeDtypeStruct((B * NT * 8, LANES), F32)),
        grid=(B, NT),
        in_specs=[pl.BlockSpec((n, D), lambda b, t: (b * NT + t, 0)),
                  pl.BlockSpec((1, D), lambda b, t: (0, 0)),
                  pl.BlockSpec((D, LANES), lambda b, t: (0, 0))],
        out_specs=(pl.BlockSpec((n, D), lambda b, t: (b * NT + t, 0)),
                   pl.BlockSpec((n, LANES), lambda b, t: (b * NT + t, 0)),
                   pl.BlockSpec((8, LANES), lambda b, t: (b * NT + t, 0))),
        scratch_shapes=[pltpu.VMEM((8, LANES), F32)],
        compiler_params=_cparams(("arbitrary", "arbitrary")),
        name="rmsnorm_router",
    )(h, g.reshape(1, D).astype(F32), rw)


def _mm_kernel(*refs, has_res):
    a_ref, w_ref = refs[:2]
    y = _dot(a_ref[...], w_ref[...])
    if has_res:
        y = y + refs[2][...]
    refs[-1][...] = y.astype(refs[-1].dtype)


def _matmul(a, w, *, tn, tm=MM_TM, residual=None, out_dtype=F32, name="matmul"):
    M, K = a.shape
    N = w.shape[1]
    in_specs = [pl.BlockSpec((tm, K), lambda j, i: (i, 0)),
                pl.BlockSpec((K, tn), lambda j, i: (0, j))]
    args = [a, w]
    if residual is not None:
        in_specs.append(pl.BlockSpec((tm, tn), lambda j, i: (i, j)))
        args.append(residual)
    return pl.pallas_call(
        functools.partial(_mm_kernel, has_res=residual is not None),
        out_shape=jax.ShapeDtypeStruct((M, N), out_dtype),
        grid=(N // tn, M // tm),
        in_specs=in_specs,
        out_specs=pl.BlockSpec((tm, tn), lambda j, i: (i, j)),
        compiler_params=_cparams(("parallel", "parallel")),
        name=name,
    )(*args)


def _swiglu_kernel(a_ref, wg_ref, wu_ref, o_ref):
    a = a_ref[...]
    g = _dot(a, wg_ref[...])
    u = _dot(a, wu_ref[...])
    o_ref[...] = (_silu(g) * u).astype(o_ref.dtype)


def _swiglu_up(a, wg, wu, *, tn, tm=MM_TM):
    M, K = a.shape
    F = wg.shape[1]
    return pl.pallas_call(
        _swiglu_kernel,
        out_shape=jax.ShapeDtypeStruct((M, F), BF16),
        grid=(F // tn, M // tm),
        in_specs=[pl.BlockSpec((tm, K), lambda j, i: (i, 0)),
                  pl.BlockSpec((K, tn), lambda j, i: (0, j)),
                  pl.BlockSpec((K, tn), lambda j, i: (0, j))],
        out_specs=pl.BlockSpec((tm, tn), lambda j, i: (i, j)),
        compiler_params=_cparams(("parallel", "parallel")),
        name="swiglu_up",
    )(a, wg, wu)


def _dispatch_kernel(lo_ref, nb_ref, d1_ref, d2_ref, rec_ref, hn_ref, o_ref, p_ref, acc_ref, pacc_ref, *, wb):
    i = pl.program_id(1)
    idx = pl.program_id(0) * pl.num_programs(1) + i
    tm = o_ref.shape[0]
    rows = i * tm + lax.broadcasted_iota(jnp.int32, (tm, wb), 0)
    lane = lax.broadcasted_iota(jnp.int32, (wb, LANES), 1)
    acc_ref[...] = jnp.zeros_like(acc_ref)
    pacc_ref[...] = jnp.zeros_like(pacc_ref)

    def hi_lo(p):
        hi = p.astype(BF16).astype(F32)
        return jnp.where(lane == 0, hi, jnp.where(lane == 1, p - hi, 0.0)).astype(BF16)

    def body(k, carry):
        start = pl.multiple_of((lo_ref[idx] + k) * wb, wb)
        hit1 = jnp.where(d1_ref[0, :, pl.ds(start, wb)] == rows, 1.0, 0.0)
        hit2 = jnp.where(d2_ref[0, :, pl.ds(start, wb)] == rows, 1.0, 0.0)
        acc_ref[...] += _dot((hit1 + hit2).astype(BF16), hn_ref[0, pl.ds(start, wb), :])
        rec = rec_ref[0, pl.ds(start, wb), :]
        pacc_ref[...] += (_dot(hit1.astype(BF16), hi_lo(rec[:, R_P1:R_P1 + 1]))
                          + _dot(hit2.astype(BF16), hi_lo(rec[:, R_P2:R_P2 + 1])))
        return carry

    lax.fori_loop(0, nb_ref[idx], body, 0)
    o_ref[...] = acc_ref[...].astype(o_ref.dtype)
    p_ref[...] = pacc_ref[...]


def _dispatch(hn3, rec3, d1, d2, lo_blk, n_blk, nti, *, tm, wb=ROW_TILE):
    B, T, D = hn3.shape
    dspec = pl.BlockSpec((1, 1, T), lambda b, i, lo, nb: (b, 0, 0))
    row_spec = lambda w: pl.BlockSpec((tm, w), lambda b, i, lo, nb: (b * nti + i, 0))
    grid_spec = pltpu.PrefetchScalarGridSpec(
        num_scalar_prefetch=2,
        grid=(B, nti),
        in_specs=[dspec, dspec,
                  pl.BlockSpec((1, T, LANES), lambda b, i, lo, nb: (b, 0, 0)),
                  pl.BlockSpec((1, T, D), lambda b, i, lo, nb: (b, 0, 0), pipeline_mode=pl.Buffered(1))],
        out_specs=(row_spec(D), row_spec(LANES)),
        scratch_shapes=[pltpu.VMEM((tm, D), F32), pltpu.VMEM((tm, LANES), F32)])
    return pl.pallas_call(
        functools.partial(_dispatch_kernel, wb=wb),
        out_shape=(jax.ShapeDtypeStruct((B * nti * tm, D), BF16),
                   jax.ShapeDtypeStruct((B * nti * tm, LANES), F32)),
        grid_spec=grid_spec,
        compiler_params=_cparams(("arbitrary", "arbitrary")),
        name="moe_dispatch",
    )(lo_blk, n_blk, d1.reshape(B, 1, T), d2.reshape(B, 1, T), rec3, hn3)


def _expert_up_kernel(te_ref, nb_ref, x_ref, wg_ref, wu_ref, p_ref, o_ref):
    i = pl.program_id(0)

    @pl.when(nb_ref[i] > 0)
    def _():
        x = x_ref[...]
        p = p_ref[:, 0:1] + p_ref[:, 1:2]
        o_ref[...] = (_silu(_dot(x, wg_ref[0])) * _dot(x, wu_ref[0]) * p).astype(o_ref.dtype)

    @pl.when(nb_ref[i] == 0)
    def _():
        o_ref[...] = jnp.zeros_like(o_ref)


def _expert_down_kernel(te_ref, nb_ref, a_ref, wd_ref, o_ref):
    i = pl.program_id(0)

    @pl.when(nb_ref[i] > 0)
    def _():
        o_ref[...] = _dot(a_ref[...], wd_ref[0]).astype(o_ref.dtype)

    @pl.when(nb_ref[i] == 0)
    def _():
        o_ref[...] = jnp.zeros_like(o_ref)


def _expert_ffn(x, p_rows, tile_expert, n_blk, wg, wu, wd, *, tm):
    R, D = x.shape
    E, _, F = wg.shape
    up_spec = pltpu.PrefetchScalarGridSpec(
        num_scalar_prefetch=2,
        grid=(R // tm,),
        in_specs=[pl.BlockSpec((tm, D), lambda i, te, nb: (i, 0)),
                  pl.BlockSpec((1, D, F), lambda i, te, nb: (te[i], 0, 0)),
                  pl.BlockSpec((1, D, F), lambda i, te, nb: (te[i], 0, 0)),
                  pl.BlockSpec((tm, LANES), lambda i, te, nb: (i, 0))],
        out_specs=pl.BlockSpec((tm, F), lambda i, te, nb: (i, 0)))
    act = pl.pallas_call(
        _expert_up_kernel,
        out_shape=jax.ShapeDtypeStruct((R, F), BF16),
        grid_spec=up_spec,
        compiler_params=_cparams(("arbitrary",)),
        name="moe_up",
    )(tile_expert, n_blk, x, wg, wu, p_rows)
    down_spec = pltpu.PrefetchScalarGridSpec(
        num_scalar_prefetch=2,
        grid=(R // tm,),
        in_specs=[pl.BlockSpec((tm, F), lambda i, te, nb: (i, 0)),
                  pl.BlockSpec((1, F, D), lambda i, te, nb: (te[i], 0, 0))],
        out_specs=pl.BlockSpec((tm, D), lambda i, te, nb: (i, 0)))
    return pl.pallas_call(
        _expert_down_kernel,
        out_shape=jax.ShapeDtypeStruct((R, D), BF16),
        grid_spec=down_spec,
        compiler_params=_cparams(("arbitrary",)),
        name="moe_down",
    )(tile_expert, n_blk, act, wd)


def _combine_kernel(yb_ref, yv_ref, h_ref, d1_ref, d2_ref, *refs, tm):
    y_refs, o_ref = refs[:-1], refs[-1]
    n = o_ref.shape[0]
    base = (pl.program_id(0) * pl.num_programs(1) + pl.program_id(1)) * len(y_refs)
    d1, d2 = d1_ref[...], d2_ref[...]
    lane = lax.broadcasted_iota(jnp.int32, (n, tm), 1)
    o_ref[...] = h_ref[...]
    for k, y_ref in enumerate(y_refs):
        @pl.when(yv_ref[base + k] > 0)
        def _():
            rows = yb_ref[base + k] * tm + lane
            hit = jnp.where(d1 == rows, 1.0, 0.0) + jnp.where(d2 == rows, 1.0, 0.0)
            o_ref[...] += _dot(hit.astype(BF16), y_ref[...])


def _combine(h, y, d1, d2, y_blk, y_valid, B, T, *, tm):
    M, D = h.shape
    n = ROW_TILE
    NT = T // n
    slots = y_blk.shape[0] // (B * NT)
    y_specs = [pl.BlockSpec((tm, D), lambda b, t, yb, yv, k=k: (yb[(b * NT + t) * slots + k], 0))
               for k in range(slots)]
    tok = lambda w: pl.BlockSpec((n, w), lambda b, t, yb, yv: (b * NT + t, 0))
    grid_spec = pltpu.PrefetchScalarGridSpec(
        num_scalar_prefetch=2,
        grid=(B, NT),
        in_specs=[tok(D), tok(1), tok(1)] + y_specs,
        out_specs=tok(D))
    return pl.pallas_call(
        functools.partial(_combine_kernel, tm=tm),
        out_shape=jax.ShapeDtypeStruct((M, D), F32),
        grid_spec=grid_spec,
        compiler_params=_cparams(("arbitrary", "arbitrary")),
        name="moe_combine",
    )(y_blk, y_valid, h, d1, d2, *([y] * slots))


def _moe(h, hn, rec, cnt_before, wg, wu, wd, B, T):
    M, D = h.shape
    E, tm, n = N_EXPERTS, MOE_TM, ROW_TILE
    NT = T // n
    rb = TOP_K * T + E * tm
    nti = rb // tm
    i32 = jnp.int32
    col = lambda ln: rec[:, ln].reshape(B, T)
    e1, e2 = col(R_E1).astype(i32), col(R_E2).astype(i32)
    r1, r2 = col(R_R1).astype(i32), col(R_R2).astype(i32)
    cntb = cnt_before.reshape(B, NT, 8, LANES)[:, :, 0, :E].astype(i32)
    oh1, oh2 = jax.nn.one_hot(e1, E, dtype=i32), jax.nn.one_hot(e2, E, dtype=i32)
    counts = (oh1 + oh2).sum(axis=1)
    padded = (counts + tm - 1) // tm * tm
    ends = jnp.cumsum(padded, axis=1)
    off = ends - padded
    d1 = (oh1 * off[:, None, :]).sum(-1) + r1
    d2 = (oh2 * off[:, None, :]).sum(-1) + r2
    cnt_end = jnp.concatenate([cntb[:, 1:], counts[:, None, :]], axis=1)
    tile_start = jnp.arange(nti, dtype=i32)[None, :] * tm
    tile_expert = jnp.minimum((tile_start[:, :, None] >= ends[:, None, :]).sum(-1), E - 1).astype(i32)
    te_oh = jax.nn.one_hot(tile_expert, E, dtype=i32)
    pick = lambda per_expert: (te_oh * per_expert[:, None, :]).sum(-1)
    rank_lo = tile_start - pick(off)
    rank_hi = jnp.minimum(rank_lo + tm, pick(counts)) - 1
    through = (te_oh[:, :, None, :] * cnt_end[:, None, :, :]).sum(-1)
    lo_blk = (through <= rank_lo[..., None]).sum(-1)
    hi_blk = (through <= rank_hi[..., None]).sum(-1)
    nonempty = (tile_start < pick(ends)) & (rank_hi >= rank_lo)
    lo_blk = jnp.where(nonempty, lo_blk, 0).astype(i32)
    n_blk = jnp.where(nonempty, hi_blk - lo_blk + 1, 0).astype(i32)
    x, p_rows = _dispatch(hn.reshape(B, T, D), rec.reshape(B, T, LANES), d1, d2,
                          lo_blk.reshape(-1), n_blk.reshape(-1), nti, tm=tm)
    y = _expert_ffn(x, p_rows, tile_expert.reshape(-1), n_blk.reshape(-1), wg, wu, wd, tm=tm)
    first = off[:, None, :] + cntb
    last = off[:, None, :] + cnt_end - 1
    used = cnt_end > cntb
    fb = jnp.minimum(first // tm, nti - 1)
    lb = jnp.minimum(jnp.maximum(last, first) // tm, nti - 1)
    blk = jnp.stack([fb, lb], axis=-1) + (jnp.arange(B, dtype=i32) * nti)[:, None, None, None]
    valid = jnp.stack([used, used & (lb > fb)], axis=-1)
    row0 = (jnp.arange(B, dtype=i32) * rb)[:, None]
    return _combine(h, y, (d1 + row0).reshape(M, 1), (d2 + row0).reshape(M, 1),
                    blk.reshape(-1).astype(i32), valid.reshape(-1).astype(i32), B, T, tm=tm)


def _merge_kernel(hn_ref, b0_ref, b1_ref, b2_ref, b3_ref, wg_ref, wb_ref, o_ref):
    hn = hn_ref[...]
    acc = None
    for b, br_ref in enumerate((b0_ref, b1_ref, b2_ref, b3_ref)):
        gate = jax.nn.sigmoid(_dot(hn, wg_ref[b]))
        term = gate * _dot(br_ref[...], wb_ref[b])
        acc = term if acc is None else acc + term
    o_ref[...] = acc.astype(o_ref.dtype)


def _merge(hn, branches, wg, wb, *, tn=512, tm=MM_TM):
    M, D = hn.shape
    N = wg.shape[2]
    bspec = pl.BlockSpec((tm, BRANCH_W), lambda j, i: (i, 0))
    return pl.pallas_call(
        _merge_kernel,
        out_shape=jax.ShapeDtypeStruct((M, N), BF16),
        grid=(N // tn, M // tm),
        in_specs=[pl.BlockSpec((tm, D), lambda j, i: (i, 0)), bspec, bspec, bspec, bspec,
                  pl.BlockSpec((N_BRANCH, D, tn), lambda j, i: (0, 0, j)),
                  pl.BlockSpec((N_BRANCH, BRANCH_W, tn), lambda j, i: (0, 0, j))],
        out_specs=pl.BlockSpec((tm, tn), lambda j, i: (i, j)),
        compiler_params=_cparams(("parallel", "parallel")),
        name="gate_merge",
    )(hn, *branches, wg, wb)


def _causal_conv(x, carry_ref, cw):
    n = x.shape[0]
    xext = jnp.concatenate([carry_ref[...], x], axis=0)
    y = cw[0:1] * xext[5:5 + n]
    for i in range(1, CONV_K):
        y = y + cw[i:i + 1] * xext[5 + i:5 + i + n]
    carry_ref[...] = x[n - 8:n]
    return y


def _tile_masks(n, chunk):
    row = lax.broadcasted_iota(jnp.int32, (n, n), 0)
    col = lax.broadcasted_iota(jnp.int32, (n, n), 1)
    if chunk == n:
        return col <= row, col < row
    in_chunk = col >= (row // chunk) * chunk
    return in_chunk & (col <= row), in_chunk & (col < row)


def _gdn_kernel(qkv_ref, z_ref, sm_ref, cw_ref, alog_ref, dtb_ref, ng_ref, o_ref, s_ref, carry_ref):
    @pl.when(pl.program_id(1) == 0)
    def _():
        s_ref[...] = jnp.zeros_like(s_ref)
        carry_ref[...] = jnp.zeros_like(carry_ref)

    n = ROW_TILE
    y = _silu(_causal_conv(qkv_ref[0], carry_ref, cw_ref[...]))
    sm = sm_ref[0]
    z = z_ref[0]
    beta_all = jax.nn.sigmoid(sm)
    g_all = -jnp.exp(alog_ref[...]) * _softplus(sm + dtb_ref[...])
    row = lax.broadcasted_iota(jnp.int32, (n, n), 0)
    col = lax.broadcasted_iota(jnp.int32, (n, n), 1)
    causal, strict = col <= row, col < row
    diff_bits = row ^ col
    levels = int(math.log2(n))
    level_masks = [(diff_bits >= (1 << l)) & (diff_bits < (2 << l)) for l in range(levels)]
    eye = jnp.where(row == col, 1.0, 0.0)
    gcs_all = _dot(causal.astype(F32), g_all, HI)
    gcs_t = gcs_all.T
    heads = []
    for h in range(GDN_HEADS):
        q = y[:, h * GDN_DK:(h + 1) * GDN_DK]
        k = y[:, GDN_HEADS * GDN_DK + h * GDN_DK:GDN_HEADS * GDN_DK + (h + 1) * GDN_DK]
        v = y[:, 2 * GDN_HEADS * GDN_DK + h * GDN_DV:2 * GDN_HEADS * GDN_DK + (h + 1) * GDN_DV]
        q = q * lax.rsqrt(jnp.sum(q * q, axis=-1, keepdims=True) + EPS) * GDN_DK ** -0.5
        k = k * lax.rsqrt(jnp.sum(k * k, axis=-1, keepdims=True) + EPS)
        beta = beta_all[:, L_BETA + h:L_BETA + h + 1]
        gc = gcs_all[:, L_GA + h:L_GA + h + 1]
        gr = gcs_t[L_GA + h:L_GA + h + 1, :]
        decay = jnp.exp(jnp.where(causal, gc - gr, NEG))
        kb = k * beta
        k16 = k.astype(BF16)
        a = jnp.where(strict, _dot_nt(kb.astype(BF16), k16) * decay, 0.0)
        heads.append(dict(q=q, k=k, v=v, beta=beta, gc=gc, decay=decay, kb=kb, k16=k16, a=a,
                          t=eye - jnp.where(level_masks[0], a, 0.0)))
    for l in range(1, levels):
        for hd in heads:
            t16 = hd["t"].astype(BF16)
            a_l = jnp.where(level_masks[l], hd["a"], 0.0).astype(BF16)
            hd["t"] = hd["t"] - _dot(_dot(t16, a_l).astype(BF16), t16)
    outs = []
    for h, hd in enumerate(heads):
        q, k, gc = hd["q"], hd["k"], hd["gc"]
        egc = jnp.exp(gc)
        rhs = jnp.concatenate([hd["v"] * hd["beta"], hd["kb"] * egc], axis=1)
        uw = _dot(hd["t"].astype(BF16), rhs.astype(BF16))
        u, w = uw[:, :GDN_DV], uw[:, GDN_DV:]
        att = _dot_nt(q.astype(BF16), hd["k16"]) * hd["decay"]
        g_last = gc[n - 1:n, :]
        kd = k * jnp.exp(g_last - gc)
        s = s_ref[h]
        ws = _dot(jnp.concatenate([w, q * egc], axis=0).astype(BF16), s.astype(BF16))
        v_new = (u - ws[:n]).astype(BF16)
        o = ws[n:] + _dot(att.astype(BF16), v_new)
        s_ref[h] = s * jnp.exp(g_last) + _dot(kd.T.astype(BF16), v_new)
        o = o * lax.rsqrt(jnp.mean(o * o, axis=-1, keepdims=True) + EPS) * ng_ref[...]
        outs.append(o * _silu(z[:, h * GDN_DV:(h + 1) * GDN_DV]))
    o_ref[0] = jnp.concatenate(outs, axis=1).astype(o_ref.dtype)


def _lane_vec(vals, lane0):
    v = jnp.zeros((1, LANES), F32)
    return v.at[0, lane0:lane0 + vals.shape[0]].set(vals.astype(F32))


def _gdn(proj, conv_w, a_log, dt_bias, norm_g):
    B, T, _ = proj.shape
    n = ROW_TILE
    W = 2 * GDN_HEADS * GDN_DK + GDN_HEADS * GDN_DV
    ZW = GDN_HEADS * GDN_DV
    return pl.pallas_call(
        _gdn_kernel,
        out_shape=jax.ShapeDtypeStruct((B, T, ZW), BF16),
        grid=(B, T // n),
        in_specs=[pl.BlockSpec((1, n, W), lambda b, t: (b, t, COL_GDN_QKV // W)),
                  pl.BlockSpec((1, n, ZW), lambda b, t: (b, t, COL_GDN_Z // ZW)),
                  pl.BlockSpec((1, n, LANES), lambda b, t: (b, t, COL_SMALL // LANES)),
                  pl.BlockSpec((CONV_K, W), lambda b, t: (0, 0)),
                  pl.BlockSpec((1, LANES), lambda b, t: (0, 0)),
                  pl.BlockSpec((1, LANES), lambda b, t: (0, 0)),
                  pl.BlockSpec((1, GDN_DV), lambda b, t: (0, 0))],
        out_specs=pl.BlockSpec((1, n, ZW), lambda b, t: (b, t, 0)),
        scratch_shapes=[pltpu.VMEM((GDN_HEADS, GDN_DK, GDN_DV), F32),
                        pltpu.VMEM((8, W), F32)],
        compiler_params=_cparams(("arbitrary", "arbitrary")),
        name="gdn",
    )(proj, proj, proj, conv_w.astype(F32), _lane_vec(a_log, L_GA), _lane_vec(dt_bias, L_GA),
      norm_g.reshape(1, GDN_DV).astype(F32))


def _ssd_kernel(xbc_ref, z_ref, sm_ref, cw_ref, cb_ref, alog_ref, dtb_ref, dvec_ref, ng_ref,
                o_ref, hs_ref, carry_ref):
    @pl.when(pl.program_id(1) == 0)
    def _():
        hs_ref[...] = jnp.zeros_like(hs_ref)
        carry_ref[...] = jnp.zeros_like(carry_ref)

    n, P, N = ROW_TILE, SSD_HEADDIM, SSD_STATE
    y = _silu(_causal_conv(xbc_ref[0], carry_ref, cw_ref[...]) + cb_ref[...])
    xs = y[:, :SSD_INNER]
    bm = y[:, SSD_INNER:SSD_INNER + SSD_GROUPS * N]
    cm = y[:, SSD_INNER + SSD_GROUPS * N:]
    sm = sm_ref[0]
    dt_all = _softplus(sm + dtb_ref[...])
    a_all = dt_all * (-jnp.exp(alog_ref[...]))
    causal, _ = _tile_masks(n, n)
    acs_all = _dot(causal.astype(F32), a_all, HI)
    acs_t = acs_all.T
    ys = []
    for g in range(SSD_GROUPS):
        bg = bm[:, g * N:(g + 1) * N]
        cg16 = cm[:, g * N:(g + 1) * N].astype(BF16)
        cb = _dot_nt(cg16, bg.astype(BF16))
        bgt16 = bg.T.astype(BF16)
        for j in range(SSD_HG):
            hh = g * SSD_HG + j
            ac = acs_all[:, L_DT + hh:L_DT + hh + 1]
            ar = acs_t[L_DT + hh:L_DT + hh + 1, :]
            lmat = jnp.exp(jnp.where(causal, ac - ar, NEG))
            xdt = xs[:, hh * P:(hh + 1) * P] * dt_all[:, L_DT + hh:L_DT + hh + 1]
            y_diag = _dot((cb * lmat).astype(BF16), xdt.astype(BF16))
            a_last = ac[n - 1:n, :]
            st = _dot(bgt16, (xdt * jnp.exp(a_last - ac)).astype(BF16))
            h_prev = hs_ref[hh]
            y_off = _dot(cg16, h_prev.astype(BF16)) * jnp.exp(ac)
            hs_ref[hh] = h_prev * jnp.exp(a_last) + st
            ys.append(y_diag + y_off)
    yy = jnp.concatenate(ys, axis=1) + xs * dvec_ref[...]
    yy = yy * _silu(z_ref[0])
    gw = SSD_HG * P
    outs = []
    for g in range(SSD_GROUPS):
        seg = yy[:, g * gw:(g + 1) * gw]
        outs.append(seg * lax.rsqrt(jnp.mean(seg * seg, axis=-1, keepdims=True) + EPS)
                    * ng_ref[:, g * gw:(g + 1) * gw])
    o_ref[0] = jnp.concatenate(outs, axis=1).astype(o_ref.dtype)


def _ssd(proj, conv_w, conv_b, dt_bias, a_log, d_skip, norm_g):
    B, T, _ = proj.shape
    n = ROW_TILE
    W = SSD_INNER + 2 * SSD_GROUPS * SSD_STATE
    dvec = jnp.repeat(d_skip.astype(F32), SSD_HEADDIM).reshape(1, SSD_INNER)
    return pl.pallas_call(
        _ssd_kernel,
        out_shape=jax.ShapeDtypeStruct((B, T, SSD_INNER), BF16),
        grid=(B, T // n),
        in_specs=[pl.BlockSpec((1, n, W), lambda b, t: (b, t, COL_SSD_XBC // W)),
                  pl.BlockSpec((1, n, SSD_INNER), lambda b, t: (b, t, COL_SSD_Z // SSD_INNER)),
                  pl.BlockSpec((1, n, LANES), lambda b, t: (b, t, COL_SMALL // LANES)),
                  pl.BlockSpec((CONV_K, W), lambda b, t: (0, 0)),
                  pl.BlockSpec((1, W), lambda b, t: (0, 0)),
                  pl.BlockSpec((1, LANES), lambda b, t: (0, 0)),
                  pl.BlockSpec((1, LANES), lambda b, t: (0, 0)),
                  pl.BlockSpec((1, SSD_INNER), lambda b, t: (0, 0)),
                  pl.BlockSpec((1, SSD_INNER), lambda b, t: (0, 0))],
        out_specs=pl.BlockSpec((1, n, SSD_INNER), lambda b, t: (b, t, 0)),
        scratch_shapes=[pltpu.VMEM((SSD_HEADS, SSD_STATE, SSD_HEADDIM), F32),
                        pltpu.VMEM((8, W), F32)],
        compiler_params=_cparams(("arbitrary", "arbitrary")),
        name="ssd",
    )(proj, proj, proj, conv_w.astype(F32), conv_b.reshape(1, W).astype(F32),
      _lane_vec(a_log, L_DT), _lane_vec(dt_bias, L_DT), dvec,
      norm_g.reshape(1, SSD_INNER).astype(F32))


def _ones_lane0(n):
    lane = lax.broadcasted_iota(jnp.int32, (n, LANES), 1)
    return jnp.where(lane == 0, 1.0, 0.0)


def _fox_prep_kernel(qkv_ref, sm_ref, qg_ref, kg_ref, bf_ref, q_ref, k_ref, v_ref, run_ref):
    @pl.when(pl.program_id(1) == 0)
    def _():
        run_ref[...] = jnp.zeros_like(run_ref)

    n, dh, H = ROW_TILE, FOX_DH, FOX_HEADS
    x = qkv_ref[0]
    log_f = -_softplus(-(sm_ref[0] + bf_ref[...]))
    causal, _ = _tile_masks(n, n)
    cum = _dot(causal.astype(F32), log_f, HI) + run_ref[0:1, :]
    run_ref[...] = jnp.broadcast_to(cum[n - 1:n, :], run_ref.shape)
    lane = lax.broadcasted_iota(jnp.int32, (n, LANES), 1)
    ones = _ones_lane0(n)
    q_bias = jnp.where(lane < 3, 1.0, 0.0)
    qs, ks, vs = [], [], []
    for h in range(H):
        q = x[:, h * dh:(h + 1) * dh]
        k = x[:, H * dh + h * dh:H * dh + (h + 1) * dh]
        qs.append(q * lax.rsqrt(jnp.mean(q * q, axis=-1, keepdims=True) + EPS)
                  * (qg_ref[...] * (dh ** -0.5 * LOG2E)))
        qs.append(q_bias)
        ks.append(k * lax.rsqrt(jnp.mean(k * k, axis=-1, keepdims=True) + EPS) * kg_ref[...])
        c = cum[:, L_FF + h:L_FF + h + 1] * (-LOG2E)
        c_hi = c.astype(BF16).astype(F32)
        c_mid = (c - c_hi).astype(BF16).astype(F32)
        c_lo = c - c_hi - c_mid
        ks.append(jnp.where(lane == 0, c_hi, jnp.where(lane == 1, c_mid, jnp.where(lane == 2, c_lo, 0.0))))
        vs.append(x[:, 2 * H * dh + h * dh:2 * H * dh + (h + 1) * dh])
        vs.append(ones)
    q_ref[0] = jnp.concatenate(qs, axis=1).astype(q_ref.dtype)
    k_ref[0] = jnp.concatenate(ks, axis=1).astype(k_ref.dtype)
    v_ref[0] = jnp.concatenate(vs, axis=1).astype(v_ref.dtype)


def _fox_prep(proj, qn_g, kn_g, b_f):
    B, T, _ = proj.shape
    n = ROW_TILE
    W = 3 * FOX_HEADS * FOX_DH
    HW = FOX_HEADS * 2 * LANES
    ospec = pl.BlockSpec((1, n, HW), lambda b, t: (b, t, 0))
    return pl.pallas_call(
        _fox_prep_kernel,
        out_shape=(jax.ShapeDtypeStruct((B, T, HW), BF16),) * 3,
        grid=(B, T // n),
        in_specs=[pl.BlockSpec((1, n, W), lambda b, t: (b, t, COL_FOX_QKV // W)),
                  pl.BlockSpec((1, n, LANES), lambda b, t: (b, t, COL_SMALL // LANES)),
                  pl.BlockSpec((1, FOX_DH), lambda b, t: (0, 0)),
                  pl.BlockSpec((1, FOX_DH), lambda b, t: (0, 0)),
                  pl.BlockSpec((1, LANES), lambda b, t: (0, 0))],
        out_specs=(ospec, ospec, ospec),
        scratch_shapes=[pltpu.VMEM((8, LANES), F32)],
        compiler_params=_cparams(("arbitrary", "arbitrary")),
        name="fox_prep",
    )(proj, proj, qn_g.reshape(1, FOX_DH).astype(F32), kn_g.reshape(1, FOX_DH).astype(F32),
      _lane_vec(b_f, L_FF))


def _mla_prep_kernel(qa_ref, kva_ref, sm_ref, cos_ref, sin_ref, qag_ref, wq_ref, kvag_ref, wkv_ref,
                     qgn_ref, qgr_ref, kgn_ref, kgr_ref, rot_ref, exp_ref, q_ref, k_ref, v_ref):
    n, H, dn, dr = ROW_TILE, MLA_HEADS, MLA_NOPE, MLA_ROPE
    qa = qa_ref[0]
    qa = qa * lax.rsqrt(jnp.mean(qa * qa, axis=-1, keepdims=True) + EPS) * qag_ref[...]
    qq = _dot(qa.astype(BF16), wq_ref[...])
    kva = kva_ref[0]
    kva = kva * lax.rsqrt(jnp.mean(kva * kva, axis=-1, keepdims=True) + EPS) * kvag_ref[...]
    kv = _dot(kva.astype(BF16), wkv_ref[...])
    sm = sm_ref[0]
    lane = lax.broadcasted_iota(jnp.int32, (n, LANES), 1)
    is_kpe = (lane >= L_KPE) & (lane < L_KPE + dr)
    kpe_ss = jnp.sum(jnp.where(is_kpe, sm * sm, 0.0), axis=-1, keepdims=True)
    kpe4 = _dot(sm, exp_ref[...], HI)
    qr = qq[:, H * dn:]
    grp = lax.broadcasted_iota(jnp.int32, (n, H * dr), 1) // dr
    q_rs, k_rs = [], []
    for h in range(H):
        qn = qq[:, h * dn:(h + 1) * dn]
        ssr = jnp.sum(jnp.where(grp == h, qr * qr, 0.0), axis=-1, keepdims=True)
        q_rs.append(lax.rsqrt((jnp.sum(qn * qn, axis=-1, keepdims=True) + ssr) / MLA_DQK + EPS))
        kn = kv[:, h * dn:(h + 1) * dn]
        k_rs.append(lax.rsqrt((jnp.sum(kn * kn, axis=-1, keepdims=True) + kpe_ss) / MLA_DQK + EPS))

    def per_group(vals):
        out = vals[H - 1]
        for h in range(H - 2, -1, -1):
            out = jnp.where(grp == h, vals[h], out)
        return out

    cos, sin, rot = cos_ref[...], sin_ref[...], rot_ref[...]
    tq = qr * per_group(q_rs) * qgr_ref[...]
    tq = tq * cos + _dot(tq, rot, HI) * sin
    tk = kpe4 * per_group(k_rs) * kgr_ref[...]
    tk = tk * cos + _dot(tk, rot, HI) * sin
    scale = MLA_DQK ** -0.5 * LOG2E
    half_id = lane // dr
    ones = _ones_lane0(n)
    q_parts, k_parts, v_parts = [], [], []
    for h in range(H):
        blk = slice((h // 2) * LANES, (h // 2 + 1) * LANES)
        q_parts.append(qq[:, h * dn:(h + 1) * dn] * q_rs[h] * (qgn_ref[...] * scale))
        q_parts.append(jnp.where(half_id == h % 2, tq[:, blk] * scale, 0.0))
        k_parts.append(kv[:, h * dn:(h + 1) * dn] * k_rs[h] * kgn_ref[...])
        k_parts.append(tk[:, blk])
        v_parts.append(kv[:, H * dn + h * MLA_V:H * dn + (h + 1) * MLA_V])
        v_parts.append(ones)
    q_ref[0] = jnp.concatenate(q_parts, axis=1).astype(q_ref.dtype)
    k_ref[0] = jnp.concatenate(k_parts, axis=1).astype(k_ref.dtype)
    v_ref[0] = jnp.concatenate(v_parts, axis=1).astype(v_ref.dtype)


def _rope_consts(T):
    H, dr = MLA_HEADS, MLA_ROPE
    inv = 1.0 / (ROPE_BASE ** (jnp.arange(0, dr, 2, dtype=F32) / dr))
    ang = jnp.arange(T, dtype=F32)[:, None] * inv[None, :]
    ang = jnp.concatenate([ang, ang], axis=-1)
    cos4 = jnp.tile(jnp.cos(ang), (1, H))
    sin4 = jnp.tile(jnp.sin(ang), (1, H))
    rot = np.zeros((H * dr, H * dr), np.float32)
    for h in range(H):
        for c in range(dr // 2):
            rot[h * dr + c + dr // 2, h * dr + c] = -1.0
            rot[h * dr + c, h * dr + c + dr // 2] = 1.0
    expand = np.zeros((LANES, H * dr), np.float32)
    for h in range(H):
        for c in range(dr):
            expand[L_KPE + c, h * dr + c] = 1.0
    return cos4, sin4, jnp.asarray(rot), jnp.asarray(expand)


def _mla_prep(proj, rope, qa_g, wq_b, kva_g, wkv_b, qn_g, kn_g):
    B, T, _ = proj.shape
    n, H, dn, dr, dv = ROW_TILE, MLA_HEADS, MLA_NOPE, MLA_ROPE, MLA_V
    cos4, sin4, rot, expand = rope
    wq = wq_b.reshape(MLA_Q_RANK, H, MLA_DQK)
    wq = jnp.concatenate([wq[:, :, :dn].reshape(MLA_Q_RANK, H * dn),
                          wq[:, :, dn:].reshape(MLA_Q_RANK, H * dr)], axis=1).astype(BF16)
    wkv = wkv_b.reshape(MLA_KV_RANK, H, dn + dv)
    wkv = jnp.concatenate([wkv[:, :, :dn].reshape(MLA_KV_RANK, H * dn),
                           wkv[:, :, dn:].reshape(MLA_KV_RANK, H * dv)], axis=1).astype(BF16)
    qg, kg = qn_g.astype(F32), kn_g.astype(F32)
    const = lambda shape: pl.BlockSpec(shape, lambda b, t: (0,) * len(shape))
    QW = H * 2 * LANES
    return pl.pallas_call(
        _mla_prep_kernel,
        out_shape=(jax.ShapeDtypeStruct((B, T, QW), BF16), jax.ShapeDtypeStruct((B, T, QW), BF16),
                   jax.ShapeDtypeStruct((B, T, QW), BF16)),
        grid=(B, T // n),
        in_specs=[pl.BlockSpec((1, n, MLA_Q_RANK), lambda b, t: (b, t, COL_MLA_QA // MLA_Q_RANK)),
                  pl.BlockSpec((1, n, MLA_KV_RANK), lambda b, t: (b, t, COL_MLA_KVA // MLA_KV_RANK)),
                  pl.BlockSpec((1, n, LANES), lambda b, t: (b, t, COL_SMALL // LANES)),
                  pl.BlockSpec((n, H * dr), lambda b, t: (t, 0)),
                  pl.BlockSpec((n, H * dr), lambda b, t: (t, 0)),
                  const((1, MLA_Q_RANK)), const(wq.shape), const((1, MLA_KV_RANK)), const(wkv.shape),
                  const((1, dn)), const((1, H * dr)), const((1, dn)), const((1, H * dr)),
                  const(rot.shape), const(expand.shape)],
        out_specs=(pl.BlockSpec((1, n, QW), lambda b, t: (b, t, 0)),
                   pl.BlockSpec((1, n, QW), lambda b, t: (b, t, 0)),
                   pl.BlockSpec((1, n, QW), lambda b, t: (b, t, 0))),
        compiler_params=_cparams(("parallel", "parallel")),
        name="mla_prep",
    )(proj, proj, proj, cos4, sin4,
      qa_g.reshape(1, -1).astype(F32), wq, kva_g.reshape(1, -1).astype(F32), wkv,
      qg[:dn].reshape(1, dn), jnp.tile(qg[dn:], H).reshape(1, H * dr),
      kg[:dn].reshape(1, dn), jnp.tile(kg[dn:], H).reshape(1, H * dr), rot, expand)


def _flash_kernel(q_ref, k_ref, v_ref, o_ref, m_ref, acc_ref, *, tq):
    qi = pl.program_id(2)
    nh = m_ref.shape[0]
    dqk, dv2 = q_ref.shape[-1] // nh, v_ref.shape[-1] // nh
    dv = dv2 // 2
    rb = ROW_TILE
    m_ref[...] = jnp.full(m_ref.shape, NEG, F32)
    acc_ref[...] = jnp.zeros(acc_ref.shape, F32)

    def step(j, masked):
        start = pl.multiple_of(j * tq, tq)
        chains = [(h, r) for h in range(nh) for r in range(0, tq, rb)]
        logits, probs = {}, {}

        def qk(c):
            h, r = chains[c]
            logits[c] = _dot_nt(q_ref[0, r:r + rb, h * dqk:(h + 1) * dqk],
                                k_ref[0, pl.ds(start, tq), h * dqk:(h + 1) * dqk])

        def softmax(c):
            h, r = chains[c]
            s = logits.pop(c)
            if masked:
                row = lax.broadcasted_iota(jnp.int32, s.shape, 0) + r
                col = lax.broadcasted_iota(jnp.int32, s.shape, 1)
                s = jnp.where(col <= row, s, NEG)
            lane_max = s[:, :LANES]
            for cb in range(1, tq // LANES):
                lane_max = jnp.maximum(lane_max, s[:, cb * LANES:(cb + 1) * LANES])
            m_old = m_ref[h, r:r + rb]
            m_new = jnp.maximum(m_old, jnp.max(lane_max, axis=-1, keepdims=True))
            m_ref[h, r:r + rb] = m_new
            probs[c] = (jnp.exp2(s - m_new).astype(BF16), jnp.exp2(m_old - m_new))

        def pv(c):
            h, r = chains[c]
            p, alpha = probs.pop(c)
            acc_ref[h, r:r + rb] = (alpha * acc_ref[h, r:r + rb]
                                    + _dot(p, v_ref[0, pl.ds(start, tq), h * dv2:(h + 1) * dv2]))

        for c in range(len(chains)):
            qk(c)
            softmax(c)
            pv(c)

    def body(jj, carry):
        step(2 * jj, False)
        step(2 * jj + 1, False)
        return carry

    lax.fori_loop(0, qi // 2, body, 0)

    @pl.when(qi % 2 == 1)
    def _():
        step(qi - 1, False)

    step(qi, True)
    outs = []
    for h in range(nh):
        acc = acc_ref[h]
        outs.append(acc[:, :dv] / acc[:, dv:dv + 1])
    o_ref[0] = jnp.concatenate(outs, axis=1).astype(o_ref.dtype)


def _flash(q, k, v):
    B, T, QW = q.shape
    H, nh = MLA_HEADS, FLASH_HEADS_PER_STEP
    dqk, dv2 = QW // H, v.shape[-1] // H
    dv = dv2 // 2
    tq = next(t for t in FLASH_TILES if T % t == 0)
    return pl.pallas_call(
        functools.partial(_flash_kernel, tq=tq),
        out_shape=jax.ShapeDtypeStruct((B, T, H * dv), BF16),
        grid=(B, H // nh, T // tq),
        in_specs=[pl.BlockSpec((1, tq, nh * dqk), lambda b, h, i: (b, i, h)),
                  pl.BlockSpec((1, T, nh * dqk), lambda b, h, i: (b, 0, h)),
                  pl.BlockSpec((1, T, nh * dv2), lambda b, h, i: (b, 0, h))],
        out_specs=pl.BlockSpec((1, tq, nh * dv), lambda b, h, i: (b, i, h)),
        scratch_shapes=[pltpu.VMEM((nh, tq, 1), F32), pltpu.VMEM((nh, tq, dv2), F32)],
        compiler_params=_cparams(("parallel", "parallel", "arbitrary")),
        name="flash",
    )(q, k, v)


def _reorder_w_in(w_in):
    D = w_in.shape[0]
    w = w_in.astype(BF16)
    z = lambda n: jnp.zeros((D, n), BF16)
    small = [w[:, 2048:2052], w[:, 2052:2056], w[:, 4424:4428], w[:, 5964:5972], w[:, 2824:2888]]
    n_small = sum(s.shape[1] for s in small)
    cols = [w[:, 0:2048],
            w[:, 4940:5964],
            w[:, 4428:4940],
            w[:, 2056:2824],
            *small, z(LANES - n_small), z(COL_FOX_QKV - COL_SMALL - LANES),
            w[:, 2888:4424]]
    out = jnp.concatenate(cols, axis=1)
    assert out.shape[1] == PROJ_W
    return out


def _mixer(h, hn, B, T, rope, w_in, gdn_conv_w, gdn_A_log, gdn_dt_bias, gdn_norm_g,
           mla_qa_g, mla_wq_b, mla_kva_g, mla_wkv_b, mla_qn_g, mla_kn_g,
           fox_qn_g, fox_kn_g, fox_b_f,
           ssd_conv_w, ssd_conv_b, ssd_dt_bias, ssd_A_log, ssd_D, ssd_norm_g,
           w_gate, w_branch, w_o):
    M = B * T
    proj = _matmul(hn, _reorder_w_in(w_in), tn=1024, name="in_proj").reshape(B, T, PROJ_W)
    o_gdn = _gdn(proj, gdn_conv_w, gdn_A_log, gdn_dt_bias, gdn_norm_g)
    mq, mk, mv = _mla_prep(proj, rope, mla_qa_g, mla_wq_b, mla_kva_g, mla_wkv_b, mla_qn_g, mla_kn_g)
    o_mla = _flash(mq, mk, mv)
    o_fox = _flash(*_fox_prep(proj, fox_qn_g, fox_kn_g, fox_b_f))
    o_ssd = _ssd(proj, ssd_conv_w, ssd_conv_b, ssd_dt_bias, ssd_A_log, ssd_D, ssd_norm_g)
    branches = [o.reshape(M, BRANCH_W) for o in (o_gdn, o_mla, o_fox, o_ssd)]
    merged = _merge(hn, branches, w_gate.astype(BF16), w_branch.astype(BF16))
    return _matmul(merged, w_o.astype(BF16), tn=w_o.shape[1], residual=h, name="out_proj")


def kernel(x, meta_tokens, mix_norm_g, w_in, gdn_conv_w, gdn_A_log, gdn_dt_bias, gdn_norm_g, mla_qa_g, mla_wq_b, mla_kva_g, mla_wkv_b, mla_qn_g, mla_kn_g, fox_qn_g, fox_kn_g, fox_b_f, ssd_conv_w, ssd_conv_b, ssd_dt_bias, ssd_A_log, ssd_D, ssd_norm_g, w_gate, w_branch, w_o, ffn_norm_g, dense_w_gate, dense_w_up, dense_w_down, router_w, moe_w_gate, moe_w_up, moe_w_down):
    B, S, D = x.shape
    L = N_META + S
    T = -(-L // ROW_TILE) * ROW_TILE
    assert (B * T) % MM_TM == 0
    depth = w_in.shape[0]
    meta = jnp.broadcast_to(meta_tokens[None].astype(x.dtype), (B, N_META, D))
    h = jnp.concatenate([meta, x, jnp.zeros((B, T - L, D), x.dtype)], axis=1).reshape(B * T, D)
    rope = _rope_consts(T)
    for layer in range(depth):
        hn = _rmsnorm(h, mix_norm_g[layer])
        h = _mixer(h, hn, B, T, rope, w_in[layer],
                   gdn_conv_w[layer], gdn_A_log[layer], gdn_dt_bias[layer], gdn_norm_g[layer],
                   mla_qa_g[layer], mla_wq_b[layer], mla_kva_g[layer], mla_wkv_b[layer],
                   mla_qn_g[layer], mla_kn_g[layer],
                   fox_qn_g[layer], fox_kn_g[layer], fox_b_f[layer],
                   ssd_conv_w[layer], ssd_conv_b[layer], ssd_dt_bias[layer], ssd_A_log[layer],
                   ssd_D[layer], ssd_norm_g[layer],
                   w_gate[layer], w_branch[layer], w_o[layer])
        i = layer // 2
        if layer % 2 == 0:
            hn = _rmsnorm(h, ffn_norm_g[layer])
            act = _swiglu_up(hn, dense_w_gate[i].astype(BF16), dense_w_up[i].astype(BF16), tn=512)
            h = _matmul(act, dense_w_down[i].astype(BF16), tn=512, residual=h, name="ffn_down")
        else:
            hn, rec, cnt_before = _rmsnorm_router(h, ffn_norm_g[layer], router_w[i], B, T)
            h = _moe(h, hn, rec, cnt_before, moe_w_gate[i].astype(BF16), moe_w_up[i].astype(BF16),
                     moe_w_down[i].astype(BF16), B, T)
    return h.reshape(B, T, D)[:, N_META:L].astype(x.dtype)
```

```python
import functools
import math

import numpy as np
import jax
import jax.numpy as jnp
from jax import lax
from jax.experimental import pallas as pl
from jax.experimental.pallas import tpu as pltpu

F32 = jnp.float32
BF16 = jnp.bfloat16
HI = lax.Precision.HIGHEST
NT_DIMS = (((1,), (1,)), ((), ()))

D_MODEL = 2048
N_META = 16
EPS = 1e-6
NEG = -1e30
CONV_K = 4

GDN_HEADS, GDN_DK, GDN_DV = 4, 128, 128
MLA_HEADS, MLA_Q_RANK, MLA_KV_RANK, MLA_NOPE, MLA_ROPE, MLA_V = 4, 512, 256, 128, 64, 128
MLA_DQK = MLA_NOPE + MLA_ROPE
ROPE_BASE = 10000.0
FOX_HEADS, FOX_DH = 4, 128
SSD_HEADS, SSD_HEADDIM, SSD_GROUPS, SSD_STATE = 8, 64, 2, 128
SSD_HG = SSD_HEADS // SSD_GROUPS
SSD_INNER = SSD_HEADS * SSD_HEADDIM
N_BRANCH, BRANCH_W = 4, 512
N_EXPERTS, TOP_K = 8, 2

LANES = 128
ROW_TILE = 256
FLASH_TILES = (768, 256)
FLASH_HEADS_PER_STEP = 2
LOG2E = 1.4426950408889634
MM_TM = 768
MOE_TM = 256
R_E1, R_E2, R_P1, R_P2, R_R1, R_R2 = range(6)
VMEM_LIMIT = 56 * 1024 * 1024

PROJ_W = 6144
COL_GDN_QKV, COL_GDN_Z = 0, 1536
COL_SSD_XBC, COL_SSD_Z = 2048, 3072
COL_MLA_QA, COL_MLA_KVA = 3584, 4096
COL_SMALL = 4352
COL_FOX_QKV = 4608
L_BETA, L_GA, L_FF, L_DT, L_KPE = 0, 4, 8, 12, 20


def _cparams(sem, vmem=VMEM_LIMIT):
    return pltpu.CompilerParams(dimension_semantics=sem, vmem_limit_bytes=vmem)


def _softplus(x):
    return jnp.maximum(x, 0.0) + jnp.log1p(jnp.exp(-jnp.abs(x)))


def _silu(x):
    return x * jax.nn.sigmoid(x)


def _dot(a, b, precision=None):
    return jnp.dot(a, b, preferred_element_type=F32, precision=precision)


def _dot_nt(a, b):
    return lax.dot_general(a, b, NT_DIMS, preferred_element_type=F32)


def _rmsnorm_kernel(h_ref, g_ref, o_ref):
    x = h_ref[...]
    y = x * lax.rsqrt(jnp.mean(x * x, axis=-1, keepdims=True) + EPS) * g_ref[...]
    o_ref[...] = y.astype(o_ref.dtype)


def _rmsnorm(h, g):
    M, D = h.shape
    return pl.pallas_call(
        _rmsnorm_kernel,
        out_shape=jax.ShapeDtypeStruct((M, D), BF16),
        grid=(M // MM_TM,),
        in_specs=[pl.BlockSpec((MM_TM, D), lambda i: (i, 0)),
                  pl.BlockSpec((1, D), lambda i: (0, 0))],
        out_specs=pl.BlockSpec((MM_TM, D), lambda i: (i, 0)),
        compiler_params=_cparams(("parallel",)),
        name="rmsnorm",
    )(h, g.reshape(1, D).astype(F32))


def _rmsnorm_router_kernel(h_ref, g_ref, rw_ref, o_ref, rec_ref, cnt_ref, run_ref):
    @pl.when(pl.program_id(1) == 0)
    def _():
        run_ref[...] = jnp.zeros_like(run_ref)

    x = h_ref[...]
    n = x.shape[0]
    y = x * lax.rsqrt(jnp.mean(x * x, axis=-1, keepdims=True) + EPS) * g_ref[...]
    o_ref[...] = y.astype(o_ref.dtype)
    logits = _dot(y, rw_ref[...], HI)
    lane = lax.broadcasted_iota(jnp.int32, logits.shape, 1)
    logits = jnp.where(lane < N_EXPERTS, logits, NEG)
    m1 = jnp.max(logits, axis=-1, keepdims=True)
    i1 = jnp.min(jnp.where(logits == m1, lane, LANES), axis=-1, keepdims=True)
    rest = jnp.where(lane == i1, NEG, logits)
    m2 = jnp.max(rest, axis=-1, keepdims=True)
    i2 = jnp.min(jnp.where(rest == m2, lane, LANES), axis=-1, keepdims=True)
    e2 = jnp.exp(m2 - m1)
    p1 = 1.0 / (1.0 + e2)
    p2 = e2 * p1
    sel = jnp.where(lane == i1, 1.0, 0.0) + jnp.where(lane == i2, 1.0, 0.0)
    row = lax.broadcasted_iota(jnp.int32, (n, n), 0)
    col = lax.broadcasted_iota(jnp.int32, (n, n), 1)
    earlier = jnp.where(col < row, 1.0, 0.0).astype(BF16)
    run = run_ref[0:1, :]
    before = _dot(earlier, sel.astype(BF16)) + run
    r1 = jnp.sum(jnp.where(lane == i1, before, 0.0), axis=-1, keepdims=True)
    r2 = jnp.sum(jnp.where(lane == i2, before, 0.0), axis=-1, keepdims=True)
    cnt_ref[...] = jnp.broadcast_to(run, cnt_ref.shape)
    run_ref[...] = jnp.broadcast_to(before[n - 1:n] + sel[n - 1:n], run_ref.shape)
    rec = jnp.zeros_like(logits)
    for ln, val in ((R_E1, i1.astype(F32)), (R_E2, i2.astype(F32)), (R_P1, p1), (R_P2, p2), (R_R1, r1), (R_R2, r2)):
        rec = jnp.where(lane == ln, val, rec)
    rec_ref[...] = rec


def _rmsnorm_router(h, g, router_w, B, T):
    M, D = h.shape
    n = ROW_TILE
    NT = T // n
    rw = jnp.zeros((D, LANES), F32).at[:, :N_EXPERTS].set(router_w.astype(F32))
    return pl.pallas_call(
        _rmsnorm_router_kernel,
        out_shape=(jax.ShapeDtypeStruct((M, D), BF16), jax.ShapeDtypeStruct((M, LANES), F32),
                   jax.ShapeDtypeStruct((B * NT * 8, LANES), F32)),
        grid=(B, NT),
        in_specs=[pl.BlockSpec((n, D), lambda b, t: (b * NT + t, 0)),
                  pl.BlockSpec((1, D), lambda b, t: (0, 0)),
                  pl.BlockSpec((D, LANES), lambda b, t: (0, 0))],
        out_specs=(pl.BlockSpec((n, D), lambda b, t: (b * NT + t, 0)),
                   pl.BlockSpec((n, LANES), lambda b, t: (b * NT + t, 0)),
                   pl.BlockSpec((8, LANES), lambda b, t: (b * NT + t, 0))),
        scratch_shapes=[pltpu.VMEM((8, LANES), F32)],
        compiler_params=_cparams(("arbitrary", "arbitrary")),
        name="rmsnorm_router",
    )(h, g.reshape(1, D).astype(F32), rw)


def _mm_kernel(*refs, has_res):
    a_ref, w_ref = refs[:2]
    y = _dot(a_ref[...], w_ref[...])
    if has_res:
        y = y + refs[2][...]
    refs[-1][...] = y.astype(refs[-1].dtype)


def _matmul(a, w, *, tn, tm=MM_TM, residual=None, out_dtype=F32, name="matmul"):
    M, K = a.shape
    N = w.shape[1]
    in_specs = [pl.BlockSpec((tm, K), lambda j, i: (i, 0)),
                pl.BlockSpec((K, tn), lambda j, i: (0, j))]
    args = [a, w]
    if residual is not None:
        in_specs.append(pl.BlockSpec((tm, tn), lambda j, i: (i, j)))
        args.append(residual)
    return pl.pallas_call(
        functools.partial(_mm_kernel, has_res=residual is not None),
        out_shape=jax.ShapeDtypeStruct((M, N), out_dtype),
        grid=(N // tn, M // tm),
        in_specs=in_specs,
        out_specs=pl.BlockSpec((tm, tn), lambda j, i: (i, j)),
        compiler_params=_cparams(("parallel", "parallel")),
        name=name,
    )(*args)


def _swiglu_kernel(a_ref, wg_ref, wu_ref, o_ref):
    a = a_ref[...]
    g = _dot(a, wg_ref[...])
    u = _dot(a, wu_ref[...])
    o_ref[...] = (_silu(g) * u).astype(o_ref.dtype)


def _swiglu_up(a, wg, wu, *, tn, tm=MM_TM):
    M, K = a.shape
    F = wg.shape[1]
    return pl.pallas_call(
        _swiglu_kernel,
        out_shape=jax.ShapeDtypeStruct((M, F), BF16),
        grid=(F // tn, M // tm),
        in_specs=[pl.BlockSpec((tm, K), lambda j, i: (i, 0)),
                  pl.BlockSpec((K, tn), lambda j, i: (0, j)),
                  pl.BlockSpec((K, tn), lambda j, i: (0, j))],
        out_specs=pl.BlockSpec((tm, tn), lambda j, i: (i, j)),
        compiler_params=_cparams(("parallel", "parallel")),
        name="swiglu_up",
    )(a, wg, wu)


def _dispatch_kernel(lo_ref, nb_ref, d1_ref, d2_ref, rec_ref, hn_ref, o_ref, p_ref, acc_ref, pacc_ref, *, wb):
    i = pl.program_id(1)
    idx = pl.program_id(0) * pl.num_programs(1) + i
    tm = o_ref.shape[0]
    rows = i * tm + lax.broadcasted_iota(jnp.int32, (tm, wb), 0)
    lane = lax.broadcasted_iota(jnp.int32, (wb, LANES), 1)
    acc_ref[...] = jnp.zeros_like(acc_ref)
    pacc_ref[...] = jnp.zeros_like(pacc_ref)

    def hi_lo(p):
        hi = p.astype(BF16).astype(F32)
        return jnp.where(lane == 0, hi, jnp.where(lane == 1, p - hi, 0.0)).astype(BF16)

    def body(k, carry):
        start = pl.multiple_of((lo_ref[idx] + k) * wb, wb)
        hit1 = jnp.where(d1_ref[0, :, pl.ds(start, wb)] == rows, 1.0, 0.0)
        hit2 = jnp.where(d2_ref[0, :, pl.ds(start, wb)] == rows, 1.0, 0.0)
        acc_ref[...] += _dot((hit1 + hit2).astype(BF16), hn_ref[0, pl.ds(start, wb), :])
        rec = rec_ref[0, pl.ds(start, wb), :]
        pacc_ref[...] += (_dot(hit1.astype(BF16), hi_lo(rec[:, R_P1:R_P1 + 1]))
                          + _dot(hit2.astype(BF16), hi_lo(rec[:, R_P2:R_P2 + 1])))
        return carry

    lax.fori_loop(0, nb_ref[idx], body, 0)
    o_ref[...] = acc_ref[...].astype(o_ref.dtype)
    p_ref[...] = pacc_ref[...]


def _dispatch(hn3, rec3, d1, d2, lo_blk, n_blk, nti, *, tm, wb=ROW_TILE):
    B, T, D = hn3.shape
    dspec = pl.BlockSpec((1, 1, T), lambda b, i, lo, nb: (b, 0, 0))
    row_spec = lambda w: pl.BlockSpec((tm, w), lambda b, i, lo, nb: (b * nti + i, 0))
    grid_spec = pltpu.PrefetchScalarGridSpec(
        num_scalar_prefetch=2,
        grid=(B, nti),
        in_specs=[dspec, dspec,
                  pl.BlockSpec((1, T, LANES), lambda b, i, lo, nb: (b, 0, 0)),
                  pl.BlockSpec((1, T, D), lambda b, i, lo, nb: (b, 0, 0), pipeline_mode=pl.Buffered(1))],
        out_specs=(row_spec(D), row_spec(LANES)),
        scratch_shapes=[pltpu.VMEM((tm, D), F32), pltpu.VMEM((tm, LANES), F32)])
    return pl.pallas_call(
        functools.partial(_dispatch_kernel, wb=wb),
        out_shape=(jax.ShapeDtypeStruct((B * nti * tm, D), BF16),
                   jax.ShapeDtypeStruct((B * nti * tm, LANES), F32)),
        grid_spec=grid_spec,
        compiler_params=_cparams(("arbitrary", "arbitrary")),
        name="moe_dispatch",
    )(lo_blk, n_blk, d1.reshape(B, 1, T), d2.reshape(B, 1, T), rec3, hn3)


def _expert_up_kernel(te_ref, nb_ref, x_ref, wg_ref, wu_ref, p_ref, o_ref):
    i = pl.program_id(0)

    @pl.when(nb_ref[i] > 0)
    def _():
        x = x_ref[...]
        p = p_ref[:, 0:1] + p_ref[:, 1:2]
        o_ref[...] = (_silu(_dot(x, wg_ref[0])) * _dot(x, wu_ref[0]) * p).astype(o_ref.dtype)

    @pl.when(nb_ref[i] == 0)
    def _():
        o_ref[...] = jnp.zeros_like(o_ref)


def _expert_down_kernel(te_ref, nb_ref, a_ref, wd_ref, o_ref):
    i = pl.program_id(0)

    @pl.when(nb_ref[i] > 0)
    def _():
        o_ref[...] = _dot(a_ref[...], wd_ref[0]).astype(o_ref.dtype)

    @pl.when(nb_ref[i] == 0)
    def _():
        o_ref[...] = jnp.zeros_like(o_ref)


def _expert_ffn(x, p_rows, tile_expert, n_blk, wg, wu, wd, *, tm):
    R, D = x.shape
    E, _, F = wg.shape
    up_spec = pltpu.PrefetchScalarGridSpec(
        num_scalar_prefetch=2,
        grid=(R // tm,),
        in_specs=[pl.BlockSpec((tm, D), lambda i, te, nb: (i, 0)),
                  pl.BlockSpec((1, D, F), lambda i, te, nb: (te[i], 0, 0)),
                  pl.BlockSpec((1, D, F), lambda i, te, nb: (te[i], 0, 0)),
                  pl.BlockSpec((tm, LANES), lambda i, te, nb: (i, 0))],
        out_specs=pl.BlockSpec((tm, F), lambda i, te, nb: (i, 0)))
    act = pl.pallas_call(
        _expert_up_kernel,
        out_shape=jax.ShapeDtypeStruct((R, F), BF16),
        grid_spec=up_spec,
        compiler_params=_cparams(("arbitrary",)),
        name="moe_up",
    )(tile_expert, n_blk, x, wg, wu, p_rows)
    down_spec = pltpu.PrefetchScalarGridSpec(
        num_scalar_prefetch=2,
        grid=(R // tm,),
        in_specs=[pl.BlockSpec((tm, F), lambda i, te, nb: (i, 0)),
                  pl.BlockSpec((1, F, D), lambda i, te, nb: (te[i], 0, 0))],
        out_specs=pl.BlockSpec((tm, D), lambda i, te, nb: (i, 0)))
    return pl.pallas_call(
        _expert_down_kernel,
        out_shape=jax.ShapeDtypeStruct((R, D), BF16),
        grid_spec=down_spec,
        compiler_params=_cparams(("arbitrary",)),
        name="moe_down",
    )(tile_expert, n_blk, act, wd)


def _combine_kernel(yb_ref, yv_ref, h_ref, d1_ref, d2_ref, *refs, tm):
    y_refs, o_ref = refs[:-1], refs[-1]
    n = o_ref.shape[0]
    base = (pl.program_id(0) * pl.num_programs(1) + pl.program_id(1)) * len(y_refs)
    d1, d2 = d1_ref[...], d2_ref[...]
    lane = lax.broadcasted_iota(jnp.int32, (n, tm), 1)
    o_ref[...] = h_ref[...]
    for k, y_ref in enumerate(y_refs):
        @pl.when(yv_ref[base + k] > 0)
        def _():
            rows = yb_ref[base + k] * tm + lane
            hit = jnp.where(d1 == rows, 1.0, 0.0) + jnp.where(d2 == rows, 1.0, 0.0)
            o_ref[...] += _dot(hit.astype(BF16), y_ref[...])


def _combine(h, y, d1, d2, y_blk, y_valid, B, T, *, tm):
    M, D = h.shape
    n = ROW_TILE
    NT = T // n
    slots = y_blk.shape[0] // (B * NT)
    y_specs = [pl.BlockSpec((tm, D), lambda b, t, yb, yv, k=k: (yb[(b * NT + t) * slots + k], 0))
               for k in range(slots)]
    tok = lambda w: pl.BlockSpec((n, w), lambda b, t, yb, yv: (b * NT + t, 0))
    grid_spec = pltpu.PrefetchScalarGridSpec(
        num_scalar_prefetch=2,
        grid=(B, NT),
        in_specs=[tok(D), tok(1), tok(1)] + y_specs,
        out_specs=tok(D))
    return pl.pallas_call(
        functools.partial(_combine_kernel, tm=tm),
        out_shape=jax.ShapeDtypeStruct((M, D), F32),
        grid_spec=grid_spec,
        compiler_params=_cparams(("arbitrary", "arbitrary")),
        name="moe_combine",
    )(y_blk, y_valid, h, d1, d2, *([y] * slots))


def _moe(h, hn, rec, cnt_before, wg, wu, wd, B, T):
    M, D = h.shape
    E, tm, n = N_EXPERTS, MOE_TM, ROW_TILE
    NT = T // n
    rb = TOP_K * T + E * tm
    nti = rb // tm
    i32 = jnp.int32
    col = lambda ln: rec[:, ln].reshape(B, T)
    e1, e2 = col(R_E1).astype(i32), col(R_E2).astype(i32)
    r1, r2 = col(R_R1).astype(i32), col(R_R2).astype(i32)
    cntb = cnt_before.reshape(B, NT, 8, LANES)[:, :, 0, :E].astype(i32)
    oh1, oh2 = jax.nn.one_hot(e1, E, dtype=i32), jax.nn.one_hot(e2, E, dtype=i32)
    counts = (oh1 + oh2).sum(axis=1)
    padded = (counts + tm - 1) // tm * tm
    ends = jnp.cumsum(padded, axis=1)
    off = ends - padded
    d1 = (oh1 * off[:, None, :]).sum(-1) + r1
    d2 = (oh2 * off[:, None, :]).sum(-1) + r2
    cnt_end = jnp.concatenate([cntb[:, 1:], counts[:, None, :]], axis=1)
    tile_start = jnp.arange(nti, dtype=i32)[None, :] * tm
    tile_expert = jnp.minimum((tile_start[:, :, None] >= ends[:, None, :]).sum(-1), E - 1).astype(i32)
    te_oh = jax.nn.one_hot(tile_expert, E, dtype=i32)
    pick = lambda per_expert: (te_oh * per_expert[:, None, :]).sum(-1)
    rank_lo = tile_start - pick(off)
    rank_hi = jnp.minimum(rank_lo + tm, pick(counts)) - 1
    through = (te_oh[:, :, None, :] * cnt_end[:, None, :, :]).sum(-1)
    lo_blk = (through <= rank_lo[..., None]).sum(-1)
    hi_blk = (through <= rank_hi[..., None]).sum(-1)
    nonempty = (tile_start < pick(ends)) & (rank_hi >= rank_lo)
    lo_blk = jnp.where(nonempty, lo_blk, 0).astype(i32)
    n_blk = jnp.where(nonempty, hi_blk - lo_blk + 1, 0).astype(i32)
    x, p_rows = _dispatch(hn.reshape(B, T, D), rec.reshape(B, T, LANES), d1, d2,
                          lo_blk.reshape(-1), n_blk.reshape(-1), nti, tm=tm)
    y = _expert_ffn(x, p_rows, tile_expert.reshape(-1), n_blk.reshape(-1), wg, wu, wd, tm=tm)
    first = off[:, None, :] + cntb
    last = off[:, None, :] + cnt_end - 1
    used = cnt_end > cntb
    fb = jnp.minimum(first // tm, nti - 1)
    lb = jnp.minimum(jnp.maximum(last, first) // tm, nti - 1)
    blk = jnp.stack([fb, lb], axis=-1) + (jnp.arange(B, dtype=i32) * nti)[:, None, None, None]
    valid = jnp.stack([used, used & (lb > fb)], axis=-1)
    row0 = (jnp.arange(B, dtype=i32) * rb)[:, None]
    return _combine(h, y, (d1 + row0).reshape(M, 1), (d2 + row0).reshape(M, 1),
                    blk.reshape(-1).astype(i32), valid.reshape(-1).astype(i32), B, T, tm=tm)


def _merge_kernel(hn_ref, b0_ref, b1_ref, b2_ref, b3_ref, wg_ref, wb_ref, o_ref):
    hn = hn_ref[...]
    acc = None
    for b, br_ref in enumerate((b0_ref, b1_ref, b2_ref, b3_ref)):
        gate = jax.nn.sigmoid(_dot(hn, wg_ref[b]))
        term = gate * _dot(br_ref[...], wb_ref[b])
        acc = term if acc is None else acc + term
    o_ref[...] = acc.astype(o_ref.dtype)


def _merge(hn, branches, wg, wb, *, tn=512, tm=MM_TM):
    M, D = hn.shape
    N = wg.shape[2]
    bspec = pl.BlockSpec((tm, BRANCH_W), lambda j, i: (i, 0))
    return pl.pallas_call(
        _merge_kernel,
        out_shape=jax.ShapeDtypeStruct((M, N), BF16),
        grid=(N // tn, M // tm),
        in_specs=[pl.BlockSpec((tm, D), lambda j, i: (i, 0)), bspec, bspec, bspec, bspec,
                  pl.BlockSpec((N_BRANCH, D, tn), lambda j, i: (0, 0, j)),
                  pl.BlockSpec((N_BRANCH, BRANCH_W, tn), lambda j, i: (0, 0, j))],
        out_specs=pl.BlockSpec((tm, tn), lambda j, i: (i, j)),
        compiler_params=_cparams(("parallel", "parallel")),
        name="gate_merge",
    )(hn, *branches, wg, wb)


def _causal_conv(x, carry_ref, cw):
    n = x.shape[0]
    xext = jnp.concatenate([carry_ref[...], x], axis=0)
    y = cw[0:1] * xext[5:5 + n]
    for i in range(1, CONV_K):
        y = y + cw[i:i + 1] * xext[5 + i:5 + i + n]
    carry_ref[...] = x[n - 8:n]
    return y


def _tile_masks(n, chunk):
    row = lax.broadcasted_iota(jnp.int32, (n, n), 0)
    col = lax.broadcasted_iota(jnp.int32, (n, n), 1)
    if chunk == n:
        return col <= row, col < row
    in_chunk = col >= (row // chunk) * chunk
    return in_chunk & (col <= row), in_chunk & (col < row)


def _gdn_kernel(qkv_ref, z_ref, sm_ref, cw_ref, alog_ref, dtb_ref, ng_ref, o_ref, s_ref, carry_ref):
    @pl.when(pl.program_id(1) == 0)
    def _():
        s_ref[...] = jnp.zeros_like(s_ref)
        carry_ref[...] = jnp.zeros_like(carry_ref)

    n = ROW_TILE
    y = _silu(_causal_conv(qkv_ref[0], carry_ref, cw_ref[...]))
    sm = sm_ref[0]
    z = z_ref[0]
    beta_all = jax.nn.sigmoid(sm)
    g_all = -jnp.exp(alog_ref[...]) * _softplus(sm + dtb_ref[...])
    row = lax.broadcasted_iota(jnp.int32, (n, n), 0)
    col = lax.broadcasted_iota(jnp.int32, (n, n), 1)
    causal, strict = col <= row, col < row
    diff_bits = row ^ col
    levels = int(math.log2(n))
    level_masks = [(diff_bits >= (1 << l)) & (diff_bits < (2 << l)) for l in range(levels)]
    eye = jnp.where(row == col, 1.0, 0.0)
    gcs_all = _dot(causal.astype(F32), g_all, HI)
    gcs_t = gcs_all.T
    heads = []
    for h in range(GDN_HEADS):
        q = y[:, h * GDN_DK:(h + 1) * GDN_DK]
        k = y[:, GDN_HEADS * GDN_DK + h * GDN_DK:GDN_HEADS * GDN_DK + (h + 1) * GDN_DK]
        v = y[:, 2 * GDN_HEADS * GDN_DK + h * GDN_DV:2 * GDN_HEADS * GDN_DK + (h + 1) * GDN_DV]
        q = q * lax.rsqrt(jnp.sum(q * q, axis=-1, keepdims=True) + EPS) * GDN_DK ** -0.5
        k = k * lax.rsqrt(jnp.sum(k * k, axis=-1, keepdims=True) + EPS)
        beta = beta_all[:, L_BETA + h:L_BETA + h + 1]
        gc = gcs_all[:, L_GA + h:L_GA + h + 1]
        gr = gcs_t[L_GA + h:L_GA + h + 1, :]
        decay = jnp.exp(jnp.where(causal, gc - gr, NEG))
        kb = k * beta
        k16 = k.astype(BF16)
        a = jnp.where(strict, _dot_nt(kb.astype(BF16), k16) * decay, 0.0)
        heads.append(dict(q=q, k=k, v=v, beta=beta, gc=gc, decay=decay, kb=kb, k16=k16, a=a,
                          t=eye - jnp.where(level_masks[0], a, 0.0)))
    for l in range(1, levels):
        for hd in heads:
            t16 = hd["t"].astype(BF16)
            a_l = jnp.where(level_masks[l], hd["a"], 0.0).astype(BF16)
            hd["t"] = hd["t"] - _dot(_dot(t16, a_l).astype(BF16), t16)
    outs = []
    for h, hd in enumerate(heads):
        q, k, gc = hd["q"], hd["k"], hd["gc"]
        egc = jnp.exp(gc)
        rhs = jnp.concatenate([hd["v"] * hd["beta"], hd["kb"] * egc], axis=1)
        uw = _dot(hd["t"].astype(BF16), rhs.astype(BF16))
        u, w = uw[:, :GDN_DV], uw[:, GDN_DV:]
        att = _dot_nt(q.astype(BF16), hd["k16"]) * hd["decay"]
        g_last = gc[n - 1:n, :]
        kd = k * jnp.exp(g_last - gc)
        s = s_ref[h]
        ws = _dot(jnp.concatenate([w, q * egc], axis=0).astype(BF16), s.astype(BF16))
        v_new = (u - ws[:n]).astype(BF16)
        o = ws[n:] + _dot(att.astype(BF16), v_new)
        s_ref[h] = s * jnp.exp(g_last) + _dot(kd.T.astype(BF16), v_new)
        o = o * lax.rsqrt(jnp.mean(o * o, axis=-1, keepdims=True) + EPS) * ng_ref[...]
        outs.append(o * _silu(z[:, h * GDN_DV:(h + 1) * GDN_DV]))
    o_ref[0] = jnp.concatenate(outs, axis=1).astype(o_ref.dtype)


def _lane_vec(vals, lane0):
    v = jnp.zeros((1, LANES), F32)
    return v.at[0, lane0:lane0 + vals.shape[0]].set(vals.astype(F32))


def _gdn(proj, conv_w, a_log, dt_bias, norm_g):
    B, T, _ = proj.shape
    n = ROW_TILE
    W = 2 * GDN_HEADS * GDN_DK + GDN_HEADS * GDN_DV
    ZW = GDN_HEADS * GDN_DV
    return pl.pallas_call(
        _gdn_kernel,
        out_shape=jax.ShapeDtypeStruct((B, T, ZW), BF16),
        grid=(B, T // n),
        in_specs=[pl.BlockSpec((1, n, W), lambda b, t: (b, t, COL_GDN_QKV // W)),
                  pl.BlockSpec((1, n, ZW), lambda b, t: (b, t, COL_GDN_Z // ZW)),
                  pl.BlockSpec((1, n, LANES), lambda b, t: (b, t, COL_SMALL // LANES)),
                  pl.BlockSpec((CONV_K, W), lambda b, t: (0, 0)),
                  pl.BlockSpec((1, LANES), lambda b, t: (0, 0)),
                  pl.BlockSpec((1, LANES), lambda b, t: (0, 0)),
                  pl.BlockSpec((1, GDN_DV), lambda b, t: (0, 0))],
        out_specs=pl.BlockSpec((1, n, ZW), lambda b, t: (b, t, 0)),
        scratch_shapes=[pltpu.VMEM((GDN_HEADS, GDN_DK, GDN_DV), F32),
                        pltpu.VMEM((8, W), F32)],
        compiler_params=_cparams(("arbitrary", "arbitrary")),
        name="gdn",
    )(proj, proj, proj, conv_w.astype(F32), _lane_vec(a_log, L_GA), _lane_vec(dt_bias, L_GA),
      norm_g.reshape(1, GDN_DV).astype(F32))


def _ssd_kernel(xbc_ref, z_ref, sm_ref, cw_ref, cb_ref, alog_ref, dtb_ref, dvec_ref, ng_ref,
                o_ref, hs_ref, carry_ref):
    @pl.when(pl.program_id(1) == 0)
    def _():
        hs_ref[...] = jnp.zeros_like(hs_ref)
        carry_ref[...] = jnp.zeros_like(carry_ref)

    n, P, N = ROW_TILE, SSD_HEADDIM, SSD_STATE
    y = _silu(_causal_conv(xbc_ref[0], carry_ref, cw_ref[...]) + cb_ref[...])
    xs = y[:, :SSD_INNER]
    bm = y[:, SSD_INNER:SSD_INNER + SSD_GROUPS * N]
    cm = y[:, SSD_INNER + SSD_GROUPS * N:]
    sm = sm_ref[0]
    dt_all = _softplus(sm + dtb_ref[...])
    a_all = dt_all * (-jnp.exp(alog_ref[...]))
    causal, _ = _tile_masks(n, n)
    acs_all = _dot(causal.astype(F32), a_all, HI)
    acs_t = acs_all.T
    ys = []
    for g in range(SSD_GROUPS):
        bg = bm[:, g * N:(g + 1) * N]
        cg16 = cm[:, g * N:(g + 1) * N].astype(BF16)
        cb = _dot_nt(cg16, bg.astype(BF16))
        bgt16 = bg.T.astype(BF16)
        for j in range(SSD_HG):
            hh = g * SSD_HG + j
            ac = acs_all[:, L_DT + hh:L_DT + hh + 1]
            ar = acs_t[L_DT + hh:L_DT + hh + 1, :]
            lmat = jnp.exp(jnp.where(causal, ac - ar, NEG))
            xdt = xs[:, hh * P:(hh + 1) * P] * dt_all[:, L_DT + hh:L_DT + hh + 1]
            y_diag = _dot((cb * lmat).astype(BF16), xdt.astype(BF16))
            a_last = ac[n - 1:n, :]
            st = _dot(bgt16, (xdt * jnp.exp(a_last - ac)).astype(BF16))
            h_prev = hs_ref[hh]
            y_off = _dot(cg16, h_prev.astype(BF16)) * jnp.exp(ac)
            hs_ref[hh] = h_prev * jnp.exp(a_last) + st
            ys.append(y_diag + y_off)
    yy = jnp.concatenate(ys, axis=1) + xs * dvec_ref[...]
    yy = yy * _silu(z_ref[0])
    gw = SSD_HG * P
    outs = []
    for g in range(SSD_GROUPS):
        seg = yy[:, g * gw:(g + 1) * gw]
        outs.append(seg * lax.rsqrt(jnp.mean(seg * seg, axis=-1, keepdims=True) + EPS)
                    * ng_ref[:, g * gw:(g + 1) * gw])
    o_ref[0] = jnp.concatenate(outs, axis=1).astype(o_ref.dtype)


def _ssd(proj, conv_w, conv_b, dt_bias, a_log, d_skip, norm_g):
    B, T, _ = proj.shape
    n = ROW_TILE
    W = SSD_INNER + 2 * SSD_GROUPS * SSD_STATE
    dvec = jnp.repeat(d_skip.astype(F32), SSD_HEADDIM).reshape(1, SSD_INNER)
    return pl.pallas_call(
        _ssd_kernel,
        out_shape=jax.ShapeDtypeStruct((B, T, SSD_INNER), BF16),
        grid=(B, T // n),
        in_specs=[pl.BlockSpec((1, n, W), lambda b, t: (b, t, COL_SSD_XBC // W)),
                  pl.BlockSpec((1, n, SSD_INNER), lambda b, t: (b, t, COL_SSD_Z // SSD_INNER)),
                  pl.BlockSpec((1, n, LANES), lambda b, t: (b, t, COL_SMALL // LANES)),
                  pl.BlockSpec((CONV_K, W), lambda b, t: (0, 0)),
                  pl.BlockSpec((1, W), lambda b, t: (0, 0)),
                  pl.BlockSpec((1, LANES), lambda b, t: (0, 0)),
                  pl.BlockSpec((1, LANES), lambda b, t: (0, 0)),
                  pl.BlockSpec((1, SSD_INNER), lambda b, t: (0, 0)),
                  pl.BlockSpec((1, SSD_INNER), lambda b, t: (0, 0))],
        out_specs=pl.BlockSpec((1, n, SSD_INNER), lambda b, t: (b, t, 0)),
        scratch_shapes=[pltpu.VMEM((SSD_HEADS, SSD_STATE, SSD_HEADDIM), F32),
                        pltpu.VMEM((8, W), F32)],
        compiler_params=_cparams(("arbitrary", "arbitrary")),
        name="ssd",
    )(proj, proj, proj, conv_w.astype(F32), conv_b.reshape(1, W).astype(F32),
      _lane_vec(a_log, L_DT), _lane_vec(dt_bias, L_DT), dvec,
      norm_g.reshape(1, SSD_INNER).astype(F32))


def _ones_lane0(n):
    lane = lax.broadcasted_iota(jnp.int32, (n, LANES), 1)
    return jnp.where(lane == 0, 1.0, 0.0)


def _fox_prep_kernel(qkv_ref, sm_ref, qg_ref, kg_ref, bf_ref, q_ref, k_ref, v_ref, run_ref):
    @pl.when(pl.program_id(1) == 0)
    def _():
        run_ref[...] = jnp.zeros_like(run_ref)

    n, dh, H = ROW_TILE, FOX_DH, FOX_HEADS
    x = qkv_ref[0]
    log_f = -_softplus(-(sm_ref[0] + bf_ref[...]))
    causal, _ = _tile_masks(n, n)
    cum = _dot(causal.astype(F32), log_f, HI) + run_ref[0:1, :]
    run_ref[...] = jnp.broadcast_to(cum[n - 1:n, :], run_ref.shape)
    lane = lax.broadcasted_iota(jnp.int32, (n, LANES), 1)
    ones = _ones_lane0(n)
    q_bias = jnp.where(lane < 3, 1.0, 0.0)
    qs, ks, vs = [], [], []
    for h in range(H):
        q = x[:, h * dh:(h + 1) * dh]
        k = x[:, H * dh + h * dh:H * dh + (h + 1) * dh]
        qs.append(q * lax.rsqrt(jnp.mean(q * q, axis=-1, keepdims=True) + EPS)
                  * (qg_ref[...] * (dh ** -0.5 * LOG2E)))
        qs.append(q_bias)
        ks.append(k * lax.rsqrt(jnp.mean(k * k, axis=-1, keepdims=True) + EPS) * kg_ref[...])
        c = cum[:, L_FF + h:L_FF + h + 1] * (-LOG2E)
        c_hi = c.astype(BF16).astype(F32)
        c_mid = (c - c_hi).astype(BF16).astype(F32)
        c_lo = c - c_hi - c_mid
        ks.append(jnp.where(lane == 0, c_hi, jnp.where(lane == 1, c_mid, jnp.where(lane == 2, c_lo, 0.0))))
        vs.append(x[:, 2 * H * dh + h * dh:2 * H * dh + (h + 1) * dh])
        vs.append(ones)
    q_ref[0] = jnp.concatenate(qs, axis=1).astype(q_ref.dtype)
    k_ref[0] = jnp.concatenate(ks, axis=1).astype(k_ref.dtype)
    v_ref[0] = jnp.concatenate(vs, axis=1).astype(v_ref.dtype)


def _fox_prep(proj, qn_g, kn_g, b_f):
    B, T, _ = proj.shape
    n = ROW_TILE
    W = 3 * FOX_HEADS * FOX_DH
    HW = FOX_HEADS * 2 * LANES
    ospec = pl.BlockSpec((1, n, HW), lambda b, t: (b, t, 0))
    return pl.pallas_call(
        _fox_prep_kernel,
        out_shape=(jax.ShapeDtypeStruct((B, T, HW), BF16),) * 3,
        grid=(B, T // n),
        in_specs=[pl.BlockSpec((1, n, W), lambda b, t: (b, t, COL_FOX_QKV // W)),
                  pl.BlockSpec((1, n, LANES), lambda b, t: (b, t, COL_SMALL // LANES)),
                  pl.BlockSpec((1, FOX_DH), lambda b, t: (0, 0)),
                  pl.BlockSpec((1, FOX_DH), lambda b, t: (0, 0)),
                  pl.BlockSpec((1, LANES), lambda b, t: (0, 0))],
        out_specs=(ospec, ospec, ospec),
        scratch_shapes=[pltpu.VMEM((8, LANES), F32)],
        compiler_params=_cparams(("arbitrary", "arbitrary")),
        name="fox_prep",
    )(proj, proj, qn_g.reshape(1, FOX_DH).astype(F32), kn_g.reshape(1, FOX_DH).astype(F32),
      _lane_vec(b_f, L_FF))


def _mla_prep_kernel(qa_ref, kva_ref, sm_ref, cos_ref, sin_ref, qag_ref, wq_ref, kvag_ref, wkv_ref,
                     qgn_ref, qgr_ref, kgn_ref, kgr_ref, rot_ref, exp_ref, q_ref, k_ref, v_ref):
    n, H, dn, dr = ROW_TILE, MLA_HEADS, MLA_NOPE, MLA_ROPE
    qa = qa_ref[0]
    qa = qa * lax.rsqrt(jnp.mean(qa * qa, axis=-1, keepdims=True) + EPS) * qag_ref[...]
    qq = _dot(qa.astype(BF16), wq_ref[...])
    kva = kva_ref[0]
    kva = kva * lax.rsqrt(jnp.mean(kva * kva, axis=-1, keepdims=True) + EPS) * kvag_ref[...]
    kv = _dot(kva.astype(BF16), wkv_ref[...])
    sm = sm_ref[0]
    lane = lax.broadcasted_iota(jnp.int32, (n, LANES), 1)
    is_kpe = (lane >= L_KPE) & (lane < L_KPE + dr)
    kpe_ss = jnp.sum(jnp.where(is_kpe, sm * sm, 0.0), axis=-1, keepdims=True)
    kpe4 = _dot(sm, exp_ref[...], HI)
    qr = qq[:, H * dn:]
    grp = lax.broadcasted_iota(jnp.int32, (n, H * dr), 1) // dr
    q_rs, k_rs = [], []
    for h in range(H):
        qn = qq[:, h * dn:(h + 1) * dn]
        ssr = jnp.sum(jnp.where(grp == h, qr * qr, 0.0), axis=-1, keepdims=True)
        q_rs.append(lax.rsqrt((jnp.sum(qn * qn, axis=-1, keepdims=True) + ssr) / MLA_DQK + EPS))
        kn = kv[:, h * dn:(h + 1) * dn]
        k_rs.append(lax.rsqrt((jnp.sum(kn * kn, axis=-1, keepdims=True) + kpe_ss) / MLA_DQK + EPS))

    def per_group(vals):
        out = vals[H - 1]
        for h in range(H - 2, -1, -1):
            out = jnp.where(grp == h, vals[h], out)
        return out

    cos, sin, rot = cos_ref[...], sin_ref[...], rot_ref[...]
    tq = qr * per_group(q_rs) * qgr_ref[...]
    tq = tq * cos + _dot(tq, rot, HI) * sin
    tk = kpe4 * per_group(k_rs) * kgr_ref[...]
    tk = tk * cos + _dot(tk, rot, HI) * sin
    scale = MLA_DQK ** -0.5 * LOG2E
    half_id = lane // dr
    ones = _ones_lane0(n)
    q_parts, k_parts, v_parts = [], [], []
    for h in range(H):
        blk = slice((h // 2) * LANES, (h // 2 + 1) * LANES)
        q_parts.append(qq[:, h * dn:(h + 1) * dn] * q_rs[h] * (qgn_ref[...] * scale))
        q_parts.append(jnp.where(half_id == h % 2, tq[:, blk] * scale, 0.0))
        k_parts.append(kv[:, h * dn:(h + 1) * dn] * k_rs[h] * kgn_ref[...])
        k_parts.append(tk[:, blk])
        v_parts.append(kv[:, H * dn + h * MLA_V:H * dn + (h + 1) * MLA_V])
        v_parts.append(ones)
    q_ref[0] = jnp.concatenate(q_parts, axis=1).astype(q_ref.dtype)
    k_ref[0] = jnp.concatenate(k_parts, axis=1).astype(k_ref.dtype)
    v_ref[0] = jnp.concatenate(v_parts, axis=1).astype(v_ref.dtype)


def _rope_consts(T):
    H, dr = MLA_HEADS, MLA_ROPE
    inv = 1.0 / (ROPE_BASE ** (jnp.arange(0, dr, 2, dtype=F32) / dr))
    ang = jnp.arange(T, dtype=F32)[:, None] * inv[None, :]
    ang = jnp.concatenate([ang, ang], axis=-1)
    cos4 = jnp.tile(jnp.cos(ang), (1, H))
    sin4 = jnp.tile(jnp.sin(ang), (1, H))
    rot = np.zeros((H * dr, H * dr), np.float32)
    for h in range(H):
        for c in range(dr // 2):
            rot[h * dr + c + dr // 2, h * dr + c] = -1.0
            rot[h * dr + c, h * dr + c + dr // 2] = 1.0
    expand = np.zeros((LANES, H * dr), np.float32)
    for h in range(H):
        for c in range(dr):
            expand[L_KPE + c, h * dr + c] = 1.0
    return cos4, sin4, jnp.asarray(rot), jnp.asarray(expand)


def _mla_prep(proj, rope, qa_g, wq_b, kva_g, wkv_b, qn_g, kn_g):
    B, T, _ = proj.shape
    n, H, dn, dr, dv = ROW_TILE, MLA_HEADS, MLA_NOPE, MLA_ROPE, MLA_V
    cos4, sin4, rot, expand = rope
    wq = wq_b.reshape(MLA_Q_RANK, H, MLA_DQK)
    wq = jnp.concatenate([wq[:, :, :dn].reshape(MLA_Q_RANK, H * dn),
                          wq[:, :, dn:].reshape(MLA_Q_RANK, H * dr)], axis=1).astype(BF16)
    wkv = wkv_b.reshape(MLA_KV_RANK, H, dn + dv)
    wkv = jnp.concatenate([wkv[:, :, :dn].reshape(MLA_KV_RANK, H * dn),
                           wkv[:, :, dn:].reshape(MLA_KV_RANK, H * dv)], axis=1).astype(BF16)
    qg, kg = qn_g.astype(F32), kn_g.astype(F32)
    const = lambda shape: pl.BlockSpec(shape, lambda b, t: (0,) * len(shape))
    QW = H * 2 * LANES
    return pl.pallas_call(
        _mla_prep_kernel,
        out_shape=(jax.ShapeDtypeStruct((B, T, QW), BF16), jax.ShapeDtypeStruct((B, T, QW), BF16),
                   jax.ShapeDtypeStruct((B, T, QW), BF16)),
        grid=(B, T // n),
        in_specs=[pl.BlockSpec((1, n, MLA_Q_RANK), lambda b, t: (b, t, COL_MLA_QA // MLA_Q_RANK)),
                  pl.BlockSpec((1, n, MLA_KV_RANK), lambda b, t: (b, t, COL_MLA_KVA // MLA_KV_RANK)),
                  pl.BlockSpec((1, n, LANES), lambda b, t: (b, t, COL_SMALL // LANES)),
                  pl.BlockSpec((n, H * dr), lambda b, t: (t, 0)),
                  pl.BlockSpec((n, H * dr), lambda b, t: (t, 0)),
                  const((1, MLA_Q_RANK)), const(wq.shape), const((1, MLA_KV_RANK)), const(wkv.shape),
                  const((1, dn)), const((1, H * dr)), const((1, dn)), const((1, H * dr)),
                  const(rot.shape), const(expand.shape)],
        out_specs=(pl.BlockSpec((1, n, QW), lambda b, t: (b, t, 0)),
                   pl.BlockSpec((1, n, QW), lambda b, t: (b, t, 0)),
                   pl.BlockSpec((1, n, QW), lambda b, t: (b, t, 0))),
        compiler_params=_cparams(("parallel", "parallel")),
        name="mla_prep",
    )(proj, proj, proj, cos4, sin4,
      qa_g.reshape(1, -1).astype(F32), wq, kva_g.reshape(1, -1).astype(F32), wkv,
      qg[:dn].reshape(1, dn), jnp.tile(qg[dn:], H).reshape(1, H * dr),
      kg[:dn].reshape(1, dn), jnp.tile(kg[dn:], H).reshape(1, H * dr), rot, expand)


def _flash_kernel(q_ref, k_ref, v_ref, o_ref, m_ref, acc_ref, *, tq):
    qi = pl.program_id(2)
    nh = m_ref.shape[0]
    dqk, dv2 = q_ref.shape[-1] // nh, v_ref.shape[-1] // nh
    dv = dv2 // 2
    rb = ROW_TILE
    m_ref[...] = jnp.full(m_ref.shape, NEG, F32)
    acc_ref[...] = jnp.zeros(acc_ref.shape, F32)

    def step(j, masked):
        start = pl.multiple_of(j * tq, tq)
        chains = [(h, r) for h in range(nh) for r in range(0, tq, rb)]
        logits, probs = {}, {}

        def qk(c):
            h, r = chains[c]
            logits[c] = _dot_nt(q_ref[0, r:r + rb, h * dqk:(h + 1) * dqk],
                                k_ref[0, pl.ds(start, tq), h * dqk:(h + 1) * dqk])

        def softmax(c):
            h, r = chains[c]
            s = logits.pop(c)
            if masked:
                row = lax.broadcasted_iota(jnp.int32, s.shape, 0) + r
                col = lax.broadcasted_iota(jnp.int32, s.shape, 1)
                s = jnp.where(col <= row, s, NEG)
            lane_max = s[:, :LANES]
            for cb in range(1, tq // LANES):
                lane_max = jnp.maximum(lane_max, s[:, cb * LANES:(cb + 1) * LANES])
            m_old = m_ref[h, r:r + rb]
            m_new = jnp.maximum(m_old, jnp.max(lane_max, axis=-1, keepdims=True))
            m_ref[h, r:r + rb] = m_new
            probs[c] = (jnp.exp2(s - m_new).astype(BF16), jnp.exp2(m_old - m_new))

        def pv(c):
            h, r = chains[c]
            p, alpha = probs.pop(c)
            acc_ref[h, r:r + rb] = (alpha * acc_ref[h, r:r + rb]
                                    + _dot(p, v_ref[0, pl.ds(start, tq), h * dv2:(h + 1) * dv2]))

        for c in range(len(chains)):
            qk(c)
            softmax(c)
            pv(c)

    def body(jj, carry):
        step(2 * jj, False)
        step(2 * jj + 1, False)
        return carry

    lax.fori_loop(0, qi // 2, body, 0)

    @pl.when(qi % 2 == 1)
    def _():
        step(qi - 1, False)

    step(qi, True)
    outs = []
    for h in range(nh):
        acc = acc_ref[h]
        outs.append(acc[:, :dv] / acc[:, dv:dv + 1])
    o_ref[0] = jnp.concatenate(outs, axis=1).astype(o_ref.dtype)


def _flash(q, k, v):
    B, T, QW = q.shape
    H, nh = MLA_HEADS, FLASH_HEADS_PER_STEP
    dqk, dv2 = QW // H, v.shape[-1] // H
    dv = dv2 // 2
    tq = next(t for t in FLASH_TILES if T % t == 0)
    return pl.pallas_call(
        functools.partial(_flash_kernel, tq=tq),
        out_shape=jax.ShapeDtypeStruct((B, T, H * dv), BF16),
        grid=(B, H // nh, T // tq),
        in_specs=[pl.BlockSpec((1, tq, nh * dqk), lambda b, h, i: (b, i, h)),
                  pl.BlockSpec((1, T, nh * dqk), lambda b, h, i: (b, 0, h)),
                  pl.BlockSpec((1, T, nh * dv2), lambda b, h, i: (b, 0, h))],
        out_specs=pl.BlockSpec((1, tq, nh * dv), lambda b, h, i: (b, i, h)),
        scratch_shapes=[pltpu.VMEM((nh, tq, 1), F32), pltpu.VMEM((nh, tq, dv2), F32)],
        compiler_params=_cparams(("parallel", "parallel", "arbitrary")),
        name="flash",
    )(q, k, v)


def _reorder_kernel(w_ref, o_ref):
    w = w_ref[...]
    n = w.shape[0]
    z = lambda width: jnp.zeros((n, width), w.dtype)
    small = [w[:, 2048:2052], w[:, 2052:2056], w[:, 4424:4428], w[:, 5964:5972], w[:, 2824:2888]]
    n_small = sum(s.shape[1] for s in small)
    cols = [w[:, 0:2048],
            w[:, 4940:5964],
            w[:, 4428:4940],
            w[:, 2056:2824],
            *small, z(LANES - n_small), z(COL_FOX_QKV - COL_SMALL - LANES),
            w[:, 2888:4424]]
    o_ref[...] = jnp.concatenate(cols, axis=1).astype(o_ref.dtype)


def _reorder_w_in(w_in):
    D, W = w_in.shape
    rows = ROW_TILE
    return pl.pallas_call(
        _reorder_kernel,
        out_shape=jax.ShapeDtypeStruct((D, PROJ_W), BF16),
        grid=(D // rows,),
        in_specs=[pl.BlockSpec((rows, W), lambda i: (i, 0))],
        out_specs=pl.BlockSpec((rows, PROJ_W), lambda i: (i, 0)),
        compiler_params=_cparams(("parallel",)),
        name="reorder_w_in",
    )(w_in)


def _mixer(h, hn, B, T, rope, w_in, gdn_conv_w, gdn_A_log, gdn_dt_bias, gdn_norm_g,
           mla_qa_g, mla_wq_b, mla_kva_g, mla_wkv_b, mla_qn_g, mla_kn_g,
           fox_qn_g, fox_kn_g, fox_b_f,
           ssd_conv_w, ssd_conv_b, ssd_dt_bias, ssd_A_log, ssd_D, ssd_norm_g,
           w_gate, w_branch, w_o):
    M = B * T
    proj = _matmul(hn, _reorder_w_in(w_in), tn=1024, name="in_proj").reshape(B, T, PROJ_W)
    o_gdn = _gdn(proj, gdn_conv_w, gdn_A_log, gdn_dt_bias, gdn_norm_g)
    mq, mk, mv = _mla_prep(proj, rope, mla_qa_g, mla_wq_b, mla_kva_g, mla_wkv_b, mla_qn_g, mla_kn_g)
    o_mla = _flash(mq, mk, mv)
    o_fox = _flash(*_fox_prep(proj, fox_qn_g, fox_kn_g, fox_b_f))
    o_ssd = _ssd(proj, ssd_conv_w, ssd_conv_b, ssd_dt_bias, ssd_A_log, ssd_D, ssd_norm_g)
    branches = [o.reshape(M, BRANCH_W) for o in (o_gdn, o_mla, o_fox, o_ssd)]
    merged = _merge(hn, branches, w_gate.astype(BF16), w_branch.astype(BF16))
    return _matmul(merged, w_o.astype(BF16), tn=w_o.shape[1], residual=h, name="out_proj")


def kernel(x, meta_tokens, mix_norm_g, w_in, gdn_conv_w, gdn_A_log, gdn_dt_bias, gdn_norm_g, mla_qa_g, mla_wq_b, mla_kva_g, mla_wkv_b, mla_qn_g, mla_kn_g, fox_qn_g, fox_kn_g, fox_b_f, ssd_conv_w, ssd_conv_b, ssd_dt_bias, ssd_A_log, ssd_D, ssd_norm_g, w_gate, w_branch, w_o, ffn_norm_g, dense_w_gate, dense_w_up, dense_w_down, router_w, moe_w_gate, moe_w_up, moe_w_down):
    B, S, D = x.shape
    L = N_META + S
    T = -(-L // ROW_TILE) * ROW_TILE
    assert (B * T) % MM_TM == 0
    depth = w_in.shape[0]
    meta = jnp.broadcast_to(meta_tokens[None].astype(x.dtype), (B, N_META, D))
    h = jnp.concatenate([meta, x, jnp.zeros((B, T - L, D), x.dtype)], axis=1).reshape(B * T, D)
    rope = _rope_consts(T)
    for layer in range(depth):
        hn = _rmsnorm(h, mix_norm_g[layer])
        h = _mixer(h, hn, B, T, rope, w_in[layer],
                   gdn_conv_w[layer], gdn_A_log[layer], gdn_dt_bias[layer], gdn_norm_g[layer],
                   mla_qa_g[layer], mla_wq_b[layer], mla_kva_g[layer], mla_wkv_b[layer],
                   mla_qn_g[layer], mla_kn_g[layer],
                   fox_qn_g[layer], fox_kn_g[layer], fox_b_f[layer],
                   ssd_conv_w[layer], ssd_conv_b[layer], ssd_dt_bias[layer], ssd_A_log[layer],
                   ssd_D[layer], ssd_norm_g[layer],
                   w_gate[layer], w_branch[layer], w_o[layer])
        i = layer // 2
        if layer % 2 == 0:
            hn = _rmsnorm(h, ffn_norm_g[layer])
            act = _swiglu_up(hn, dense_w_gate[i].astype(BF16), dense_w_up[i].astype(BF16), tn=512)
            h = _matmul(act, dense_w_down[i].astype(BF16), tn=512, residual=h, name="ffn_down")
        else:
            hn, rec, cnt_before = _rmsnorm_router(h, ffn_norm_g[layer], router_w[i], B, T)
            h = _moe(h, hn, rec, cnt_before, moe_w_gate[i].astype(BF16), moe_w_up[i].astype(BF16),
                     moe_w_down[i].astype(BF16), B, T)
    return h.reshape(B, T, D)[:, N_META:L].astype(x.dtype)
```

```python
import functools
import math

import numpy as np
import jax
import jax.numpy as jnp
from jax import lax
from jax.experimental import pallas as pl
from jax.experimental.pallas import tpu as pltpu

F32 = jnp.float32
BF16 = jnp.bfloat16
HI = lax.Precision.HIGHEST
NT_DIMS = (((1,), (1,)), ((), ()))

D_MODEL = 2048
N_META = 16
EPS = 1e-6
NEG = -1e30
CONV_K = 4

GDN_HEADS, GDN_DK, GDN_DV = 4, 128, 128
MLA_HEADS, MLA_Q_RANK, MLA_KV_RANK, MLA_NOPE, MLA_ROPE, MLA_V = 4, 512, 256, 128, 64, 128
MLA_DQK = MLA_NOPE + MLA_ROPE
ROPE_BASE = 10000.0
FOX_HEADS, FOX_DH = 4, 128
SSD_HEADS, SSD_HEADDIM, SSD_GROUPS, SSD_STATE = 8, 64, 2, 128
SSD_HG = SSD_HEADS // SSD_GROUPS
SSD_INNER = SSD_HEADS * SSD_HEADDIM
N_BRANCH, BRANCH_W = 4, 512
N_EXPERTS, TOP_K = 8, 2

LANES = 128
ROW_TILE = 256
FLASH_TILES = (768, 256)
FLASH_HEADS_PER_STEP = 2
LOG2E = 1.4426950408889634
MM_TM = 768
MOE_TM = 256
MOE_DISPATCH_ROWS = 128
R_E1, R_E2, R_P1, R_P2, R_R1, R_R2 = range(6)
VMEM_LIMIT = 56 * 1024 * 1024

PROJ_W = 6144
COL_GDN_QKV, COL_GDN_Z = 0, 1536
COL_SSD_XBC, COL_SSD_Z = 2048, 3072
COL_MLA_QA, COL_MLA_KVA = 3584, 4096
COL_SMALL = 4352
COL_FOX_QKV = 4608
L_BETA, L_GA, L_FF, L_DT, L_KPE = 0, 4, 8, 12, 20


def _cparams(sem, vmem=VMEM_LIMIT):
    return pltpu.CompilerParams(dimension_semantics=sem, vmem_limit_bytes=vmem)


def _softplus(x):
    return jnp.maximum(x, 0.0) + jnp.log1p(jnp.exp(-jnp.abs(x)))


def _silu(x):
    return x * jax.nn.sigmoid(x)


def _dot(a, b, precision=None):
    return jnp.dot(a, b, preferred_element_type=F32, precision=precision)


def _dot_nt(a, b):
    return lax.dot_general(a, b, NT_DIMS, preferred_element_type=F32)


def _rmsnorm_kernel(h_ref, g_ref, o_ref):
    x = h_ref[...]
    y = x * lax.rsqrt(jnp.mean(x * x, axis=-1, keepdims=True) + EPS) * g_ref[...]
    o_ref[...] = y.astype(o_ref.dtype)


def _rmsnorm(h, g):
    M, D = h.shape
    return pl.pallas_call(
        _rmsnorm_kernel,
        out_shape=jax.ShapeDtypeStruct((M, D), BF16),
        grid=(M // MM_TM,),
        in_specs=[pl.BlockSpec((MM_TM, D), lambda i: (i, 0)),
                  pl.BlockSpec((1, D), lambda i: (0, 0))],
        out_specs=pl.BlockSpec((MM_TM, D), lambda i: (i, 0)),
        compiler_params=_cparams(("parallel",)),
        name="rmsnorm",
    )(h, g.reshape(1, D).astype(F32))


def _rmsnorm_router_kernel(h_ref, g_ref, rw_ref, o_ref, rec_ref, cnt_ref, run_ref):
    @pl.when(pl.program_id(1) == 0)
    def _():
        run_ref[...] = jnp.zeros_like(run_ref)

    x = h_ref[...]
    n = x.shape[0]
    y = x * lax.rsqrt(jnp.mean(x * x, axis=-1, keepdims=True) + EPS) * g_ref[...]
    o_ref[...] = y.astype(o_ref.dtype)
    logits = _dot(y, rw_ref[...], HI)
    lane = lax.broadcasted_iota(jnp.int32, logits.shape, 1)
    logits = jnp.where(lane < N_EXPERTS, logits, NEG)
    m1 = jnp.max(logits, axis=-1, keepdims=True)
    i1 = jnp.min(jnp.where(logits == m1, lane, LANES), axis=-1, keepdims=True)
    rest = jnp.where(lane == i1, NEG, logits)
    m2 = jnp.max(rest, axis=-1, keepdims=True)
    i2 = jnp.min(jnp.where(rest == m2, lane, LANES), axis=-1, keepdims=True)
    e2 = jnp.exp(m2 - m1)
    p1 = 1.0 / (1.0 + e2)
    p2 = e2 * p1
    sel = jnp.where(lane == i1, 1.0, 0.0) + jnp.where(lane == i2, 1.0, 0.0)
    row = lax.broadcasted_iota(jnp.int32, (n, n), 0)
    col = lax.broadcasted_iota(jnp.int32, (n, n), 1)
    earlier = jnp.where(col < row, 1.0, 0.0).astype(BF16)
    run = run_ref[0:1, :]
    before = _dot(earlier, sel.astype(BF16)) + run
    r1 = jnp.sum(jnp.where(lane == i1, before, 0.0), axis=-1, keepdims=True)
    r2 = jnp.sum(jnp.where(lane == i2, before, 0.0), axis=-1, keepdims=True)
    cnt_ref[...] = jnp.broadcast_to(run, cnt_ref.shape)
    run_ref[...] = jnp.broadcast_to(before[n - 1:n] + sel[n - 1:n], run_ref.shape)
    rec = jnp.zeros_like(logits)
    for ln, val in ((R_E1, i1.astype(F32)), (R_E2, i2.astype(F32)), (R_P1, p1), (R_P2, p2), (R_R1, r1), (R_R2, r2)):
        rec = jnp.where(lane == ln, val, rec)
    rec_ref[...] = rec


def _rmsnorm_router(h, g, router_w, B, T):
    M, D = h.shape
    n = ROW_TILE
    NT = T // n
    rw = jnp.zeros((D, LANES), F32).at[:, :N_EXPERTS].set(router_w.astype(F32))
    return pl.pallas_call(
        _rmsnorm_router_kernel,
        out_shape=(jax.ShapeDtypeStruct((M, D), BF16), jax.ShapeDtypeStruct((M, LANES), F32),
                   jax.ShapeDtypeStruct((B * NT * 8, LANES), F32)),
        grid=(B, NT),
        in_specs=[pl.BlockSpec((n, D), lambda b, t: (b * NT + t, 0)),
                  pl.BlockSpec((1, D), lambda b, t: (0, 0)),
                  pl.BlockSpec((D, LANES), lambda b, t: (0, 0))],
        out_specs=(pl.BlockSpec((n, D), lambda b, t: (b * NT + t, 0)),
                   pl.BlockSpec((n, LANES), lambda b, t: (b * NT + t, 0)),
                   pl.BlockSpec((8, LANES), lambda b, t: (b * NT + t, 0))),
        scratch_shapes=[pltpu.VMEM((8, LANES), F32)],
        compiler_params=_cparams(("arbitrary", "arbitrary")),
        name="rmsnorm_router",
    )(h, g.reshape(1, D).astype(F32), rw)


def _mm_kernel(*refs, has_res, has_norm):
    a_ref, w_ref = refs[:2]
    y = _dot(a_ref[...], w_ref[...])
    if has_res:
        y = y + refs[2][...]
    if has_norm:
        g_ref, o_ref, n_ref = refs[-3:]
        n_ref[...] = (y * lax.rsqrt(jnp.mean(y * y, axis=-1, keepdims=True) + EPS) * g_ref[...]).astype(n_ref.dtype)
    else:
        o_ref = refs[-1]
    o_ref[...] = y.astype(o_ref.dtype)


def _matmul(a, w, *, tn, tm=MM_TM, residual=None, norm_g=None, out_dtype=F32, name="matmul"):
    M, K = a.shape
    N = w.shape[1]
    in_specs = [pl.BlockSpec((tm, K), lambda j, i: (i, 0)),
                pl.BlockSpec((K, tn), lambda j, i: (0, j))]
    args = [a, w]
    out_shape = jax.ShapeDtypeStruct((M, N), out_dtype)
    out_specs = pl.BlockSpec((tm, tn), lambda j, i: (i, j))
    if residual is not None:
        in_specs.append(pl.BlockSpec((tm, tn), lambda j, i: (i, j)))
        args.append(residual)
    if norm_g is not None:
        assert tn == N
        in_specs.append(pl.BlockSpec((1, N), lambda j, i: (0, 0)))
        args.append(norm_g.reshape(1, N).astype(F32))
        out_shape = (out_shape, jax.ShapeDtypeStruct((M, N), BF16))
        out_specs = (out_specs, pl.BlockSpec((tm, tn), lambda j, i: (i, j)))
    return pl.pallas_call(
        functools.partial(_mm_kernel, has_res=residual is not None, has_norm=norm_g is not None),
        out_shape=out_shape,
        grid=(N // tn, M // tm),
        in_specs=in_specs,
        out_specs=out_specs,
        compiler_params=_cparams(("parallel", "parallel")),
        name=name,
    )(*args)


def _swiglu_kernel(a_ref, wg_ref, wu_ref, o_ref):
    a = a_ref[...]
    g = _dot(a, wg_ref[...])
    u = _dot(a, wu_ref[...])
    o_ref[...] = (_silu(g) * u).astype(o_ref.dtype)


def _swiglu_up(a, wg, wu, *, tn, tm=MM_TM):
    M, K = a.shape
    F = wg.shape[1]
    return pl.pallas_call(
        _swiglu_kernel,
        out_shape=jax.ShapeDtypeStruct((M, F), BF16),
        grid=(F // tn, M // tm),
        in_specs=[pl.BlockSpec((tm, K), lambda j, i: (i, 0)),
                  pl.BlockSpec((K, tn), lambda j, i: (0, j)),
                  pl.BlockSpec((K, tn), lambda j, i: (0, j))],
        out_specs=pl.BlockSpec((tm, tn), lambda j, i: (i, j)),
        compiler_params=_cparams(("parallel", "parallel")),
        name="swiglu_up",
    )(a, wg, wu)


def _dispatch_kernel(lo_ref, nb_ref, d1_ref, d2_ref, rec_ref, hn_ref, o_ref, p_ref, acc_ref, pacc_ref, *, wb):
    i = pl.program_id(1)
    idx = pl.program_id(0) * pl.num_programs(1) + i
    tm = o_ref.shape[0]
    rows = i * tm + lax.broadcasted_iota(jnp.int32, (tm, wb), 0)
    lane = lax.broadcasted_iota(jnp.int32, (wb, LANES), 1)
    acc_ref[...] = jnp.zeros_like(acc_ref)
    pacc_ref[...] = jnp.zeros_like(pacc_ref)

    def hi_lo(p):
        hi = p.astype(BF16).astype(F32)
        return jnp.where(lane == 0, hi, jnp.where(lane == 1, p - hi, 0.0)).astype(BF16)

    def body(k, carry):
        start = pl.multiple_of((lo_ref[idx] + k) * wb, wb)
        hit1 = jnp.where(d1_ref[0, :, pl.ds(start, wb)] == rows, 1.0, 0.0)
        hit2 = jnp.where(d2_ref[0, :, pl.ds(start, wb)] == rows, 1.0, 0.0)
        acc_ref[...] += _dot((hit1 + hit2).astype(BF16), hn_ref[0, pl.ds(start, wb), :])
        rec = rec_ref[0, pl.ds(start, wb), :]
        pacc_ref[...] += (_dot(hit1.astype(BF16), hi_lo(rec[:, R_P1:R_P1 + 1]))
                          + _dot(hit2.astype(BF16), hi_lo(rec[:, R_P2:R_P2 + 1])))
        return carry

    lax.fori_loop(0, nb_ref[idx], body, 0)
    o_ref[...] = acc_ref[...].astype(o_ref.dtype)
    p_ref[...] = pacc_ref[...]


def _dispatch(hn3, rec3, d1, d2, lo_blk, n_blk, nti, *, tm, wb=ROW_TILE):
    B, T, D = hn3.shape
    dspec = pl.BlockSpec((1, 1, T), lambda b, i, lo, nb: (b, 0, 0))
    row_spec = lambda w: pl.BlockSpec((tm, w), lambda b, i, lo, nb: (b * nti + i, 0))
    grid_spec = pltpu.PrefetchScalarGridSpec(
        num_scalar_prefetch=2,
        grid=(B, nti),
        in_specs=[dspec, dspec,
                  pl.BlockSpec((1, T, LANES), lambda b, i, lo, nb: (b, 0, 0)),
                  pl.BlockSpec((1, T, D), lambda b, i, lo, nb: (b, 0, 0), pipeline_mode=pl.Buffered(1))],
        out_specs=(row_spec(D), row_spec(LANES)),
        scratch_shapes=[pltpu.VMEM((tm, D), F32), pltpu.VMEM((tm, LANES), F32)])
    return pl.pallas_call(
        functools.partial(_dispatch_kernel, wb=wb),
        out_shape=(jax.ShapeDtypeStruct((B * nti * tm, D), BF16),
                   jax.ShapeDtypeStruct((B * nti * tm, LANES), F32)),
        grid_spec=grid_spec,
        compiler_params=_cparams(("arbitrary", "arbitrary")),
        name="moe_dispatch",
    )(lo_blk, n_blk, d1.reshape(B, 1, T), d2.reshape(B, 1, T), rec3, hn3)


def _expert_up_kernel(te_ref, nb_ref, x_ref, wg_ref, wu_ref, p_ref, o_ref):
    i = pl.program_id(0)

    @pl.when(nb_ref[i] > 0)
    def _():
        x = x_ref[...]
        p = p_ref[:, 0:1] + p_ref[:, 1:2]
        o_ref[...] = (_silu(_dot(x, wg_ref[0])) * _dot(x, wu_ref[0]) * p).astype(o_ref.dtype)

    @pl.when(nb_ref[i] == 0)
    def _():
        o_ref[...] = jnp.zeros_like(o_ref)


def _expert_down_kernel(te_ref, nb_ref, a_ref, wd_ref, o_ref):
    i = pl.program_id(0)

    @pl.when(nb_ref[i] > 0)
    def _():
        o_ref[...] = _dot(a_ref[...], wd_ref[0]).astype(o_ref.dtype)

    @pl.when(nb_ref[i] == 0)
    def _():
        o_ref[...] = jnp.zeros_like(o_ref)


def _expert_ffn(x, p_rows, tile_expert, n_blk, wg, wu, wd, *, tm):
    R, D = x.shape
    E, _, F = wg.shape
    up_spec = pltpu.PrefetchScalarGridSpec(
        num_scalar_prefetch=2,
        grid=(R // tm,),
        in_specs=[pl.BlockSpec((tm, D), lambda i, te, nb: (i, 0)),
                  pl.BlockSpec((1, D, F), lambda i, te, nb: (te[i], 0, 0)),
                  pl.BlockSpec((1, D, F), lambda i, te, nb: (te[i], 0, 0)),
                  pl.BlockSpec((tm, LANES), lambda i, te, nb: (i, 0))],
        out_specs=pl.BlockSpec((tm, F), lambda i, te, nb: (i, 0)))
    act = pl.pallas_call(
        _expert_up_kernel,
        out_shape=jax.ShapeDtypeStruct((R, F), BF16),
        grid_spec=up_spec,
        compiler_params=_cparams(("arbitrary",)),
        name="moe_up",
    )(tile_expert, n_blk, x, wg, wu, p_rows)
    down_spec = pltpu.PrefetchScalarGridSpec(
        num_scalar_prefetch=2,
        grid=(R // tm,),
        in_specs=[pl.BlockSpec((tm, F), lambda i, te, nb: (i, 0)),
                  pl.BlockSpec((1, F, D), lambda i, te, nb: (te[i], 0, 0))],
        out_specs=pl.BlockSpec((tm, D), lambda i, te, nb: (i, 0)))
    return pl.pallas_call(
        _expert_down_kernel,
        out_shape=jax.ShapeDtypeStruct((R, D), BF16),
        grid_spec=down_spec,
        compiler_params=_cparams(("arbitrary",)),
        name="moe_down",
    )(tile_expert, n_blk, act, wd)


def _combine_kernel(yb_ref, yv_ref, h_ref, d1_ref, d2_ref, *refs, tm):
    y_refs, o_ref = refs[:-1], refs[-1]
    n = o_ref.shape[0]
    base = (pl.program_id(0) * pl.num_programs(1) + pl.program_id(1)) * len(y_refs)
    d1, d2 = d1_ref[...], d2_ref[...]
    lane = lax.broadcasted_iota(jnp.int32, (n, tm), 1)
    o_ref[...] = h_ref[...]
    for k, y_ref in enumerate(y_refs):
        @pl.when(yv_ref[base + k] > 0)
        def _():
            rows = yb_ref[base + k] * tm + lane
            hit = jnp.where(d1 == rows, 1.0, 0.0) + jnp.where(d2 == rows, 1.0, 0.0)
            o_ref[...] += _dot(hit.astype(BF16), y_ref[...])


def _combine(h, y, d1, d2, y_blk, y_valid, B, T, *, tm):
    M, D = h.shape
    n = ROW_TILE
    NT = T // n
    slots = y_blk.shape[0] // (B * NT)
    y_specs = [pl.BlockSpec((tm, D), lambda b, t, yb, yv, k=k: (yb[(b * NT + t) * slots + k], 0))
               for k in range(slots)]
    tok = lambda w: pl.BlockSpec((n, w), lambda b, t, yb, yv: (b * NT + t, 0))
    grid_spec = pltpu.PrefetchScalarGridSpec(
        num_scalar_prefetch=2,
        grid=(B, NT),
        in_specs=[tok(D), tok(1), tok(1)] + y_specs,
        out_specs=tok(D))
    return pl.pallas_call(
        functools.partial(_combine_kernel, tm=tm),
        out_shape=jax.ShapeDtypeStruct((M, D), F32),
        grid_spec=grid_spec,
        compiler_params=_cparams(("arbitrary", "arbitrary")),
        name="moe_combine",
    )(y_blk, y_valid, h, d1, d2, *([y] * slots))


def _moe(h, hn, rec, cnt_before, wg, wu, wd, B, T):
    M, D = h.shape
    E, tm, n = N_EXPERTS, MOE_TM, ROW_TILE
    NT = T // n
    rb = TOP_K * T + E * tm
    i32 = jnp.int32
    col = lambda ln: rec[:, ln].reshape(B, T)
    e1, e2 = col(R_E1).astype(i32), col(R_E2).astype(i32)
    r1, r2 = col(R_R1).astype(i32), col(R_R2).astype(i32)
    cntb = cnt_before.reshape(B, NT, 8, LANES)[:, :, 0, :E].astype(i32)
    oh1, oh2 = jax.nn.one_hot(e1, E, dtype=i32), jax.nn.one_hot(e2, E, dtype=i32)
    counts = (oh1 + oh2).sum(axis=1)
    padded = (counts + tm - 1) // tm * tm
    ends = jnp.cumsum(padded, axis=1)
    off = ends - padded
    d1 = (oh1 * off[:, None, :]).sum(-1) + r1
    d2 = (oh2 * off[:, None, :]).sum(-1) + r2
    cnt_end = jnp.concatenate([cntb[:, 1:], counts[:, None, :]], axis=1)
    def tiles(rows):
        start = jnp.arange(rb // rows, dtype=i32)[None, :] * rows
        expert = jnp.minimum((start[:, :, None] >= ends[:, None, :]).sum(-1), E - 1).astype(i32)
        e_oh = jax.nn.one_hot(expert, E, dtype=i32)
        pick = lambda per_expert: (e_oh * per_expert[:, None, :]).sum(-1)
        rank_lo = start - pick(off)
        rank_hi = jnp.minimum(rank_lo + rows, pick(counts)) - 1
        through = (e_oh[:, :, None, :] * cnt_end[:, None, :, :]).sum(-1)
        lo_blk = (through <= rank_lo[..., None]).sum(-1)
        hi_blk = (through <= rank_hi[..., None]).sum(-1)
        nonempty = (start < pick(ends)) & (rank_hi >= rank_lo)
        return (expert.reshape(-1), jnp.where(nonempty, lo_blk, 0).astype(i32).reshape(-1),
                jnp.where(nonempty, hi_blk - lo_blk + 1, 0).astype(i32).reshape(-1))

    pr = MOE_DISPATCH_ROWS
    _, lo_blk, n_blk = tiles(pr)
    x, p_rows = _dispatch(hn.reshape(B, T, D), rec.reshape(B, T, LANES), d1, d2, lo_blk, n_blk, rb // pr, tm=pr)
    tile_expert, _, tile_used = tiles(tm)
    y = _expert_ffn(x, p_rows, tile_expert, tile_used, wg, wu, wd, tm=tm)
    first = off[:, None, :] + cntb
    last = off[:, None, :] + cnt_end - 1
    used = cnt_end > cntb
    span = jnp.arange(n // tm + 1, dtype=i32)
    blk = first[..., None] // tm + span
    valid = used[..., None] & (blk <= last[..., None] // tm)
    blk = jnp.minimum(blk, rb // tm - 1) + (jnp.arange(B, dtype=i32) * (rb // tm))[:, None, None, None]
    row0 = (jnp.arange(B, dtype=i32) * rb)[:, None]
    return _combine(h, y, (d1 + row0).reshape(M, 1), (d2 + row0).reshape(M, 1),
                    blk.reshape(-1).astype(i32), valid.reshape(-1).astype(i32), B, T, tm=tm)


def _merge_kernel(hn_ref, b0_ref, b1_ref, b2_ref, b3_ref, wg_ref, wb_ref, o_ref):
    hn = hn_ref[...]
    acc = None
    for b, br_ref in enumerate((b0_ref, b1_ref, b2_ref, b3_ref)):
        gate = jax.nn.sigmoid(_dot(hn, wg_ref[b]))
        term = gate * _dot(br_ref[...], wb_ref[b])
        acc = term if acc is None else acc + term
    o_ref[...] = acc.astype(o_ref.dtype)


def _merge(hn, branches, wg, wb, *, tn=512, tm=512):
    M, D = hn.shape
    N = wg.shape[2]
    bspec = pl.BlockSpec((tm, BRANCH_W), lambda j, i: (i, 0))
    return pl.pallas_call(
        _merge_kernel,
        out_shape=jax.ShapeDtypeStruct((M, N), BF16),
        grid=(N // tn, M // tm),
        in_specs=[pl.BlockSpec((tm, D), lambda j, i: (i, 0)), bspec, bspec, bspec, bspec,
                  pl.BlockSpec((N_BRANCH, D, tn), lambda j, i: (0, 0, j)),
                  pl.BlockSpec((N_BRANCH, BRANCH_W, tn), lambda j, i: (0, 0, j))],
        out_specs=pl.BlockSpec((tm, tn), lambda j, i: (i, j)),
        compiler_params=_cparams(("parallel", "parallel")),
        name="gate_merge",
    )(hn, *branches, wg, wb)


def _causal_conv(x, carry_ref, cw):
    n = x.shape[0]
    xext = jnp.concatenate([carry_ref[...], x], axis=0)
    y = cw[0:1] * xext[5:5 + n]
    for i in range(1, CONV_K):
        y = y + cw[i:i + 1] * xext[5 + i:5 + i + n]
    carry_ref[...] = x[n - 8:n]
    return y


def _tile_masks(n, chunk):
    row = lax.broadcasted_iota(jnp.int32, (n, n), 0)
    col = lax.broadcasted_iota(jnp.int32, (n, n), 1)
    if chunk == n:
        return col <= row, col < row
    in_chunk = col >= (row // chunk) * chunk
    return in_chunk & (col <= row), in_chunk & (col < row)


def _gdn_kernel(qkv_ref, z_ref, sm_ref, cw_ref, alog_ref, dtb_ref, ng_ref, o_ref, s_ref, carry_ref):
    @pl.when(pl.program_id(1) == 0)
    def _():
        s_ref[...] = jnp.zeros_like(s_ref)
        carry_ref[...] = jnp.zeros_like(carry_ref)

    n = ROW_TILE
    y = _silu(_causal_conv(qkv_ref[0], carry_ref, cw_ref[...]))
    sm = sm_ref[0]
    z = z_ref[0]
    beta_all = jax.nn.sigmoid(sm)
    g_all = -jnp.exp(alog_ref[...]) * _softplus(sm + dtb_ref[...])
    row = lax.broadcasted_iota(jnp.int32, (n, n), 0)
    col = lax.broadcasted_iota(jnp.int32, (n, n), 1)
    causal, strict = col <= row, col < row
    diff_bits = row ^ col
    levels = int(math.log2(n))
    level_masks = [(diff_bits >= (1 << l)) & (diff_bits < (2 << l)) for l in range(levels)]
    eye = jnp.where(row == col, 1.0, 0.0)
    gcs_all = _dot(causal.astype(F32), g_all, HI)
    gcs_t = gcs_all.T
    heads = []
    for h in range(GDN_HEADS):
        q = y[:, h * GDN_DK:(h + 1) * GDN_DK]
        k = y[:, GDN_HEADS * GDN_DK + h * GDN_DK:GDN_HEADS * GDN_DK + (h + 1) * GDN_DK]
        v = y[:, 2 * GDN_HEADS * GDN_DK + h * GDN_DV:2 * GDN_HEADS * GDN_DK + (h + 1) * GDN_DV]
        q = q * lax.rsqrt(jnp.sum(q * q, axis=-1, keepdims=True) + EPS) * GDN_DK ** -0.5
        k = k * lax.rsqrt(jnp.sum(k * k, axis=-1, keepdims=True) + EPS)
        beta = beta_all[:, L_BETA + h:L_BETA + h + 1]
        gc = gcs_all[:, L_GA + h:L_GA + h + 1]
        gr = gcs_t[L_GA + h:L_GA + h + 1, :]
        decay = jnp.exp(jnp.where(causal, gc - gr, NEG))
        kb = k * beta
        k16 = k.astype(BF16)
        a = jnp.where(strict, _dot_nt(kb.astype(BF16), k16) * decay, 0.0)
        heads.append(dict(q=q, k=k, v=v, beta=beta, gc=gc, decay=decay, kb=kb, k16=k16, a=a,
                          t=eye - jnp.where(level_masks[0], a, 0.0)))
    for l in range(1, levels):
        for hd in heads:
            t16 = hd["t"].astype(BF16)
            a_l = jnp.where(level_masks[l], hd["a"], 0.0).astype(BF16)
            hd["t"] = hd["t"] - _dot(_dot(t16, a_l).astype(BF16), t16)
    outs = []
    for h, hd in enumerate(heads):
        q, k, gc = hd["q"], hd["k"], hd["gc"]
        egc = jnp.exp(gc)
        rhs = jnp.concatenate([hd["v"] * hd["beta"], hd["kb"] * egc], axis=1)
        uw = _dot(hd["t"].astype(BF16), rhs.astype(BF16))
        u, w = uw[:, :GDN_DV], uw[:, GDN_DV:]
        att = _dot_nt(q.astype(BF16), hd["k16"]) * hd["decay"]
        g_last = gc[n - 1:n, :]
        kd = k * jnp.exp(g_last - gc)
        s = s_ref[h]
        ws = _dot(jnp.concatenate([w, q * egc], axis=0).astype(BF16), s.astype(BF16))
        v_new = (u - ws[:n]).astype(BF16)
        o = ws[n:] + _dot(att.astype(BF16), v_new)
        s_ref[h] = s * jnp.exp(g_last) + _dot(kd.T.astype(BF16), v_new)
        o = o * lax.rsqrt(jnp.mean(o * o, axis=-1, keepdims=True) + EPS) * ng_ref[...]
        outs.append(o * _silu(z[:, h * GDN_DV:(h + 1) * GDN_DV]))
    o_ref[0] = jnp.concatenate(outs, axis=1).astype(o_ref.dtype)


def _lane_vec(vals, lane0):
    v = jnp.zeros((1, LANES), F32)
    return v.at[0, lane0:lane0 + vals.shape[0]].set(vals.astype(F32))


def _gdn(proj, conv_w, a_log, dt_bias, norm_g):
    B, T, _ = proj.shape
    n = ROW_TILE
    W = 2 * GDN_HEADS * GDN_DK + GDN_HEADS * GDN_DV
    ZW = GDN_HEADS * GDN_DV
    return pl.pallas_call(
        _gdn_kernel,
        out_shape=jax.ShapeDtypeStruct((B, T, ZW), BF16),
        grid=(B, T // n),
        in_specs=[pl.BlockSpec((1, n, W), lambda b, t: (b, t, COL_GDN_QKV // W)),
                  pl.BlockSpec((1, n, ZW), lambda b, t: (b, t, COL_GDN_Z // ZW)),
                  pl.BlockSpec((1, n, LANES), lambda b, t: (b, t, COL_SMALL // LANES)),
                  pl.BlockSpec((CONV_K, W), lambda b, t: (0, 0)),
                  pl.BlockSpec((1, LANES), lambda b, t: (0, 0)),
                  pl.BlockSpec((1, LANES), lambda b, t: (0, 0)),
                  pl.BlockSpec((1, GDN_DV), lambda b, t: (0, 0))],
        out_specs=pl.BlockSpec((1, n, ZW), lambda b, t: (b, t, 0)),
        scratch_shapes=[pltpu.VMEM((GDN_HEADS, GDN_DK, GDN_DV), F32),
                        pltpu.VMEM((8, W), F32)],
        compiler_params=_cparams(("arbitrary", "arbitrary")),
        name="gdn",
    )(proj, proj, proj, conv_w.astype(F32), _lane_vec(a_log, L_GA), _lane_vec(dt_bias, L_GA),
      norm_g.reshape(1, GDN_DV).astype(F32))


def _ssd_kernel(xbc_ref, z_ref, sm_ref, cw_ref, cb_ref, alog_ref, dtb_ref, dvec_ref, ng_ref,
                o_ref, hs_ref, carry_ref):
    @pl.when(pl.program_id(1) == 0)
    def _():
        hs_ref[...] = jnp.zeros_like(hs_ref)
        carry_ref[...] = jnp.zeros_like(carry_ref)

    n, P, N = ROW_TILE, SSD_HEADDIM, SSD_STATE
    y = _silu(_causal_conv(xbc_ref[0], carry_ref, cw_ref[...]) + cb_ref[...])
    xs = y[:, :SSD_INNER]
    bm = y[:, SSD_INNER:SSD_INNER + SSD_GROUPS * N]
    cm = y[:, SSD_INNER + SSD_GROUPS * N:]
    sm = sm_ref[0]
    dt_all = _softplus(sm + dtb_ref[...])
    a_all = dt_all * (-jnp.exp(alog_ref[...]))
    causal, _ = _tile_masks(n, n)
    acs_all = _dot(causal.astype(F32), a_all, HI)
    acs_t = acs_all.T
    ys = []
    for g in range(SSD_GROUPS):
        bg = bm[:, g * N:(g + 1) * N]
        cg16 = cm[:, g * N:(g + 1) * N].astype(BF16)
        cb = _dot_nt(cg16, bg.astype(BF16))
        bgt16 = bg.T.astype(BF16)
        for j in range(SSD_HG):
            hh = g * SSD_HG + j
            ac = acs_all[:, L_DT + hh:L_DT + hh + 1]
            ar = acs_t[L_DT + hh:L_DT + hh + 1, :]
            lmat = jnp.exp(jnp.where(causal, ac - ar, NEG))
            xdt = xs[:, hh * P:(hh + 1) * P] * dt_all[:, L_DT + hh:L_DT + hh + 1]
            y_diag = _dot((cb * lmat).astype(BF16), xdt.astype(BF16))
            a_last = ac[n - 1:n, :]
            st = _dot(bgt16, (xdt * jnp.exp(a_last - ac)).astype(BF16))
            h_prev = hs_ref[hh]
            y_off = _dot(cg16, h_prev.astype(BF16)) * jnp.exp(ac)
            hs_ref[hh] = h_prev * jnp.exp(a_last) + st
            ys.append(y_diag + y_off)
    yy = jnp.concatenate(ys, axis=1) + xs * dvec_ref[...]
    yy = yy * _silu(z_ref[0])
    gw = SSD_HG * P
    outs = []
    for g in range(SSD_GROUPS):
        seg = yy[:, g * gw:(g + 1) * gw]
        outs.append(seg * lax.rsqrt(jnp.mean(seg * seg, axis=-1, keepdims=True) + EPS)
                    * ng_ref[:, g * gw:(g + 1) * gw])
    o_ref[0] = jnp.concatenate(outs, axis=1).astype(o_ref.dtype)


def _ssd(proj, conv_w, conv_b, dt_bias, a_log, d_skip, norm_g):
    B, T, _ = proj.shape
    n = ROW_TILE
    W = SSD_INNER + 2 * SSD_GROUPS * SSD_STATE
    dvec = jnp.repeat(d_skip.astype(F32), SSD_HEADDIM).reshape(1, SSD_INNER)
    return pl.pallas_call(
        _ssd_kernel,
        out_shape=jax.ShapeDtypeStruct((B, T, SSD_INNER), BF16),
        grid=(B, T // n),
        in_specs=[pl.BlockSpec((1, n, W), lambda b, t: (b, t, COL_SSD_XBC // W)),
                  pl.BlockSpec((1, n, SSD_INNER), lambda b, t: (b, t, COL_SSD_Z // SSD_INNER)),
                  pl.BlockSpec((1, n, LANES), lambda b, t: (b, t, COL_SMALL // LANES)),
                  pl.BlockSpec((CONV_K, W), lambda b, t: (0, 0)),
                  pl.BlockSpec((1, W), lambda b, t: (0, 0)),
                  pl.BlockSpec((1, LANES), lambda b, t: (0, 0)),
                  pl.BlockSpec((1, LANES), lambda b, t: (0, 0)),
                  pl.BlockSpec((1, SSD_INNER), lambda b, t: (0, 0)),
                  pl.BlockSpec((1, SSD_INNER), lambda b, t: (0, 0))],
        out_specs=pl.BlockSpec((1, n, SSD_INNER), lambda b, t: (b, t, 0)),
        scratch_shapes=[pltpu.VMEM((SSD_HEADS, SSD_STATE, SSD_HEADDIM), F32),
                        pltpu.VMEM((8, W), F32)],
        compiler_params=_cparams(("arbitrary", "arbitrary")),
        name="ssd",
    )(proj, proj, proj, conv_w.astype(F32), conv_b.reshape(1, W).astype(F32),
      _lane_vec(a_log, L_DT), _lane_vec(dt_bias, L_DT), dvec,
      norm_g.reshape(1, SSD_INNER).astype(F32))


def _ones_lane0(n):
    lane = lax.broadcasted_iota(jnp.int32, (n, LANES), 1)
    return jnp.where(lane == 0, 1.0, 0.0)


def _fox_prep_kernel(qkv_ref, sm_ref, qg_ref, kg_ref, bf_ref, q_ref, k_ref, v_ref, run_ref):
    @pl.when(pl.program_id(1) == 0)
    def _():
        run_ref[...] = jnp.zeros_like(run_ref)

    n, dh, H = ROW_TILE, FOX_DH, FOX_HEADS
    x = qkv_ref[0]
    log_f = -_softplus(-(sm_ref[0] + bf_ref[...]))
    causal, _ = _tile_masks(n, n)
    cum = _dot(causal.astype(F32), log_f, HI) + run_ref[0:1, :]
    run_ref[...] = jnp.broadcast_to(cum[n - 1:n, :], run_ref.shape)
    lane = lax.broadcasted_iota(jnp.int32, (n, LANES), 1)
    ones = _ones_lane0(n)
    q_bias = jnp.where(lane < 3, 1.0, 0.0)
    qs, ks, vs = [], [], []
    for h in range(H):
        q = x[:, h * dh:(h + 1) * dh]
        k = x[:, H * dh + h * dh:H * dh + (h + 1) * dh]
        qs.append(q * lax.rsqrt(jnp.mean(q * q, axis=-1, keepdims=True) + EPS)
                  * (qg_ref[...] * (dh ** -0.5 * LOG2E)))
        qs.append(q_bias)
        ks.append(k * lax.rsqrt(jnp.mean(k * k, axis=-1, keepdims=True) + EPS) * kg_ref[...])
        c = cum[:, L_FF + h:L_FF + h + 1] * (-LOG2E)
        c_hi = c.astype(BF16).astype(F32)
        c_mid = (c - c_hi).astype(BF16).astype(F32)
        c_lo = c - c_hi - c_mid
        ks.append(jnp.where(lane == 0, c_hi, jnp.where(lane == 1, c_mid, jnp.where(lane == 2, c_lo, 0.0))))
        vs.append(x[:, 2 * H * dh + h * dh:2 * H * dh + (h + 1) * dh])
        vs.append(ones)
    q_ref[0] = jnp.concatenate(qs, axis=1).astype(q_ref.dtype)
    k_ref[0] = jnp.concatenate(ks, axis=1).astype(k_ref.dtype)
    v_ref[0] = jnp.concatenate(vs, axis=1).astype(v_ref.dtype)


def _fox_prep(proj, qn_g, kn_g, b_f):
    B, T, _ = proj.shape
    n = ROW_TILE
    W = 3 * FOX_HEADS * FOX_DH
    HW = FOX_HEADS * 2 * LANES
    ospec = pl.BlockSpec((1, n, HW), lambda b, t: (b, t, 0))
    return pl.pallas_call(
        _fox_prep_kernel,
        out_shape=(jax.ShapeDtypeStruct((B, T, HW), BF16),) * 3,
        grid=(B, T // n),
        in_specs=[pl.BlockSpec((1, n, W), lambda b, t: (b, t, COL_FOX_QKV // W)),
                  pl.BlockSpec((1, n, LANES), lambda b, t: (b, t, COL_SMALL // LANES)),
                  pl.BlockSpec((1, FOX_DH), lambda b, t: (0, 0)),
                  pl.BlockSpec((1, FOX_DH), lambda b, t: (0, 0)),
                  pl.BlockSpec((1, LANES), lambda b, t: (0, 0))],
        out_specs=(ospec, ospec, ospec),
        scratch_shapes=[pltpu.VMEM((8, LANES), F32)],
        compiler_params=_cparams(("arbitrary", "arbitrary")),
        name="fox_prep",
    )(proj, proj, qn_g.reshape(1, FOX_DH).astype(F32), kn_g.reshape(1, FOX_DH).astype(F32),
      _lane_vec(b_f, L_FF))


def _mla_prep_kernel(qa_ref, kva_ref, sm_ref, cos_ref, sin_ref, qag_ref, wq_ref, kvag_ref, wkv_ref,
                     qgn_ref, qgr_ref, kgn_ref, kgr_ref, rot_ref, exp_ref, q_ref, k_ref, v_ref):
    n, H, dn, dr = ROW_TILE, MLA_HEADS, MLA_NOPE, MLA_ROPE
    qa = qa_ref[0]
    qa = qa * lax.rsqrt(jnp.mean(qa * qa, axis=-1, keepdims=True) + EPS) * qag_ref[...]
    qq = _dot(qa.astype(BF16), wq_ref[...])
    kva = kva_ref[0]
    kva = kva * lax.rsqrt(jnp.mean(kva * kva, axis=-1, keepdims=True) + EPS) * kvag_ref[...]
    kv = _dot(kva.astype(BF16), wkv_ref[...])
    sm = sm_ref[0]
    lane = lax.broadcasted_iota(jnp.int32, (n, LANES), 1)
    is_kpe = (lane >= L_KPE) & (lane < L_KPE + dr)
    kpe_ss = jnp.sum(jnp.where(is_kpe, sm * sm, 0.0), axis=-1, keepdims=True)
    kpe4 = _dot(sm, exp_ref[...], HI)
    qr = qq[:, H * dn:]
    grp = lax.broadcasted_iota(jnp.int32, (n, H * dr), 1) // dr
    q_rs, k_rs = [], []
    for h in range(H):
        qn = qq[:, h * dn:(h + 1) * dn]
        ssr = jnp.sum(jnp.where(grp == h, qr * qr, 0.0), axis=-1, keepdims=True)
        q_rs.append(lax.rsqrt((jnp.sum(qn * qn, axis=-1, keepdims=True) + ssr) / MLA_DQK + EPS))
        kn = kv[:, h * dn:(h + 1) * dn]
        k_rs.append(lax.rsqrt((jnp.sum(kn * kn, axis=-1, keepdims=True) + kpe_ss) / MLA_DQK + EPS))

    def per_group(vals):
        out = vals[H - 1]
        for h in range(H - 2, -1, -1):
            out = jnp.where(grp == h, vals[h], out)
        return out

    cos, sin, rot = cos_ref[...], sin_ref[...], rot_ref[...]
    tq = qr * per_group(q_rs) * qgr_ref[...]
    tq = tq * cos + _dot(tq, rot, HI) * sin
    tk = kpe4 * per_group(k_rs) * kgr_ref[...]
    tk = tk * cos + _dot(tk, rot, HI) * sin
    scale = MLA_DQK ** -0.5 * LOG2E
    half_id = lane // dr
    ones = _ones_lane0(n)
    q_parts, k_parts, v_parts = [], [], []
    for h in range(H):
        blk = slice((h // 2) * LANES, (h // 2 + 1) * LANES)
        q_parts.append(qq[:, h * dn:(h + 1) * dn] * q_rs[h] * (qgn_ref[...] * scale))
        q_parts.append(jnp.where(half_id == h % 2, tq[:, blk] * scale, 0.0))
        k_parts.append(kv[:, h * dn:(h + 1) * dn] * k_rs[h] * kgn_ref[...])
        k_parts.append(tk[:, blk])
        v_parts.append(kv[:, H * dn + h * MLA_V:H * dn + (h + 1) * MLA_V])
        v_parts.append(ones)
    q_ref[0] = jnp.concatenate(q_parts, axis=1).astype(q_ref.dtype)
    k_ref[0] = jnp.concatenate(k_parts, axis=1).astype(k_ref.dtype)
    v_ref[0] = jnp.concatenate(v_parts, axis=1).astype(v_ref.dtype)


def _rope_consts(T):
    H, dr = MLA_HEADS, MLA_ROPE
    inv = 1.0 / (ROPE_BASE ** (jnp.arange(0, dr, 2, dtype=F32) / dr))
    ang = jnp.arange(T, dtype=F32)[:, None] * inv[None, :]
    ang = jnp.concatenate([ang, ang], axis=-1)
    cos4 = jnp.tile(jnp.cos(ang), (1, H))
    sin4 = jnp.tile(jnp.sin(ang), (1, H))
    rot = np.zeros((H * dr, H * dr), np.float32)
    for h in range(H):
        for c in range(dr // 2):
            rot[h * dr + c + dr // 2, h * dr + c] = -1.0
            rot[h * dr + c, h * dr + c + dr // 2] = 1.0
    expand = np.zeros((LANES, H * dr), np.float32)
    for h in range(H):
        for c in range(dr):
            expand[L_KPE + c, h * dr + c] = 1.0
    return cos4, sin4, jnp.asarray(rot), jnp.asarray(expand)


def _mla_prep(proj, rope, qa_g, wq_b, kva_g, wkv_b, qn_g, kn_g):
    B, T, _ = proj.shape
    n, H, dn, dr, dv = ROW_TILE, MLA_HEADS, MLA_NOPE, MLA_ROPE, MLA_V
    cos4, sin4, rot, expand = rope
    wq = wq_b.reshape(MLA_Q_RANK, H, MLA_DQK)
    wq = jnp.concatenate([wq[:, :, :dn].reshape(MLA_Q_RANK, H * dn),
                          wq[:, :, dn:].reshape(MLA_Q_RANK, H * dr)], axis=1).astype(BF16)
    wkv = wkv_b.reshape(MLA_KV_RANK, H, dn + dv)
    wkv = jnp.concatenate([wkv[:, :, :dn].reshape(MLA_KV_RANK, H * dn),
                           wkv[:, :, dn:].reshape(MLA_KV_RANK, H * dv)], axis=1).astype(BF16)
    qg, kg = qn_g.astype(F32), kn_g.astype(F32)
    const = lambda shape: pl.BlockSpec(shape, lambda b, t: (0,) * len(shape))
    QW = H * 2 * LANES
    return pl.pallas_call(
        _mla_prep_kernel,
        out_shape=(jax.ShapeDtypeStruct((B, T, QW), BF16), jax.ShapeDtypeStruct((B, T, QW), BF16),
                   jax.ShapeDtypeStruct((B, T, QW), BF16)),
        grid=(B, T // n),
        in_specs=[pl.BlockSpec((1, n, MLA_Q_RANK), lambda b, t: (b, t, COL_MLA_QA // MLA_Q_RANK)),
                  pl.BlockSpec((1, n, MLA_KV_RANK), lambda b, t: (b, t, COL_MLA_KVA // MLA_KV_RANK)),
                  pl.BlockSpec((1, n, LANES), lambda b, t: (b, t, COL_SMALL // LANES)),
                  pl.BlockSpec((n, H * dr), lambda b, t: (t, 0)),
                  pl.BlockSpec((n, H * dr), lambda b, t: (t, 0)),
                  const((1, MLA_Q_RANK)), const(wq.shape), const((1, MLA_KV_RANK)), const(wkv.shape),
                  const((1, dn)), const((1, H * dr)), const((1, dn)), const((1, H * dr)),
                  const(rot.shape), const(expand.shape)],
        out_specs=(pl.BlockSpec((1, n, QW), lambda b, t: (b, t, 0)),
                   pl.BlockSpec((1, n, QW), lambda b, t: (b, t, 0)),
                   pl.BlockSpec((1, n, QW), lambda b, t: (b, t, 0))),
        compiler_params=_cparams(("parallel", "parallel")),
        name="mla_prep",
    )(proj, proj, proj, cos4, sin4,
      qa_g.reshape(1, -1).astype(F32), wq, kva_g.reshape(1, -1).astype(F32), wkv,
      qg[:dn].reshape(1, dn), jnp.tile(qg[dn:], H).reshape(1, H * dr),
      kg[:dn].reshape(1, dn), jnp.tile(kg[dn:], H).reshape(1, H * dr), rot, expand)


def _flash_kernel(q_ref, k_ref, v_ref, o_ref, m_ref, acc_ref, *, tq):
    qi = pl.program_id(2)
    nh = m_ref.shape[0]
    dqk, dv2 = q_ref.shape[-1] // nh, v_ref.shape[-1] // nh
    dv = dv2 // 2
    rb = ROW_TILE
    m_ref[...] = jnp.full(m_ref.shape, NEG, F32)
    acc_ref[...] = jnp.zeros(acc_ref.shape, F32)

    def step(j, masked):
        start = pl.multiple_of(j * tq, tq)
        chains = [(h, r) for h in range(nh) for r in range(0, tq, rb)]
        logits, probs = {}, {}

        def qk(c):
            h, r = chains[c]
            logits[c] = _dot_nt(q_ref[0, r:r + rb, h * dqk:(h + 1) * dqk],
                                k_ref[0, pl.ds(start, tq), h * dqk:(h + 1) * dqk])

        def softmax(c):
            h, r = chains[c]
            s = logits.pop(c)
            if masked:
                row = lax.broadcasted_iota(jnp.int32, s.shape, 0) + r
                col = lax.broadcasted_iota(jnp.int32, s.shape, 1)
                s = jnp.where(col <= row, s, NEG)
            lane_max = s[:, :LANES]
            for cb in range(1, tq // LANES):
                lane_max = jnp.maximum(lane_max, s[:, cb * LANES:(cb + 1) * LANES])
            m_old = m_ref[h, r:r + rb]
            m_new = jnp.maximum(m_old, jnp.max(lane_max, axis=-1, keepdims=True))
            m_ref[h, r:r + rb] = m_new
            probs[c] = (jnp.exp2(s - m_new).astype(BF16), jnp.exp2(m_old - m_new))

        def pv(c):
            h, r = chains[c]
            p, alpha = probs.pop(c)
            acc_ref[h, r:r + rb] = (alpha * acc_ref[h, r:r + rb]
                                    + _dot(p, v_ref[0, pl.ds(start, tq), h * dv2:(h + 1) * dv2]))

        for c in range(len(chains)):
            qk(c)
            softmax(c)
            pv(c)

    def body(jj, carry):
        step(2 * jj, False)
        step(2 * jj + 1, False)
        return carry

    lax.fori_loop(0, qi // 2, body, 0)

    @pl.when(qi % 2 == 1)
    def _():
        step(qi - 1, False)

    step(qi, True)
    outs = []
    for h in range(nh):
        acc = acc_ref[h]
        outs.append(acc[:, :dv] / acc[:, dv:dv + 1])
    o_ref[0] = jnp.concatenate(outs, axis=1).astype(o_ref.dtype)


def _flash(q, k, v):
    B, T, QW = q.shape
    H, nh = MLA_HEADS, FLASH_HEADS_PER_STEP
    dqk, dv2 = QW // H, v.shape[-1] // H
    dv = dv2 // 2
    tq = next(t for t in FLASH_TILES if T % t == 0)
    return pl.pallas_call(
        functools.partial(_flash_kernel, tq=tq),
        out_shape=jax.ShapeDtypeStruct((B, T, H * dv), BF16),
        grid=(B, H // nh, T // tq),
        in_specs=[pl.BlockSpec((1, tq, nh * dqk), lambda b, h, i: (b, i, h)),
                  pl.BlockSpec((1, T, nh * dqk), lambda b, h, i: (b, 0, h)),
                  pl.BlockSpec((1, T, nh * dv2), lambda b, h, i: (b, 0, h))],
        out_specs=pl.BlockSpec((1, tq, nh * dv), lambda b, h, i: (b, i, h)),
        scratch_shapes=[pltpu.VMEM((nh, tq, 1), F32), pltpu.VMEM((nh, tq, dv2), F32)],
        compiler_params=_cparams(("parallel", "parallel", "arbitrary")),
        name="flash",
    )(q, k, v)


def _reorder_kernel(w_ref, o_ref):
    w = w_ref[...]
    n = w.shape[0]
    z = lambda width: jnp.zeros((n, width), w.dtype)
    small = [w[:, 2048:2052], w[:, 2052:2056], w[:, 4424:4428], w[:, 5964:5972], w[:, 2824:2888]]
    n_small = sum(s.shape[1] for s in small)
    cols = [w[:, 0:2048],
            w[:, 4940:5964],
            w[:, 4428:4940],
            w[:, 2056:2824],
            *small, z(LANES - n_small), z(COL_FOX_QKV - COL_SMALL - LANES),
            w[:, 2888:4424]]
    o_ref[...] = jnp.concatenate(cols, axis=1).astype(o_ref.dtype)


def _reorder_w_in(w_in):
    D, W = w_in.shape
    rows = ROW_TILE
    return pl.pallas_call(
        _reorder_kernel,
        out_shape=jax.ShapeDtypeStruct((D, PROJ_W), BF16),
        grid=(D // rows,),
        in_specs=[pl.BlockSpec((rows, W), lambda i: (i, 0))],
        out_specs=pl.BlockSpec((rows, PROJ_W), lambda i: (i, 0)),
        compiler_params=_cparams(("parallel",)),
        name="reorder_w_in",
    )(w_in)


def _mixer(h, hn, B, T, rope, w_in, gdn_conv_w, gdn_A_log, gdn_dt_bias, gdn_norm_g,
           mla_qa_g, mla_wq_b, mla_kva_g, mla_wkv_b, mla_qn_g, mla_kn_g,
           fox_qn_g, fox_kn_g, fox_b_f,
           ssd_conv_w, ssd_conv_b, ssd_dt_bias, ssd_A_log, ssd_D, ssd_norm_g,
           w_gate, w_branch, w_o, next_norm_g=None):
    M = B * T
    proj = _matmul(hn, _reorder_w_in(w_in), tn=1024, name="in_proj").reshape(B, T, PROJ_W)
    o_gdn = _gdn(proj, gdn_conv_w, gdn_A_log, gdn_dt_bias, gdn_norm_g)
    mq, mk, mv = _mla_prep(proj, rope, mla_qa_g, mla_wq_b, mla_kva_g, mla_wkv_b, mla_qn_g, mla_kn_g)
    o_mla = _flash(mq, mk, mv)
    o_fox = _flash(*_fox_prep(proj, fox_qn_g, fox_kn_g, fox_b_f))
    o_ssd = _ssd(proj, ssd_conv_w, ssd_conv_b, ssd_dt_bias, ssd_A_log, ssd_D, ssd_norm_g)
    branches = [o.reshape(M, BRANCH_W) for o in (o_gdn, o_mla, o_fox, o_ssd)]
    merged = _merge(hn, branches, w_gate.astype(BF16), w_branch.astype(BF16))
    return _matmul(merged, w_o.astype(BF16), tn=w_o.shape[1], tm=512, residual=h, norm_g=next_norm_g,
                   name="out_proj")


def kernel(x, meta_tokens, mix_norm_g, w_in, gdn_conv_w, gdn_A_log, gdn_dt_bias, gdn_norm_g, mla_qa_g, mla_wq_b, mla_kva_g, mla_wkv_b, mla_qn_g, mla_kn_g, fox_qn_g, fox_kn_g, fox_b_f, ssd_conv_w, ssd_conv_b, ssd_dt_bias, ssd_A_log, ssd_D, ssd_norm_g, w_gate, w_branch, w_o, ffn_norm_g, dense_w_gate, dense_w_up, dense_w_down, router_w, moe_w_gate, moe_w_up, moe_w_down):
    B, S, D = x.shape
    L = N_META + S
    T = -(-L // ROW_TILE) * ROW_TILE
    assert (B * T) % MM_TM == 0
    depth = w_in.shape[0]
    meta = jnp.broadcast_to(meta_tokens[None].astype(x.dtype), (B, N_META, D))
    h = jnp.concatenate([meta, x, jnp.zeros((B, T - L, D), x.dtype)], axis=1).reshape(B * T, D)
    rope = _rope_consts(T)
    for layer in range(depth):
        hn = _rmsnorm(h, mix_norm_g[layer])
        dense = layer % 2 == 0
        mixed = _mixer(h, hn, B, T, rope, w_in[layer],
                       gdn_conv_w[layer], gdn_A_log[layer], gdn_dt_bias[layer], gdn_norm_g[layer],
                       mla_qa_g[layer], mla_wq_b[layer], mla_kva_g[layer], mla_wkv_b[layer],
                       mla_qn_g[layer], mla_kn_g[layer],
                       fox_qn_g[layer], fox_kn_g[layer], fox_b_f[layer],
                       ssd_conv_w[layer], ssd_conv_b[layer], ssd_dt_bias[layer], ssd_A_log[layer],
                       ssd_D[layer], ssd_norm_g[layer],
                       w_gate[layer], w_branch[layer], w_o[layer],
                       next_norm_g=ffn_norm_g[layer] if dense else None)
        i = layer // 2
        if dense:
            h, hn = mixed
            act = _swiglu_up(hn, dense_w_gate[i].astype(BF16), dense_w_up[i].astype(BF16), tn=512)
            h = _matmul(act, dense_w_down[i].astype(BF16), tn=512, residual=h, name="ffn_down")
        else:
            h = mixed
            hn, rec, cnt_before = _rmsnorm_router(h, ffn_norm_g[layer], router_w[i], B, T)
            h = _moe(h, hn, rec, cnt_before, moe_w_gate[i].astype(BF16), moe_w_up[i].astype(BF16),
                     moe_w_down[i].astype(BF16), B, T)
    return h.reshape(B, T, D)[:, N_META:L].astype(x.dtype)
```

```python
import functools
import math

import numpy as np
import jax
import jax.numpy as jnp
from jax import lax
from jax.experimental import pallas as pl
from jax.experimental.pallas import tpu as pltpu

F32 = jnp.float32
BF16 = jnp.bfloat16
HI = lax.Precision.HIGHEST
NT_DIMS = (((1,), (1,)), ((), ()))

D_MODEL = 2048
N_META = 16
EPS = 1e-6
NEG = -1e30
CONV_K = 4

GDN_HEADS, GDN_DK, GDN_DV = 4, 128, 128
MLA_HEADS, MLA_Q_RANK, MLA_KV_RANK, MLA_NOPE, MLA_ROPE, MLA_V = 4, 512, 256, 128, 64, 128
MLA_DQK = MLA_NOPE + MLA_ROPE
ROPE_BASE = 10000.0
FOX_HEADS, FOX_DH = 4, 128
SSD_HEADS, SSD_HEADDIM, SSD_GROUPS, SSD_STATE = 8, 64, 2, 128
SSD_HG = SSD_HEADS // SSD_GROUPS
SSD_INNER = SSD_HEADS * SSD_HEADDIM
N_BRANCH, BRANCH_W = 4, 512
N_EXPERTS, TOP_K = 8, 2

LANES = 128
ROW_TILE = 256
FLASH_TILES = (768, 256)
FLASH_HEADS_PER_STEP = 2
LOG2E = 1.4426950408889634
MM_TM = 768
MOE_TM = 256
MOE_DISPATCH_ROWS = 128
R_E1, R_E2, R_P1, R_P2, R_R1, R_R2 = range(6)
VMEM_LIMIT = 56 * 1024 * 1024

PROJ_W = 6144
COL_GDN_QKV, COL_GDN_Z = 0, 1536
COL_SSD_XBC, COL_SSD_Z = 2048, 3072
COL_MLA_QA, COL_MLA_KVA = 3584, 4096
COL_SMALL = 4352
COL_FOX_QKV = 4608
L_BETA, L_GA, L_FF, L_DT, L_KPE = 0, 4, 8, 12, 20


def _cparams(sem, vmem=VMEM_LIMIT):
    return pltpu.CompilerParams(dimension_semantics=sem, vmem_limit_bytes=vmem)


def _softplus(x):
    return jnp.maximum(x, 0.0) + jnp.log1p(jnp.exp(-jnp.abs(x)))


def _silu(x):
    return x * jax.nn.sigmoid(x)


def _dot(a, b, precision=None):
    return jnp.dot(a, b, preferred_element_type=F32, precision=precision)


def _dot_nt(a, b):
    return lax.dot_general(a, b, NT_DIMS, preferred_element_type=F32)


def _rmsnorm_kernel(h_ref, g_ref, o_ref):
    x = h_ref[...]
    y = x * lax.rsqrt(jnp.mean(x * x, axis=-1, keepdims=True) + EPS) * g_ref[...]
    o_ref[...] = y.astype(o_ref.dtype)


def _rmsnorm(h, g):
    M, D = h.shape
    return pl.pallas_call(
        _rmsnorm_kernel,
        out_shape=jax.ShapeDtypeStruct((M, D), BF16),
        grid=(M // MM_TM,),
        in_specs=[pl.BlockSpec((MM_TM, D), lambda i: (i, 0)),
                  pl.BlockSpec((1, D), lambda i: (0, 0))],
        out_specs=pl.BlockSpec((MM_TM, D), lambda i: (i, 0)),
        compiler_params=_cparams(("parallel",)),
        name="rmsnorm",
    )(h, g.reshape(1, D).astype(F32))


def _rmsnorm_router_kernel(h_ref, g_ref, rw_ref, o_ref, rec_ref, cnt_ref, run_ref):
    @pl.when(pl.program_id(1) == 0)
    def _():
        run_ref[...] = jnp.zeros_like(run_ref)

    x = h_ref[...]
    n = x.shape[0]
    y = x * lax.rsqrt(jnp.mean(x * x, axis=-1, keepdims=True) + EPS) * g_ref[...]
    o_ref[...] = y.astype(o_ref.dtype)
    logits = _dot(y, rw_ref[...], HI)
    lane = lax.broadcasted_iota(jnp.int32, logits.shape, 1)
    logits = jnp.where(lane < N_EXPERTS, logits, NEG)
    m1 = jnp.max(logits, axis=-1, keepdims=True)
    i1 = jnp.min(jnp.where(logits == m1, lane, LANES), axis=-1, keepdims=True)
    rest = jnp.where(lane == i1, NEG, logits)
    m2 = jnp.max(rest, axis=-1, keepdims=True)
    i2 = jnp.min(jnp.where(rest == m2, lane, LANES), axis=-1, keepdims=True)
    e2 = jnp.exp(m2 - m1)
    p1 = 1.0 / (1.0 + e2)
    p2 = e2 * p1
    sel = jnp.where(lane == i1, 1.0, 0.0) + jnp.where(lane == i2, 1.0, 0.0)
    row = lax.broadcasted_iota(jnp.int32, (n, n), 0)
    col = lax.broadcasted_iota(jnp.int32, (n, n), 1)
    earlier = jnp.where(col < row, 1.0, 0.0).astype(BF16)
    run = run_ref[0:1, :]
    before = _dot(earlier, sel.astype(BF16)) + run
    r1 = jnp.sum(jnp.where(lane == i1, before, 0.0), axis=-1, keepdims=True)
    r2 = jnp.sum(jnp.where(lane == i2, before, 0.0), axis=-1, keepdims=True)
    cnt_ref[...] = jnp.broadcast_to(run, cnt_ref.shape)
    run_ref[...] = jnp.broadcast_to(before[n - 1:n] + sel[n - 1:n], run_ref.shape)
    rec = jnp.zeros_like(logits)
    for ln, val in ((R_E1, i1.astype(F32)), (R_E2, i2.astype(F32)), (R_P1, p1), (R_P2, p2), (R_R1, r1), (R_R2, r2)):
        rec = jnp.where(lane == ln, val, rec)
    rec_ref[...] = rec


def _rmsnorm_router(h, g, router_w, B, T):
    M, D = h.shape
    n = ROW_TILE
    NT = T // n
    rw = jnp.zeros((D, LANES), F32).at[:, :N_EXPERTS].set(router_w.astype(F32))
    return pl.pallas_call(
        _rmsnorm_router_kernel,
        out_shape=(jax.ShapeDtypeStruct((M, D), BF16), jax.ShapeDtypeStruct((M, LANES), F32),
                   jax.ShapeDtypeStruct((B * NT * 8, LANES), F32)),
        grid=(B, NT),
        in_specs=[pl.BlockSpec((n, D), lambda b, t: (b * NT + t, 0)),
                  pl.BlockSpec((1, D), lambda b, t: (0, 0)),
                  pl.BlockSpec((D, LANES), lambda b, t: (0, 0))],
        out_specs=(pl.BlockSpec((n, D), lambda b, t: (b * NT + t, 0)),
                   pl.BlockSpec((n, LANES), lambda b, t: (b * NT + t, 0)),
                   pl.BlockSpec((8, LANES), lambda b, t: (b * NT + t, 0))),
        scratch_shapes=[pltpu.VMEM((8, LANES), F32)],
        compiler_params=_cparams(("arbitrary", "arbitrary")),
        name="rmsnorm_router",
    )(h, g.reshape(1, D).astype(F32), rw)


def _mm_kernel(*refs, has_res, has_norm):
    a_ref, w_ref = refs[:2]
    y = _dot(a_ref[...], w_ref[...])
    if has_res:
        y = y + refs[2][...]
    if has_norm:
        g_ref, o_ref, n_ref = refs[-3:]
        n_ref[...] = (y * lax.rsqrt(jnp.mean(y * y, axis=-1, keepdims=True) + EPS) * g_ref[...]).astype(n_ref.dtype)
    else:
        o_ref = refs[-1]
    o_ref[...] = y.astype(o_ref.dtype)


def _matmul(a, w, *, tn, tm=MM_TM, residual=None, norm_g=None, out_dtype=F32, name="matmul"):
    M, K = a.shape
    N = w.shape[1]
    in_specs = [pl.BlockSpec((tm, K), lambda j, i: (i, 0)),
                pl.BlockSpec((K, tn), lambda j, i: (0, j))]
    args = [a, w]
    out_shape = jax.ShapeDtypeStruct((M, N), out_dtype)
    out_specs = pl.BlockSpec((tm, tn), lambda j, i: (i, j))
    if residual is not None:
        in_specs.append(pl.BlockSpec((tm, tn), lambda j, i: (i, j)))
        args.append(residual)
    if norm_g is not None:
        assert tn == N
        in_specs.append(pl.BlockSpec((1, N), lambda j, i: (0, 0)))
        args.append(norm_g.reshape(1, N).astype(F32))
        out_shape = (out_shape, jax.ShapeDtypeStruct((M, N), BF16))
        out_specs = (out_specs, pl.BlockSpec((tm, tn), lambda j, i: (i, j)))
    return pl.pallas_call(
        functools.partial(_mm_kernel, has_res=residual is not None, has_norm=norm_g is not None),
        out_shape=out_shape,
        grid=(N // tn, M // tm),
        in_specs=in_specs,
        out_specs=out_specs,
        compiler_params=_cparams(("parallel", "parallel")),
        name=name,
    )(*args)


def _swiglu_kernel(a_ref, wg_ref, wu_ref, o_ref):
    a = a_ref[...]
    g = _dot(a, wg_ref[...])
    u = _dot(a, wu_ref[...])
    o_ref[...] = (_silu(g) * u).astype(o_ref.dtype)


def _swiglu_up(a, wg, wu, *, tn, tm=MM_TM):
    M, K = a.shape
    F = wg.shape[1]
    return pl.pallas_call(
        _swiglu_kernel,
        out_shape=jax.ShapeDtypeStruct((M, F), BF16),
        grid=(F // tn, M // tm),
        in_specs=[pl.BlockSpec((tm, K), lambda j, i: (i, 0)),
                  pl.BlockSpec((K, tn), lambda j, i: (0, j)),
                  pl.BlockSpec((K, tn), lambda j, i: (0, j))],
        out_specs=pl.BlockSpec((tm, tn), lambda j, i: (i, j)),
        compiler_params=_cparams(("parallel", "parallel")),
        name="swiglu_up",
    )(a, wg, wu)


def _dispatch_kernel(lo_ref, nb_ref, d1_ref, d2_ref, rec_ref, hn_ref, o_ref, p_ref, acc_ref, pacc_ref, *, wb):
    i = pl.program_id(1)
    idx = pl.program_id(0) * pl.num_programs(1) + i
    tm = o_ref.shape[0]
    rows = i * tm + lax.broadcasted_iota(jnp.int32, (tm, wb), 0)
    lane = lax.broadcasted_iota(jnp.int32, (wb, LANES), 1)
    acc_ref[...] = jnp.zeros_like(acc_ref)
    pacc_ref[...] = jnp.zeros_like(pacc_ref)

    def hi_lo(p):
        hi = p.astype(BF16).astype(F32)
        return jnp.where(lane == 0, hi, jnp.where(lane == 1, p - hi, 0.0)).astype(BF16)

    def body(k, carry):
        start = pl.multiple_of((lo_ref[idx] + k) * wb, wb)
        hit1 = jnp.where(d1_ref[0, :, pl.ds(start, wb)] == rows, 1.0, 0.0)
        hit2 = jnp.where(d2_ref[0, :, pl.ds(start, wb)] == rows, 1.0, 0.0)
        acc_ref[...] += _dot((hit1 + hit2).astype(BF16), hn_ref[0, pl.ds(start, wb), :]).astype(BF16)
        rec = rec_ref[0, pl.ds(start, wb), :]
        pacc_ref[...] += (_dot(hit1.astype(BF16), hi_lo(rec[:, R_P1:R_P1 + 1]))
                          + _dot(hit2.astype(BF16), hi_lo(rec[:, R_P2:R_P2 + 1])))
        return carry

    lax.fori_loop(0, nb_ref[idx], body, 0)
    o_ref[...] = acc_ref[...].astype(o_ref.dtype)
    p_ref[...] = pacc_ref[...]


def _dispatch(hn3, rec3, d1, d2, lo_blk, n_blk, nti, *, tm, wb=ROW_TILE):
    B, T, D = hn3.shape
    dspec = pl.BlockSpec((1, 1, T), lambda b, i, lo, nb: (b, 0, 0))
    row_spec = lambda w: pl.BlockSpec((tm, w), lambda b, i, lo, nb: (b * nti + i, 0))
    grid_spec = pltpu.PrefetchScalarGridSpec(
        num_scalar_prefetch=2,
        grid=(B, nti),
        in_specs=[dspec, dspec,
                  pl.BlockSpec((1, T, LANES), lambda b, i, lo, nb: (b, 0, 0)),
                  pl.BlockSpec((1, T, D), lambda b, i, lo, nb: (b, 0, 0), pipeline_mode=pl.Buffered(1))],
        out_specs=(row_spec(D), row_spec(LANES)),
        scratch_shapes=[pltpu.VMEM((tm, D), BF16), pltpu.VMEM((tm, LANES), F32)])
    return pl.pallas_call(
        functools.partial(_dispatch_kernel, wb=wb),
        out_shape=(jax.ShapeDtypeStruct((B * nti * tm, D), BF16),
                   jax.ShapeDtypeStruct((B * nti * tm, LANES), F32)),
        grid_spec=grid_spec,
        compiler_params=_cparams(("arbitrary", "arbitrary")),
        name="moe_dispatch",
    )(lo_blk, n_blk, d1.reshape(B, 1, T), d2.reshape(B, 1, T), rec3, hn3)


def _expert_up_kernel(te_ref, nb_ref, x_ref, wg_ref, wu_ref, p_ref, o_ref):
    i = pl.program_id(0)

    @pl.when(nb_ref[i] > 0)
    def _():
        x = x_ref[...]
        p = p_ref[:, 0:1] + p_ref[:, 1:2]
        o_ref[...] = (_silu(_dot(x, wg_ref[0])) * _dot(x, wu_ref[0]) * p).astype(o_ref.dtype)

    @pl.when(nb_ref[i] == 0)
    def _():
        o_ref[...] = jnp.zeros_like(o_ref)


def _expert_down_kernel(te_ref, nb_ref, a_ref, wd_ref, o_ref):
    i = pl.program_id(0)

    @pl.when(nb_ref[i] > 0)
    def _():
        o_ref[...] = _dot(a_ref[...], wd_ref[0]).astype(o_ref.dtype)

    @pl.when(nb_ref[i] == 0)
    def _():
        o_ref[...] = jnp.zeros_like(o_ref)


def _expert_ffn(x, p_rows, tile_expert, n_blk, wg, wu, wd, *, tm):
    R, D = x.shape
    E, _, F = wg.shape
    up_spec = pltpu.PrefetchScalarGridSpec(
        num_scalar_prefetch=2,
        grid=(R // tm,),
        in_specs=[pl.BlockSpec((tm, D), lambda i, te, nb: (i, 0)),
                  pl.BlockSpec((1, D, F), lambda i, te, nb: (te[i], 0, 0)),
                  pl.BlockSpec((1, D, F), lambda i, te, nb: (te[i], 0, 0)),
                  pl.BlockSpec((tm, LANES), lambda i, te, nb: (i, 0))],
        out_specs=pl.BlockSpec((tm, F), lambda i, te, nb: (i, 0)))
    act = pl.pallas_call(
        _expert_up_kernel,
        out_shape=jax.ShapeDtypeStruct((R, F), BF16),
        grid_spec=up_spec,
        compiler_params=_cparams(("arbitrary",)),
        name="moe_up",
    )(tile_expert, n_blk, x, wg, wu, p_rows)
    down_spec = pltpu.PrefetchScalarGridSpec(
        num_scalar_prefetch=2,
        grid=(R // tm,),
        in_specs=[pl.BlockSpec((tm, F), lambda i, te, nb: (i, 0)),
                  pl.BlockSpec((1, F, D), lambda i, te, nb: (te[i], 0, 0))],
        out_specs=pl.BlockSpec((tm, D), lambda i, te, nb: (i, 0)))
    return pl.pallas_call(
        _expert_down_kernel,
        out_shape=jax.ShapeDtypeStruct((R, D), BF16),
        grid_spec=down_spec,
        compiler_params=_cparams(("arbitrary",)),
        name="moe_down",
    )(tile_expert, n_blk, act, wd)


def _combine_kernel(yb_ref, yv_ref, h_ref, d1_ref, d2_ref, *refs, tm):
    y_refs, o_ref = refs[:-1], refs[-1]
    n = o_ref.shape[0]
    slots = len(y_refs) // N_EXPERTS
    base = (pl.program_id(0) * pl.num_programs(1) + pl.program_id(1)) * len(y_refs)
    d1, d2 = d1_ref[...], d2_ref[...]
    lane = lax.broadcasted_iota(jnp.int32, (n, tm), 1)

    def picked(k, weight):
        rows = yb_ref[base + k] * tm + lane
        hit = jnp.where(d1 == rows, weight, 0.0) + jnp.where(d2 == rows, weight, 0.0)
        return _dot(hit.astype(BF16), y_refs[k][...])

    acc = h_ref[...]
    for k in range(0, len(y_refs), slots):
        acc = acc + picked(k, jnp.where(yv_ref[base + k] > 0, 1.0, 0.0))
    o_ref[...] = acc
    for k in range(len(y_refs)):
        if k % slots:
            @pl.when(yv_ref[base + k] > 0)
            def _():
                o_ref[...] += picked(k, 1.0)


def _combine(h, y, d1, d2, y_blk, y_valid, B, T, *, tm):
    M, D = h.shape
    n = ROW_TILE
    NT = T // n
    slots = y_blk.shape[0] // (B * NT)
    y_specs = [pl.BlockSpec((tm, D), lambda b, t, yb, yv, k=k: (yb[(b * NT + t) * slots + k], 0))
               for k in range(slots)]
    tok = lambda w: pl.BlockSpec((n, w), lambda b, t, yb, yv: (b * NT + t, 0))
    grid_spec = pltpu.PrefetchScalarGridSpec(
        num_scalar_prefetch=2,
        grid=(B, NT),
        in_specs=[tok(D), tok(1), tok(1)] + y_specs,
        out_specs=tok(D))
    return pl.pallas_call(
        functools.partial(_combine_kernel, tm=tm),
        out_shape=jax.ShapeDtypeStruct((M, D), F32),
        grid_spec=grid_spec,
        compiler_params=_cparams(("arbitrary", "arbitrary")),
        name="moe_combine",
    )(y_blk, y_valid, h, d1, d2, *([y] * slots))


def _moe(h, hn, rec, cnt_before, wg, wu, wd, B, T):
    M, D = h.shape
    E, tm, n = N_EXPERTS, MOE_TM, ROW_TILE
    NT = T // n
    rb = TOP_K * T + E * tm
    i32 = jnp.int32
    col = lambda ln: rec[:, ln].reshape(B, T)
    e1, e2 = col(R_E1).astype(i32), col(R_E2).astype(i32)
    r1, r2 = col(R_R1).astype(i32), col(R_R2).astype(i32)
    cntb = cnt_before.reshape(B, NT, 8, LANES)[:, :, 0, :E].astype(i32)
    oh1, oh2 = jax.nn.one_hot(e1, E, dtype=i32), jax.nn.one_hot(e2, E, dtype=i32)
    counts = (oh1 + oh2).sum(axis=1)
    padded = (counts + tm - 1) // tm * tm
    ends = jnp.cumsum(padded, axis=1)
    off = ends - padded
    d1 = (oh1 * off[:, None, :]).sum(-1) + r1
    d2 = (oh2 * off[:, None, :]).sum(-1) + r2
    cnt_end = jnp.concatenate([cntb[:, 1:], counts[:, None, :]], axis=1)
    def tiles(rows):
        start = jnp.arange(rb // rows, dtype=i32)[None, :] * rows
        expert = jnp.minimum((start[:, :, None] >= ends[:, None, :]).sum(-1), E - 1).astype(i32)
        e_oh = jax.nn.one_hot(expert, E, dtype=i32)
        pick = lambda per_expert: (e_oh * per_expert[:, None, :]).sum(-1)
        rank_lo = start - pick(off)
        rank_hi = jnp.minimum(rank_lo + rows, pick(counts)) - 1
        through = (e_oh[:, :, None, :] * cnt_end[:, None, :, :]).sum(-1)
        lo_blk = (through <= rank_lo[..., None]).sum(-1)
        hi_blk = (through <= rank_hi[..., None]).sum(-1)
        nonempty = (start < pick(ends)) & (rank_hi >= rank_lo)
        return (expert.reshape(-1), jnp.where(nonempty, lo_blk, 0).astype(i32).reshape(-1),
                jnp.where(nonempty, hi_blk - lo_blk + 1, 0).astype(i32).reshape(-1))

    pr = MOE_DISPATCH_ROWS
    _, lo_blk, n_blk = tiles(pr)
    x, p_rows = _dispatch(hn.reshape(B, T, D), rec.reshape(B, T, LANES), d1, d2, lo_blk, n_blk, rb // pr, tm=pr)
    tile_expert, _, tile_used = tiles(tm)
    y = _expert_ffn(x, p_rows, tile_expert, tile_used, wg, wu, wd, tm=tm)
    first = off[:, None, :] + cntb
    last = off[:, None, :] + cnt_end - 1
    used = cnt_end > cntb
    span = jnp.arange(n // tm + 1, dtype=i32)
    blk = first[..., None] // tm + span
    valid = used[..., None] & (blk <= last[..., None] // tm)
    blk = jnp.minimum(blk, rb // tm - 1) + (jnp.arange(B, dtype=i32) * (rb // tm))[:, None, None, None]
    row0 = (jnp.arange(B, dtype=i32) * rb)[:, None]
    return _combine(h, y, (d1 + row0).reshape(M, 1), (d2 + row0).reshape(M, 1),
                    blk.reshape(-1).astype(i32), valid.reshape(-1).astype(i32), B, T, tm=tm)


def _merge_kernel(hn_ref, b0_ref, b1_ref, b2_ref, b3_ref, wg_ref, wb_ref, o_ref):
    hn = hn_ref[...]
    acc = None
    for b, br_ref in enumerate((b0_ref, b1_ref, b2_ref, b3_ref)):
        gate = jax.nn.sigmoid(_dot(hn, wg_ref[b]))
        term = gate * _dot(br_ref[...], wb_ref[b])
        acc = term if acc is None else acc + term
    o_ref[...] = acc.astype(o_ref.dtype)


def _merge(hn, branches, wg, wb, *, tn=512, tm=512):
    M, D = hn.shape
    N = wg.shape[2]
    bspec = pl.BlockSpec((tm, BRANCH_W), lambda j, i: (i, 0))
    return pl.pallas_call(
        _merge_kernel,
        out_shape=jax.ShapeDtypeStruct((M, N), BF16),
        grid=(N // tn, M // tm),
        in_specs=[pl.BlockSpec((tm, D), lambda j, i: (i, 0)), bspec, bspec, bspec, bspec,
                  pl.BlockSpec((N_BRANCH, D, tn), lambda j, i: (0, 0, j)),
                  pl.BlockSpec((N_BRANCH, BRANCH_W, tn), lambda j, i: (0, 0, j))],
        out_specs=pl.BlockSpec((tm, tn), lambda j, i: (i, j)),
        compiler_params=_cparams(("parallel", "parallel")),
        name="gate_merge",
    )(hn, *branches, wg, wb)


def _causal_conv(x, carry_ref, cw):
    n = x.shape[0]
    xext = jnp.concatenate([carry_ref[...], x], axis=0)
    y = cw[0:1] * xext[5:5 + n]
    for i in range(1, CONV_K):
        y = y + cw[i:i + 1] * xext[5 + i:5 + i + n]
    carry_ref[...] = x[n - 8:n]
    return y


def _tile_masks(n, chunk):
    row = lax.broadcasted_iota(jnp.int32, (n, n), 0)
    col = lax.broadcasted_iota(jnp.int32, (n, n), 1)
    if chunk == n:
        return col <= row, col < row
    in_chunk = col >= (row // chunk) * chunk
    return in_chunk & (col <= row), in_chunk & (col < row)


def _gdn_kernel(qkv_ref, z_ref, sm_ref, cw_ref, alog_ref, dtb_ref, ng_ref, o_ref, s_ref, carry_ref):
    @pl.when(pl.program_id(1) == 0)
    def _():
        s_ref[...] = jnp.zeros_like(s_ref)
        carry_ref[...] = jnp.zeros_like(carry_ref)

    n = ROW_TILE
    y = _silu(_causal_conv(qkv_ref[0], carry_ref, cw_ref[...]))
    sm = sm_ref[0]
    z = z_ref[0]
    beta_all = jax.nn.sigmoid(sm)
    g_all = -jnp.exp(alog_ref[...]) * _softplus(sm + dtb_ref[...])
    row = lax.broadcasted_iota(jnp.int32, (n, n), 0)
    col = lax.broadcasted_iota(jnp.int32, (n, n), 1)
    causal, strict = col <= row, col < row
    diff_bits = row ^ col
    levels = int(math.log2(n))
    level_masks = [(diff_bits >= (1 << l)) & (diff_bits < (2 << l)) for l in range(levels)]
    eye = jnp.where(row == col, 1.0, 0.0)
    gcs_all = _dot(causal.astype(F32), g_all, HI)
    gcs_t = gcs_all.T
    heads = []
    for h in range(GDN_HEADS):
        q = y[:, h * GDN_DK:(h + 1) * GDN_DK]
        k = y[:, GDN_HEADS * GDN_DK + h * GDN_DK:GDN_HEADS * GDN_DK + (h + 1) * GDN_DK]
        v = y[:, 2 * GDN_HEADS * GDN_DK + h * GDN_DV:2 * GDN_HEADS * GDN_DK + (h + 1) * GDN_DV]
        q = q * lax.rsqrt(jnp.sum(q * q, axis=-1, keepdims=True) + EPS) * GDN_DK ** -0.5
        k = k * lax.rsqrt(jnp.sum(k * k, axis=-1, keepdims=True) + EPS)
        beta = beta_all[:, L_BETA + h:L_BETA + h + 1]
        gc = gcs_all[:, L_GA + h:L_GA + h + 1]
        gr = gcs_t[L_GA + h:L_GA + h + 1, :]
        decay = jnp.exp(jnp.where(causal, gc - gr, NEG))
        kb = k * beta
        k16 = k.astype(BF16)
        a = jnp.where(strict, _dot_nt(kb.astype(BF16), k16) * decay, 0.0)
        heads.append(dict(q=q, k=k, v=v, beta=beta, gc=gc, decay=decay, kb=kb, k16=k16, a=a,
                          t=eye - jnp.where(level_masks[0], a, 0.0)))
    for l in range(1, levels):
        for hd in heads:
            t16 = hd["t"].astype(BF16)
            a_l = jnp.where(level_masks[l], hd["a"], 0.0).astype(BF16)
            hd["t"] = hd["t"] - _dot(_dot(t16, a_l).astype(BF16), t16)
    outs = []
    for h, hd in enumerate(heads):
        q, k, gc = hd["q"], hd["k"], hd["gc"]
        egc = jnp.exp(gc)
        rhs = jnp.concatenate([hd["v"] * hd["beta"], hd["kb"] * egc], axis=1)
        uw = _dot(hd["t"].astype(BF16), rhs.astype(BF16))
        u, w = uw[:, :GDN_DV], uw[:, GDN_DV:]
        att = _dot_nt(q.astype(BF16), hd["k16"]) * hd["decay"]
        g_last = gc[n - 1:n, :]
        kd = k * jnp.exp(g_last - gc)
        s = s_ref[h]
        ws = _dot(jnp.concatenate([w, q * egc], axis=0).astype(BF16), s.astype(BF16))
        v_new = (u - ws[:n]).astype(BF16)
        o = ws[n:] + _dot(att.astype(BF16), v_new)
        s_ref[h] = s * jnp.exp(g_last) + _dot(kd.T.astype(BF16), v_new)
        o = o * lax.rsqrt(jnp.mean(o * o, axis=-1, keepdims=True) + EPS) * ng_ref[...]
        outs.append(o * _silu(z[:, h * GDN_DV:(h + 1) * GDN_DV]))
    o_ref[0] = jnp.concatenate(outs, axis=1).astype(o_ref.dtype)


def _lane_vec(vals, lane0):
    v = jnp.zeros((1, LANES), F32)
    return v.at[0, lane0:lane0 + vals.shape[0]].set(vals.astype(F32))


def _gdn(proj, conv_w, a_log, dt_bias, norm_g):
    B, T, _ = proj.shape
    n = ROW_TILE
    W = 2 * GDN_HEADS * GDN_DK + GDN_HEADS * GDN_DV
    ZW = GDN_HEADS * GDN_DV
    return pl.pallas_call(
        _gdn_kernel,
        out_shape=jax.ShapeDtypeStruct((B, T, ZW), BF16),
        grid=(B, T // n),
        in_specs=[pl.BlockSpec((1, n, W), lambda b, t: (b, t, COL_GDN_QKV // W)),
                  pl.BlockSpec((1, n, ZW), lambda b, t: (b, t, COL_GDN_Z // ZW)),
                  pl.BlockSpec((1, n, LANES), lambda b, t: (b, t, COL_SMALL // LANES)),
                  pl.BlockSpec((CONV_K, W), lambda b, t: (0, 0)),
                  pl.BlockSpec((1, LANES), lambda b, t: (0, 0)),
                  pl.BlockSpec((1, LANES), lambda b, t: (0, 0)),
                  pl.BlockSpec((1, GDN_DV), lambda b, t: (0, 0))],
        out_specs=pl.BlockSpec((1, n, ZW), lambda b, t: (b, t, 0)),
        scratch_shapes=[pltpu.VMEM((GDN_HEADS, GDN_DK, GDN_DV), F32),
                        pltpu.VMEM((8, W), F32)],
        compiler_params=_cparams(("arbitrary", "arbitrary")),
        name="gdn",
    )(proj, proj, proj, conv_w.astype(F32), _lane_vec(a_log, L_GA), _lane_vec(dt_bias, L_GA),
      norm_g.reshape(1, GDN_DV).astype(F32))


def _ssd_kernel(xbc_ref, z_ref, sm_ref, cw_ref, cb_ref, alog_ref, dtb_ref, dvec_ref, ng_ref,
                o_ref, hs_ref, carry_ref):
    @pl.when(pl.program_id(1) == 0)
    def _():
        hs_ref[...] = jnp.zeros_like(hs_ref)
        carry_ref[...] = jnp.zeros_like(carry_ref)

    n, P, N = ROW_TILE, SSD_HEADDIM, SSD_STATE
    y = _silu(_causal_conv(xbc_ref[0], carry_ref, cw_ref[...]) + cb_ref[...])
    xs = y[:, :SSD_INNER]
    bm = y[:, SSD_INNER:SSD_INNER + SSD_GROUPS * N]
    cm = y[:, SSD_INNER + SSD_GROUPS * N:]
    sm = sm_ref[0]
    dt_all = _softplus(sm + dtb_ref[...])
    a_all = dt_all * (-jnp.exp(alog_ref[...]))
    causal, _ = _tile_masks(n, n)
    acs_all = _dot(causal.astype(F32), a_all, HI)
    acs_t = acs_all.T
    ys = []
    for g in range(SSD_GROUPS):
        bg = bm[:, g * N:(g + 1) * N]
        cg16 = cm[:, g * N:(g + 1) * N].astype(BF16)
        cb = _dot_nt(cg16, bg.astype(BF16))
        bgt16 = bg.T.astype(BF16)
        for j in range(SSD_HG):
            hh = g * SSD_HG + j
            ac = acs_all[:, L_DT + hh:L_DT + hh + 1]
            ar = acs_t[L_DT + hh:L_DT + hh + 1, :]
            lmat = jnp.exp(jnp.where(causal, ac - ar, NEG))
            xdt = xs[:, hh * P:(hh + 1) * P] * dt_all[:, L_DT + hh:L_DT + hh + 1]
            y_diag = _dot((cb * lmat).astype(BF16), xdt.astype(BF16))
            a_last = ac[n - 1:n, :]
            st = _dot(bgt16, (xdt * jnp.exp(a_last - ac)).astype(BF16))
            h_prev = hs_ref[hh]
            y_off = _dot(cg16, h_prev.astype(BF16)) * jnp.exp(ac)
            hs_ref[hh] = h_prev * jnp.exp(a_last) + st
            ys.append(y_diag + y_off)
    yy = jnp.concatenate(ys, axis=1) + xs * dvec_ref[...]
    yy = yy * _silu(z_ref[0])
    gw = SSD_HG * P
    outs = []
    for g in range(SSD_GROUPS):
        seg = yy[:, g * gw:(g + 1) * gw]
        outs.append(seg * lax.rsqrt(jnp.mean(seg * seg, axis=-1, keepdims=True) + EPS)
                    * ng_ref[:, g * gw:(g + 1) * gw])
    o_ref[0] = jnp.concatenate(outs, axis=1).astype(o_ref.dtype)


def _ssd(proj, conv_w, conv_b, dt_bias, a_log, d_skip, norm_g):
    B, T, _ = proj.shape
    n = ROW_TILE
    W = SSD_INNER + 2 * SSD_GROUPS * SSD_STATE
    dvec = jnp.repeat(d_skip.astype(F32), SSD_HEADDIM).reshape(1, SSD_INNER)
    return pl.pallas_call(
        _ssd_kernel,
        out_shape=jax.ShapeDtypeStruct((B, T, SSD_INNER), BF16),
        grid=(B, T // n),
        in_specs=[pl.BlockSpec((1, n, W), lambda b, t: (b, t, COL_SSD_XBC // W)),
                  pl.BlockSpec((1, n, SSD_INNER), lambda b, t: (b, t, COL_SSD_Z // SSD_INNER)),
                  pl.BlockSpec((1, n, LANES), lambda b, t: (b, t, COL_SMALL // LANES)),
                  pl.BlockSpec((CONV_K, W), lambda b, t: (0, 0)),
                  pl.BlockSpec((1, W), lambda b, t: (0, 0)),
                  pl.BlockSpec((1, LANES), lambda b, t: (0, 0)),
                  pl.BlockSpec((1, LANES), lambda b, t: (0, 0)),
                  pl.BlockSpec((1, SSD_INNER), lambda b, t: (0, 0)),
                  pl.BlockSpec((1, SSD_INNER), lambda b, t: (0, 0))],
        out_specs=pl.BlockSpec((1, n, SSD_INNER), lambda b, t: (b, t, 0)),
        scratch_shapes=[pltpu.VMEM((SSD_HEADS, SSD_STATE, SSD_HEADDIM), F32),
                        pltpu.VMEM((8, W), F32)],
        compiler_params=_cparams(("arbitrary", "arbitrary")),
        name="ssd",
    )(proj, proj, proj, conv_w.astype(F32), conv_b.reshape(1, W).astype(F32),
      _lane_vec(a_log, L_DT), _lane_vec(dt_bias, L_DT), dvec,
      norm_g.reshape(1, SSD_INNER).astype(F32))


def _ones_lane0(n):
    lane = lax.broadcasted_iota(jnp.int32, (n, LANES), 1)
    return jnp.where(lane == 0, 1.0, 0.0)


def _fox_prep_kernel(qkv_ref, sm_ref, qg_ref, kg_ref, bf_ref, q_ref, k_ref, v_ref, run_ref):
    @pl.when(pl.program_id(1) == 0)
    def _():
        run_ref[...] = jnp.zeros_like(run_ref)

    n, dh, H = ROW_TILE, FOX_DH, FOX_HEADS
    x = qkv_ref[0]
    log_f = -_softplus(-(sm_ref[0] + bf_ref[...]))
    causal, _ = _tile_masks(n, n)
    cum = _dot(causal.astype(F32), log_f, HI) + run_ref[0:1, :]
    run_ref[...] = jnp.broadcast_to(cum[n - 1:n, :], run_ref.shape)
    lane = lax.broadcasted_iota(jnp.int32, (n, LANES), 1)
    ones = _ones_lane0(n)
    q_bias = jnp.where(lane < 3, 1.0, 0.0)
    qs, ks, vs = [], [], []
    for h in range(H):
        q = x[:, h * dh:(h + 1) * dh]
        k = x[:, H * dh + h * dh:H * dh + (h + 1) * dh]
        qs.append(q * lax.rsqrt(jnp.mean(q * q, axis=-1, keepdims=True) + EPS)
                  * (qg_ref[...] * (dh ** -0.5 * LOG2E)))
        qs.append(q_bias)
        ks.append(k * lax.rsqrt(jnp.mean(k * k, axis=-1, keepdims=True) + EPS) * kg_ref[...])
        c = cum[:, L_FF + h:L_FF + h + 1] * (-LOG2E)
        c_hi = c.astype(BF16).astype(F32)
        c_mid = (c - c_hi).astype(BF16).astype(F32)
        c_lo = c - c_hi - c_mid
        ks.append(jnp.where(lane == 0, c_hi, jnp.where(lane == 1, c_mid, jnp.where(lane == 2, c_lo, 0.0))))
        vs.append(x[:, 2 * H * dh + h * dh:2 * H * dh + (h + 1) * dh])
        vs.append(ones)
    q_ref[0] = jnp.concatenate(qs, axis=1).astype(q_ref.dtype)
    k_ref[0] = jnp.concatenate(ks, axis=1).astype(k_ref.dtype)
    v_ref[0] = jnp.concatenate(vs, axis=1).astype(v_ref.dtype)


def _fox_prep(proj, qn_g, kn_g, b_f):
    B, T, _ = proj.shape
    n = ROW_TILE
    W = 3 * FOX_HEADS * FOX_DH
    HW = FOX_HEADS * 2 * LANES
    ospec = pl.BlockSpec((1, n, HW), lambda b, t: (b, t, 0))
    return pl.pallas_call(
        _fox_prep_kernel,
        out_shape=(jax.ShapeDtypeStruct((B, T, HW), BF16),) * 3,
        grid=(B, T // n),
        in_specs=[pl.BlockSpec((1, n, W), lambda b, t: (b, t, COL_FOX_QKV // W)),
                  pl.BlockSpec((1, n, LANES), lambda b, t: (b, t, COL_SMALL // LANES)),
                  pl.BlockSpec((1, FOX_DH), lambda b, t: (0, 0)),
                  pl.BlockSpec((1, FOX_DH), lambda b, t: (0, 0)),
                  pl.BlockSpec((1, LANES), lambda b, t: (0, 0))],
        out_specs=(ospec, ospec, ospec),
        scratch_shapes=[pltpu.VMEM((8, LANES), F32)],
        compiler_params=_cparams(("arbitrary", "arbitrary")),
        name="fox_prep",
    )(proj, proj, qn_g.reshape(1, FOX_DH).astype(F32), kn_g.reshape(1, FOX_DH).astype(F32),
      _lane_vec(b_f, L_FF))


def _mla_prep_kernel(qa_ref, kva_ref, sm_ref, cos_ref, sin_ref, qag_ref, wq_ref, kvag_ref, wkv_ref,
                     qgn_ref, qgr_ref, kgn_ref, kgr_ref, rot_ref, exp_ref, q_ref, k_ref, v_ref):
    n, H, dn, dr = ROW_TILE, MLA_HEADS, MLA_NOPE, MLA_ROPE
    qa = qa_ref[0]
    qa = qa * lax.rsqrt(jnp.mean(qa * qa, axis=-1, keepdims=True) + EPS) * qag_ref[...]
    qq = _dot(qa.astype(BF16), wq_ref[...])
    kva = kva_ref[0]
    kva = kva * lax.rsqrt(jnp.mean(kva * kva, axis=-1, keepdims=True) + EPS) * kvag_ref[...]
    kv = _dot(kva.astype(BF16), wkv_ref[...])
    sm = sm_ref[0]
    lane = lax.broadcasted_iota(jnp.int32, (n, LANES), 1)
    is_kpe = (lane >= L_KPE) & (lane < L_KPE + dr)
    kpe_ss = jnp.sum(jnp.where(is_kpe, sm * sm, 0.0), axis=-1, keepdims=True)
    kpe4 = _dot(sm, exp_ref[...], HI)
    qr = qq[:, H * dn:]
    grp = lax.broadcasted_iota(jnp.int32, (n, H * dr), 1) // dr
    q_rs, k_rs = [], []
    for h in range(H):
        qn = qq[:, h * dn:(h + 1) * dn]
        ssr = jnp.sum(jnp.where(grp == h, qr * qr, 0.0), axis=-1, keepdims=True)
        q_rs.append(lax.rsqrt((jnp.sum(qn * qn, axis=-1, keepdims=True) + ssr) / MLA_DQK + EPS))
        kn = kv[:, h * dn:(h + 1) * dn]
        k_rs.append(lax.rsqrt((jnp.sum(kn * kn, axis=-1, keepdims=True) + kpe_ss) / MLA_DQK + EPS))

    def per_group(vals):
        out = vals[H - 1]
        for h in range(H - 2, -1, -1):
            out = jnp.where(grp == h, vals[h], out)
        return out

    cos, sin, rot = cos_ref[...], sin_ref[...], rot_ref[...]
    tq = qr * per_group(q_rs) * qgr_ref[...]
    tq = tq * cos + _dot(tq, rot, HI) * sin
    tk = kpe4 * per_group(k_rs) * kgr_ref[...]
    tk = tk * cos + _dot(tk, rot, HI) * sin
    scale = MLA_DQK ** -0.5 * LOG2E
    half_id = lane // dr
    ones = _ones_lane0(n)
    q_parts, k_parts, v_parts = [], [], []
    for h in range(H):
        blk = slice((h // 2) * LANES, (h // 2 + 1) * LANES)
        q_parts.append(qq[:, h * dn:(h + 1) * dn] * q_rs[h] * (qgn_ref[...] * scale))
        q_parts.append(jnp.where(half_id == h % 2, tq[:, blk] * scale, 0.0))
        k_parts.append(kv[:, h * dn:(h + 1) * dn] * k_rs[h] * kgn_ref[...])
        k_parts.append(tk[:, blk])
        v_parts.append(kv[:, H * dn + h * MLA_V:H * dn + (h + 1) * MLA_V])
        v_parts.append(ones)
    q_ref[0] = jnp.concatenate(q_parts, axis=1).astype(q_ref.dtype)
    k_ref[0] = jnp.concatenate(k_parts, axis=1).astype(k_ref.dtype)
    v_ref[0] = jnp.concatenate(v_parts, axis=1).astype(v_ref.dtype)


def _rope_consts(T):
    H, dr = MLA_HEADS, MLA_ROPE
    inv = 1.0 / (ROPE_BASE ** (jnp.arange(0, dr, 2, dtype=F32) / dr))
    ang = jnp.arange(T, dtype=F32)[:, None] * inv[None, :]
    ang = jnp.concatenate([ang, ang], axis=-1)
    cos4 = jnp.tile(jnp.cos(ang), (1, H))
    sin4 = jnp.tile(jnp.sin(ang), (1, H))
    rot = np.zeros((H * dr, H * dr), np.float32)
    for h in range(H):
        for c in range(dr // 2):
            rot[h * dr + c + dr // 2, h * dr + c] = -1.0
            rot[h * dr + c, h * dr + c + dr // 2] = 1.0
    expand = np.zeros((LANES, H * dr), np.float32)
    for h in range(H):
        for c in range(dr):
            expand[L_KPE + c, h * dr + c] = 1.0
    return cos4, sin4, jnp.asarray(rot), jnp.asarray(expand)


def _mla_prep(proj, rope, qa_g, wq_b, kva_g, wkv_b, qn_g, kn_g):
    B, T, _ = proj.shape
    n, H, dn, dr, dv = ROW_TILE, MLA_HEADS, MLA_NOPE, MLA_ROPE, MLA_V
    cos4, sin4, rot, expand = rope
    wq = wq_b.reshape(MLA_Q_RANK, H, MLA_DQK)
    wq = jnp.concatenate([wq[:, :, :dn].reshape(MLA_Q_RANK, H * dn),
                          wq[:, :, dn:].reshape(MLA_Q_RANK, H * dr)], axis=1).astype(BF16)
    wkv = wkv_b.reshape(MLA_KV_RANK, H, dn + dv)
    wkv = jnp.concatenate([wkv[:, :, :dn].reshape(MLA_KV_RANK, H * dn),
                           wkv[:, :, dn:].reshape(MLA_KV_RANK, H * dv)], axis=1).astype(BF16)
    qg, kg = qn_g.astype(F32), kn_g.astype(F32)
    const = lambda shape: pl.BlockSpec(shape, lambda b, t: (0,) * len(shape))
    QW = H * 2 * LANES
    return pl.pallas_call(
        _mla_prep_kernel,
        out_shape=(jax.ShapeDtypeStruct((B, T, QW), BF16), jax.ShapeDtypeStruct((B, T, QW), BF16),
                   jax.ShapeDtypeStruct((B, T, QW), BF16)),
        grid=(B, T // n),
        in_specs=[pl.BlockSpec((1, n, MLA_Q_RANK), lambda b, t: (b, t, COL_MLA_QA // MLA_Q_RANK)),
                  pl.BlockSpec((1, n, MLA_KV_RANK), lambda b, t: (b, t, COL_MLA_KVA // MLA_KV_RANK)),
                  pl.BlockSpec((1, n, LANES), lambda b, t: (b, t, COL_SMALL // LANES)),
                  pl.BlockSpec((n, H * dr), lambda b, t: (t, 0)),
                  pl.BlockSpec((n, H * dr), lambda b, t: (t, 0)),
                  const((1, MLA_Q_RANK)), const(wq.shape), const((1, MLA_KV_RANK)), const(wkv.shape),
                  const((1, dn)), const((1, H * dr)), const((1, dn)), const((1, H * dr)),
                  const(rot.shape), const(expand.shape)],
        out_specs=(pl.BlockSpec((1, n, QW), lambda b, t: (b, t, 0)),
                   pl.BlockSpec((1, n, QW), lambda b, t: (b, t, 0)),
                   pl.BlockSpec((1, n, QW), lambda b, t: (b, t, 0))),
        compiler_params=_cparams(("parallel", "parallel")),
        name="mla_prep",
    )(proj, proj, proj, cos4, sin4,
      qa_g.reshape(1, -1).astype(F32), wq, kva_g.reshape(1, -1).astype(F32), wkv,
      qg[:dn].reshape(1, dn), jnp.tile(qg[dn:], H).reshape(1, H * dr),
      kg[:dn].reshape(1, dn), jnp.tile(kg[dn:], H).reshape(1, H * dr), rot, expand)


def _flash_kernel(q_ref, k_ref, v_ref, o_ref, m_ref, acc_ref, *, tq):
    qi = pl.program_id(2)
    nh = m_ref.shape[0]
    dqk, dv2 = q_ref.shape[-1] // nh, v_ref.shape[-1] // nh
    dv = dv2 // 2
    rb = ROW_TILE
    m_ref[...] = jnp.full(m_ref.shape, NEG, F32)
    acc_ref[...] = jnp.zeros(acc_ref.shape, F32)

    def step(j, masked):
        start = pl.multiple_of(j * tq, tq)
        chains = [(h, r) for h in range(nh) for r in range(0, tq, rb)]
        logits, probs = {}, {}

        def qk(c):
            h, r = chains[c]
            logits[c] = _dot_nt(q_ref[0, r:r + rb, h * dqk:(h + 1) * dqk],
                                k_ref[0, pl.ds(start, tq), h * dqk:(h + 1) * dqk])

        def softmax(c):
            h, r = chains[c]
            s = logits.pop(c)
            if masked:
                row = lax.broadcasted_iota(jnp.int32, s.shape, 0) + r
                col = lax.broadcasted_iota(jnp.int32, s.shape, 1)
                s = jnp.where(col <= row, s, NEG)
            lane_max = s[:, :LANES]
            for cb in range(1, tq // LANES):
                lane_max = jnp.maximum(lane_max, s[:, cb * LANES:(cb + 1) * LANES])
            m_old = m_ref[h, r:r + rb]
            m_new = jnp.maximum(m_old, jnp.max(lane_max, axis=-1, keepdims=True))
            m_ref[h, r:r + rb] = m_new
            probs[c] = (jnp.exp2(s - m_new).astype(BF16), jnp.exp2(m_old - m_new))

        def pv(c):
            h, r = chains[c]
            p, alpha = probs.pop(c)
            acc_ref[h, r:r + rb] = (alpha * acc_ref[h, r:r + rb]
                                    + _dot(p, v_ref[0, pl.ds(start, tq), h * dv2:(h + 1) * dv2]))

        for c in range(len(chains)):
            qk(c)
            softmax(c)
            pv(c)

    def body(jj, carry):
        step(2 * jj, False)
        step(2 * jj + 1, False)
        return carry

    lax.fori_loop(0, qi // 2, body, 0)

    @pl.when(qi % 2 == 1)
    def _():
        step(qi - 1, False)

    step(qi, True)
    outs = []
    for h in range(nh):
        acc = acc_ref[h]
        outs.append(acc[:, :dv] / acc[:, dv:dv + 1])
    o_ref[0] = jnp.concatenate(outs, axis=1).astype(o_ref.dtype)


def _flash(q, k, v):
    B, T, QW = q.shape
    H, nh = MLA_HEADS, FLASH_HEADS_PER_STEP
    dqk, dv2 = QW // H, v.shape[-1] // H
    dv = dv2 // 2
    tq = next(t for t in FLASH_TILES if T % t == 0)
    return pl.pallas_call(
        functools.partial(_flash_kernel, tq=tq),
        out_shape=jax.ShapeDtypeStruct((B, T, H * dv), BF16),
        grid=(B, H // nh, T // tq),
        in_specs=[pl.BlockSpec((1, tq, nh * dqk), lambda b, h, i: (b, i, h)),
                  pl.BlockSpec((1, T, nh * dqk), lambda b, h, i: (b, 0, h)),
                  pl.BlockSpec((1, T, nh * dv2), lambda b, h, i: (b, 0, h))],
        out_specs=pl.BlockSpec((1, tq, nh * dv), lambda b, h, i: (b, i, h)),
        scratch_shapes=[pltpu.VMEM((nh, tq, 1), F32), pltpu.VMEM((nh, tq, dv2), F32)],
        compiler_params=_cparams(("parallel", "parallel", "arbitrary")),
        name="flash",
    )(q, k, v)


def _reorder_kernel(w_ref, o_ref):
    w = w_ref[...]
    n = w.shape[0]
    z = lambda width: jnp.zeros((n, width), w.dtype)
    small = [w[:, 2048:2052], w[:, 2052:2056], w[:, 4424:4428], w[:, 5964:5972], w[:, 2824:2888]]
    n_small = sum(s.shape[1] for s in small)
    cols = [w[:, 0:2048],
            w[:, 4940:5964],
            w[:, 4428:4940],
            w[:, 2056:2824],
            *small, z(LANES - n_small), z(COL_FOX_QKV - COL_SMALL - LANES),
            w[:, 2888:4424]]
    o_ref[...] = jnp.concatenate(cols, axis=1).astype(o_ref.dtype)


def _reorder_w_in(w_in):
    D, W = w_in.shape
    rows = ROW_TILE
    return pl.pallas_call(
        _reorder_kernel,
        out_shape=jax.ShapeDtypeStruct((D, PROJ_W), BF16),
        grid=(D // rows,),
        in_specs=[pl.BlockSpec((rows, W), lambda i: (i, 0))],
        out_specs=pl.BlockSpec((rows, PROJ_W), lambda i: (i, 0)),
        compiler_params=_cparams(("parallel",)),
        name="reorder_w_in",
    )(w_in)


def _mixer(h, hn, B, T, rope, w_in, gdn_conv_w, gdn_A_log, gdn_dt_bias, gdn_norm_g,
           mla_qa_g, mla_wq_b, mla_kva_g, mla_wkv_b, mla_qn_g, mla_kn_g,
           fox_qn_g, fox_kn_g, fox_b_f,
           ssd_conv_w, ssd_conv_b, ssd_dt_bias, ssd_A_log, ssd_D, ssd_norm_g,
           w_gate, w_branch, w_o, next_norm_g=None):
    M = B * T
    proj = _matmul(hn, _reorder_w_in(w_in), tn=1024, name="in_proj").reshape(B, T, PROJ_W)
    o_gdn = _gdn(proj, gdn_conv_w, gdn_A_log, gdn_dt_bias, gdn_norm_g)
    mq, mk, mv = _mla_prep(proj, rope, mla_qa_g, mla_wq_b, mla_kva_g, mla_wkv_b, mla_qn_g, mla_kn_g)
    o_mla = _flash(mq, mk, mv)
    o_fox = _flash(*_fox_prep(proj, fox_qn_g, fox_kn_g, fox_b_f))
    o_ssd = _ssd(proj, ssd_conv_w, ssd_conv_b, ssd_dt_bias, ssd_A_log, ssd_D, ssd_norm_g)
    branches = [o.reshape(M, BRANCH_W) for o in (o_gdn, o_mla, o_fox, o_ssd)]
    merged = _merge(hn, branches, w_gate.astype(BF16), w_branch.astype(BF16))
    return _matmul(merged, w_o.astype(BF16), tn=w_o.shape[1], tm=512, residual=h, norm_g=next_norm_g,
                   name="out_proj")


def kernel(x, meta_tokens, mix_norm_g, w_in, gdn_conv_w, gdn_A_log, gdn_dt_bias, gdn_norm_g, mla_qa_g, mla_wq_b, mla_kva_g, mla_wkv_b, mla_qn_g, mla_kn_g, fox_qn_g, fox_kn_g, fox_b_f, ssd_conv_w, ssd_conv_b, ssd_dt_bias, ssd_A_log, ssd_D, ssd_norm_g, w_gate, w_branch, w_o, ffn_norm_g, dense_w_gate, dense_w_up, dense_w_down, router_w, moe_w_gate, moe_w_up, moe_w_down):
    B, S, D = x.shape
    L = N_META + S
    T = -(-L // ROW_TILE) * ROW_TILE
    assert (B * T) % MM_TM == 0
    depth = w_in.shape[0]
    meta = jnp.broadcast_to(meta_tokens[None].astype(x.dtype), (B, N_META, D))
    h = jnp.concatenate([meta, x, jnp.zeros((B, T - L, D), x.dtype)], axis=1).reshape(B * T, D)
    rope = _rope_consts(T)
    for layer in range(depth):
        hn = _rmsnorm(h, mix_norm_g[layer])
        dense = layer % 2 == 0
        mixed = _mixer(h, hn, B, T, rope, w_in[layer],
                       gdn_conv_w[layer], gdn_A_log[layer], gdn_dt_bias[layer], gdn_norm_g[layer],
                       mla_qa_g[layer], mla_wq_b[layer], mla_kva_g[layer], mla_wkv_b[layer],
                       mla_qn_g[layer], mla_kn_g[layer],
                       fox_qn_g[layer], fox_kn_g[layer], fox_b_f[layer],
                       ssd_conv_w[layer], ssd_conv_b[layer], ssd_dt_bias[layer], ssd_A_log[layer],
                       ssd_D[layer], ssd_norm_g[layer],
                       w_gate[layer], w_branch[layer], w_o[layer],
                       next_norm_g=ffn_norm_g[layer] if dense else None)
        i = layer // 2
        if dense:
            h, hn = mixed
            act = _swiglu_up(hn, dense_w_gate[i].astype(BF16), dense_w_up[i].astype(BF16), tn=512)
            h = _matmul(act, dense_w_down[i].astype(BF16), tn=512, residual=h, name="ffn_down")
        else:
            h = mixed
            hn, rec, cnt_before = _rmsnorm_router(h, ffn_norm_g[layer], router_w[i], B, T)
            h = _moe(h, hn, rec, cnt_before, moe_w_gate[i].astype(BF16), moe_w_up[i].astype(BF16),
                     moe_w_down[i].astype(BF16), B, T)
    return h.reshape(B, T, D)[:, N_META:L].astype(x.dtype)
```

```python
import functools
import math

import numpy as np
import jax
import jax.numpy as jnp
from jax import lax
from jax.experimental import pallas as pl
from jax.experimental.pallas import tpu as pltpu

F32 = jnp.float32
BF16 = jnp.bfloat16
HI = lax.Precision.HIGHEST
NT_DIMS = (((1,), (1,)), ((), ()))

D_MODEL = 2048
N_META = 16
EPS = 1e-6
NEG = -1e30
CONV_K = 4

GDN_HEADS, GDN_DK, GDN_DV = 4, 128, 128
MLA_HEADS, MLA_Q_RANK, MLA_KV_RANK, MLA_NOPE, MLA_ROPE, MLA_V = 4, 512, 256, 128, 64, 128
MLA_DQK = MLA_NOPE + MLA_ROPE
ROPE_BASE = 10000.0
FOX_HEADS, FOX_DH = 4, 128
SSD_HEADS, SSD_HEADDIM, SSD_GROUPS, SSD_STATE = 8, 64, 2, 128
SSD_HG = SSD_HEADS // SSD_GROUPS
SSD_INNER = SSD_HEADS * SSD_HEADDIM
N_BRANCH, BRANCH_W = 4, 512
N_EXPERTS, TOP_K = 8, 2

LANES = 128
ROW_TILE = 256
FLASH_TILES = (768, 256)
FLASH_HEADS_PER_STEP = 2
LOG2E = 1.4426950408889634
MM_TM = 768
FFN_DOWN_TM = 384
MOE_TM = 256
MOE_DISPATCH_ROWS = 128
R_E1, R_E2, R_P1, R_P2, R_R1, R_R2 = range(6)
VMEM_LIMIT = 56 * 1024 * 1024

PROJ_W = 6144
COL_GDN_QKV, COL_GDN_Z = 0, 1536
COL_SSD_XBC, COL_SSD_Z = 2048, 3072
COL_MLA_QA, COL_MLA_KVA = 3584, 4096
COL_SMALL = 4352
COL_FOX_QKV = 4608
L_BETA, L_GA, L_FF, L_DT, L_KPE = 0, 4, 8, 12, 20


def _cparams(sem, vmem=VMEM_LIMIT):
    return pltpu.CompilerParams(dimension_semantics=sem, vmem_limit_bytes=vmem)


def _softplus(x):
    return jnp.maximum(x, 0.0) + jnp.log1p(jnp.exp(-jnp.abs(x)))


def _silu(x):
    return x * jax.nn.sigmoid(x)


def _dot(a, b, precision=None):
    return jnp.dot(a, b, preferred_element_type=F32, precision=precision)


def _dot_nt(a, b):
    return lax.dot_general(a, b, NT_DIMS, preferred_element_type=F32)


def _rmsnorm_kernel(h_ref, g_ref, o_ref):
    x = h_ref[...]
    y = x * lax.rsqrt(jnp.mean(x * x, axis=-1, keepdims=True) + EPS) * g_ref[...]
    o_ref[...] = y.astype(o_ref.dtype)


def _rmsnorm(h, g):
    M, D = h.shape
    return pl.pallas_call(
        _rmsnorm_kernel,
        out_shape=jax.ShapeDtypeStruct((M, D), BF16),
        grid=(M // MM_TM,),
        in_specs=[pl.BlockSpec((MM_TM, D), lambda i: (i, 0)),
                  pl.BlockSpec((1, D), lambda i: (0, 0))],
        out_specs=pl.BlockSpec((MM_TM, D), lambda i: (i, 0)),
        compiler_params=_cparams(("parallel",)),
        name="rmsnorm",
    )(h, g.reshape(1, D).astype(F32))


def _rmsnorm_router_kernel(h_ref, g_ref, rw_ref, o_ref, rec_ref, cnt_ref, run_ref):
    @pl.when(pl.program_id(1) == 0)
    def _():
        run_ref[...] = jnp.zeros_like(run_ref)

    x = h_ref[...]
    n = x.shape[0]
    y = x * lax.rsqrt(jnp.mean(x * x, axis=-1, keepdims=True) + EPS) * g_ref[...]
    o_ref[...] = y.astype(o_ref.dtype)
    logits = _dot(y, rw_ref[...], HI)
    lane = lax.broadcasted_iota(jnp.int32, logits.shape, 1)
    logits = jnp.where(lane < N_EXPERTS, logits, NEG)
    m1 = jnp.max(logits, axis=-1, keepdims=True)
    i1 = jnp.min(jnp.where(logits == m1, lane, LANES), axis=-1, keepdims=True)
    rest = jnp.where(lane == i1, NEG, logits)
    m2 = jnp.max(rest, axis=-1, keepdims=True)
    i2 = jnp.min(jnp.where(rest == m2, lane, LANES), axis=-1, keepdims=True)
    e2 = jnp.exp(m2 - m1)
    p1 = 1.0 / (1.0 + e2)
    p2 = e2 * p1
    sel = jnp.where(lane == i1, 1.0, 0.0) + jnp.where(lane == i2, 1.0, 0.0)
    row = lax.broadcasted_iota(jnp.int32, (n, n), 0)
    col = lax.broadcasted_iota(jnp.int32, (n, n), 1)
    earlier = jnp.where(col < row, 1.0, 0.0).astype(BF16)
    run = run_ref[0:1, :]
    before = _dot(earlier, sel.astype(BF16)) + run
    r1 = jnp.sum(jnp.where(lane == i1, before, 0.0), axis=-1, keepdims=True)
    r2 = jnp.sum(jnp.where(lane == i2, before, 0.0), axis=-1, keepdims=True)
    cnt_ref[...] = jnp.broadcast_to(run, cnt_ref.shape)
    run_ref[...] = jnp.broadcast_to(before[n - 1:n] + sel[n - 1:n], run_ref.shape)
    rec = jnp.zeros_like(logits)
    for ln, val in ((R_E1, i1.astype(F32)), (R_E2, i2.astype(F32)), (R_P1, p1), (R_P2, p2), (R_R1, r1), (R_R2, r2)):
        rec = jnp.where(lane == ln, val, rec)
    rec_ref[...] = rec


def _rmsnorm_router(h, g, router_w, B, T):
    M, D = h.shape
    n = ROW_TILE
    NT = T // n
    rw = jnp.zeros((D, LANES), F32).at[:, :N_EXPERTS].set(router_w.astype(F32))
    return pl.pallas_call(
        _rmsnorm_router_kernel,
        out_shape=(jax.ShapeDtypeStruct((M, D), BF16), jax.ShapeDtypeStruct((M, LANES), F32),
                   jax.ShapeDtypeStruct((B * NT * 8, LANES), F32)),
        grid=(B, NT),
        in_specs=[pl.BlockSpec((n, D), lambda b, t: (b * NT + t, 0)),
                  pl.BlockSpec((1, D), lambda b, t: (0, 0)),
                  pl.BlockSpec((D, LANES), lambda b, t: (0, 0))],
        out_specs=(pl.BlockSpec((n, D), lambda b, t: (b * NT + t, 0)),
                   pl.BlockSpec((n, LANES), lambda b, t: (b * NT + t, 0)),
                   pl.BlockSpec((8, LANES), lambda b, t: (b * NT + t, 0))),
        scratch_shapes=[pltpu.VMEM((8, LANES), F32)],
        compiler_params=_cparams(("arbitrary", "arbitrary")),
        name="rmsnorm_router",
    )(h, g.reshape(1, D).astype(F32), rw)


def _mm_kernel(*refs, has_res, has_norm):
    a_ref, w_ref = refs[:2]
    y = _dot(a_ref[...], w_ref[...])
    if has_res:
        y = y + refs[2][...]
    if has_norm:
        g_ref, o_ref, n_ref = refs[-3:]
        n_ref[...] = (y * lax.rsqrt(jnp.mean(y * y, axis=-1, keepdims=True) + EPS) * g_ref[...]).astype(n_ref.dtype)
    else:
        o_ref = refs[-1]
    o_ref[...] = y.astype(o_ref.dtype)


def _matmul(a, w, *, tn, tm=MM_TM, residual=None, norm_g=None, out_dtype=F32, w_buffers=2, name="matmul"):
    M, K = a.shape
    N = w.shape[1]
    in_specs = [pl.BlockSpec((tm, K), lambda j, i: (i, 0)),
                pl.BlockSpec((K, tn), lambda j, i: (0, j), pipeline_mode=pl.Buffered(w_buffers))]
    args = [a, w]
    out_shape = jax.ShapeDtypeStruct((M, N), out_dtype)
    out_specs = pl.BlockSpec((tm, tn), lambda j, i: (i, j))
    if residual is not None:
        in_specs.append(pl.BlockSpec((tm, tn), lambda j, i: (i, j)))
        args.append(residual)
    if norm_g is not None:
        assert tn == N
        in_specs.append(pl.BlockSpec((1, N), lambda j, i: (0, 0)))
        args.append(norm_g.reshape(1, N).astype(F32))
        out_shape = (out_shape, jax.ShapeDtypeStruct((M, N), BF16))
        out_specs = (out_specs, pl.BlockSpec((tm, tn), lambda j, i: (i, j)))
    return pl.pallas_call(
        functools.partial(_mm_kernel, has_res=residual is not None, has_norm=norm_g is not None),
        out_shape=out_shape,
        grid=(N // tn, M // tm),
        in_specs=in_specs,
        out_specs=out_specs,
        compiler_params=_cparams(("parallel", "parallel")),
        name=name,
    )(*args)


def _swiglu_kernel(a_ref, wg_ref, wu_ref, o_ref):
    a = a_ref[...]
    g = _dot(a, wg_ref[...])
    u = _dot(a, wu_ref[...])
    o_ref[...] = (_silu(g) * u).astype(o_ref.dtype)


def _swiglu_up(a, wg, wu, *, tn, tm=MM_TM):
    M, K = a.shape
    F = wg.shape[1]
    return pl.pallas_call(
        _swiglu_kernel,
        out_shape=jax.ShapeDtypeStruct((M, F), BF16),
        grid=(F // tn, M // tm),
        in_specs=[pl.BlockSpec((tm, K), lambda j, i: (i, 0)),
                  pl.BlockSpec((K, tn), lambda j, i: (0, j)),
                  pl.BlockSpec((K, tn), lambda j, i: (0, j))],
        out_specs=pl.BlockSpec((tm, tn), lambda j, i: (i, j)),
        compiler_params=_cparams(("parallel", "parallel")),
        name="swiglu_up",
    )(a, wg, wu)


def _dispatch_kernel(lo_ref, nb_ref, d1_ref, d2_ref, rec_ref, hn_ref, o_ref, p_ref, acc_ref, pacc_ref, *, wb):
    i = pl.program_id(1)
    idx = pl.program_id(0) * pl.num_programs(1) + i
    tm = o_ref.shape[0]
    rows = i * tm + lax.broadcasted_iota(jnp.int32, (tm, wb), 0)
    lane = lax.broadcasted_iota(jnp.int32, (wb, LANES), 1)
    acc_ref[...] = jnp.zeros_like(acc_ref)
    pacc_ref[...] = jnp.zeros_like(pacc_ref)

    def hi_lo(p):
        hi = p.astype(BF16).astype(F32)
        return jnp.where(lane == 0, hi, jnp.where(lane == 1, p - hi, 0.0)).astype(BF16)

    def body(k, carry):
        start = pl.multiple_of((lo_ref[idx] + k) * wb, wb)
        hit1 = jnp.where(d1_ref[0, :, pl.ds(start, wb)] == rows, 1.0, 0.0)
        hit2 = jnp.where(d2_ref[0, :, pl.ds(start, wb)] == rows, 1.0, 0.0)
        acc_ref[...] += _dot((hit1 + hit2).astype(BF16), hn_ref[0, pl.ds(start, wb), :]).astype(BF16)
        rec = rec_ref[0, pl.ds(start, wb), :]
        pacc_ref[...] += (_dot(hit1.astype(BF16), hi_lo(rec[:, R_P1:R_P1 + 1]))
                          + _dot(hit2.astype(BF16), hi_lo(rec[:, R_P2:R_P2 + 1])))
        return carry

    lax.fori_loop(0, nb_ref[idx], body, 0)
    o_ref[...] = acc_ref[...].astype(o_ref.dtype)
    p_ref[...] = pacc_ref[...]


def _dispatch(hn3, rec3, d1, d2, lo_blk, n_blk, nti, *, tm, wb=ROW_TILE):
    B, T, D = hn3.shape
    dspec = pl.BlockSpec((1, 1, T), lambda b, i, lo, nb: (b, 0, 0))
    row_spec = lambda w: pl.BlockSpec((tm, w), lambda b, i, lo, nb: (b * nti + i, 0))
    grid_spec = pltpu.PrefetchScalarGridSpec(
        num_scalar_prefetch=2,
        grid=(B, nti),
        in_specs=[dspec, dspec,
                  pl.BlockSpec((1, T, LANES), lambda b, i, lo, nb: (b, 0, 0)),
                  pl.BlockSpec((1, T, D), lambda b, i, lo, nb: (b, 0, 0), pipeline_mode=pl.Buffered(1))],
        out_specs=(row_spec(D), row_spec(LANES)),
        scratch_shapes=[pltpu.VMEM((tm, D), BF16), pltpu.VMEM((tm, LANES), F32)])
    return pl.pallas_call(
        functools.partial(_dispatch_kernel, wb=wb),
        out_shape=(jax.ShapeDtypeStruct((B * nti * tm, D), BF16),
                   jax.ShapeDtypeStruct((B * nti * tm, LANES), F32)),
        grid_spec=grid_spec,
        compiler_params=_cparams(("arbitrary", "arbitrary")),
        name="moe_dispatch",
    )(lo_blk, n_blk, d1.reshape(B, 1, T), d2.reshape(B, 1, T), rec3, hn3)


def _expert_up_kernel(te_ref, nb_ref, x_ref, wg_ref, wu_ref, p_ref, o_ref):
    i = pl.program_id(0)

    @pl.when(nb_ref[i] > 0)
    def _():
        x = x_ref[...]
        p = p_ref[:, 0:1] + p_ref[:, 1:2]
        o_ref[...] = (_silu(_dot(x, wg_ref[0])) * _dot(x, wu_ref[0]) * p).astype(o_ref.dtype)

    @pl.when(nb_ref[i] == 0)
    def _():
        o_ref[...] = jnp.zeros_like(o_ref)


def _expert_down_kernel(te_ref, nb_ref, a_ref, wd_ref, o_ref):
    i = pl.program_id(0)

    @pl.when(nb_ref[i] > 0)
    def _():
        o_ref[...] = _dot(a_ref[...], wd_ref[0]).astype(o_ref.dtype)

    @pl.when(nb_ref[i] == 0)
    def _():
        o_ref[...] = jnp.zeros_like(o_ref)


def _expert_ffn(x, p_rows, tile_expert, n_blk, wg, wu, wd, *, tm):
    R, D = x.shape
    E, _, F = wg.shape
    up_spec = pltpu.PrefetchScalarGridSpec(
        num_scalar_prefetch=2,
        grid=(R // tm,),
        in_specs=[pl.BlockSpec((tm, D), lambda i, te, nb: (i, 0)),
                  pl.BlockSpec((1, D, F), lambda i, te, nb: (te[i], 0, 0)),
                  pl.BlockSpec((1, D, F), lambda i, te, nb: (te[i], 0, 0)),
                  pl.BlockSpec((tm, LANES), lambda i, te, nb: (i, 0))],
        out_specs=pl.BlockSpec((tm, F), lambda i, te, nb: (i, 0)))
    act = pl.pallas_call(
        _expert_up_kernel,
        out_shape=jax.ShapeDtypeStruct((R, F), BF16),
        grid_spec=up_spec,
        compiler_params=_cparams(("arbitrary",)),
        name="moe_up",
    )(tile_expert, n_blk, x, wg, wu, p_rows)
    down_spec = pltpu.PrefetchScalarGridSpec(
        num_scalar_prefetch=2,
        grid=(R // tm,),
        in_specs=[pl.BlockSpec((tm, F), lambda i, te, nb: (i, 0)),
                  pl.BlockSpec((1, F, D), lambda i, te, nb: (te[i], 0, 0))],
        out_specs=pl.BlockSpec((tm, D), lambda i, te, nb: (i, 0)))
    return pl.pallas_call(
        _expert_down_kernel,
        out_shape=jax.ShapeDtypeStruct((R, D), BF16),
        grid_spec=down_spec,
        compiler_params=_cparams(("arbitrary",)),
        name="moe_down",
    )(tile_expert, n_blk, act, wd)


def _combine_kernel(yb_ref, yv_ref, h_ref, d1_ref, d2_ref, *refs, tm):
    y_refs, o_ref = refs[:-1], refs[-1]
    n = o_ref.shape[0]
    slots = len(y_refs) // N_EXPERTS
    base = (pl.program_id(0) * pl.num_programs(1) + pl.program_id(1)) * len(y_refs)
    d1, d2 = d1_ref[...], d2_ref[...]
    lane = lax.broadcasted_iota(jnp.int32, (n, tm), 1)

    def picked(k, weight):
        rows = yb_ref[base + k] * tm + lane
        hit = jnp.where(d1 == rows, weight, 0.0) + jnp.where(d2 == rows, weight, 0.0)
        return _dot(hit.astype(BF16), y_refs[k][...])

    acc = h_ref[...]
    for k in range(0, len(y_refs), slots):
        acc = acc + picked(k, jnp.where(yv_ref[base + k] > 0, 1.0, 0.0))
    o_ref[...] = acc
    for k in range(len(y_refs)):
        if k % slots:
            @pl.when(yv_ref[base + k] > 0)
            def _():
                o_ref[...] += picked(k, 1.0)


def _combine(h, y, d1, d2, y_blk, y_valid, B, T, *, tm):
    M, D = h.shape
    n = ROW_TILE
    NT = T // n
    slots = y_blk.shape[0] // (B * NT)
    y_specs = [pl.BlockSpec((tm, D), lambda b, t, yb, yv, k=k: (yb[(b * NT + t) * slots + k], 0))
               for k in range(slots)]
    tok = lambda w: pl.BlockSpec((n, w), lambda b, t, yb, yv: (b * NT + t, 0))
    grid_spec = pltpu.PrefetchScalarGridSpec(
        num_scalar_prefetch=2,
        grid=(B, NT),
        in_specs=[tok(D), tok(1), tok(1)] + y_specs,
        out_specs=tok(D))
    return pl.pallas_call(
        functools.partial(_combine_kernel, tm=tm),
        out_shape=jax.ShapeDtypeStruct((M, D), F32),
        grid_spec=grid_spec,
        compiler_params=_cparams(("arbitrary", "arbitrary")),
        name="moe_combine",
    )(y_blk, y_valid, h, d1, d2, *([y] * slots))


def _moe(h, hn, rec, cnt_before, wg, wu, wd, B, T):
    M, D = h.shape
    E, tm, n = N_EXPERTS, MOE_TM, ROW_TILE
    NT = T // n
    rb = TOP_K * T + E * tm
    i32 = jnp.int32
    col = lambda ln: rec[:, ln].reshape(B, T)
    e1, e2 = col(R_E1).astype(i32), col(R_E2).astype(i32)
    r1, r2 = col(R_R1).astype(i32), col(R_R2).astype(i32)
    cntb = cnt_before.reshape(B, NT, 8, LANES)[:, :, 0, :E].astype(i32)
    oh1, oh2 = jax.nn.one_hot(e1, E, dtype=i32), jax.nn.one_hot(e2, E, dtype=i32)
    counts = (oh1 + oh2).sum(axis=1)
    padded = (counts + tm - 1) // tm * tm
    ends = jnp.cumsum(padded, axis=1)
    off = ends - padded
    d1 = (oh1 * off[:, None, :]).sum(-1) + r1
    d2 = (oh2 * off[:, None, :]).sum(-1) + r2
    cnt_end = jnp.concatenate([cntb[:, 1:], counts[:, None, :]], axis=1)
    def tiles(rows):
        start = jnp.arange(rb // rows, dtype=i32)[None, :] * rows
        expert = jnp.minimum((start[:, :, None] >= ends[:, None, :]).sum(-1), E - 1).astype(i32)
        e_oh = jax.nn.one_hot(expert, E, dtype=i32)
        pick = lambda per_expert: (e_oh * per_expert[:, None, :]).sum(-1)
        rank_lo = start - pick(off)
        rank_hi = jnp.minimum(rank_lo + rows, pick(counts)) - 1
        through = (e_oh[:, :, None, :] * cnt_end[:, None, :, :]).sum(-1)
        lo_blk = (through <= rank_lo[..., None]).sum(-1)
        hi_blk = (through <= rank_hi[..., None]).sum(-1)
        nonempty = (start < pick(ends)) & (rank_hi >= rank_lo)
        return (expert.reshape(-1), jnp.where(nonempty, lo_blk, 0).astype(i32).reshape(-1),
                jnp.where(nonempty, hi_blk - lo_blk + 1, 0).astype(i32).reshape(-1))

    pr = MOE_DISPATCH_ROWS
    _, lo_blk, n_blk = tiles(pr)
    x, p_rows = _dispatch(hn.reshape(B, T, D), rec.reshape(B, T, LANES), d1, d2, lo_blk, n_blk, rb // pr, tm=pr)
    tile_expert, _, tile_used = tiles(tm)
    y = _expert_ffn(x, p_rows, tile_expert, tile_used, wg, wu, wd, tm=tm)
    first = off[:, None, :] + cntb
    last = off[:, None, :] + cnt_end - 1
    used = cnt_end > cntb
    span = jnp.arange(n // tm + 1, dtype=i32)
    blk = first[..., None] // tm + span
    valid = used[..., None] & (blk <= last[..., None] // tm)
    blk = jnp.minimum(blk, rb // tm - 1) + (jnp.arange(B, dtype=i32) * (rb // tm))[:, None, None, None]
    row0 = (jnp.arange(B, dtype=i32) * rb)[:, None]
    return _combine(h, y, (d1 + row0).reshape(M, 1), (d2 + row0).reshape(M, 1),
                    blk.reshape(-1).astype(i32), valid.reshape(-1).astype(i32), B, T, tm=tm)


def _merge_kernel(hn_ref, b0_ref, b1_ref, b2_ref, b3_ref, wg_ref, wb_ref, o_ref):
    hn = hn_ref[...]
    acc = None
    for b, br_ref in enumerate((b0_ref, b1_ref, b2_ref, b3_ref)):
        gate = jax.nn.sigmoid(_dot(hn, wg_ref[b]))
        term = gate * _dot(br_ref[...], wb_ref[b])
        acc = term if acc is None else acc + term
    o_ref[...] = acc.astype(o_ref.dtype)


def _merge(hn, branches, wg, wb, *, tn=512, tm=512):
    M, D = hn.shape
    N = wg.shape[2]
    bspec = pl.BlockSpec((tm, BRANCH_W), lambda j, i: (i, 0))
    return pl.pallas_call(
        _merge_kernel,
        out_shape=jax.ShapeDtypeStruct((M, N), BF16),
        grid=(N // tn, M // tm),
        in_specs=[pl.BlockSpec((tm, D), lambda j, i: (i, 0)), bspec, bspec, bspec, bspec,
                  pl.BlockSpec((N_BRANCH, D, tn), lambda j, i: (0, 0, j)),
                  pl.BlockSpec((N_BRANCH, BRANCH_W, tn), lambda j, i: (0, 0, j))],
        out_specs=pl.BlockSpec((tm, tn), lambda j, i: (i, j)),
        compiler_params=_cparams(("parallel", "parallel")),
        name="gate_merge",
    )(hn, *branches, wg, wb)


def _causal_conv(x, carry_ref, cw):
    n = x.shape[0]
    xext = jnp.concatenate([carry_ref[...], x], axis=0)
    y = cw[0:1] * xext[5:5 + n]
    for i in range(1, CONV_K):
        y = y + cw[i:i + 1] * xext[5 + i:5 + i + n]
    carry_ref[...] = x[n - 8:n]
    return y


def _tile_masks(n, chunk):
    row = lax.broadcasted_iota(jnp.int32, (n, n), 0)
    col = lax.broadcasted_iota(jnp.int32, (n, n), 1)
    if chunk == n:
        return col <= row, col < row
    in_chunk = col >= (row // chunk) * chunk
    return in_chunk & (col <= row), in_chunk & (col < row)


def _gdn_kernel(qkv_ref, z_ref, sm_ref, cw_ref, alog_ref, dtb_ref, ng_ref, o_ref, s_ref, carry_ref):
    @pl.when(pl.program_id(1) == 0)
    def _():
        s_ref[...] = jnp.zeros_like(s_ref)
        carry_ref[...] = jnp.zeros_like(carry_ref)

    n = ROW_TILE
    y = _silu(_causal_conv(qkv_ref[0], carry_ref, cw_ref[...]))
    sm = sm_ref[0]
    z = z_ref[0]
    beta_all = jax.nn.sigmoid(sm)
    g_all = -jnp.exp(alog_ref[...]) * _softplus(sm + dtb_ref[...])
    row = lax.broadcasted_iota(jnp.int32, (n, n), 0)
    col = lax.broadcasted_iota(jnp.int32, (n, n), 1)
    causal, strict = col <= row, col < row
    diff_bits = row ^ col
    levels = int(math.log2(n))
    level_masks = [(diff_bits >= (1 << l)) & (diff_bits < (2 << l)) for l in range(levels)]
    eye = jnp.where(row == col, 1.0, 0.0)
    gcs_all = _dot(causal.astype(F32), g_all, HI)
    gcs_t = gcs_all.T
    heads = []
    for h in range(GDN_HEADS):
        q = y[:, h * GDN_DK:(h + 1) * GDN_DK]
        k = y[:, GDN_HEADS * GDN_DK + h * GDN_DK:GDN_HEADS * GDN_DK + (h + 1) * GDN_DK]
        v = y[:, 2 * GDN_HEADS * GDN_DK + h * GDN_DV:2 * GDN_HEADS * GDN_DK + (h + 1) * GDN_DV]
        q = q * lax.rsqrt(jnp.sum(q * q, axis=-1, keepdims=True) + EPS) * GDN_DK ** -0.5
        k = k * lax.rsqrt(jnp.sum(k * k, axis=-1, keepdims=True) + EPS)
        beta = beta_all[:, L_BETA + h:L_BETA + h + 1]
        gc = gcs_all[:, L_GA + h:L_GA + h + 1]
        gr = gcs_t[L_GA + h:L_GA + h + 1, :]
        decay = jnp.exp(jnp.where(causal, gc - gr, NEG))
        kb = k * beta
        k16 = k.astype(BF16)
        a = jnp.where(strict, _dot_nt(kb.astype(BF16), k16) * decay, 0.0)
        heads.append(dict(q=q, k=k, v=v, beta=beta, gc=gc, decay=decay, kb=kb, k16=k16, a=a,
                          t=eye - jnp.where(level_masks[0], a, 0.0)))
    for l in range(1, levels):
        for hd in heads:
            t16 = hd["t"].astype(BF16)
            a_l = jnp.where(level_masks[l], hd["a"], 0.0).astype(BF16)
            hd["t"] = hd["t"] - _dot(_dot(t16, a_l).astype(BF16), t16)
    outs = []
    for h, hd in enumerate(heads):
        q, k, gc = hd["q"], hd["k"], hd["gc"]
        egc = jnp.exp(gc)
        rhs = jnp.concatenate([hd["v"] * hd["beta"], hd["kb"] * egc], axis=1)
        uw = _dot(hd["t"].astype(BF16), rhs.astype(BF16))
        u, w = uw[:, :GDN_DV], uw[:, GDN_DV:]
        att = _dot_nt(q.astype(BF16), hd["k16"]) * hd["decay"]
        g_last = gc[n - 1:n, :]
        kd = k * jnp.exp(g_last - gc)
        s = s_ref[h]
        ws = _dot(jnp.concatenate([w, q * egc], axis=0).astype(BF16), s.astype(BF16))
        v_new = (u - ws[:n]).astype(BF16)
        o = ws[n:] + _dot(att.astype(BF16), v_new)
        s_ref[h] = s * jnp.exp(g_last) + _dot(kd.T.astype(BF16), v_new)
        o = o * lax.rsqrt(jnp.mean(o * o, axis=-1, keepdims=True) + EPS) * ng_ref[...]
        outs.append(o * _silu(z[:, h * GDN_DV:(h + 1) * GDN_DV]))
    o_ref[0] = jnp.concatenate(outs, axis=1).astype(o_ref.dtype)


def _lane_vec(vals, lane0):
    v = jnp.zeros((1, LANES), F32)
    return v.at[0, lane0:lane0 + vals.shape[0]].set(vals.astype(F32))


def _gdn(proj, conv_w, a_log, dt_bias, norm_g):
    B, T, _ = proj.shape
    n = ROW_TILE
    W = 2 * GDN_HEADS * GDN_DK + GDN_HEADS * GDN_DV
    ZW = GDN_HEADS * GDN_DV
    return pl.pallas_call(
        _gdn_kernel,
        out_shape=jax.ShapeDtypeStruct((B, T, ZW), BF16),
        grid=(B, T // n),
        in_specs=[pl.BlockSpec((1, n, W), lambda b, t: (b, t, COL_GDN_QKV // W)),
                  pl.BlockSpec((1, n, ZW), lambda b, t: (b, t, COL_GDN_Z // ZW)),
                  pl.BlockSpec((1, n, LANES), lambda b, t: (b, t, COL_SMALL // LANES)),
                  pl.BlockSpec((CONV_K, W), lambda b, t: (0, 0)),
                  pl.BlockSpec((1, LANES), lambda b, t: (0, 0)),
                  pl.BlockSpec((1, LANES), lambda b, t: (0, 0)),
                  pl.BlockSpec((1, GDN_DV), lambda b, t: (0, 0))],
        out_specs=pl.BlockSpec((1, n, ZW), lambda b, t: (b, t, 0)),
        scratch_shapes=[pltpu.VMEM((GDN_HEADS, GDN_DK, GDN_DV), F32),
                        pltpu.VMEM((8, W), F32)],
        compiler_params=_cparams(("arbitrary", "arbitrary")),
        name="gdn",
    )(proj, proj, proj, conv_w.astype(F32), _lane_vec(a_log, L_GA), _lane_vec(dt_bias, L_GA),
      norm_g.reshape(1, GDN_DV).astype(F32))


def _ssd_kernel(xbc_ref, z_ref, sm_ref, cw_ref, cb_ref, alog_ref, dtb_ref, dvec_ref, ng_ref,
                o_ref, hs_ref, carry_ref):
    @pl.when(pl.program_id(1) == 0)
    def _():
        hs_ref[...] = jnp.zeros_like(hs_ref)
        carry_ref[...] = jnp.zeros_like(carry_ref)

    n, P, N = ROW_TILE, SSD_HEADDIM, SSD_STATE
    y = _silu(_causal_conv(xbc_ref[0], carry_ref, cw_ref[...]) + cb_ref[...])
    xs = y[:, :SSD_INNER]
    bm = y[:, SSD_INNER:SSD_INNER + SSD_GROUPS * N]
    cm = y[:, SSD_INNER + SSD_GROUPS * N:]
    sm = sm_ref[0]
    dt_all = _softplus(sm + dtb_ref[...])
    a_all = dt_all * (-jnp.exp(alog_ref[...]))
    causal, _ = _tile_masks(n, n)
    acs_all = _dot(causal.astype(F32), a_all, HI)
    acs_t = acs_all.T
    ys = []
    for g in range(SSD_GROUPS):
        bg = bm[:, g * N:(g + 1) * N]
        cg16 = cm[:, g * N:(g + 1) * N].astype(BF16)
        cb = _dot_nt(cg16, bg.astype(BF16))
        bgt16 = bg.T.astype(BF16)
        for j in range(SSD_HG):
            hh = g * SSD_HG + j
            ac = acs_all[:, L_DT + hh:L_DT + hh + 1]
            ar = acs_t[L_DT + hh:L_DT + hh + 1, :]
            lmat = jnp.exp(jnp.where(causal, ac - ar, NEG))
            xdt = xs[:, hh * P:(hh + 1) * P] * dt_all[:, L_DT + hh:L_DT + hh + 1]
            y_diag = _dot((cb * lmat).astype(BF16), xdt.astype(BF16))
            a_last = ac[n - 1:n, :]
            st = _dot(bgt16, (xdt * jnp.exp(a_last - ac)).astype(BF16))
            h_prev = hs_ref[hh]
            y_off = _dot(cg16, h_prev.astype(BF16)) * jnp.exp(ac)
            hs_ref[hh] = h_prev * jnp.exp(a_last) + st
            ys.append(y_diag + y_off)
    yy = jnp.concatenate(ys, axis=1) + xs * dvec_ref[...]
    yy = yy * _silu(z_ref[0])
    gw = SSD_HG * P
    outs = []
    for g in range(SSD_GROUPS):
        seg = yy[:, g * gw:(g + 1) * gw]
        outs.append(seg * lax.rsqrt(jnp.mean(seg * seg, axis=-1, keepdims=True) + EPS)
                    * ng_ref[:, g * gw:(g + 1) * gw])
    o_ref[0] = jnp.concatenate(outs, axis=1).astype(o_ref.dtype)


def _ssd(proj, conv_w, conv_b, dt_bias, a_log, d_skip, norm_g):
    B, T, _ = proj.shape
    n = ROW_TILE
    W = SSD_INNER + 2 * SSD_GROUPS * SSD_STATE
    dvec = jnp.repeat(d_skip.astype(F32), SSD_HEADDIM).reshape(1, SSD_INNER)
    return pl.pallas_call(
        _ssd_kernel,
        out_shape=jax.ShapeDtypeStruct((B, T, SSD_INNER), BF16),
        grid=(B, T // n),
        in_specs=[pl.BlockSpec((1, n, W), lambda b, t: (b, t, COL_SSD_XBC // W)),
                  pl.BlockSpec((1, n, SSD_INNER), lambda b, t: (b, t, COL_SSD_Z // SSD_INNER)),
                  pl.BlockSpec((1, n, LANES), lambda b, t: (b, t, COL_SMALL // LANES)),
                  pl.BlockSpec((CONV_K, W), lambda b, t: (0, 0)),
                  pl.BlockSpec((1, W), lambda b, t: (0, 0)),
                  pl.BlockSpec((1, LANES), lambda b, t: (0, 0)),
                  pl.BlockSpec((1, LANES), lambda b, t: (0, 0)),
                  pl.BlockSpec((1, SSD_INNER), lambda b, t: (0, 0)),
                  pl.BlockSpec((1, SSD_INNER), lambda b, t: (0, 0))],
        out_specs=pl.BlockSpec((1, n, SSD_INNER), lambda b, t: (b, t, 0)),
        scratch_shapes=[pltpu.VMEM((SSD_HEADS, SSD_STATE, SSD_HEADDIM), F32),
                        pltpu.VMEM((8, W), F32)],
        compiler_params=_cparams(("arbitrary", "arbitrary")),
        name="ssd",
    )(proj, proj, proj, conv_w.astype(F32), conv_b.reshape(1, W).astype(F32),
      _lane_vec(a_log, L_DT), _lane_vec(dt_bias, L_DT), dvec,
      norm_g.reshape(1, SSD_INNER).astype(F32))


def _ones_lane0(n):
    lane = lax.broadcasted_iota(jnp.int32, (n, LANES), 1)
    return jnp.where(lane == 0, 1.0, 0.0)


def _fox_prep_kernel(qkv_ref, sm_ref, qg_ref, kg_ref, bf_ref, q_ref, k_ref, v_ref, run_ref):
    @pl.when(pl.program_id(1) == 0)
    def _():
        run_ref[...] = jnp.zeros_like(run_ref)

    n, dh, H = ROW_TILE, FOX_DH, FOX_HEADS
    x = qkv_ref[0]
    log_f = -_softplus(-(sm_ref[0] + bf_ref[...]))
    causal, _ = _tile_masks(n, n)
    cum = _dot(causal.astype(F32), log_f, HI) + run_ref[0:1, :]
    run_ref[...] = jnp.broadcast_to(cum[n - 1:n, :], run_ref.shape)
    lane = lax.broadcasted_iota(jnp.int32, (n, LANES), 1)
    ones = _ones_lane0(n)
    q_bias = jnp.where(lane < 3, 1.0, 0.0)
    qs, ks, vs = [], [], []
    for h in range(H):
        q = x[:, h * dh:(h + 1) * dh]
        k = x[:, H * dh + h * dh:H * dh + (h + 1) * dh]
        qs.append(q * lax.rsqrt(jnp.mean(q * q, axis=-1, keepdims=True) + EPS)
                  * (qg_ref[...] * (dh ** -0.5 * LOG2E)))
        qs.append(q_bias)
        ks.append(k * lax.rsqrt(jnp.mean(k * k, axis=-1, keepdims=True) + EPS) * kg_ref[...])
        c = cum[:, L_FF + h:L_FF + h + 1] * (-LOG2E)
        c_hi = c.astype(BF16).astype(F32)
        c_mid = (c - c_hi).astype(BF16).astype(F32)
        c_lo = c - c_hi - c_mid
        ks.append(jnp.where(lane == 0, c_hi, jnp.where(lane == 1, c_mid, jnp.where(lane == 2, c_lo, 0.0))))
        vs.append(x[:, 2 * H * dh + h * dh:2 * H * dh + (h + 1) * dh])
        vs.append(ones)
    q_ref[0] = jnp.concatenate(qs, axis=1).astype(q_ref.dtype)
    k_ref[0] = jnp.concatenate(ks, axis=1).astype(k_ref.dtype)
    v_ref[0] = jnp.concatenate(vs, axis=1).astype(v_ref.dtype)


def _fox_prep(proj, qn_g, kn_g, b_f):
    B, T, _ = proj.shape
    n = ROW_TILE
    W = 3 * FOX_HEADS * FOX_DH
    HW = FOX_HEADS * 2 * LANES
    ospec = pl.BlockSpec((1, n, HW), lambda b, t: (b, t, 0))
    return pl.pallas_call(
        _fox_prep_kernel,
        out_shape=(jax.ShapeDtypeStruct((B, T, HW), BF16),) * 3,
        grid=(B, T // n),
        in_specs=[pl.BlockSpec((1, n, W), lambda b, t: (b, t, COL_FOX_QKV // W)),
                  pl.BlockSpec((1, n, LANES), lambda b, t: (b, t, COL_SMALL // LANES)),
                  pl.BlockSpec((1, FOX_DH), lambda b, t: (0, 0)),
                  pl.BlockSpec((1, FOX_DH), lambda b, t: (0, 0)),
                  pl.BlockSpec((1, LANES), lambda b, t: (0, 0))],
        out_specs=(ospec, ospec, ospec),
        scratch_shapes=[pltpu.VMEM((8, LANES), F32)],
        compiler_params=_cparams(("arbitrary", "arbitrary")),
        name="fox_prep",
    )(proj, proj, qn_g.reshape(1, FOX_DH).astype(F32), kn_g.reshape(1, FOX_DH).astype(F32),
      _lane_vec(b_f, L_FF))


def _mla_prep_kernel(qa_ref, kva_ref, sm_ref, cos_ref, sin_ref, qag_ref, wq_ref, kvag_ref, wkv_ref,
                     qgn_ref, qgr_ref, kgn_ref, kgr_ref, rot_ref, exp_ref, q_ref, k_ref, v_ref):
    n, H, dn, dr = ROW_TILE, MLA_HEADS, MLA_NOPE, MLA_ROPE
    qa = qa_ref[0]
    qa = qa * lax.rsqrt(jnp.mean(qa * qa, axis=-1, keepdims=True) + EPS) * qag_ref[...]
    qq = _dot(qa.astype(BF16), wq_ref[...])
    kva = kva_ref[0]
    kva = kva * lax.rsqrt(jnp.mean(kva * kva, axis=-1, keepdims=True) + EPS) * kvag_ref[...]
    kv = _dot(kva.astype(BF16), wkv_ref[...])
    sm = sm_ref[0]
    lane = lax.broadcasted_iota(jnp.int32, (n, LANES), 1)
    is_kpe = (lane >= L_KPE) & (lane < L_KPE + dr)
    kpe_ss = jnp.sum(jnp.where(is_kpe, sm * sm, 0.0), axis=-1, keepdims=True)
    kpe4 = _dot(sm, exp_ref[...], HI)
    qr = qq[:, H * dn:]
    grp = lax.broadcasted_iota(jnp.int32, (n, H * dr), 1) // dr
    q_rs, k_rs = [], []
    for h in range(H):
        qn = qq[:, h * dn:(h + 1) * dn]
        ssr = jnp.sum(jnp.where(grp == h, qr * qr, 0.0), axis=-1, keepdims=True)
        q_rs.append(lax.rsqrt((jnp.sum(qn * qn, axis=-1, keepdims=True) + ssr) / MLA_DQK + EPS))
        kn = kv[:, h * dn:(h + 1) * dn]
        k_rs.append(lax.rsqrt((jnp.sum(kn * kn, axis=-1, keepdims=True) + kpe_ss) / MLA_DQK + EPS))

    def per_group(vals):
        out = vals[H - 1]
        for h in range(H - 2, -1, -1):
            out = jnp.where(grp == h, vals[h], out)
        return out

    cos, sin, rot = cos_ref[...], sin_ref[...], rot_ref[...]
    tq = qr * per_group(q_rs) * qgr_ref[...]
    tq = tq * cos + _dot(tq, rot, HI) * sin
    tk = kpe4 * per_group(k_rs) * kgr_ref[...]
    tk = tk * cos + _dot(tk, rot, HI) * sin
    scale = MLA_DQK ** -0.5 * LOG2E
    half_id = lane // dr
    ones = _ones_lane0(n)
    q_parts, k_parts, v_parts = [], [], []
    for h in range(H):
        blk = slice((h // 2) * LANES, (h // 2 + 1) * LANES)
        q_parts.append(qq[:, h * dn:(h + 1) * dn] * q_rs[h] * (qgn_ref[...] * scale))
        q_parts.append(jnp.where(half_id == h % 2, tq[:, blk] * scale, 0.0))
        k_parts.append(kv[:, h * dn:(h + 1) * dn] * k_rs[h] * kgn_ref[...])
        k_parts.append(tk[:, blk])
        v_parts.append(kv[:, H * dn + h * MLA_V:H * dn + (h + 1) * MLA_V])
        v_parts.append(ones)
    q_ref[0] = jnp.concatenate(q_parts, axis=1).astype(q_ref.dtype)
    k_ref[0] = jnp.concatenate(k_parts, axis=1).astype(k_ref.dtype)
    v_ref[0] = jnp.concatenate(v_parts, axis=1).astype(v_ref.dtype)


def _rope_consts(T):
    H, dr = MLA_HEADS, MLA_ROPE
    inv = 1.0 / (ROPE_BASE ** (jnp.arange(0, dr, 2, dtype=F32) / dr))
    ang = jnp.arange(T, dtype=F32)[:, None] * inv[None, :]
    ang = jnp.concatenate([ang, ang], axis=-1)
    cos4 = jnp.tile(jnp.cos(ang), (1, H))
    sin4 = jnp.tile(jnp.sin(ang), (1, H))
    rot = np.zeros((H * dr, H * dr), np.float32)
    for h in range(H):
        for c in range(dr // 2):
            rot[h * dr + c + dr // 2, h * dr + c] = -1.0
            rot[h * dr + c, h * dr + c + dr // 2] = 1.0
    expand = np.zeros((LANES, H * dr), np.float32)
    for h in range(H):
        for c in range(dr):
            expand[L_KPE + c, h * dr + c] = 1.0
    return cos4, sin4, jnp.asarray(rot), jnp.asarray(expand)


def _mla_prep(proj, rope, qa_g, wq_b, kva_g, wkv_b, qn_g, kn_g):
    B, T, _ = proj.shape
    n, H, dn, dr, dv = ROW_TILE, MLA_HEADS, MLA_NOPE, MLA_ROPE, MLA_V
    cos4, sin4, rot, expand = rope
    wq = wq_b.reshape(MLA_Q_RANK, H, MLA_DQK)
    wq = jnp.concatenate([wq[:, :, :dn].reshape(MLA_Q_RANK, H * dn),
                          wq[:, :, dn:].reshape(MLA_Q_RANK, H * dr)], axis=1).astype(BF16)
    wkv = wkv_b.reshape(MLA_KV_RANK, H, dn + dv)
    wkv = jnp.concatenate([wkv[:, :, :dn].reshape(MLA_KV_RANK, H * dn),
                           wkv[:, :, dn:].reshape(MLA_KV_RANK, H * dv)], axis=1).astype(BF16)
    qg, kg = qn_g.astype(F32), kn_g.astype(F32)
    const = lambda shape: pl.BlockSpec(shape, lambda b, t: (0,) * len(shape))
    QW = H * 2 * LANES
    return pl.pallas_call(
        _mla_prep_kernel,
        out_shape=(jax.ShapeDtypeStruct((B, T, QW), BF16), jax.ShapeDtypeStruct((B, T, QW), BF16),
                   jax.ShapeDtypeStruct((B, T, QW), BF16)),
        grid=(B, T // n),
        in_specs=[pl.BlockSpec((1, n, MLA_Q_RANK), lambda b, t: (b, t, COL_MLA_QA // MLA_Q_RANK)),
                  pl.BlockSpec((1, n, MLA_KV_RANK), lambda b, t: (b, t, COL_MLA_KVA // MLA_KV_RANK)),
                  pl.BlockSpec((1, n, LANES), lambda b, t: (b, t, COL_SMALL // LANES)),
                  pl.BlockSpec((n, H * dr), lambda b, t: (t, 0)),
                  pl.BlockSpec((n, H * dr), lambda b, t: (t, 0)),
                  const((1, MLA_Q_RANK)), const(wq.shape), const((1, MLA_KV_RANK)), const(wkv.shape),
                  const((1, dn)), const((1, H * dr)), const((1, dn)), const((1, H * dr)),
                  const(rot.shape), const(expand.shape)],
        out_specs=(pl.BlockSpec((1, n, QW), lambda b, t: (b, t, 0)),
                   pl.BlockSpec((1, n, QW), lambda b, t: (b, t, 0)),
                   pl.BlockSpec((1, n, QW), lambda b, t: (b, t, 0))),
        compiler_params=_cparams(("parallel", "parallel")),
        name="mla_prep",
    )(proj, proj, proj, cos4, sin4,
      qa_g.reshape(1, -1).astype(F32), wq, kva_g.reshape(1, -1).astype(F32), wkv,
      qg[:dn].reshape(1, dn), jnp.tile(qg[dn:], H).reshape(1, H * dr),
      kg[:dn].reshape(1, dn), jnp.tile(kg[dn:], H).reshape(1, H * dr), rot, expand)


def _flash_kernel(q_ref, k_ref, v_ref, o_ref, m_ref, acc_ref, *, tq):
    qi = pl.program_id(2)
    nh = m_ref.shape[0]
    dqk, dv2 = q_ref.shape[-1] // nh, v_ref.shape[-1] // nh
    dv = dv2 // 2
    rb = ROW_TILE
    m_ref[...] = jnp.full(m_ref.shape, NEG, F32)
    acc_ref[...] = jnp.zeros(acc_ref.shape, F32)

    def step(j, masked):
        start = pl.multiple_of(j * tq, tq)
        chains = [(h, r) for h in range(nh) for r in range(0, tq, rb)]
        logits, probs = {}, {}

        def qk(c):
            h, r = chains[c]
            logits[c] = _dot_nt(q_ref[0, r:r + rb, h * dqk:(h + 1) * dqk],
                                k_ref[0, pl.ds(start, tq), h * dqk:(h + 1) * dqk])

        def softmax(c):
            h, r = chains[c]
            s = logits.pop(c)
            if masked:
                row = lax.broadcasted_iota(jnp.int32, s.shape, 0) + r
                col = lax.broadcasted_iota(jnp.int32, s.shape, 1)
                s = jnp.where(col <= row, s, NEG)
            lane_max = s[:, :LANES]
            for cb in range(1, tq // LANES):
                lane_max = jnp.maximum(lane_max, s[:, cb * LANES:(cb + 1) * LANES])
            m_old = m_ref[h, r:r + rb]
            m_new = jnp.maximum(m_old, jnp.max(lane_max, axis=-1, keepdims=True))
            m_ref[h, r:r + rb] = m_new
            probs[c] = (jnp.exp2(s - m_new).astype(BF16), jnp.exp2(m_old - m_new))

        def pv(c):
            h, r = chains[c]
            p, alpha = probs.pop(c)
            acc_ref[h, r:r + rb] = (alpha * acc_ref[h, r:r + rb]
                                    + _dot(p, v_ref[0, pl.ds(start, tq), h * dv2:(h + 1) * dv2]))

        for c in range(len(chains)):
            qk(c)
            softmax(c)
            pv(c)

    def body(jj, carry):
        step(2 * jj, False)
        step(2 * jj + 1, False)
        return carry

    lax.fori_loop(0, qi // 2, body, 0)

    @pl.when(qi % 2 == 1)
    def _():
        step(qi - 1, False)

    step(qi, True)
    outs = []
    for h in range(nh):
        acc = acc_ref[h]
        outs.append(acc[:, :dv] / acc[:, dv:dv + 1])
    o_ref[0] = jnp.concatenate(outs, axis=1).astype(o_ref.dtype)


def _flash(q, k, v):
    B, T, QW = q.shape
    H, nh = MLA_HEADS, FLASH_HEADS_PER_STEP
    dqk, dv2 = QW // H, v.shape[-1] // H
    dv = dv2 // 2
    tq = next(t for t in FLASH_TILES if T % t == 0)
    return pl.pallas_call(
        functools.partial(_flash_kernel, tq=tq),
        out_shape=jax.ShapeDtypeStruct((B, T, H * dv), BF16),
        grid=(B, H // nh, T // tq),
        in_specs=[pl.BlockSpec((1, tq, nh * dqk), lambda b, h, i: (b, i, h)),
                  pl.BlockSpec((1, T, nh * dqk), lambda b, h, i: (b, 0, h)),
                  pl.BlockSpec((1, T, nh * dv2), lambda b, h, i: (b, 0, h))],
        out_specs=pl.BlockSpec((1, tq, nh * dv), lambda b, h, i: (b, i, h)),
        scratch_shapes=[pltpu.VMEM((nh, tq, 1), F32), pltpu.VMEM((nh, tq, dv2), F32)],
        compiler_params=_cparams(("parallel", "parallel", "arbitrary")),
        name="flash",
    )(q, k, v)


def _reorder_kernel(w_ref, o_ref):
    w = w_ref[...]
    n = w.shape[0]
    z = lambda width: jnp.zeros((n, width), w.dtype)
    small = [w[:, 2048:2052], w[:, 2052:2056], w[:, 4424:4428], w[:, 5964:5972], w[:, 2824:2888]]
    n_small = sum(s.shape[1] for s in small)
    cols = [w[:, 0:2048],
            w[:, 4940:5964],
            w[:, 4428:4940],
            w[:, 2056:2824],
            *small, z(LANES - n_small), z(COL_FOX_QKV - COL_SMALL - LANES),
            w[:, 2888:4424]]
    o_ref[...] = jnp.concatenate(cols, axis=1).astype(o_ref.dtype)


def _reorder_w_in(w_in, layer):
    _, D, W = w_in.shape
    rows = ROW_TILE
    return pl.pallas_call(
        _reorder_kernel,
        out_shape=jax.ShapeDtypeStruct((D, PROJ_W), BF16),
        grid=(D // rows,),
        in_specs=[pl.BlockSpec((None, rows, W), lambda i: (layer, i, 0))],
        out_specs=pl.BlockSpec((rows, PROJ_W), lambda i: (i, 0)),
        compiler_params=_cparams(("parallel",)),
        name="reorder_w_in",
    )(w_in)


def _mixer(h, hn, B, T, rope, w_in_cols, gdn_conv_w, gdn_A_log, gdn_dt_bias, gdn_norm_g,
           mla_qa_g, mla_wq_b, mla_kva_g, mla_wkv_b, mla_qn_g, mla_kn_g,
           fox_qn_g, fox_kn_g, fox_b_f,
           ssd_conv_w, ssd_conv_b, ssd_dt_bias, ssd_A_log, ssd_D, ssd_norm_g,
           w_gate, w_branch, w_o, next_norm_g=None):
    M = B * T
    proj = _matmul(hn, w_in_cols, tn=1024, name="in_proj").reshape(B, T, PROJ_W)
    o_gdn = _gdn(proj, gdn_conv_w, gdn_A_log, gdn_dt_bias, gdn_norm_g)
    mq, mk, mv = _mla_prep(proj, rope, mla_qa_g, mla_wq_b, mla_kva_g, mla_wkv_b, mla_qn_g, mla_kn_g)
    o_mla = _flash(mq, mk, mv)
    o_fox = _flash(*_fox_prep(proj, fox_qn_g, fox_kn_g, fox_b_f))
    o_ssd = _ssd(proj, ssd_conv_w, ssd_conv_b, ssd_dt_bias, ssd_A_log, ssd_D, ssd_norm_g)
    branches = [o.reshape(M, BRANCH_W) for o in (o_gdn, o_mla, o_fox, o_ssd)]
    merged = _merge(hn, branches, w_gate.astype(BF16), w_branch.astype(BF16))
    return _matmul(merged, w_o.astype(BF16), tn=w_o.shape[1], tm=512, residual=h, norm_g=next_norm_g,
                   name="out_proj")


def kernel(x, meta_tokens, mix_norm_g, w_in, gdn_conv_w, gdn_A_log, gdn_dt_bias, gdn_norm_g, mla_qa_g, mla_wq_b, mla_kva_g, mla_wkv_b, mla_qn_g, mla_kn_g, fox_qn_g, fox_kn_g, fox_b_f, ssd_conv_w, ssd_conv_b, ssd_dt_bias, ssd_A_log, ssd_D, ssd_norm_g, w_gate, w_branch, w_o, ffn_norm_g, dense_w_gate, dense_w_up, dense_w_down, router_w, moe_w_gate, moe_w_up, moe_w_down):
    B, S, D = x.shape
    L = N_META + S
    T = -(-L // ROW_TILE) * ROW_TILE
    assert (B * T) % MM_TM == 0
    depth = w_in.shape[0]
    meta = jnp.broadcast_to(meta_tokens[None].astype(x.dtype), (B, N_META, D))
    h = jnp.concatenate([meta, x, jnp.zeros((B, T - L, D), x.dtype)], axis=1).reshape(B * T, D)
    rope = _rope_consts(T)
    hn = _rmsnorm(h, mix_norm_g[0])
    for layer in range(depth):
        dense = layer % 2 == 0
        next_mix_g = mix_norm_g[layer + 1] if layer + 1 < depth else None
        mixed = _mixer(h, hn, B, T, rope, _reorder_w_in(w_in, layer),
                       gdn_conv_w[layer], gdn_A_log[layer], gdn_dt_bias[layer], gdn_norm_g[layer],
                       mla_qa_g[layer], mla_wq_b[layer], mla_kva_g[layer], mla_wkv_b[layer],
                       mla_qn_g[layer], mla_kn_g[layer],
                       fox_qn_g[layer], fox_kn_g[layer], fox_b_f[layer],
                       ssd_conv_w[layer], ssd_conv_b[layer], ssd_dt_bias[layer], ssd_A_log[layer],
                       ssd_D[layer], ssd_norm_g[layer],
                       w_gate[layer], w_branch[layer], w_o[layer],
                       next_norm_g=ffn_norm_g[layer] if dense else None)
        i = layer // 2
        if dense:
            h, hn = mixed
            act = _swiglu_up(hn, dense_w_gate[i].astype(BF16), dense_w_up[i].astype(BF16), tn=512)
            out = _matmul(act, dense_w_down[i].astype(BF16), tn=D, tm=FFN_DOWN_TM, residual=h,
                          norm_g=next_mix_g, w_buffers=1, name="ffn_down")
            h, hn = out if next_mix_g is not None else (out, None)
        else:
            h = mixed
            hn, rec, cnt_before = _rmsnorm_router(h, ffn_norm_g[layer], router_w[i], B, T)
            h = _moe(h, hn, rec, cnt_before, moe_w_gate[i].astype(BF16), moe_w_up[i].astype(BF16),
                     moe_w_down[i].astype(BF16), B, T)
            hn = _rmsnorm(h, next_mix_g) if next_mix_g is not None else None
    return h.reshape(B, T, D)[:, N_META:L].astype(x.dtype)
```

```python
import functools
import math

import numpy as np
import jax
import jax.numpy as jnp
from jax import lax
from jax.experimental import pallas as pl
from jax.experimental.pallas import tpu as pltpu

F32 = jnp.float32
BF16 = jnp.bfloat16
HI = lax.Precision.HIGHEST
NT_DIMS = (((1,), (1,)), ((), ()))

D_MODEL = 2048
N_META = 16
EPS = 1e-6
NEG = -1e30
CONV_K = 4

GDN_HEADS, GDN_DK, GDN_DV = 4, 128, 128
MLA_HEADS, MLA_Q_RANK, MLA_KV_RANK, MLA_NOPE, MLA_ROPE, MLA_V = 4, 512, 256, 128, 64, 128
MLA_DQK = MLA_NOPE + MLA_ROPE
ROPE_BASE = 10000.0
FOX_HEADS, FOX_DH = 4, 128
SSD_HEADS, SSD_HEADDIM, SSD_GROUPS, SSD_STATE = 8, 64, 2, 128
SSD_HG = SSD_HEADS // SSD_GROUPS
SSD_INNER = SSD_HEADS * SSD_HEADDIM
N_BRANCH, BRANCH_W = 4, 512
N_EXPERTS, TOP_K = 8, 2

LANES = 128
ROW_TILE = 256
FLASH_TILES = (768, 256)
FLASH_HEADS_PER_STEP = 2
LOG2E = 1.4426950408889634
MM_TM = 768
FFN_DOWN_TM = 384
MOE_TM = 256
MOE_DISPATCH_ROWS = 128
R_E1, R_E2, R_P1, R_P2, R_R1, R_R2 = range(6)
VMEM_LIMIT = 56 * 1024 * 1024

PROJ_W = 6144
COL_GDN_QKV, COL_GDN_Z = 0, 1536
COL_SSD_XBC, COL_SSD_Z = 2048, 3072
COL_MLA_QA, COL_MLA_KVA = 3584, 4096
COL_SMALL = 4352
COL_FOX_QKV = 4608
L_BETA, L_GA, L_FF, L_DT, L_KPE = 0, 4, 8, 12, 20


def _cparams(sem, vmem=VMEM_LIMIT):
    return pltpu.CompilerParams(dimension_semantics=sem, vmem_limit_bytes=vmem)


def _softplus(x):
    return jnp.maximum(x, 0.0) + jnp.log1p(jnp.exp(-jnp.abs(x)))


def _silu(x):
    return x * jax.nn.sigmoid(x)


def _dot(a, b, precision=None):
    return jnp.dot(a, b, preferred_element_type=F32, precision=precision)


def _dot_nt(a, b):
    return lax.dot_general(a, b, NT_DIMS, preferred_element_type=F32)


def _rmsnorm_kernel(h_ref, g_ref, o_ref):
    x = h_ref[...]
    y = x * lax.rsqrt(jnp.mean(x * x, axis=-1, keepdims=True) + EPS) * g_ref[...]
    o_ref[...] = y.astype(o_ref.dtype)


def _rmsnorm(h, g):
    M, D = h.shape
    return pl.pallas_call(
        _rmsnorm_kernel,
        out_shape=jax.ShapeDtypeStruct((M, D), BF16),
        grid=(M // MM_TM,),
        in_specs=[pl.BlockSpec((MM_TM, D), lambda i: (i, 0)),
                  pl.BlockSpec((1, D), lambda i: (0, 0))],
        out_specs=pl.BlockSpec((MM_TM, D), lambda i: (i, 0)),
        compiler_params=_cparams(("parallel",)),
        name="rmsnorm",
    )(h, g.reshape(1, D).astype(F32))


def _rmsnorm_router_kernel(h_ref, g_ref, rw_ref, o_ref, rec_ref, cnt_ref, run_ref):
    @pl.when(pl.program_id(1) == 0)
    def _():
        run_ref[...] = jnp.zeros_like(run_ref)

    x = h_ref[...]
    n = x.shape[0]
    y = x * lax.rsqrt(jnp.mean(x * x, axis=-1, keepdims=True) + EPS) * g_ref[...]
    y16 = y.astype(BF16)
    o_ref[...] = y16
    y_lo = (y - y16.astype(F32)).astype(BF16)
    both = _dot(y16, rw_ref[...])
    logits = both[:, :LANES] + both[:, LANES:] + _dot(y_lo, rw_ref[:, :LANES])
    lane = lax.broadcasted_iota(jnp.int32, logits.shape, 1)
    logits = jnp.where(lane < N_EXPERTS, logits, NEG)
    m1 = jnp.max(logits, axis=-1, keepdims=True)
    i1 = jnp.min(jnp.where(logits == m1, lane, LANES), axis=-1, keepdims=True)
    rest = jnp.where(lane == i1, NEG, logits)
    m2 = jnp.max(rest, axis=-1, keepdims=True)
    i2 = jnp.min(jnp.where(rest == m2, lane, LANES), axis=-1, keepdims=True)
    e2 = jnp.exp(m2 - m1)
    p1 = 1.0 / (1.0 + e2)
    p2 = e2 * p1
    sel = jnp.where(lane == i1, 1.0, 0.0) + jnp.where(lane == i2, 1.0, 0.0)
    row = lax.broadcasted_iota(jnp.int32, (n, n), 0)
    col = lax.broadcasted_iota(jnp.int32, (n, n), 1)
    earlier = jnp.where(col < row, 1.0, 0.0).astype(BF16)
    run = run_ref[0:1, :]
    before = _dot(earlier, sel.astype(BF16)) + run
    r1 = jnp.sum(jnp.where(lane == i1, before, 0.0), axis=-1, keepdims=True)
    r2 = jnp.sum(jnp.where(lane == i2, before, 0.0), axis=-1, keepdims=True)
    cnt_ref[...] = jnp.broadcast_to(run, cnt_ref.shape)
    run_ref[...] = jnp.broadcast_to(before[n - 1:n] + sel[n - 1:n], run_ref.shape)
    rec = jnp.zeros_like(logits)
    for ln, val in ((R_E1, i1.astype(F32)), (R_E2, i2.astype(F32)), (R_P1, p1), (R_P2, p2), (R_R1, r1), (R_R2, r2)):
        rec = jnp.where(lane == ln, val, rec)
    rec_ref[...] = rec


def _rmsnorm_router(h, g, router_w, B, T):
    M, D = h.shape
    n = ROW_TILE
    NT = T // n
    rw = jnp.zeros((D, LANES), F32).at[:, :N_EXPERTS].set(router_w.astype(F32))
    rw_hi = rw.astype(BF16)
    rw = jnp.concatenate([rw_hi, (rw - rw_hi.astype(F32)).astype(BF16)], axis=1)
    return pl.pallas_call(
        _rmsnorm_router_kernel,
        out_shape=(jax.ShapeDtypeStruct((M, D), BF16), jax.ShapeDtypeStruct((M, LANES), F32),
                   jax.ShapeDtypeStruct((B * NT * 8, LANES), F32)),
        grid=(B, NT),
        in_specs=[pl.BlockSpec((n, D), lambda b, t: (b * NT + t, 0)),
                  pl.BlockSpec((1, D), lambda b, t: (0, 0)),
                  pl.BlockSpec((D, 2 * LANES), lambda b, t: (0, 0))],
        out_specs=(pl.BlockSpec((n, D), lambda b, t: (b * NT + t, 0)),
                   pl.BlockSpec((n, LANES), lambda b, t: (b * NT + t, 0)),
                   pl.BlockSpec((8, LANES), lambda b, t: (b * NT + t, 0))),
        scratch_shapes=[pltpu.VMEM((8, LANES), F32)],
        compiler_params=_cparams(("arbitrary", "arbitrary")),
        name="rmsnorm_router",
    )(h, g.reshape(1, D).astype(F32), rw)


def _mm_kernel(*refs, has_res, has_norm):
    a_ref, w_ref = refs[:2]
    y = _dot(a_ref[...], w_ref[...])
    if has_res:
        y = y + refs[2][...]
    if has_norm:
        g_ref, o_ref, n_ref = refs[-3:]
        n_ref[...] = (y * lax.rsqrt(jnp.mean(y * y, axis=-1, keepdims=True) + EPS) * g_ref[...]).astype(n_ref.dtype)
    else:
        o_ref = refs[-1]
    o_ref[...] = y.astype(o_ref.dtype)


def _matmul(a, w, *, tn, tm=MM_TM, residual=None, norm_g=None, out_dtype=F32, w_buffers=2, name="matmul"):
    M, K = a.shape
    N = w.shape[1]
    in_specs = [pl.BlockSpec((tm, K), lambda j, i: (i, 0)),
                pl.BlockSpec((K, tn), lambda j, i: (0, j), pipeline_mode=pl.Buffered(w_buffers))]
    args = [a, w]
    out_shape = jax.ShapeDtypeStruct((M, N), out_dtype)
    out_specs = pl.BlockSpec((tm, tn), lambda j, i: (i, j))
    if residual is not None:
        in_specs.append(pl.BlockSpec((tm, tn), lambda j, i: (i, j)))
        args.append(residual)
    if norm_g is not None:
        assert tn == N
        in_specs.append(pl.BlockSpec((1, N), lambda j, i: (0, 0)))
        args.append(norm_g.reshape(1, N).astype(F32))
        out_shape = (out_shape, jax.ShapeDtypeStruct((M, N), BF16))
        out_specs = (out_specs, pl.BlockSpec((tm, tn), lambda j, i: (i, j)))
    return pl.pallas_call(
        functools.partial(_mm_kernel, has_res=residual is not None, has_norm=norm_g is not None),
        out_shape=out_shape,
        grid=(N // tn, M // tm),
        in_specs=in_specs,
        out_specs=out_specs,
        compiler_params=_cparams(("parallel", "parallel")),
        name=name,
    )(*args)


def _swiglu_kernel(a_ref, wg_ref, wu_ref, o_ref):
    a = a_ref[...]
    g = _dot(a, wg_ref[...])
    u = _dot(a, wu_ref[...])
    o_ref[...] = (_silu(g) * u).astype(o_ref.dtype)


def _swiglu_up(a, wg, wu, *, tn, tm=MM_TM):
    M, K = a.shape
    F = wg.shape[1]
    return pl.pallas_call(
        _swiglu_kernel,
        out_shape=jax.ShapeDtypeStruct((M, F), BF16),
        grid=(F // tn, M // tm),
        in_specs=[pl.BlockSpec((tm, K), lambda j, i: (i, 0)),
                  pl.BlockSpec((K, tn), lambda j, i: (0, j)),
                  pl.BlockSpec((K, tn), lambda j, i: (0, j))],
        out_specs=pl.BlockSpec((tm, tn), lambda j, i: (i, j)),
        compiler_params=_cparams(("parallel", "parallel")),
        name="swiglu_up",
    )(a, wg, wu)


def _dispatch_kernel(lo_ref, nb_ref, d1_ref, d2_ref, rec_ref, hn_ref, o_ref, p_ref, acc_ref, pacc_ref, *, wb):
    i = pl.program_id(1)
    idx = pl.program_id(0) * pl.num_programs(1) + i
    tm = o_ref.shape[0]
    rows = i * tm + lax.broadcasted_iota(jnp.int32, (tm, wb), 0)
    lane = lax.broadcasted_iota(jnp.int32, (wb, LANES), 1)
    acc_ref[...] = jnp.zeros_like(acc_ref)
    pacc_ref[...] = jnp.zeros_like(pacc_ref)

    def hi_lo(p):
        hi = p.astype(BF16).astype(F32)
        return jnp.where(lane == 0, hi, jnp.where(lane == 1, p - hi, 0.0)).astype(BF16)

    def body(k, carry):
        start = pl.multiple_of((lo_ref[idx] + k) * wb, wb)
        hit1 = jnp.where(d1_ref[0, :, pl.ds(start, wb)] == rows, 1.0, 0.0)
        hit2 = jnp.where(d2_ref[0, :, pl.ds(start, wb)] == rows, 1.0, 0.0)
        acc_ref[...] += _dot((hit1 + hit2).astype(BF16), hn_ref[0, pl.ds(start, wb), :]).astype(BF16)
        rec = rec_ref[0, pl.ds(start, wb), :]
        pacc_ref[...] += (_dot(hit1.astype(BF16), hi_lo(rec[:, R_P1:R_P1 + 1]))
                          + _dot(hit2.astype(BF16), hi_lo(rec[:, R_P2:R_P2 + 1])))
        return carry

    lax.fori_loop(0, nb_ref[idx], body, 0)
    o_ref[...] = acc_ref[...].astype(o_ref.dtype)
    p_ref[...] = pacc_ref[...]


def _dispatch(hn3, rec3, d1, d2, lo_blk, n_blk, nti, *, tm, wb=ROW_TILE):
    B, T, D = hn3.shape
    dspec = pl.BlockSpec((1, 1, T), lambda b, i, lo, nb: (b, 0, 0))
    row_spec = lambda w: pl.BlockSpec((tm, w), lambda b, i, lo, nb: (b * nti + i, 0))
    grid_spec = pltpu.PrefetchScalarGridSpec(
        num_scalar_prefetch=2,
        grid=(B, nti),
        in_specs=[dspec, dspec,
                  pl.BlockSpec((1, T, LANES), lambda b, i, lo, nb: (b, 0, 0)),
                  pl.BlockSpec((1, T, D), lambda b, i, lo, nb: (b, 0, 0), pipeline_mode=pl.Buffered(1))],
        out_specs=(row_spec(D), row_spec(LANES)),
        scratch_shapes=[pltpu.VMEM((tm, D), BF16), pltpu.VMEM((tm, LANES), F32)])
    return pl.pallas_call(
        functools.partial(_dispatch_kernel, wb=wb),
        out_shape=(jax.ShapeDtypeStruct((B * nti * tm, D), BF16),
                   jax.ShapeDtypeStruct((B * nti * tm, LANES), F32)),
        grid_spec=grid_spec,
        compiler_params=_cparams(("arbitrary", "arbitrary")),
        name="moe_dispatch",
    )(lo_blk, n_blk, d1.reshape(B, 1, T), d2.reshape(B, 1, T), rec3, hn3)


def _expert_up_kernel(te_ref, nb_ref, x_ref, wg_ref, wu_ref, p_ref, o_ref):
    i = pl.program_id(0)

    @pl.when(nb_ref[i] > 0)
    def _():
        x = x_ref[...]
        p = p_ref[:, 0:1] + p_ref[:, 1:2]
        o_ref[...] = (_silu(_dot(x, wg_ref[0])) * _dot(x, wu_ref[0]) * p).astype(o_ref.dtype)

    @pl.when(nb_ref[i] == 0)
    def _():
        o_ref[...] = jnp.zeros_like(o_ref)


def _expert_down_kernel(te_ref, nb_ref, a_ref, wd_ref, o_ref):
    i = pl.program_id(0)

    @pl.when(nb_ref[i] > 0)
    def _():
        o_ref[...] = _dot(a_ref[...], wd_ref[0]).astype(o_ref.dtype)

    @pl.when(nb_ref[i] == 0)
    def _():
        o_ref[...] = jnp.zeros_like(o_ref)


def _expert_ffn(x, p_rows, tile_expert, n_blk, wg, wu, wd, *, tm):
    R, D = x.shape
    E, _, F = wg.shape
    up_spec = pltpu.PrefetchScalarGridSpec(
        num_scalar_prefetch=2,
        grid=(R // tm,),
        in_specs=[pl.BlockSpec((tm, D), lambda i, te, nb: (i, 0)),
                  pl.BlockSpec((1, D, F), lambda i, te, nb: (te[i], 0, 0)),
                  pl.BlockSpec((1, D, F), lambda i, te, nb: (te[i], 0, 0)),
                  pl.BlockSpec((tm, LANES), lambda i, te, nb: (i, 0))],
        out_specs=pl.BlockSpec((tm, F), lambda i, te, nb: (i, 0)))
    act = pl.pallas_call(
        _expert_up_kernel,
        out_shape=jax.ShapeDtypeStruct((R, F), BF16),
        grid_spec=up_spec,
        compiler_params=_cparams(("arbitrary",)),
        name="moe_up",
    )(tile_expert, n_blk, x, wg, wu, p_rows)
    down_spec = pltpu.PrefetchScalarGridSpec(
        num_scalar_prefetch=2,
        grid=(R // tm,),
        in_specs=[pl.BlockSpec((tm, F), lambda i, te, nb: (i, 0)),
                  pl.BlockSpec((1, F, D), lambda i, te, nb: (te[i], 0, 0))],
        out_specs=pl.BlockSpec((tm, D), lambda i, te, nb: (i, 0)))
    return pl.pallas_call(
        _expert_down_kernel,
        out_shape=jax.ShapeDtypeStruct((R, D), BF16),
        grid_spec=down_spec,
        compiler_params=_cparams(("arbitrary",)),
        name="moe_down",
    )(tile_expert, n_blk, act, wd)


def _combine_kernel(yb_ref, yv_ref, h_ref, d1_ref, d2_ref, *refs, tm):
    y_refs, o_ref = refs[:-1], refs[-1]
    n = o_ref.shape[0]
    slots = len(y_refs) // N_EXPERTS
    base = (pl.program_id(0) * pl.num_programs(1) + pl.program_id(1)) * len(y_refs)
    d1, d2 = d1_ref[...], d2_ref[...]
    lane = lax.broadcasted_iota(jnp.int32, (n, tm), 1)

    def picked(k, weight):
        rows = yb_ref[base + k] * tm + lane
        hit = jnp.where(d1 == rows, weight, 0.0) + jnp.where(d2 == rows, weight, 0.0)
        return _dot(hit.astype(BF16), y_refs[k][...])

    acc = h_ref[...]
    for k in range(0, len(y_refs), slots):
        acc = acc + picked(k, jnp.where(yv_ref[base + k] > 0, 1.0, 0.0))
    o_ref[...] = acc
    for k in range(len(y_refs)):
        if k % slots:
            @pl.when(yv_ref[base + k] > 0)
            def _():
                o_ref[...] += picked(k, 1.0)


def _combine(h, y, d1, d2, y_blk, y_valid, B, T, *, tm):
    M, D = h.shape
    n = ROW_TILE
    NT = T // n
    slots = y_blk.shape[0] // (B * NT)
    y_specs = [pl.BlockSpec((tm, D), lambda b, t, yb, yv, k=k: (yb[(b * NT + t) * slots + k], 0))
               for k in range(slots)]
    tok = lambda w: pl.BlockSpec((n, w), lambda b, t, yb, yv: (b * NT + t, 0))
    grid_spec = pltpu.PrefetchScalarGridSpec(
        num_scalar_prefetch=2,
        grid=(B, NT),
        in_specs=[tok(D), tok(1), tok(1)] + y_specs,
        out_specs=tok(D))
    return pl.pallas_call(
        functools.partial(_combine_kernel, tm=tm),
        out_shape=jax.ShapeDtypeStruct((M, D), F32),
        grid_spec=grid_spec,
        compiler_params=_cparams(("arbitrary", "arbitrary")),
        name="moe_combine",
    )(y_blk, y_valid, h, d1, d2, *([y] * slots))


def _moe(h, hn, rec, cnt_before, wg, wu, wd, B, T):
    M, D = h.shape
    E, tm, n = N_EXPERTS, MOE_TM, ROW_TILE
    NT = T // n
    rb = TOP_K * T + E * tm
    i32 = jnp.int32
    col = lambda ln: rec[:, ln].reshape(B, T)
    e1, e2 = col(R_E1).astype(i32), col(R_E2).astype(i32)
    r1, r2 = col(R_R1).astype(i32), col(R_R2).astype(i32)
    cntb = cnt_before.reshape(B, NT, 8, LANES)[:, :, 0, :E].astype(i32)
    oh1, oh2 = jax.nn.one_hot(e1, E, dtype=i32), jax.nn.one_hot(e2, E, dtype=i32)
    counts = (oh1 + oh2).sum(axis=1)
    padded = (counts + tm - 1) // tm * tm
    ends = jnp.cumsum(padded, axis=1)
    off = ends - padded
    d1 = (oh1 * off[:, None, :]).sum(-1) + r1
    d2 = (oh2 * off[:, None, :]).sum(-1) + r2
    cnt_end = jnp.concatenate([cntb[:, 1:], counts[:, None, :]], axis=1)
    def tiles(rows):
        start = jnp.arange(rb // rows, dtype=i32)[None, :] * rows
        expert = jnp.minimum((start[:, :, None] >= ends[:, None, :]).sum(-1), E - 1).astype(i32)
        e_oh = jax.nn.one_hot(expert, E, dtype=i32)
        pick = lambda per_expert: (e_oh * per_expert[:, None, :]).sum(-1)
        rank_lo = start - pick(off)
        rank_hi = jnp.minimum(rank_lo + rows, pick(counts)) - 1
        through = (e_oh[:, :, None, :] * cnt_end[:, None, :, :]).sum(-1)
        lo_blk = (through <= rank_lo[..., None]).sum(-1)
        hi_blk = (through <= rank_hi[..., None]).sum(-1)
        nonempty = (start < pick(ends)) & (rank_hi >= rank_lo)
        return (expert.reshape(-1), jnp.where(nonempty, lo_blk, 0).astype(i32).reshape(-1),
                jnp.where(nonempty, hi_blk - lo_blk + 1, 0).astype(i32).reshape(-1))

    pr = MOE_DISPATCH_ROWS
    _, lo_blk, n_blk = tiles(pr)
    x, p_rows = _dispatch(hn.reshape(B, T, D), rec.reshape(B, T, LANES), d1, d2, lo_blk, n_blk, rb // pr, tm=pr)
    tile_expert, _, tile_used = tiles(tm)
    y = _expert_ffn(x, p_rows, tile_expert, tile_used, wg, wu, wd, tm=tm)
    first = off[:, None, :] + cntb
    last = off[:, None, :] + cnt_end - 1
    used = cnt_end > cntb
    span = jnp.arange(n // tm + 1, dtype=i32)
    blk = first[..., None] // tm + span
    valid = used[..., None] & (blk <= last[..., None] // tm)
    blk = jnp.minimum(blk, rb // tm - 1) + (jnp.arange(B, dtype=i32) * (rb // tm))[:, None, None, None]
    row0 = (jnp.arange(B, dtype=i32) * rb)[:, None]
    return _combine(h, y, (d1 + row0).reshape(M, 1), (d2 + row0).reshape(M, 1),
                    blk.reshape(-1).astype(i32), valid.reshape(-1).astype(i32), B, T, tm=tm)


def _merge_kernel(hn_ref, b0_ref, b1_ref, b2_ref, b3_ref, wg_ref, wb_ref, o_ref):
    hn = hn_ref[...]
    acc = None
    for b, br_ref in enumerate((b0_ref, b1_ref, b2_ref, b3_ref)):
        gate = jax.nn.sigmoid(_dot(hn, wg_ref[b]))
        term = gate * _dot(br_ref[...], wb_ref[b])
        acc = term if acc is None else acc + term
    o_ref[...] = acc.astype(o_ref.dtype)


def _merge(hn, branches, wg, wb, *, tn=512, tm=512):
    M, D = hn.shape
    N = wg.shape[2]
    bspec = pl.BlockSpec((tm, BRANCH_W), lambda j, i: (i, 0))
    return pl.pallas_call(
        _merge_kernel,
        out_shape=jax.ShapeDtypeStruct((M, N), BF16),
        grid=(N // tn, M // tm),
        in_specs=[pl.BlockSpec((tm, D), lambda j, i: (i, 0)), bspec, bspec, bspec, bspec,
                  pl.BlockSpec((N_BRANCH, D, tn), lambda j, i: (0, 0, j)),
                  pl.BlockSpec((N_BRANCH, BRANCH_W, tn), lambda j, i: (0, 0, j))],
        out_specs=pl.BlockSpec((tm, tn), lambda j, i: (i, j)),
        compiler_params=_cparams(("parallel", "parallel")),
        name="gate_merge",
    )(hn, *branches, wg, wb)


def _causal_conv(x, carry_ref, cw):
    n = x.shape[0]
    xext = jnp.concatenate([carry_ref[...], x], axis=0)
    y = cw[0:1] * xext[5:5 + n]
    for i in range(1, CONV_K):
        y = y + cw[i:i + 1] * xext[5 + i:5 + i + n]
    carry_ref[...] = x[n - 8:n]
    return y


def _tile_masks(n, chunk):
    row = lax.broadcasted_iota(jnp.int32, (n, n), 0)
    col = lax.broadcasted_iota(jnp.int32, (n, n), 1)
    if chunk == n:
        return col <= row, col < row
    in_chunk = col >= (row // chunk) * chunk
    return in_chunk & (col <= row), in_chunk & (col < row)


def _gdn_kernel(qkv_ref, z_ref, sm_ref, cw_ref, alog_ref, dtb_ref, ng_ref, o_ref, s_ref, carry_ref):
    @pl.when(pl.program_id(1) == 0)
    def _():
        s_ref[...] = jnp.zeros_like(s_ref)
        carry_ref[...] = jnp.zeros_like(carry_ref)

    n = ROW_TILE
    y = _silu(_causal_conv(qkv_ref[0], carry_ref, cw_ref[...]))
    sm = sm_ref[0]
    z = z_ref[0]
    beta_all = jax.nn.sigmoid(sm)
    g_all = -jnp.exp(alog_ref[...]) * _softplus(sm + dtb_ref[...])
    row = lax.broadcasted_iota(jnp.int32, (n, n), 0)
    col = lax.broadcasted_iota(jnp.int32, (n, n), 1)
    causal, strict = col <= row, col < row
    diff_bits = row ^ col
    levels = int(math.log2(n))
    level_masks = [(diff_bits >= (1 << l)) & (diff_bits < (2 << l)) for l in range(levels)]
    eye = jnp.where(row == col, 1.0, 0.0)
    gcs_all = _dot(causal.astype(F32), g_all, HI)
    gcs_t = gcs_all.T
    heads = []
    for h in range(GDN_HEADS):
        q = y[:, h * GDN_DK:(h + 1) * GDN_DK]
        k = y[:, GDN_HEADS * GDN_DK + h * GDN_DK:GDN_HEADS * GDN_DK + (h + 1) * GDN_DK]
        v = y[:, 2 * GDN_HEADS * GDN_DK + h * GDN_DV:2 * GDN_HEADS * GDN_DK + (h + 1) * GDN_DV]
        q = q * lax.rsqrt(jnp.sum(q * q, axis=-1, keepdims=True) + EPS) * GDN_DK ** -0.5
        k = k * lax.rsqrt(jnp.sum(k * k, axis=-1, keepdims=True) + EPS)
        beta = beta_all[:, L_BETA + h:L_BETA + h + 1]
        gc = gcs_all[:, L_GA + h:L_GA + h + 1]
        gr = gcs_t[L_GA + h:L_GA + h + 1, :]
        decay = jnp.exp(jnp.where(causal, gc - gr, NEG))
        kb = k * beta
        k16 = k.astype(BF16)
        a = jnp.where(strict, _dot_nt(kb.astype(BF16), k16) * decay, 0.0)
        heads.append(dict(q=q, k=k, v=v, beta=beta, gc=gc, decay=decay, kb=kb, k16=k16, a=a,
                          t=eye - jnp.where(level_masks[0], a, 0.0)))
    for l in range(1, levels):
        for hd in heads:
            t16 = hd["t"].astype(BF16)
            a_l = jnp.where(level_masks[l], hd["a"], 0.0).astype(BF16)
            hd["t"] = hd["t"] - _dot(_dot(t16, a_l).astype(BF16), t16)
    outs = []
    for h, hd in enumerate(heads):
        q, k, gc = hd["q"], hd["k"], hd["gc"]
        egc = jnp.exp(gc)
        rhs = jnp.concatenate([hd["v"] * hd["beta"], hd["kb"] * egc], axis=1)
        uw = _dot(hd["t"].astype(BF16), rhs.astype(BF16))
        u, w = uw[:, :GDN_DV], uw[:, GDN_DV:]
        att = _dot_nt(q.astype(BF16), hd["k16"]) * hd["decay"]
        g_last = gc[n - 1:n, :]
        kd = k * jnp.exp(g_last - gc)
        s = s_ref[h]
        ws = _dot(jnp.concatenate([w, q * egc], axis=0).astype(BF16), s.astype(BF16))
        v_new = (u - ws[:n]).astype(BF16)
        o = ws[n:] + _dot(att.astype(BF16), v_new)
        s_ref[h] = s * jnp.exp(g_last) + _dot(kd.T.astype(BF16), v_new)
        o = o * lax.rsqrt(jnp.mean(o * o, axis=-1, keepdims=True) + EPS) * ng_ref[...]
        outs.append(o * _silu(z[:, h * GDN_DV:(h + 1) * GDN_DV]))
    o_ref[0] = jnp.concatenate(outs, axis=1).astype(o_ref.dtype)


def _lane_vec(vals, lane0):
    v = jnp.zeros((1, LANES), F32)
    return v.at[0, lane0:lane0 + vals.shape[0]].set(vals.astype(F32))


def _gdn(proj, conv_w, a_log, dt_bias, norm_g):
    B, T, _ = proj.shape
    n = ROW_TILE
    W = 2 * GDN_HEADS * GDN_DK + GDN_HEADS * GDN_DV
    ZW = GDN_HEADS * GDN_DV
    return pl.pallas_call(
        _gdn_kernel,
        out_shape=jax.ShapeDtypeStruct((B, T, ZW), BF16),
        grid=(B, T // n),
        in_specs=[pl.BlockSpec((1, n, W), lambda b, t: (b, t, COL_GDN_QKV // W)),
                  pl.BlockSpec((1, n, ZW), lambda b, t: (b, t, COL_GDN_Z // ZW)),
                  pl.BlockSpec((1, n, LANES), lambda b, t: (b, t, COL_SMALL // LANES)),
                  pl.BlockSpec((CONV_K, W), lambda b, t: (0, 0)),
                  pl.BlockSpec((1, LANES), lambda b, t: (0, 0)),
                  pl.BlockSpec((1, LANES), lambda b, t: (0, 0)),
                  pl.BlockSpec((1, GDN_DV), lambda b, t: (0, 0))],
        out_specs=pl.BlockSpec((1, n, ZW), lambda b, t: (b, t, 0)),
        scratch_shapes=[pltpu.VMEM((GDN_HEADS, GDN_DK, GDN_DV), F32),
                        pltpu.VMEM((8, W), F32)],
        compiler_params=_cparams(("arbitrary", "arbitrary")),
        name="gdn",
    )(proj, proj, proj, conv_w.astype(F32), _lane_vec(a_log, L_GA), _lane_vec(dt_bias, L_GA),
      norm_g.reshape(1, GDN_DV).astype(F32))


def _ssd_kernel(xbc_ref, z_ref, sm_ref, cw_ref, cb_ref, alog_ref, dtb_ref, dvec_ref, ng_ref,
                o_ref, hs_ref, carry_ref):
    @pl.when(pl.program_id(1) == 0)
    def _():
        hs_ref[...] = jnp.zeros_like(hs_ref)
        carry_ref[...] = jnp.zeros_like(carry_ref)

    n, P, N = ROW_TILE, SSD_HEADDIM, SSD_STATE
    y = _silu(_causal_conv(xbc_ref[0], carry_ref, cw_ref[...]) + cb_ref[...])
    xs = y[:, :SSD_INNER]
    bm = y[:, SSD_INNER:SSD_INNER + SSD_GROUPS * N]
    cm = y[:, SSD_INNER + SSD_GROUPS * N:]
    sm = sm_ref[0]
    dt_all = _softplus(sm + dtb_ref[...])
    a_all = dt_all * (-jnp.exp(alog_ref[...]))
    causal, _ = _tile_masks(n, n)
    acs_all = _dot(causal.astype(F32), a_all, HI)
    acs_t = acs_all.T
    ys = []
    for g in range(SSD_GROUPS):
        bg = bm[:, g * N:(g + 1) * N]
        cg16 = cm[:, g * N:(g + 1) * N].astype(BF16)
        cb = _dot_nt(cg16, bg.astype(BF16))
        bgt16 = bg.T.astype(BF16)
        for j in range(SSD_HG):
            hh = g * SSD_HG + j
            ac = acs_all[:, L_DT + hh:L_DT + hh + 1]
            ar = acs_t[L_DT + hh:L_DT + hh + 1, :]
            lmat = jnp.exp(jnp.where(causal, ac - ar, NEG))
            xdt = xs[:, hh * P:(hh + 1) * P] * dt_all[:, L_DT + hh:L_DT + hh + 1]
            y_diag = _dot((cb * lmat).astype(BF16), xdt.astype(BF16))
            a_last = ac[n - 1:n, :]
            st = _dot(bgt16, (xdt * jnp.exp(a_last - ac)).astype(BF16))
            h_prev = hs_ref[hh]
            y_off = _dot(cg16, h_prev.astype(BF16)) * jnp.exp(ac)
            hs_ref[hh] = h_prev * jnp.exp(a_last) + st
            ys.append(y_diag + y_off)
    yy = jnp.concatenate(ys, axis=1) + xs * dvec_ref[...]
    yy = yy * _silu(z_ref[0])
    gw = SSD_HG * P
    outs = []
    for g in range(SSD_GROUPS):
        seg = yy[:, g * gw:(g + 1) * gw]
        outs.append(seg * lax.rsqrt(jnp.mean(seg * seg, axis=-1, keepdims=True) + EPS)
                    * ng_ref[:, g * gw:(g + 1) * gw])
    o_ref[0] = jnp.concatenate(outs, axis=1).astype(o_ref.dtype)


def _ssd(proj, conv_w, conv_b, dt_bias, a_log, d_skip, norm_g):
    B, T, _ = proj.shape
    n = ROW_TILE
    W = SSD_INNER + 2 * SSD_GROUPS * SSD_STATE
    dvec = jnp.repeat(d_skip.astype(F32), SSD_HEADDIM).reshape(1, SSD_INNER)
    return pl.pallas_call(
        _ssd_kernel,
        out_shape=jax.ShapeDtypeStruct((B, T, SSD_INNER), BF16),
        grid=(B, T // n),
        in_specs=[pl.BlockSpec((1, n, W), lambda b, t: (b, t, COL_SSD_XBC // W)),
                  pl.BlockSpec((1, n, SSD_INNER), lambda b, t: (b, t, COL_SSD_Z // SSD_INNER)),
                  pl.BlockSpec((1, n, LANES), lambda b, t: (b, t, COL_SMALL // LANES)),
                  pl.BlockSpec((CONV_K, W), lambda b, t: (0, 0)),
                  pl.BlockSpec((1, W), lambda b, t: (0, 0)),
                  pl.BlockSpec((1, LANES), lambda b, t: (0, 0)),
                  pl.BlockSpec((1, LANES), lambda b, t: (0, 0)),
                  pl.BlockSpec((1, SSD_INNER), lambda b, t: (0, 0)),
                  pl.BlockSpec((1, SSD_INNER), lambda b, t: (0, 0))],
        out_specs=pl.BlockSpec((1, n, SSD_INNER), lambda b, t: (b, t, 0)),
        scratch_shapes=[pltpu.VMEM((SSD_HEADS, SSD_STATE, SSD_HEADDIM), F32),
                        pltpu.VMEM((8, W), F32)],
        compiler_params=_cparams(("arbitrary", "arbitrary")),
        name="ssd",
    )(proj, proj, proj, conv_w.astype(F32), conv_b.reshape(1, W).astype(F32),
      _lane_vec(a_log, L_DT), _lane_vec(dt_bias, L_DT), dvec,
      norm_g.reshape(1, SSD_INNER).astype(F32))


def _ones_lane0(n):
    lane = lax.broadcasted_iota(jnp.int32, (n, LANES), 1)
    return jnp.where(lane == 0, 1.0, 0.0)


def _fox_prep_kernel(qkv_ref, sm_ref, qg_ref, kg_ref, bf_ref, q_ref, k_ref, v_ref, run_ref):
    @pl.when(pl.program_id(1) == 0)
    def _():
        run_ref[...] = jnp.zeros_like(run_ref)

    n, dh, H = ROW_TILE, FOX_DH, FOX_HEADS
    x = qkv_ref[0]
    log_f = -_softplus(-(sm_ref[0] + bf_ref[...]))
    causal, _ = _tile_masks(n, n)
    cum = _dot(causal.astype(F32), log_f, HI) + run_ref[0:1, :]
    run_ref[...] = jnp.broadcast_to(cum[n - 1:n, :], run_ref.shape)
    lane = lax.broadcasted_iota(jnp.int32, (n, LANES), 1)
    ones = _ones_lane0(n)
    q_bias = jnp.where(lane < 3, 1.0, 0.0)
    qs, ks, vs = [], [], []
    for h in range(H):
        q = x[:, h * dh:(h + 1) * dh]
        k = x[:, H * dh + h * dh:H * dh + (h + 1) * dh]
        qs.append(q * lax.rsqrt(jnp.mean(q * q, axis=-1, keepdims=True) + EPS)
                  * (qg_ref[...] * (dh ** -0.5 * LOG2E)))
        qs.append(q_bias)
        ks.append(k * lax.rsqrt(jnp.mean(k * k, axis=-1, keepdims=True) + EPS) * kg_ref[...])
        c = cum[:, L_FF + h:L_FF + h + 1] * (-LOG2E)
        c_hi = c.astype(BF16).astype(F32)
        c_mid = (c - c_hi).astype(BF16).astype(F32)
        c_lo = c - c_hi - c_mid
        ks.append(jnp.where(lane == 0, c_hi, jnp.where(lane == 1, c_mid, jnp.where(lane == 2, c_lo, 0.0))))
        vs.append(x[:, 2 * H * dh + h * dh:2 * H * dh + (h + 1) * dh])
        vs.append(ones)
    q_ref[0] = jnp.concatenate(qs, axis=1).astype(q_ref.dtype)
    k_ref[0] = jnp.concatenate(ks, axis=1).astype(k_ref.dtype)
    v_ref[0] = jnp.concatenate(vs, axis=1).astype(v_ref.dtype)


def _fox_prep(proj, qn_g, kn_g, b_f):
    B, T, _ = proj.shape
    n = ROW_TILE
    W = 3 * FOX_HEADS * FOX_DH
    HW = FOX_HEADS * 2 * LANES
    ospec = pl.BlockSpec((1, n, HW), lambda b, t: (b, t, 0))
    return pl.pallas_call(
        _fox_prep_kernel,
        out_shape=(jax.ShapeDtypeStruct((B, T, HW), BF16),) * 3,
        grid=(B, T // n),
        in_specs=[pl.BlockSpec((1, n, W), lambda b, t: (b, t, COL_FOX_QKV // W)),
                  pl.BlockSpec((1, n, LANES), lambda b, t: (b, t, COL_SMALL // LANES)),
                  pl.BlockSpec((1, FOX_DH), lambda b, t: (0, 0)),
                  pl.BlockSpec((1, FOX_DH), lambda b, t: (0, 0)),
                  pl.BlockSpec((1, LANES), lambda b, t: (0, 0))],
        out_specs=(ospec, ospec, ospec),
        scratch_shapes=[pltpu.VMEM((8, LANES), F32)],
        compiler_params=_cparams(("arbitrary", "arbitrary")),
        name="fox_prep",
    )(proj, proj, qn_g.reshape(1, FOX_DH).astype(F32), kn_g.reshape(1, FOX_DH).astype(F32),
      _lane_vec(b_f, L_FF))


def _mla_prep_kernel(qa_ref, kva_ref, sm_ref, cos_ref, sin_ref, qag_ref, wq_ref, kvag_ref, wkv_ref,
                     qgn_ref, qgr_ref, kgn_ref, kgr_ref, rot_ref, exp_ref, q_ref, k_ref, v_ref):
    n, H, dn, dr = ROW_TILE, MLA_HEADS, MLA_NOPE, MLA_ROPE
    qa = qa_ref[0]
    qa = qa * lax.rsqrt(jnp.mean(qa * qa, axis=-1, keepdims=True) + EPS) * qag_ref[...]
    qq = _dot(qa.astype(BF16), wq_ref[...])
    kva = kva_ref[0]
    kva = kva * lax.rsqrt(jnp.mean(kva * kva, axis=-1, keepdims=True) + EPS) * kvag_ref[...]
    kv = _dot(kva.astype(BF16), wkv_ref[...])
    sm = sm_ref[0]
    lane = lax.broadcasted_iota(jnp.int32, (n, LANES), 1)
    is_kpe = (lane >= L_KPE) & (lane < L_KPE + dr)
    kpe_ss = jnp.sum(jnp.where(is_kpe, sm * sm, 0.0), axis=-1, keepdims=True)
    kpe4 = _dot(sm, exp_ref[...], HI)
    qr = qq[:, H * dn:]
    grp = lax.broadcasted_iota(jnp.int32, (n, H * dr), 1) // dr
    q_rs, k_rs = [], []
    for h in range(H):
        qn = qq[:, h * dn:(h + 1) * dn]
        ssr = jnp.sum(jnp.where(grp == h, qr * qr, 0.0), axis=-1, keepdims=True)
        q_rs.append(lax.rsqrt((jnp.sum(qn * qn, axis=-1, keepdims=True) + ssr) / MLA_DQK + EPS))
        kn = kv[:, h * dn:(h + 1) * dn]
        k_rs.append(lax.rsqrt((jnp.sum(kn * kn, axis=-1, keepdims=True) + kpe_ss) / MLA_DQK + EPS))

    def per_group(vals):
        out = vals[H - 1]
        for h in range(H - 2, -1, -1):
            out = jnp.where(grp == h, vals[h], out)
        return out

    cos, sin, rot = cos_ref[...], sin_ref[...], rot_ref[...]
    tq = qr * per_group(q_rs) * qgr_ref[...]
    tq = tq * cos + _dot(tq, rot, HI) * sin
    tk = kpe4 * per_group(k_rs) * kgr_ref[...]
    tk = tk * cos + _dot(tk, rot, HI) * sin
    scale = MLA_DQK ** -0.5 * LOG2E
    half_id = lane // dr
    ones = _ones_lane0(n)
    q_parts, k_parts, v_parts = [], [], []
    for h in range(H):
        blk = slice((h // 2) * LANES, (h // 2 + 1) * LANES)
        q_parts.append(qq[:, h * dn:(h + 1) * dn] * q_rs[h] * (qgn_ref[...] * scale))
        q_parts.append(jnp.where(half_id == h % 2, tq[:, blk] * scale, 0.0))
        k_parts.append(kv[:, h * dn:(h + 1) * dn] * k_rs[h] * kgn_ref[...])
        k_parts.append(tk[:, blk])
        v_parts.append(kv[:, H * dn + h * MLA_V:H * dn + (h + 1) * MLA_V])
        v_parts.append(ones)
    q_ref[0] = jnp.concatenate(q_parts, axis=1).astype(q_ref.dtype)
    k_ref[0] = jnp.concatenate(k_parts, axis=1).astype(k_ref.dtype)
    v_ref[0] = jnp.concatenate(v_parts, axis=1).astype(v_ref.dtype)


def _rope_consts(T):
    H, dr = MLA_HEADS, MLA_ROPE
    inv = 1.0 / (ROPE_BASE ** (jnp.arange(0, dr, 2, dtype=F32) / dr))
    ang = jnp.arange(T, dtype=F32)[:, None] * inv[None, :]
    ang = jnp.concatenate([ang, ang], axis=-1)
    cos4 = jnp.tile(jnp.cos(ang), (1, H))
    sin4 = jnp.tile(jnp.sin(ang), (1, H))
    rot = np.zeros((H * dr, H * dr), np.float32)
    for h in range(H):
        for c in range(dr // 2):
            rot[h * dr + c + dr // 2, h * dr + c] = -1.0
            rot[h * dr + c, h * dr + c + dr // 2] = 1.0
    expand = np.zeros((LANES, H * dr), np.float32)
    for h in range(H):
        for c in range(dr):
            expand[L_KPE + c, h * dr + c] = 1.0
    return cos4, sin4, jnp.asarray(rot), jnp.asarray(expand)


def _mla_prep(proj, rope, qa_g, wq_b, kva_g, wkv_b, qn_g, kn_g):
    B, T, _ = proj.shape
    n, H, dn, dr, dv = ROW_TILE, MLA_HEADS, MLA_NOPE, MLA_ROPE, MLA_V
    cos4, sin4, rot, expand = rope
    wq = wq_b.reshape(MLA_Q_RANK, H, MLA_DQK)
    wq = jnp.concatenate([wq[:, :, :dn].reshape(MLA_Q_RANK, H * dn),
                          wq[:, :, dn:].reshape(MLA_Q_RANK, H * dr)], axis=1).astype(BF16)
    wkv = wkv_b.reshape(MLA_KV_RANK, H, dn + dv)
    wkv = jnp.concatenate([wkv[:, :, :dn].reshape(MLA_KV_RANK, H * dn),
                           wkv[:, :, dn:].reshape(MLA_KV_RANK, H * dv)], axis=1).astype(BF16)
    qg, kg = qn_g.astype(F32), kn_g.astype(F32)
    const = lambda shape: pl.BlockSpec(shape, lambda b, t: (0,) * len(shape))
    QW = H * 2 * LANES
    return pl.pallas_call(
        _mla_prep_kernel,
        out_shape=(jax.ShapeDtypeStruct((B, T, QW), BF16), jax.ShapeDtypeStruct((B, T, QW), BF16),
                   jax.ShapeDtypeStruct((B, T, QW), BF16)),
        grid=(B, T // n),
        in_specs=[pl.BlockSpec((1, n, MLA_Q_RANK), lambda b, t: (b, t, COL_MLA_QA // MLA_Q_RANK)),
                  pl.BlockSpec((1, n, MLA_KV_RANK), lambda b, t: (b, t, COL_MLA_KVA // MLA_KV_RANK)),
                  pl.BlockSpec((1, n, LANES), lambda b, t: (b, t, COL_SMALL // LANES)),
                  pl.BlockSpec((n, H * dr), lambda b, t: (t, 0)),
                  pl.BlockSpec((n, H * dr), lambda b, t: (t, 0)),
                  const((1, MLA_Q_RANK)), const(wq.shape), const((1, MLA_KV_RANK)), const(wkv.shape),
                  const((1, dn)), const((1, H * dr)), const((1, dn)), const((1, H * dr)),
                  const(rot.shape), const(expand.shape)],
        out_specs=(pl.BlockSpec((1, n, QW), lambda b, t: (b, t, 0)),
                   pl.BlockSpec((1, n, QW), lambda b, t: (b, t, 0)),
                   pl.BlockSpec((1, n, QW), lambda b, t: (b, t, 0))),
        compiler_params=_cparams(("parallel", "parallel")),
        name="mla_prep",
    )(proj, proj, proj, cos4, sin4,
      qa_g.reshape(1, -1).astype(F32), wq, kva_g.reshape(1, -1).astype(F32), wkv,
      qg[:dn].reshape(1, dn), jnp.tile(qg[dn:], H).reshape(1, H * dr),
      kg[:dn].reshape(1, dn), jnp.tile(kg[dn:], H).reshape(1, H * dr), rot, expand)


def _flash_kernel(q_ref, k_ref, v_ref, o_ref, m_ref, acc_ref, *, tq):
    qi = pl.program_id(2)
    nh = m_ref.shape[0]
    dqk, dv2 = q_ref.shape[-1] // nh, v_ref.shape[-1] // nh
    dv = dv2 // 2
    rb = ROW_TILE
    m_ref[...] = jnp.full(m_ref.shape, NEG, F32)
    acc_ref[...] = jnp.zeros(acc_ref.shape, F32)

    def step(j, masked):
        start = pl.multiple_of(j * tq, tq)
        chains = [(h, r) for h in range(nh) for r in range(0, tq, rb)]
        logits, probs = {}, {}

        def qk(c):
            h, r = chains[c]
            logits[c] = _dot_nt(q_ref[0, r:r + rb, h * dqk:(h + 1) * dqk],
                                k_ref[0, pl.ds(start, tq), h * dqk:(h + 1) * dqk])

        def softmax(c):
            h, r = chains[c]
            s = logits.pop(c)
            if masked:
                row = lax.broadcasted_iota(jnp.int32, s.shape, 0) + r
                col = lax.broadcasted_iota(jnp.int32, s.shape, 1)
                s = jnp.where(col <= row, s, NEG)
            lane_max = s[:, :LANES]
            for cb in range(1, tq // LANES):
                lane_max = jnp.maximum(lane_max, s[:, cb * LANES:(cb + 1) * LANES])
            m_old = m_ref[h, r:r + rb]
            m_new = jnp.maximum(m_old, jnp.max(lane_max, axis=-1, keepdims=True))
            m_ref[h, r:r + rb] = m_new
            probs[c] = (jnp.exp2(s - m_new).astype(BF16), jnp.exp2(m_old - m_new))

        def pv(c):
            h, r = chains[c]
            p, alpha = probs.pop(c)
            acc_ref[h, r:r + rb] = (alpha * acc_ref[h, r:r + rb]
                                    + _dot(p, v_ref[0, pl.ds(start, tq), h * dv2:(h + 1) * dv2]))

        for c in range(len(chains)):
            qk(c)
            softmax(c)
            pv(c)

    def body(jj, carry):
        step(2 * jj, False)
        step(2 * jj + 1, False)
        return carry

    lax.fori_loop(0, qi // 2, body, 0)

    @pl.when(qi % 2 == 1)
    def _():
        step(qi - 1, False)

    step(qi, True)
    outs = []
    for h in range(nh):
        acc = acc_ref[h]
        outs.append(acc[:, :dv] / acc[:, dv:dv + 1])
    o_ref[0] = jnp.concatenate(outs, axis=1).astype(o_ref.dtype)


def _flash(q, k, v):
    B, T, QW = q.shape
    H, nh = MLA_HEADS, FLASH_HEADS_PER_STEP
    dqk, dv2 = QW // H, v.shape[-1] // H
    dv = dv2 // 2
    tq = next(t for t in FLASH_TILES if T % t == 0)
    return pl.pallas_call(
        functools.partial(_flash_kernel, tq=tq),
        out_shape=jax.ShapeDtypeStruct((B, T, H * dv), BF16),
        grid=(B, H // nh, T // tq),
        in_specs=[pl.BlockSpec((1, tq, nh * dqk), lambda b, h, i: (b, i, h)),
                  pl.BlockSpec((1, T, nh * dqk), lambda b, h, i: (b, 0, h)),
                  pl.BlockSpec((1, T, nh * dv2), lambda b, h, i: (b, 0, h))],
        out_specs=pl.BlockSpec((1, tq, nh * dv), lambda b, h, i: (b, i, h)),
        scratch_shapes=[pltpu.VMEM((nh, tq, 1), F32), pltpu.VMEM((nh, tq, dv2), F32)],
        compiler_params=_cparams(("parallel", "parallel", "arbitrary")),
        name="flash",
    )(q, k, v)


def _reorder_kernel(w_ref, o_ref):
    w = w_ref[...]
    n = w.shape[0]
    z = lambda width: jnp.zeros((n, width), w.dtype)
    small = [w[:, 2048:2052], w[:, 2052:2056], w[:, 4424:4428], w[:, 5964:5972], w[:, 2824:2888]]
    n_small = sum(s.shape[1] for s in small)
    cols = [w[:, 0:2048],
            w[:, 4940:5964],
            w[:, 4428:4940],
            w[:, 2056:2824],
            *small, z(LANES - n_small), z(COL_FOX_QKV - COL_SMALL - LANES),
            w[:, 2888:4424]]
    o_ref[...] = jnp.concatenate(cols, axis=1).astype(o_ref.dtype)


def _reorder_w_in(w_in):
    D, W = w_in.shape
    rows = ROW_TILE
    return pl.pallas_call(
        _reorder_kernel,
        out_shape=jax.ShapeDtypeStruct((D, PROJ_W), BF16),
        grid=(D // rows,),
        in_specs=[pl.BlockSpec((rows, W), lambda i: (i, 0))],
        out_specs=pl.BlockSpec((rows, PROJ_W), lambda i: (i, 0)),
        compiler_params=_cparams(("parallel",)),
        name="reorder_w_in",
    )(w_in)


def _mixer(h, hn, B, T, rope, w_in_cols, gdn_conv_w, gdn_A_log, gdn_dt_bias, gdn_norm_g,
           mla_qa_g, mla_wq_b, mla_kva_g, mla_wkv_b, mla_qn_g, mla_kn_g,
           fox_qn_g, fox_kn_g, fox_b_f,
           ssd_conv_w, ssd_conv_b, ssd_dt_bias, ssd_A_log, ssd_D, ssd_norm_g,
           w_gate, w_branch, w_o, next_norm_g=None):
    M = B * T
    proj = _matmul(hn, w_in_cols, tn=1024, name="in_proj").reshape(B, T, PROJ_W)
    o_gdn = _gdn(proj, gdn_conv_w, gdn_A_log, gdn_dt_bias, gdn_norm_g)
    mq, mk, mv = _mla_prep(proj, rope, mla_qa_g, mla_wq_b, mla_kva_g, mla_wkv_b, mla_qn_g, mla_kn_g)
    o_mla = _flash(mq, mk, mv)
    o_fox = _flash(*_fox_prep(proj, fox_qn_g, fox_kn_g, fox_b_f))
    o_ssd = _ssd(proj, ssd_conv_w, ssd_conv_b, ssd_dt_bias, ssd_A_log, ssd_D, ssd_norm_g)
    branches = [o.reshape(M, BRANCH_W) for o in (o_gdn, o_mla, o_fox, o_ssd)]
    merged = _merge(hn, branches, w_gate.astype(BF16), w_branch.astype(BF16))
    return _matmul(merged, w_o.astype(BF16), tn=w_o.shape[1], tm=512, residual=h, norm_g=next_norm_g,
                   name="out_proj")


def kernel(x, meta_tokens, mix_norm_g, w_in, gdn_conv_w, gdn_A_log, gdn_dt_bias, gdn_norm_g, mla_qa_g, mla_wq_b, mla_kva_g, mla_wkv_b, mla_qn_g, mla_kn_g, fox_qn_g, fox_kn_g, fox_b_f, ssd_conv_w, ssd_conv_b, ssd_dt_bias, ssd_A_log, ssd_D, ssd_norm_g, w_gate, w_branch, w_o, ffn_norm_g, dense_w_gate, dense_w_up, dense_w_down, router_w, moe_w_gate, moe_w_up, moe_w_down):
    B, S, D = x.shape
    L = N_META + S
    T = -(-L // ROW_TILE) * ROW_TILE
    assert (B * T) % MM_TM == 0
    depth = w_in.shape[0]
    meta = jnp.broadcast_to(meta_tokens[None].astype(x.dtype), (B, N_META, D))
    h = jnp.concatenate([meta, x, jnp.zeros((B, T - L, D), x.dtype)], axis=1).reshape(B * T, D)
    rope = _rope_consts(T)
    hn = _rmsnorm(h, mix_norm_g[0])
    for layer in range(depth):
        dense = layer % 2 == 0
        next_mix_g = mix_norm_g[layer + 1] if layer + 1 < depth else None
        mixed = _mixer(h, hn, B, T, rope, _reorder_w_in(w_in[layer]),
                       gdn_conv_w[layer], gdn_A_log[layer], gdn_dt_bias[layer], gdn_norm_g[layer],
                       mla_qa_g[layer], mla_wq_b[layer], mla_kva_g[layer], mla_wkv_b[layer],
                       mla_qn_g[layer], mla_kn_g[layer],
                       fox_qn_g[layer], fox_kn_g[layer], fox_b_f[layer],
                       ssd_conv_w[layer], ssd_conv_b[layer], ssd_dt_bias[layer], ssd_A_log[layer],
                       ssd_D[layer], ssd_norm_g[layer],
                       w_gate[layer], w_branch[layer], w_o[layer],
                       next_norm_g=ffn_norm_g[layer] if dense else None)
        i = layer // 2
        if dense:
            h, hn = mixed
            act = _swiglu_up(hn, dense_w_gate[i].astype(BF16), dense_w_up[i].astype(BF16), tn=512)
            out = _matmul(act, dense_w_down[i].astype(BF16), tn=D, tm=FFN_DOWN_TM, residual=h,
                          norm_g=next_mix_g, w_buffers=1, name="ffn_down")
            h, hn = out if next_mix_g is not None else (out, None)
        else:
            h = mixed
            hn, rec, cnt_before = _rmsnorm_router(h, ffn_norm_g[layer], router_w[i], B, T)
            h = _moe(h, hn, rec, cnt_before, moe_w_gate[i].astype(BF16), moe_w_up[i].astype(BF16),
                     moe_w_down[i].astype(BF16), B, T)
            hn = _rmsnorm(h, next_mix_g) if next_mix_g is not None else None
    return h.reshape(B, T, D)[:, N_META:L].astype(x.dtype)
```

```python
import functools
import math

import numpy as np
import jax
import jax.numpy as jnp
from jax import lax
from jax.experimental import pallas as pl
from jax.experimental.pallas import tpu as pltpu

F32 = jnp.float32
BF16 = jnp.bfloat16
HI = lax.Precision.HIGHEST
NT_DIMS = (((1,), (1,)), ((), ()))

D_MODEL = 2048
N_META = 16
EPS = 1e-6
NEG = -1e30
CONV_K = 4

GDN_HEADS, GDN_DK, GDN_DV = 4, 128, 128
MLA_HEADS, MLA_Q_RANK, MLA_KV_RANK, MLA_NOPE, MLA_ROPE, MLA_V = 4, 512, 256, 128, 64, 128
MLA_DQK = MLA_NOPE + MLA_ROPE
ROPE_BASE = 10000.0
FOX_HEADS, FOX_DH = 4, 128
SSD_HEADS, SSD_HEADDIM, SSD_GROUPS, SSD_STATE = 8, 64, 2, 128
SSD_HG = SSD_HEADS // SSD_GROUPS
SSD_INNER = SSD_HEADS * SSD_HEADDIM
N_BRANCH, BRANCH_W = 4, 512
N_EXPERTS, TOP_K = 8, 2

LANES = 128
ROW_TILE = 256
FLASH_TILES = (768, 256)
FLASH_HEADS_PER_STEP = 2
LOG2E = 1.4426950408889634
MM_TM = 768
FFN_DOWN_TM = 384
MOE_TM = 256
MOE_DISPATCH_ROWS = 128
R_E1, R_E2, R_P1, R_P2, R_R1, R_R2 = range(6)
VMEM_LIMIT = 56 * 1024 * 1024

PROJ_W = 6144
COL_GDN_QKV, COL_GDN_Z = 0, 1536
COL_SSD_XBC, COL_SSD_Z = 2048, 3072
COL_MLA_QA, COL_MLA_KVA = 3584, 4096
COL_SMALL = 4352
COL_FOX_QKV = 4608
L_BETA, L_GA, L_FF, L_DT, L_KPE = 0, 4, 8, 12, 20


def _cparams(sem, vmem=VMEM_LIMIT):
    return pltpu.CompilerParams(dimension_semantics=sem, vmem_limit_bytes=vmem)


def _softplus(x):
    return jnp.maximum(x, 0.0) + jnp.log1p(jnp.exp(-jnp.abs(x)))


def _silu(x):
    return x * jax.nn.sigmoid(x)


def _dot(a, b, precision=None):
    return jnp.dot(a, b, preferred_element_type=F32, precision=precision)


def _dot_nt(a, b):
    return lax.dot_general(a, b, NT_DIMS, preferred_element_type=F32)


def _rmsnorm_kernel(h_ref, g_ref, o_ref):
    x = h_ref[...]
    y = x * lax.rsqrt(jnp.mean(x * x, axis=-1, keepdims=True) + EPS) * g_ref[...]
    o_ref[...] = y.astype(o_ref.dtype)


def _rmsnorm(h, g):
    M, D = h.shape
    return pl.pallas_call(
        _rmsnorm_kernel,
        out_shape=jax.ShapeDtypeStruct((M, D), BF16),
        grid=(M // MM_TM,),
        in_specs=[pl.BlockSpec((MM_TM, D), lambda i: (i, 0)),
                  pl.BlockSpec((1, D), lambda i: (0, 0))],
        out_specs=pl.BlockSpec((MM_TM, D), lambda i: (i, 0)),
        compiler_params=_cparams(("parallel",)),
        name="rmsnorm",
    )(h, g.reshape(1, D).astype(F32))


def _rmsnorm_router_kernel(h_ref, g_ref, rw_ref, o_ref, rec_ref, cnt_ref, run_ref):
    @pl.when(pl.program_id(1) == 0)
    def _():
        run_ref[...] = jnp.zeros_like(run_ref)

    x = h_ref[...]
    n = x.shape[0]
    y = x * lax.rsqrt(jnp.mean(x * x, axis=-1, keepdims=True) + EPS) * g_ref[...]
    y16 = y.astype(BF16)
    o_ref[...] = y16
    y_lo = (y - y16.astype(F32)).astype(BF16)
    both = _dot(y16, rw_ref[...])
    logits = both[:, :LANES] + both[:, LANES:] + _dot(y_lo, rw_ref[:, :LANES])
    lane = lax.broadcasted_iota(jnp.int32, logits.shape, 1)
    logits = jnp.where(lane < N_EXPERTS, logits, NEG)
    m1 = jnp.max(logits, axis=-1, keepdims=True)
    i1 = jnp.min(jnp.where(logits == m1, lane, LANES), axis=-1, keepdims=True)
    rest = jnp.where(lane == i1, NEG, logits)
    m2 = jnp.max(rest, axis=-1, keepdims=True)
    i2 = jnp.min(jnp.where(rest == m2, lane, LANES), axis=-1, keepdims=True)
    e2 = jnp.exp(m2 - m1)
    p1 = 1.0 / (1.0 + e2)
    p2 = e2 * p1
    sel = jnp.where(lane == i1, 1.0, 0.0) + jnp.where(lane == i2, 1.0, 0.0)
    row = lax.broadcasted_iota(jnp.int32, (n, n), 0)
    col = lax.broadcasted_iota(jnp.int32, (n, n), 1)
    earlier = jnp.where(col < row, 1.0, 0.0).astype(BF16)
    run = run_ref[0:1, :]
    before = _dot(earlier, sel.astype(BF16)) + run
    r1 = jnp.sum(jnp.where(lane == i1, before, 0.0), axis=-1, keepdims=True)
    r2 = jnp.sum(jnp.where(lane == i2, before, 0.0), axis=-1, keepdims=True)
    cnt_ref[...] = jnp.broadcast_to(run, cnt_ref.shape)
    run_ref[...] = jnp.broadcast_to(before[n - 1:n] + sel[n - 1:n], run_ref.shape)
    rec = jnp.zeros_like(logits)
    for ln, val in ((R_E1, i1.astype(F32)), (R_E2, i2.astype(F32)), (R_P1, p1), (R_P2, p2), (R_R1, r1), (R_R2, r2)):
        rec = jnp.where(lane == ln, val, rec)
    rec_ref[...] = rec


def _rmsnorm_router(h, g, router_w, B, T):
    M, D = h.shape
    n = ROW_TILE
    NT = T // n
    rw = jnp.zeros((D, LANES), F32).at[:, :N_EXPERTS].set(router_w.astype(F32))
    rw_hi = rw.astype(BF16)
    rw = jnp.concatenate([rw_hi, (rw - rw_hi.astype(F32)).astype(BF16)], axis=1)
    return pl.pallas_call(
        _rmsnorm_router_kernel,
        out_shape=(jax.ShapeDtypeStruct((M, D), BF16), jax.ShapeDtypeStruct((M, LANES), F32),
                   jax.ShapeDtypeStruct((B * NT * 8, LANES), F32)),
        grid=(B, NT),
        in_specs=[pl.BlockSpec((n, D), lambda b, t: (b * NT + t, 0)),
                  pl.BlockSpec((1, D), lambda b, t: (0, 0)),
                  pl.BlockSpec((D, 2 * LANES), lambda b, t: (0, 0))],
        out_specs=(pl.BlockSpec((n, D), lambda b, t: (b * NT + t, 0)),
                   pl.BlockSpec((n, LANES), lambda b, t: (b * NT + t, 0)),
                   pl.BlockSpec((8, LANES), lambda b, t: (b * NT + t, 0))),
        scratch_shapes=[pltpu.VMEM((8, LANES), F32)],
        compiler_params=_cparams(("arbitrary", "arbitrary")),
        name="rmsnorm_router",
    )(h, g.reshape(1, D).astype(F32), rw)


def _mm_kernel(*refs, has_res, has_norm):
    a_ref, w_ref = refs[:2]
    y = _dot(a_ref[...], w_ref[...])
    if has_res:
        y = y + refs[2][...]
    if has_norm:
        g_ref, o_ref, n_ref = refs[-3:]
        n_ref[...] = (y * lax.rsqrt(jnp.mean(y * y, axis=-1, keepdims=True) + EPS) * g_ref[...]).astype(n_ref.dtype)
    else:
        o_ref = refs[-1]
    o_ref[...] = y.astype(o_ref.dtype)


def _matmul(a, w, *, tn, tm=MM_TM, residual=None, norm_g=None, out_dtype=F32, w_buffers=2, name="matmul"):
    M, K = a.shape
    N = w.shape[1]
    in_specs = [pl.BlockSpec((tm, K), lambda j, i: (i, 0)),
                pl.BlockSpec((K, tn), lambda j, i: (0, j), pipeline_mode=pl.Buffered(w_buffers))]
    args = [a, w]
    out_shape = jax.ShapeDtypeStruct((M, N), out_dtype)
    out_specs = pl.BlockSpec((tm, tn), lambda j, i: (i, j))
    if residual is not None:
        in_specs.append(pl.BlockSpec((tm, tn), lambda j, i: (i, j)))
        args.append(residual)
    if norm_g is not None:
        assert tn == N
        in_specs.append(pl.BlockSpec((1, N), lambda j, i: (0, 0)))
        args.append(norm_g.reshape(1, N).astype(F32))
        out_shape = (out_shape, jax.ShapeDtypeStruct((M, N), BF16))
        out_specs = (out_specs, pl.BlockSpec((tm, tn), lambda j, i: (i, j)))
    return pl.pallas_call(
        functools.partial(_mm_kernel, has_res=residual is not None, has_norm=norm_g is not None),
        out_shape=out_shape,
        grid=(N // tn, M // tm),
        in_specs=in_specs,
        out_specs=out_specs,
        compiler_params=_cparams(("parallel", "parallel")),
        name=name,
    )(*args)


def _swiglu_kernel(a_ref, wg_ref, wu_ref, o_ref):
    a = a_ref[...]
    g = _dot(a, wg_ref[...])
    u = _dot(a, wu_ref[...])
    o_ref[...] = (_silu(g) * u).astype(o_ref.dtype)


def _swiglu_up(a, wg, wu, *, tn, tm=MM_TM):
    M, K = a.shape
    F = wg.shape[1]
    return pl.pallas_call(
        _swiglu_kernel,
        out_shape=jax.ShapeDtypeStruct((M, F), BF16),
        grid=(F // tn, M // tm),
        in_specs=[pl.BlockSpec((tm, K), lambda j, i: (i, 0)),
                  pl.BlockSpec((K, tn), lambda j, i: (0, j)),
                  pl.BlockSpec((K, tn), lambda j, i: (0, j))],
        out_specs=pl.BlockSpec((tm, tn), lambda j, i: (i, j)),
        compiler_params=_cparams(("parallel", "parallel")),
        name="swiglu_up",
    )(a, wg, wu)


def _dispatch_kernel(lo_ref, nb_ref, d1_ref, d2_ref, rec_ref, hn_ref, o_ref, p_ref, acc_ref, pacc_ref, *, wb):
    i = pl.program_id(1)
    idx = pl.program_id(0) * pl.num_programs(1) + i
    tm = o_ref.shape[0]
    rows = i * tm + lax.broadcasted_iota(jnp.int32, (tm, wb), 0)
    lane = lax.broadcasted_iota(jnp.int32, (wb, LANES), 1)
    acc_ref[...] = jnp.zeros_like(acc_ref)
    pacc_ref[...] = jnp.zeros_like(pacc_ref)

    def hi_lo(p):
        hi = p.astype(BF16).astype(F32)
        return jnp.where(lane == 0, hi, jnp.where(lane == 1, p - hi, 0.0)).astype(BF16)

    def body(k, carry):
        start = pl.multiple_of((lo_ref[idx] + k) * wb, wb)
        hit1 = jnp.where(d1_ref[0, :, pl.ds(start, wb)] == rows, 1.0, 0.0)
        hit2 = jnp.where(d2_ref[0, :, pl.ds(start, wb)] == rows, 1.0, 0.0)
        acc_ref[...] += _dot((hit1 + hit2).astype(BF16), hn_ref[0, pl.ds(start, wb), :]).astype(BF16)
        rec = rec_ref[0, pl.ds(start, wb), :]
        pacc_ref[...] += (_dot(hit1.astype(BF16), hi_lo(rec[:, R_P1:R_P1 + 1]))
                          + _dot(hit2.astype(BF16), hi_lo(rec[:, R_P2:R_P2 + 1])))
        return carry

    lax.fori_loop(0, nb_ref[idx], body, 0)
    o_ref[...] = acc_ref[...].astype(o_ref.dtype)
    p_ref[...] = pacc_ref[...]


def _dispatch(hn3, rec3, d1, d2, lo_blk, n_blk, nti, *, tm, wb=ROW_TILE):
    B, T, D = hn3.shape
    dspec = pl.BlockSpec((1, 1, T), lambda b, i, lo, nb: (b, 0, 0))
    row_spec = lambda w: pl.BlockSpec((tm, w), lambda b, i, lo, nb: (b * nti + i, 0))
    grid_spec = pltpu.PrefetchScalarGridSpec(
        num_scalar_prefetch=2,
        grid=(B, nti),
        in_specs=[dspec, dspec,
                  pl.BlockSpec((1, T, LANES), lambda b, i, lo, nb: (b, 0, 0)),
                  pl.BlockSpec((1, T, D), lambda b, i, lo, nb: (b, 0, 0), pipeline_mode=pl.Buffered(1))],
        out_specs=(row_spec(D), row_spec(LANES)),
        scratch_shapes=[pltpu.VMEM((tm, D), BF16), pltpu.VMEM((tm, LANES), F32)])
    return pl.pallas_call(
        functools.partial(_dispatch_kernel, wb=wb),
        out_shape=(jax.ShapeDtypeStruct((B * nti * tm, D), BF16),
                   jax.ShapeDtypeStruct((B * nti * tm, LANES), F32)),
        grid_spec=grid_spec,
        compiler_params=_cparams(("arbitrary", "arbitrary")),
        name="moe_dispatch",
    )(lo_blk, n_blk, d1.reshape(B, 1, T), d2.reshape(B, 1, T), rec3, hn3)


def _expert_ffn_kernel(te_ref, nb_ref, x_ref, wg_ref, wu_ref, wd_ref, p_ref, o_ref):
    i = pl.program_id(0)

    @pl.when(nb_ref[i] > 0)
    def _():
        x = x_ref[...]
        p = p_ref[:, 0:1] + p_ref[:, 1:2]
        act = (_silu(_dot(x, wg_ref[0])) * _dot(x, wu_ref[0]) * p).astype(BF16)
        o_ref[...] = _dot(act, wd_ref[0]).astype(o_ref.dtype)

    @pl.when(nb_ref[i] == 0)
    def _():
        o_ref[...] = jnp.zeros_like(o_ref)


def _expert_ffn(x, p_rows, tile_expert, n_blk, wg, wu, wd, *, tm):
    R, D = x.shape
    E, _, F = wg.shape
    grid_spec = pltpu.PrefetchScalarGridSpec(
        num_scalar_prefetch=2,
        grid=(R // tm,),
        in_specs=[pl.BlockSpec((tm, D), lambda i, te, nb: (i, 0)),
                  pl.BlockSpec((1, D, F), lambda i, te, nb: (te[i], 0, 0)),
                  pl.BlockSpec((1, D, F), lambda i, te, nb: (te[i], 0, 0)),
                  pl.BlockSpec((1, F, D), lambda i, te, nb: (te[i], 0, 0)),
                  pl.BlockSpec((tm, LANES), lambda i, te, nb: (i, 0))],
        out_specs=pl.BlockSpec((tm, D), lambda i, te, nb: (i, 0)))
    return pl.pallas_call(
        _expert_ffn_kernel,
        out_shape=jax.ShapeDtypeStruct((R, D), BF16),
        grid_spec=grid_spec,
        compiler_params=_cparams(("arbitrary",)),
        name="moe_ffn",
    )(tile_expert, n_blk, x, wg, wu, wd, p_rows)


def _combine_kernel(yb_ref, yv_ref, h_ref, d1_ref, d2_ref, *refs, tm):
    y_refs, o_ref = refs[:-1], refs[-1]
    n = o_ref.shape[0]
    slots = len(y_refs) // N_EXPERTS
    base = (pl.program_id(0) * pl.num_programs(1) + pl.program_id(1)) * len(y_refs)
    d1, d2 = d1_ref[...], d2_ref[...]
    lane = lax.broadcasted_iota(jnp.int32, (n, tm), 1)

    def picked(k, weight):
        rows = yb_ref[base + k] * tm + lane
        hit = jnp.where(d1 == rows, weight, 0.0) + jnp.where(d2 == rows, weight, 0.0)
        return _dot(hit.astype(BF16), y_refs[k][...])

    acc = h_ref[...]
    for k in range(0, len(y_refs), slots):
        acc = acc + picked(k, jnp.where(yv_ref[base + k] > 0, 1.0, 0.0))
    o_ref[...] = acc
    for k in range(len(y_refs)):
        if k % slots:
            @pl.when(yv_ref[base + k] > 0)
            def _():
                o_ref[...] += picked(k, 1.0)


def _combine(h, y, d1, d2, y_blk, y_valid, B, T, *, tm):
    M, D = h.shape
    n = ROW_TILE
    NT = T // n
    slots = y_blk.shape[0] // (B * NT)
    y_specs = [pl.BlockSpec((tm, D), lambda b, t, yb, yv, k=k: (yb[(b * NT + t) * slots + k], 0))
               for k in range(slots)]
    tok = lambda w: pl.BlockSpec((n, w), lambda b, t, yb, yv: (b * NT + t, 0))
    grid_spec = pltpu.PrefetchScalarGridSpec(
        num_scalar_prefetch=2,
        grid=(B, NT),
        in_specs=[tok(D), tok(1), tok(1)] + y_specs,
        out_specs=tok(D))
    return pl.pallas_call(
        functools.partial(_combine_kernel, tm=tm),
        out_shape=jax.ShapeDtypeStruct((M, D), F32),
        grid_spec=grid_spec,
        compiler_params=_cparams(("arbitrary", "arbitrary")),
        name="moe_combine",
    )(y_blk, y_valid, h, d1, d2, *([y] * slots))


def _moe(h, hn, rec, cnt_before, wg, wu, wd, B, T):
    M, D = h.shape
    E, tm, n = N_EXPERTS, MOE_TM, ROW_TILE
    NT = T // n
    rb = TOP_K * T + E * tm
    i32 = jnp.int32
    col = lambda ln: rec[:, ln].reshape(B, T)
    e1, e2 = col(R_E1).astype(i32), col(R_E2).astype(i32)
    r1, r2 = col(R_R1).astype(i32), col(R_R2).astype(i32)
    cntb = cnt_before.reshape(B, NT, 8, LANES)[:, :, 0, :E].astype(i32)
    oh1, oh2 = jax.nn.one_hot(e1, E, dtype=i32), jax.nn.one_hot(e2, E, dtype=i32)
    counts = (oh1 + oh2).sum(axis=1)
    padded = (counts + tm - 1) // tm * tm
    ends = jnp.cumsum(padded, axis=1)
    off = ends - padded
    d1 = (oh1 * off[:, None, :]).sum(-1) + r1
    d2 = (oh2 * off[:, None, :]).sum(-1) + r2
    cnt_end = jnp.concatenate([cntb[:, 1:], counts[:, None, :]], axis=1)
    def tiles(rows):
        start = jnp.arange(rb // rows, dtype=i32)[None, :] * rows
        expert = jnp.minimum((start[:, :, None] >= ends[:, None, :]).sum(-1), E - 1).astype(i32)
        e_oh = jax.nn.one_hot(expert, E, dtype=i32)
        pick = lambda per_expert: (e_oh * per_expert[:, None, :]).sum(-1)
        rank_lo = start - pick(off)
        rank_hi = jnp.minimum(rank_lo + rows, pick(counts)) - 1
        through = (e_oh[:, :, None, :] * cnt_end[:, None, :, :]).sum(-1)
        lo_blk = (through <= rank_lo[..., None]).sum(-1)
        hi_blk = (through <= rank_hi[..., None]).sum(-1)
        nonempty = (start < pick(ends)) & (rank_hi >= rank_lo)
        return (expert.reshape(-1), jnp.where(nonempty, lo_blk, 0).astype(i32).reshape(-1),
                jnp.where(nonempty, hi_blk - lo_blk + 1, 0).astype(i32).reshape(-1))

    pr = MOE_DISPATCH_ROWS
    _, lo_blk, n_blk = tiles(pr)
    x, p_rows = _dispatch(hn.reshape(B, T, D), rec.reshape(B, T, LANES), d1, d2, lo_blk, n_blk, rb // pr, tm=pr)
    tile_expert, _, tile_used = tiles(tm)
    y = _expert_ffn(x, p_rows, tile_expert, tile_used, wg, wu, wd, tm=tm)
    first = off[:, None, :] + cntb
    last = off[:, None, :] + cnt_end - 1
    used = cnt_end > cntb
    span = jnp.arange(n // tm + 1, dtype=i32)
    blk = first[..., None] // tm + span
    valid = used[..., None] & (blk <= last[..., None] // tm)
    blk = jnp.minimum(blk, rb // tm - 1) + (jnp.arange(B, dtype=i32) * (rb // tm))[:, None, None, None]
    row0 = (jnp.arange(B, dtype=i32) * rb)[:, None]
    return _combine(h, y, (d1 + row0).reshape(M, 1), (d2 + row0).reshape(M, 1),
                    blk.reshape(-1).astype(i32), valid.reshape(-1).astype(i32), B, T, tm=tm)


def _merge_kernel(hn_ref, b0_ref, b1_ref, b2_ref, b3_ref, wg_ref, wb_ref, o_ref):
    hn = hn_ref[...]
    acc = None
    for b, br_ref in enumerate((b0_ref, b1_ref, b2_ref, b3_ref)):
        gate = jax.nn.sigmoid(_dot(hn, wg_ref[b]))
        term = gate * _dot(br_ref[...], wb_ref[b])
        acc = term if acc is None else acc + term
    o_ref[...] = acc.astype(o_ref.dtype)


def _merge(hn, branches, wg, wb, *, tn=512, tm=512):
    M, D = hn.shape
    N = wg.shape[2]
    bspec = pl.BlockSpec((tm, BRANCH_W), lambda j, i: (i, 0))
    return pl.pallas_call(
        _merge_kernel,
        out_shape=jax.ShapeDtypeStruct((M, N), BF16),
        grid=(N // tn, M // tm),
        in_specs=[pl.BlockSpec((tm, D), lambda j, i: (i, 0)), bspec, bspec, bspec, bspec,
                  pl.BlockSpec((N_BRANCH, D, tn), lambda j, i: (0, 0, j)),
                  pl.BlockSpec((N_BRANCH, BRANCH_W, tn), lambda j, i: (0, 0, j))],
        out_specs=pl.BlockSpec((tm, tn), lambda j, i: (i, j)),
        compiler_params=_cparams(("parallel", "parallel")),
        name="gate_merge",
    )(hn, *branches, wg, wb)


def _causal_conv(x, carry_ref, cw):
    n = x.shape[0]
    xext = jnp.concatenate([carry_ref[...], x], axis=0)
    y = cw[0:1] * xext[5:5 + n]
    for i in range(1, CONV_K):
        y = y + cw[i:i + 1] * xext[5 + i:5 + i + n]
    carry_ref[...] = x[n - 8:n]
    return y


def _tile_masks(n, chunk):
    row = lax.broadcasted_iota(jnp.int32, (n, n), 0)
    col = lax.broadcasted_iota(jnp.int32, (n, n), 1)
    if chunk == n:
        return col <= row, col < row
    in_chunk = col >= (row // chunk) * chunk
    return in_chunk & (col <= row), in_chunk & (col < row)


def _gdn_kernel(qkv_ref, z_ref, sm_ref, cw_ref, alog_ref, dtb_ref, ng_ref, o_ref, s_ref, carry_ref):
    @pl.when(pl.program_id(1) == 0)
    def _():
        s_ref[...] = jnp.zeros_like(s_ref)
        carry_ref[...] = jnp.zeros_like(carry_ref)

    n = ROW_TILE
    y = _silu(_causal_conv(qkv_ref[0], carry_ref, cw_ref[...]))
    sm = sm_ref[0]
    z = z_ref[0]
    beta_all = jax.nn.sigmoid(sm)
    g_all = -jnp.exp(alog_ref[...]) * _softplus(sm + dtb_ref[...])
    row = lax.broadcasted_iota(jnp.int32, (n, n), 0)
    col = lax.broadcasted_iota(jnp.int32, (n, n), 1)
    causal, strict = col <= row, col < row
    diff_bits = row ^ col
    levels = int(math.log2(n))
    level_masks = [(diff_bits >= (1 << l)) & (diff_bits < (2 << l)) for l in range(levels)]
    eye = jnp.where(row == col, 1.0, 0.0)
    gcs_all = _dot(causal.astype(F32), g_all, HI)
    gcs_t = gcs_all.T
    heads = []
    for h in range(GDN_HEADS):
        q = y[:, h * GDN_DK:(h + 1) * GDN_DK]
        k = y[:, GDN_HEADS * GDN_DK + h * GDN_DK:GDN_HEADS * GDN_DK + (h + 1) * GDN_DK]
        v = y[:, 2 * GDN_HEADS * GDN_DK + h * GDN_DV:2 * GDN_HEADS * GDN_DK + (h + 1) * GDN_DV]
        q = q * lax.rsqrt(jnp.sum(q * q, axis=-1, keepdims=True) + EPS) * GDN_DK ** -0.5
        k = k * lax.rsqrt(jnp.sum(k * k, axis=-1, keepdims=True) + EPS)
        beta = beta_all[:, L_BETA + h:L_BETA + h + 1]
        gc = gcs_all[:, L_GA + h:L_GA + h + 1]
        gr = gcs_t[L_GA + h:L_GA + h + 1, :]
        decay = jnp.exp(jnp.where(causal, gc - gr, NEG))
        kb = k * beta
        k16 = k.astype(BF16)
        a = jnp.where(strict, _dot_nt(kb.astype(BF16), k16) * decay, 0.0)
        heads.append(dict(q=q, k=k, v=v, beta=beta, gc=gc, decay=decay, kb=kb, k16=k16, a=a,
                          t=eye - jnp.where(level_masks[0], a, 0.0)))
    for l in range(1, levels):
        for hd in heads:
            t16 = hd["t"].astype(BF16)
            a_l = jnp.where(level_masks[l], hd["a"], 0.0).astype(BF16)
            hd["t"] = hd["t"] - _dot(_dot(t16, a_l).astype(BF16), t16)
    outs = []
    for h, hd in enumerate(heads):
        q, k, gc = hd["q"], hd["k"], hd["gc"]
        egc = jnp.exp(gc)
        rhs = jnp.concatenate([hd["v"] * hd["beta"], hd["kb"] * egc], axis=1)
        uw = _dot(hd["t"].astype(BF16), rhs.astype(BF16))
        u, w = uw[:, :GDN_DV], uw[:, GDN_DV:]
        att = _dot_nt(q.astype(BF16), hd["k16"]) * hd["decay"]
        g_last = gc[n - 1:n, :]
        kd = k * jnp.exp(g_last - gc)
        s = s_ref[h]
        ws = _dot(jnp.concatenate([w, q * egc], axis=0).astype(BF16), s.astype(BF16))
        v_new = (u - ws[:n]).astype(BF16)
        o = ws[n:] + _dot(att.astype(BF16), v_new)
        s_ref[h] = s * jnp.exp(g_last) + _dot(kd.T.astype(BF16), v_new)
        o = o * lax.rsqrt(jnp.mean(o * o, axis=-1, keepdims=True) + EPS) * ng_ref[...]
        outs.append(o * _silu(z[:, h * GDN_DV:(h + 1) * GDN_DV]))
    o_ref[0] = jnp.concatenate(outs, axis=1).astype(o_ref.dtype)


def _lane_vec(vals, lane0):
    v = jnp.zeros((1, LANES), F32)
    return v.at[0, lane0:lane0 + vals.shape[0]].set(vals.astype(F32))


def _gdn(proj, conv_w, a_log, dt_bias, norm_g):
    B, T, _ = proj.shape
    n = ROW_TILE
    W = 2 * GDN_HEADS * GDN_DK + GDN_HEADS * GDN_DV
    ZW = GDN_HEADS * GDN_DV
    return pl.pallas_call(
        _gdn_kernel,
        out_shape=jax.ShapeDtypeStruct((B, T, ZW), BF16),
        grid=(B, T // n),
        in_specs=[pl.BlockSpec((1, n, W), lambda b, t: (b, t, COL_GDN_QKV // W)),
                  pl.BlockSpec((1, n, ZW), lambda b, t: (b, t, COL_GDN_Z // ZW)),
                  pl.BlockSpec((1, n, LANES), lambda b, t: (b, t, COL_SMALL // LANES)),
                  pl.BlockSpec((CONV_K, W), lambda b, t: (0, 0)),
                  pl.BlockSpec((1, LANES), lambda b, t: (0, 0)),
                  pl.BlockSpec((1, LANES), lambda b, t: (0, 0)),
                  pl.BlockSpec((1, GDN_DV), lambda b, t: (0, 0))],
        out_specs=pl.BlockSpec((1, n, ZW), lambda b, t: (b, t, 0)),
        scratch_shapes=[pltpu.VMEM((GDN_HEADS, GDN_DK, GDN_DV), F32),
                        pltpu.VMEM((8, W), F32)],
        compiler_params=_cparams(("arbitrary", "arbitrary")),
        name="gdn",
    )(proj, proj, proj, conv_w.astype(F32), _lane_vec(a_log, L_GA), _lane_vec(dt_bias, L_GA),
      norm_g.reshape(1, GDN_DV).astype(F32))


def _ssd_kernel(xbc_ref, z_ref, sm_ref, cw_ref, cb_ref, alog_ref, dtb_ref, dvec_ref, ng_ref,
                o_ref, hs_ref, carry_ref):
    @pl.when(pl.program_id(1) == 0)
    def _():
        hs_ref[...] = jnp.zeros_like(hs_ref)
        carry_ref[...] = jnp.zeros_like(carry_ref)

    n, P, N = ROW_TILE, SSD_HEADDIM, SSD_STATE
    y = _silu(_causal_conv(xbc_ref[0], carry_ref, cw_ref[...]) + cb_ref[...])
    xs = y[:, :SSD_INNER]
    bm = y[:, SSD_INNER:SSD_INNER + SSD_GROUPS * N]
    cm = y[:, SSD_INNER + SSD_GROUPS * N:]
    sm = sm_ref[0]
    dt_all = _softplus(sm + dtb_ref[...])
    a_all = dt_all * (-jnp.exp(alog_ref[...]))
    causal, _ = _tile_masks(n, n)
    acs_all = _dot(causal.astype(F32), a_all, HI)
    acs_t = acs_all.T
    ys = []
    for g in range(SSD_GROUPS):
        bg = bm[:, g * N:(g + 1) * N]
        cg16 = cm[:, g * N:(g + 1) * N].astype(BF16)
        cb = _dot_nt(cg16, bg.astype(BF16))
        bgt16 = bg.T.astype(BF16)
        for j in range(SSD_HG):
            hh = g * SSD_HG + j
            ac = acs_all[:, L_DT + hh:L_DT + hh + 1]
            ar = acs_t[L_DT + hh:L_DT + hh + 1, :]
            lmat = jnp.exp(jnp.where(causal, ac - ar, NEG))
            xdt = xs[:, hh * P:(hh + 1) * P] * dt_all[:, L_DT + hh:L_DT + hh + 1]
            y_diag = _dot((cb * lmat).astype(BF16), xdt.astype(BF16))
            a_last = ac[n - 1:n, :]
            st = _dot(bgt16, (xdt * jnp.exp(a_last - ac)).astype(BF16))
            h_prev = hs_ref[hh]
            y_off = _dot(cg16, h_prev.astype(BF16)) * jnp.exp(ac)
            hs_ref[hh] = h_prev * jnp.exp(a_last) + st
            ys.append(y_diag + y_off)
    yy = jnp.concatenate(ys, axis=1) + xs * dvec_ref[...]
    yy = yy * _silu(z_ref[0])
    gw = SSD_HG * P
    outs = []
    for g in range(SSD_GROUPS):
        seg = yy[:, g * gw:(g + 1) * gw]
        outs.append(seg * lax.rsqrt(jnp.mean(seg * seg, axis=-1, keepdims=True) + EPS)
                    * ng_ref[:, g * gw:(g + 1) * gw])
    o_ref[0] = jnp.concatenate(outs, axis=1).astype(o_ref.dtype)


def _ssd(proj, conv_w, conv_b, dt_bias, a_log, d_skip, norm_g):
    B, T, _ = proj.shape
    n = ROW_TILE
    W = SSD_INNER + 2 * SSD_GROUPS * SSD_STATE
    dvec = jnp.repeat(d_skip.astype(F32), SSD_HEADDIM).reshape(1, SSD_INNER)
    return pl.pallas_call(
        _ssd_kernel,
        out_shape=jax.ShapeDtypeStruct((B, T, SSD_INNER), BF16),
        grid=(B, T // n),
        in_specs=[pl.BlockSpec((1, n, W), lambda b, t: (b, t, COL_SSD_XBC // W)),
                  pl.BlockSpec((1, n, SSD_INNER), lambda b, t: (b, t, COL_SSD_Z // SSD_INNER)),
                  pl.BlockSpec((1, n, LANES), lambda b, t: (b, t, COL_SMALL // LANES)),
                  pl.BlockSpec((CONV_K, W), lambda b, t: (0, 0)),
                  pl.BlockSpec((1, W), lambda b, t: (0, 0)),
                  pl.BlockSpec((1, LANES), lambda b, t: (0, 0)),
                  pl.BlockSpec((1, LANES), lambda b, t: (0, 0)),
                  pl.BlockSpec((1, SSD_INNER), lambda b, t: (0, 0)),
                  pl.BlockSpec((1, SSD_INNER), lambda b, t: (0, 0))],
        out_specs=pl.BlockSpec((1, n, SSD_INNER), lambda b, t: (b, t, 0)),
        scratch_shapes=[pltpu.VMEM((SSD_HEADS, SSD_STATE, SSD_HEADDIM), F32),
                        pltpu.VMEM((8, W), F32)],
        compiler_params=_cparams(("arbitrary", "arbitrary")),
        name="ssd",
    )(proj, proj, proj, conv_w.astype(F32), conv_b.reshape(1, W).astype(F32),
      _lane_vec(a_log, L_DT), _lane_vec(dt_bias, L_DT), dvec,
      norm_g.reshape(1, SSD_INNER).astype(F32))


def _ones_lane0(n):
    lane = lax.broadcasted_iota(jnp.int32, (n, LANES), 1)
    return jnp.where(lane == 0, 1.0, 0.0)


def _fox_prep_kernel(qkv_ref, sm_ref, qg_ref, kg_ref, bf_ref, q_ref, k_ref, v_ref, run_ref):
    @pl.when(pl.program_id(1) == 0)
    def _():
        run_ref[...] = jnp.zeros_like(run_ref)

    n, dh, H = ROW_TILE, FOX_DH, FOX_HEADS
    x = qkv_ref[0]
    log_f = -_softplus(-(sm_ref[0] + bf_ref[...]))
    causal, _ = _tile_masks(n, n)
    cum = _dot(causal.astype(F32), log_f, HI) + run_ref[0:1, :]
    run_ref[...] = jnp.broadcast_to(cum[n - 1:n, :], run_ref.shape)
    lane = lax.broadcasted_iota(jnp.int32, (n, LANES), 1)
    ones = _ones_lane0(n)
    q_bias = jnp.where(lane < 3, 1.0, 0.0)
    qs, ks, vs = [], [], []
    for h in range(H):
        q = x[:, h * dh:(h + 1) * dh]
        k = x[:, H * dh + h * dh:H * dh + (h + 1) * dh]
        qs.append(q * lax.rsqrt(jnp.mean(q * q, axis=-1, keepdims=True) + EPS)
                  * (qg_ref[...] * (dh ** -0.5 * LOG2E)))
        qs.append(q_bias)
        ks.append(k * lax.rsqrt(jnp.mean(k * k, axis=-1, keepdims=True) + EPS) * kg_ref[...])
        c = cum[:, L_FF + h:L_FF + h + 1] * (-LOG2E)
        c_hi = c.astype(BF16).astype(F32)
        c_mid = (c - c_hi).astype(BF16).astype(F32)
        c_lo = c - c_hi - c_mid
        ks.append(jnp.where(lane == 0, c_hi, jnp.where(lane == 1, c_mid, jnp.where(lane == 2, c_lo, 0.0))))
        vs.append(x[:, 2 * H * dh + h * dh:2 * H * dh + (h + 1) * dh])
        vs.append(ones)
    q_ref[0] = jnp.concatenate(qs, axis=1).astype(q_ref.dtype)
    k_ref[0] = jnp.concatenate(ks, axis=1).astype(k_ref.dtype)
    v_ref[0] = jnp.concatenate(vs, axis=1).astype(v_ref.dtype)


def _fox_prep(proj, qn_g, kn_g, b_f):
    B, T, _ = proj.shape
    n = ROW_TILE
    W = 3 * FOX_HEADS * FOX_DH
    HW = FOX_HEADS * 2 * LANES
    ospec = pl.BlockSpec((1, n, HW), lambda b, t: (b, t, 0))
    return pl.pallas_call(
        _fox_prep_kernel,
        out_shape=(jax.ShapeDtypeStruct((B, T, HW), BF16),) * 3,
        grid=(B, T // n),
        in_specs=[pl.BlockSpec((1, n, W), lambda b, t: (b, t, COL_FOX_QKV // W)),
                  pl.BlockSpec((1, n, LANES), lambda b, t: (b, t, COL_SMALL // LANES)),
                  pl.BlockSpec((1, FOX_DH), lambda b, t: (0, 0)),
                  pl.BlockSpec((1, FOX_DH), lambda b, t: (0, 0)),
                  pl.BlockSpec((1, LANES), lambda b, t: (0, 0))],
        out_specs=(ospec, ospec, ospec),
        scratch_shapes=[pltpu.VMEM((8, LANES), F32)],
        compiler_params=_cparams(("arbitrary", "arbitrary")),
        name="fox_prep",
    )(proj, proj, qn_g.reshape(1, FOX_DH).astype(F32), kn_g.reshape(1, FOX_DH).astype(F32),
      _lane_vec(b_f, L_FF))


def _mla_prep_kernel(qa_ref, kva_ref, sm_ref, cos_ref, sin_ref, qag_ref, wq_ref, kvag_ref, wkv_ref,
                     qgn_ref, qgr_ref, kgn_ref, kgr_ref, rot_ref, exp_ref, q_ref, k_ref, v_ref):
    n, H, dn, dr = ROW_TILE, MLA_HEADS, MLA_NOPE, MLA_ROPE
    qa = qa_ref[0]
    qa = qa * lax.rsqrt(jnp.mean(qa * qa, axis=-1, keepdims=True) + EPS) * qag_ref[...]
    qq = _dot(qa.astype(BF16), wq_ref[...])
    kva = kva_ref[0]
    kva = kva * lax.rsqrt(jnp.mean(kva * kva, axis=-1, keepdims=True) + EPS) * kvag_ref[...]
    kv = _dot(kva.astype(BF16), wkv_ref[...])
    sm = sm_ref[0]
    lane = lax.broadcasted_iota(jnp.int32, (n, LANES), 1)
    is_kpe = (lane >= L_KPE) & (lane < L_KPE + dr)
    kpe_ss = jnp.sum(jnp.where(is_kpe, sm * sm, 0.0), axis=-1, keepdims=True)
    kpe4 = _dot(sm, exp_ref[...], HI)
    qr = qq[:, H * dn:]
    grp = lax.broadcasted_iota(jnp.int32, (n, H * dr), 1) // dr
    q_rs, k_rs = [], []
    for h in range(H):
        qn = qq[:, h * dn:(h + 1) * dn]
        ssr = jnp.sum(jnp.where(grp == h, qr * qr, 0.0), axis=-1, keepdims=True)
        q_rs.append(lax.rsqrt((jnp.sum(qn * qn, axis=-1, keepdims=True) + ssr) / MLA_DQK + EPS))
        kn = kv[:, h * dn:(h + 1) * dn]
        k_rs.append(lax.rsqrt((jnp.sum(kn * kn, axis=-1, keepdims=True) + kpe_ss) / MLA_DQK + EPS))

    def per_group(vals):
        out = vals[H - 1]
        for h in range(H - 2, -1, -1):
            out = jnp.where(grp == h, vals[h], out)
        return out

    cos, sin, rot = cos_ref[...], sin_ref[...], rot_ref[...]
    tq = qr * per_group(q_rs) * qgr_ref[...]
    tq = tq * cos + _dot(tq, rot, HI) * sin
    tk = kpe4 * per_group(k_rs) * kgr_ref[...]
    tk = tk * cos + _dot(tk, rot, HI) * sin
    scale = MLA_DQK ** -0.5 * LOG2E
    half_id = lane // dr
    ones = _ones_lane0(n)
    q_parts, k_parts, v_parts = [], [], []
    for h in range(H):
        blk = slice((h // 2) * LANES, (h // 2 + 1) * LANES)
        q_parts.append(qq[:, h * dn:(h + 1) * dn] * q_rs[h] * (qgn_ref[...] * scale))
        q_parts.append(jnp.where(half_id == h % 2, tq[:, blk] * scale, 0.0))
        k_parts.append(kv[:, h * dn:(h + 1) * dn] * k_rs[h] * kgn_ref[...])
        k_parts.append(tk[:, blk])
        v_parts.append(kv[:, H * dn + h * MLA_V:H * dn + (h + 1) * MLA_V])
        v_parts.append(ones)
    q_ref[0] = jnp.concatenate(q_parts, axis=1).astype(q_ref.dtype)
    k_ref[0] = jnp.concatenate(k_parts, axis=1).astype(k_ref.dtype)
    v_ref[0] = jnp.concatenate(v_parts, axis=1).astype(v_ref.dtype)


def _rope_consts(T):
    H, dr = MLA_HEADS, MLA_ROPE
    inv = 1.0 / (ROPE_BASE ** (jnp.arange(0, dr, 2, dtype=F32) / dr))
    ang = jnp.arange(T, dtype=F32)[:, None] * inv[None, :]
    ang = jnp.concatenate([ang, ang], axis=-1)
    cos4 = jnp.tile(jnp.cos(ang), (1, H))
    sin4 = jnp.tile(jnp.sin(ang), (1, H))
    rot = np.zeros((H * dr, H * dr), np.float32)
    for h in range(H):
        for c in range(dr // 2):
            rot[h * dr + c + dr // 2, h * dr + c] = -1.0
            rot[h * dr + c, h * dr + c + dr // 2] = 1.0
    expand = np.zeros((LANES, H * dr), np.float32)
    for h in range(H):
        for c in range(dr):
            expand[L_KPE + c, h * dr + c] = 1.0
    return cos4, sin4, jnp.asarray(rot), jnp.asarray(expand)


def _mla_prep(proj, rope, qa_g, wq_b, kva_g, wkv_b, qn_g, kn_g):
    B, T, _ = proj.shape
    n, H, dn, dr, dv = ROW_TILE, MLA_HEADS, MLA_NOPE, MLA_ROPE, MLA_V
    cos4, sin4, rot, expand = rope
    wq = wq_b.reshape(MLA_Q_RANK, H, MLA_DQK)
    wq = jnp.concatenate([wq[:, :, :dn].reshape(MLA_Q_RANK, H * dn),
                          wq[:, :, dn:].reshape(MLA_Q_RANK, H * dr)], axis=1).astype(BF16)
    wkv = wkv_b.reshape(MLA_KV_RANK, H, dn + dv)
    wkv = jnp.concatenate([wkv[:, :, :dn].reshape(MLA_KV_RANK, H * dn),
                           wkv[:, :, dn:].reshape(MLA_KV_RANK, H * dv)], axis=1).astype(BF16)
    qg, kg = qn_g.astype(F32), kn_g.astype(F32)
    const = lambda shape: pl.BlockSpec(shape, lambda b, t: (0,) * len(shape))
    QW = H * 2 * LANES
    return pl.pallas_call(
        _mla_prep_kernel,
        out_shape=(jax.ShapeDtypeStruct((B, T, QW), BF16), jax.ShapeDtypeStruct((B, T, QW), BF16),
                   jax.ShapeDtypeStruct((B, T, QW), BF16)),
        grid=(B, T // n),
        in_specs=[pl.BlockSpec((1, n, MLA_Q_RANK), lambda b, t: (b, t, COL_MLA_QA // MLA_Q_RANK)),
                  pl.BlockSpec((1, n, MLA_KV_RANK), lambda b, t: (b, t, COL_MLA_KVA // MLA_KV_RANK)),
                  pl.BlockSpec((1, n, LANES), lambda b, t: (b, t, COL_SMALL // LANES)),
                  pl.BlockSpec((n, H * dr), lambda b, t: (t, 0)),
                  pl.BlockSpec((n, H * dr), lambda b, t: (t, 0)),
                  const((1, MLA_Q_RANK)), const(wq.shape), const((1, MLA_KV_RANK)), const(wkv.shape),
                  const((1, dn)), const((1, H * dr)), const((1, dn)), const((1, H * dr)),
                  const(rot.shape), const(expand.shape)],
        out_specs=(pl.BlockSpec((1, n, QW), lambda b, t: (b, t, 0)),
                   pl.BlockSpec((1, n, QW), lambda b, t: (b, t, 0)),
                   pl.BlockSpec((1, n, QW), lambda b, t: (b, t, 0))),
        compiler_params=_cparams(("parallel", "parallel")),
        name="mla_prep",
    )(proj, proj, proj, cos4, sin4,
      qa_g.reshape(1, -1).astype(F32), wq, kva_g.reshape(1, -1).astype(F32), wkv,
      qg[:dn].reshape(1, dn), jnp.tile(qg[dn:], H).reshape(1, H * dr),
      kg[:dn].reshape(1, dn), jnp.tile(kg[dn:], H).reshape(1, H * dr), rot, expand)


def _flash_kernel(q_ref, k_ref, v_ref, o_ref, m_ref, acc_ref, *, tq):
    qi = pl.program_id(2)
    nh = m_ref.shape[0]
    dqk, dv2 = q_ref.shape[-1] // nh, v_ref.shape[-1] // nh
    dv = dv2 // 2
    rb = ROW_TILE
    m_ref[...] = jnp.full(m_ref.shape, NEG, F32)
    acc_ref[...] = jnp.zeros(acc_ref.shape, F32)

    def step(j, masked):
        start = pl.multiple_of(j * tq, tq)
        chains = [(h, r) for h in range(nh) for r in range(0, tq, rb)]
        logits, probs = {}, {}

        def qk(c):
            h, r = chains[c]
            logits[c] = _dot_nt(q_ref[0, r:r + rb, h * dqk:(h + 1) * dqk],
                                k_ref[0, pl.ds(start, tq), h * dqk:(h + 1) * dqk])

        def softmax(c):
            h, r = chains[c]
            s = logits.pop(c)
            if masked:
                row = lax.broadcasted_iota(jnp.int32, s.shape, 0) + r
                col = lax.broadcasted_iota(jnp.int32, s.shape, 1)
                s = jnp.where(col <= row, s, NEG)
            lane_max = s[:, :LANES]
            for cb in range(1, tq // LANES):
                lane_max = jnp.maximum(lane_max, s[:, cb * LANES:(cb + 1) * LANES])
            m_old = m_ref[h, r:r + rb]
            m_new = jnp.maximum(m_old, jnp.max(lane_max, axis=-1, keepdims=True))
            m_ref[h, r:r + rb] = m_new
            probs[c] = (jnp.exp2(s - m_new).astype(BF16), jnp.exp2(m_old - m_new))

        def pv(c):
            h, r = chains[c]
            p, alpha = probs.pop(c)
            acc_ref[h, r:r + rb] = (alpha * acc_ref[h, r:r + rb]
                                    + _dot(p, v_ref[0, pl.ds(start, tq), h * dv2:(h + 1) * dv2]))

        for c in range(len(chains)):
            qk(c)
            softmax(c)
            pv(c)

    def body(jj, carry):
        step(2 * jj, False)
        step(2 * jj + 1, False)
        return carry

    lax.fori_loop(0, qi // 2, body, 0)

    @pl.when(qi % 2 == 1)
    def _():
        step(qi - 1, False)

    step(qi, True)
    outs = []
    for h in range(nh):
        acc = acc_ref[h]
        outs.append(acc[:, :dv] / acc[:, dv:dv + 1])
    o_ref[0] = jnp.concatenate(outs, axis=1).astype(o_ref.dtype)


def _flash(q, k, v):
    B, T, QW = q.shape
    H, nh = MLA_HEADS, FLASH_HEADS_PER_STEP
    dqk, dv2 = QW // H, v.shape[-1] // H
    dv = dv2 // 2
    tq = next(t for t in FLASH_TILES if T % t == 0)
    return pl.pallas_call(
        functools.partial(_flash_kernel, tq=tq),
        out_shape=jax.ShapeDtypeStruct((B, T, H * dv), BF16),
        grid=(B, H // nh, T // tq),
        in_specs=[pl.BlockSpec((1, tq, nh * dqk), lambda b, h, i: (b, i, h)),
                  pl.BlockSpec((1, T, nh * dqk), lambda b, h, i: (b, 0, h)),
                  pl.BlockSpec((1, T, nh * dv2), lambda b, h, i: (b, 0, h))],
        out_specs=pl.BlockSpec((1, tq, nh * dv), lambda b, h, i: (b, i, h)),
        scratch_shapes=[pltpu.VMEM((nh, tq, 1), F32), pltpu.VMEM((nh, tq, dv2), F32)],
        compiler_params=_cparams(("parallel", "parallel", "arbitrary")),
        name="flash",
    )(q, k, v)


def _reorder_kernel(w_ref, o_ref):
    w = w_ref[...]
    n = w.shape[0]
    z = lambda width: jnp.zeros((n, width), w.dtype)
    small = [w[:, 2048:2052], w[:, 2052:2056], w[:, 4424:4428], w[:, 5964:5972], w[:, 2824:2888]]
    n_small = sum(s.shape[1] for s in small)
    cols = [w[:, 0:2048],
            w[:, 4940:5964],
            w[:, 4428:4940],
            w[:, 2056:2824],
            *small, z(LANES - n_small), z(COL_FOX_QKV - COL_SMALL - LANES),
            w[:, 2888:4424]]
    o_ref[...] = jnp.concatenate(cols, axis=1).astype(o_ref.dtype)


def _reorder_w_in(w_in):
    D, W = w_in.shape
    rows = ROW_TILE
    return pl.pallas_call(
        _reorder_kernel,
        out_shape=jax.ShapeDtypeStruct((D, PROJ_W), BF16),
        grid=(D // rows,),
        in_specs=[pl.BlockSpec((rows, W), lambda i: (i, 0))],
        out_specs=pl.BlockSpec((rows, PROJ_W), lambda i: (i, 0)),
        compiler_params=_cparams(("parallel",)),
        name="reorder_w_in",
    )(w_in)


def _mixer(h, hn, B, T, rope, w_in_cols, gdn_conv_w, gdn_A_log, gdn_dt_bias, gdn_norm_g,
           mla_qa_g, mla_wq_b, mla_kva_g, mla_wkv_b, mla_qn_g, mla_kn_g,
           fox_qn_g, fox_kn_g, fox_b_f,
           ssd_conv_w, ssd_conv_b, ssd_dt_bias, ssd_A_log, ssd_D, ssd_norm_g,
           w_gate, w_branch, w_o, next_norm_g=None):
    M = B * T
    proj = _matmul(hn, w_in_cols, tn=1024, name="in_proj").reshape(B, T, PROJ_W)
    o_gdn = _gdn(proj, gdn_conv_w, gdn_A_log, gdn_dt_bias, gdn_norm_g)
    mq, mk, mv = _mla_prep(proj, rope, mla_qa_g, mla_wq_b, mla_kva_g, mla_wkv_b, mla_qn_g, mla_kn_g)
    o_mla = _flash(mq, mk, mv)
    o_fox = _flash(*_fox_prep(proj, fox_qn_g, fox_kn_g, fox_b_f))
    o_ssd = _ssd(proj, ssd_conv_w, ssd_conv_b, ssd_dt_bias, ssd_A_log, ssd_D, ssd_norm_g)
    branches = [o.reshape(M, BRANCH_W) for o in (o_gdn, o_mla, o_fox, o_ssd)]
    merged = _merge(hn, branches, w_gate.astype(BF16), w_branch.astype(BF16))
    return _matmul(merged, w_o.astype(BF16), tn=w_o.shape[1], tm=512, residual=h, norm_g=next_norm_g,
                   name="out_proj")


def kernel(x, meta_tokens, mix_norm_g, w_in, gdn_conv_w, gdn_A_log, gdn_dt_bias, gdn_norm_g, mla_qa_g, mla_wq_b, mla_kva_g, mla_wkv_b, mla_qn_g, mla_kn_g, fox_qn_g, fox_kn_g, fox_b_f, ssd_conv_w, ssd_conv_b, ssd_dt_bias, ssd_A_log, ssd_D, ssd_norm_g, w_gate, w_branch, w_o, ffn_norm_g, dense_w_gate, dense_w_up, dense_w_down, router_w, moe_w_gate, moe_w_up, moe_w_down):
    B, S, D = x.shape
    L = N_META + S
    T = -(-L // ROW_TILE) * ROW_TILE
    assert (B * T) % MM_TM == 0
    depth = w_in.shape[0]
    meta = jnp.broadcast_to(meta_tokens[None].astype(x.dtype), (B, N_META, D))
    h = jnp.concatenate([meta, x, jnp.zeros((B, T - L, D), x.dtype)], axis=1).reshape(B * T, D)
    rope = _rope_consts(T)
    hn = _rmsnorm(h, mix_norm_g[0])
    for layer in range(depth):
        dense = layer % 2 == 0
        next_mix_g = mix_norm_g[layer + 1] if layer + 1 < depth else None
        mixed = _mixer(h, hn, B, T, rope, _reorder_w_in(w_in[layer]),
                       gdn_conv_w[layer], gdn_A_log[layer], gdn_dt_bias[layer], gdn_norm_g[layer],
                       mla_qa_g[layer], mla_wq_b[layer], mla_kva_g[layer], mla_wkv_b[layer],
                       mla_qn_g[layer], mla_kn_g[layer],
                       fox_qn_g[layer], fox_kn_g[layer], fox_b_f[layer],
                       ssd_conv_w[layer], ssd_conv_b[layer], ssd_dt_bias[layer], ssd_A_log[layer],
                       ssd_D[layer], ssd_norm_g[layer],
                       w_gate[layer], w_branch[layer], w_o[layer],
                       next_norm_g=ffn_norm_g[layer] if dense else None)
        i = layer // 2
        if dense:
            h, hn = mixed
            act = _swiglu_up(hn, dense_w_gate[i].astype(BF16), dense_w_up[i].astype(BF16), tn=512)
            out = _matmul(act, dense_w_down[i].astype(BF16), tn=D, tm=FFN_DOWN_TM, residual=h,
                          norm_g=next_mix_g, w_buffers=1, name="ffn_down")
            h, hn = out if next_mix_g is not None else (out, None)
        else:
            h = mixed
            hn, rec, cnt_before = _rmsnorm_router(h, ffn_norm_g[layer], router_w[i], B, T)
            h = _moe(h, hn, rec, cnt_before, moe_w_gate[i].astype(BF16), moe_w_up[i].astype(BF16),
                     moe_w_down[i].astype(BF16), B, T)
            hn = _rmsnorm(h, next_mix_g) if next_mix_g is not None else None
    return h.reshape(B, T, D)[:, N_META:L].astype(x.dtype)
```

```python
import functools
import math

import numpy as np
import jax
import jax.numpy as jnp
from jax import lax
from jax.experimental import pallas as pl
from jax.experimental.pallas import tpu as pltpu

F32 = jnp.float32
BF16 = jnp.bfloat16
HI = lax.Precision.HIGHEST
NT_DIMS = (((1,), (1,)), ((), ()))

N_META = 16
EPS = 1e-6
NEG = -1e30
CONV_K = 4

GDN_HEADS, GDN_DK, GDN_DV = 4, 128, 128
MLA_HEADS, MLA_Q_RANK, MLA_KV_RANK, MLA_NOPE, MLA_ROPE, MLA_V = 4, 512, 256, 128, 64, 128
MLA_DQK = MLA_NOPE + MLA_ROPE
ROPE_BASE = 10000.0
FOX_HEADS, FOX_DH = 4, 128
SSD_HEADS, SSD_HEADDIM, SSD_GROUPS, SSD_STATE = 8, 64, 2, 128
SSD_HG = SSD_HEADS // SSD_GROUPS
SSD_INNER = SSD_HEADS * SSD_HEADDIM
N_BRANCH, BRANCH_W = 4, 512
N_EXPERTS, TOP_K = 8, 2

LANES = 128
ROW_TILE = 256
FLASH_TILES = (768, 256)
FLASH_HEADS_PER_STEP = 2
LOG2E = 1.4426950408889634
MM_TM = 768
FFN_DOWN_TM = 384
MOE_TM = 256
MOE_DISPATCH_ROWS = 128
R_E1, R_E2, R_P1, R_P2, R_R1, R_R2 = range(6)
VMEM_LIMIT = 56 * 1024 * 1024

_GDN_W, _FOX_W, _SSD_BC = GDN_HEADS * GDN_DK, FOX_HEADS * FOX_DH, SSD_GROUPS * SSD_STATE
IN_WIDTH = dict(gdn_q=_GDN_W, gdn_k=_GDN_W, gdn_v=GDN_HEADS * GDN_DV, gdn_z=GDN_HEADS * GDN_DV,
                gdn_b=GDN_HEADS, gdn_a=GDN_HEADS,
                mla_qa=MLA_Q_RANK, mla_kva=MLA_KV_RANK, mla_kpe=MLA_ROPE,
                fox_q=_FOX_W, fox_k=_FOX_W, fox_v=_FOX_W, fox_f=FOX_HEADS,
                ssd_z=SSD_INNER, ssd_x=SSD_INNER, ssd_b=_SSD_BC, ssd_c=_SSD_BC, ssd_dt=SSD_HEADS)
IN_OFFSET = dict(zip(IN_WIDTH, np.cumsum([0] + list(IN_WIDTH.values())[:-1]).tolist()))

PROJ_W = 6144
COL_GDN_QKV, COL_GDN_Z = 0, 1536
COL_SSD_XBC, COL_SSD_Z = 2048, 3072
COL_MLA_QA, COL_MLA_KVA = 3584, 4096
COL_SMALL = 4352
COL_FOX_QKV = 4608
L_BETA, L_GA, L_FF, L_DT, L_KPE = 0, 4, 8, 12, 20


def _cparams(sem, vmem=VMEM_LIMIT):
    return pltpu.CompilerParams(dimension_semantics=sem, vmem_limit_bytes=vmem)


def _softplus(x):
    return jnp.maximum(x, 0.0) + jnp.log1p(jnp.exp(-jnp.abs(x)))


def _silu(x):
    return x * jax.nn.sigmoid(x)


def _dot(a, b, precision=None):
    return jnp.dot(a, b, preferred_element_type=F32, precision=precision)


def _dot_nt(a, b):
    return lax.dot_general(a, b, NT_DIMS, preferred_element_type=F32)


def _rmsnorm_kernel(h_ref, g_ref, o_ref):
    x = h_ref[...]
    y = x * lax.rsqrt(jnp.mean(x * x, axis=-1, keepdims=True) + EPS) * g_ref[...]
    o_ref[...] = y.astype(o_ref.dtype)


def _rmsnorm(h, g):
    M, D = h.shape
    return pl.pallas_call(
        _rmsnorm_kernel,
        out_shape=jax.ShapeDtypeStruct((M, D), BF16),
        grid=(M // MM_TM,),
        in_specs=[pl.BlockSpec((MM_TM, D), lambda i: (i, 0)),
                  pl.BlockSpec((1, D), lambda i: (0, 0))],
        out_specs=pl.BlockSpec((MM_TM, D), lambda i: (i, 0)),
        compiler_params=_cparams(("parallel",)),
        name="rmsnorm",
    )(h, g.reshape(1, D).astype(F32))


def _frame_norm_kernel(meta_ref, xa_ref, xb_ref, g_ref, h_ref, hn_ref):
    t = pl.program_id(1)
    n = h_ref.shape[0]
    head = jnp.where(t == 0, meta_ref[...], xa_ref[0])
    body = jnp.where(t < pl.num_programs(1) - 1, xb_ref[0, :n - N_META], 0.0)
    x = jnp.concatenate([head, body], axis=0)
    h_ref[...] = x
    hn_ref[...] = (x * lax.rsqrt(jnp.mean(x * x, axis=-1, keepdims=True) + EPS) * g_ref[...]).astype(hn_ref.dtype)


def _frame_and_norm(x, meta, g, T):
    B, S, D = x.shape
    n = ROW_TILE
    assert S % n == 0 and T == S + n and meta.shape[0] == N_META and n % N_META == 0
    NT, per = T // n, n // N_META
    return pl.pallas_call(
        _frame_norm_kernel,
        out_shape=(jax.ShapeDtypeStruct((B * T, D), x.dtype), jax.ShapeDtypeStruct((B * T, D), BF16)),
        grid=(B, NT),
        in_specs=[pl.BlockSpec((N_META, D), lambda b, t: (0, 0)),
                  pl.BlockSpec((1, N_META, D), lambda b, t: (b, jnp.maximum(t * per - 1, 0), 0)),
                  pl.BlockSpec((1, n, D), lambda b, t: (b, jnp.minimum(t, S // n - 1), 0)),
                  pl.BlockSpec((1, D), lambda b, t: (0, 0))],
        out_specs=(pl.BlockSpec((n, D), lambda b, t: (b * NT + t, 0)),
                   pl.BlockSpec((n, D), lambda b, t: (b * NT + t, 0))),
        compiler_params=_cparams(("parallel", "parallel")),
        name="frame_norm",
    )(meta, x, x, g.reshape(1, D).astype(F32))


def _rmsnorm_router_kernel(h_ref, g_ref, rw_ref, o_ref, rec_ref, cnt_ref, run_ref):
    @pl.when(pl.program_id(1) == 0)
    def _():
        run_ref[...] = jnp.zeros_like(run_ref)

    x = h_ref[...]
    n = x.shape[0]
    y = x * lax.rsqrt(jnp.mean(x * x, axis=-1, keepdims=True) + EPS) * g_ref[...]
    y16 = y.astype(BF16)
    o_ref[...] = y16
    y_lo = (y - y16.astype(F32)).astype(BF16)
    both = _dot(y16, rw_ref[...])
    logits = both[:, :LANES] + both[:, LANES:] + _dot(y_lo, rw_ref[:, :LANES])
    lane = lax.broadcasted_iota(jnp.int32, logits.shape, 1)
    logits = jnp.where(lane < N_EXPERTS, logits, NEG)
    m1 = jnp.max(logits, axis=-1, keepdims=True)
    i1 = jnp.min(jnp.where(logits == m1, lane, LANES), axis=-1, keepdims=True)
    rest = jnp.where(lane == i1, NEG, logits)
    m2 = jnp.max(rest, axis=-1, keepdims=True)
    i2 = jnp.min(jnp.where(rest == m2, lane, LANES), axis=-1, keepdims=True)
    e2 = jnp.exp(m2 - m1)
    p1 = 1.0 / (1.0 + e2)
    p2 = e2 * p1
    sel = jnp.where(lane == i1, 1.0, 0.0) + jnp.where(lane == i2, 1.0, 0.0)
    row = lax.broadcasted_iota(jnp.int32, (n, n), 0)
    col = lax.broadcasted_iota(jnp.int32, (n, n), 1)
    earlier = jnp.where(col < row, 1.0, 0.0).astype(BF16)
    run = run_ref[0:1, :]
    before = _dot(earlier, sel.astype(BF16)) + run
    r1 = jnp.sum(jnp.where(lane == i1, before, 0.0), axis=-1, keepdims=True)
    r2 = jnp.sum(jnp.where(lane == i2, before, 0.0), axis=-1, keepdims=True)
    cnt_ref[...] = jnp.broadcast_to(run, cnt_ref.shape)
    run_ref[...] = jnp.broadcast_to(before[n - 1:n] + sel[n - 1:n], run_ref.shape)
    rec = jnp.zeros_like(logits)
    for ln, val in ((R_E1, i1.astype(F32)), (R_E2, i2.astype(F32)), (R_P1, p1), (R_P2, p2), (R_R1, r1), (R_R2, r2)):
        rec = jnp.where(lane == ln, val, rec)
    rec_ref[...] = rec


def _rmsnorm_router(h, g, router_w, B, T):
    M, D = h.shape
    n = ROW_TILE
    NT = T // n
    rw = jnp.zeros((D, LANES), F32).at[:, :N_EXPERTS].set(router_w.astype(F32))
    rw_hi = rw.astype(BF16)
    rw = jnp.concatenate([rw_hi, (rw - rw_hi.astype(F32)).astype(BF16)], axis=1)
    return pl.pallas_call(
        _rmsnorm_router_kernel,
        out_shape=(jax.ShapeDtypeStruct((M, D), BF16), jax.ShapeDtypeStruct((M, LANES), F32),
                   jax.ShapeDtypeStruct((B * NT * 8, LANES), F32)),
        grid=(B, NT),
        in_specs=[pl.BlockSpec((n, D), lambda b, t: (b * NT + t, 0)),
                  pl.BlockSpec((1, D), lambda b, t: (0, 0)),
                  pl.BlockSpec((D, 2 * LANES), lambda b, t: (0, 0))],
        out_specs=(pl.BlockSpec((n, D), lambda b, t: (b * NT + t, 0)),
                   pl.BlockSpec((n, LANES), lambda b, t: (b * NT + t, 0)),
                   pl.BlockSpec((8, LANES), lambda b, t: (b * NT + t, 0))),
        scratch_shapes=[pltpu.VMEM((8, LANES), F32)],
        compiler_params=_cparams(("arbitrary", "arbitrary")),
        name="rmsnorm_router",
    )(h, g.reshape(1, D).astype(F32), rw)


def _mm_kernel(*refs, has_res, has_norm):
    a_ref, w_ref = refs[:2]
    y = _dot(a_ref[...], w_ref[...])
    if has_res:
        y = y + refs[2][...]
    if has_norm:
        g_ref, o_ref, n_ref = refs[-3:]
        n_ref[...] = (y * lax.rsqrt(jnp.mean(y * y, axis=-1, keepdims=True) + EPS) * g_ref[...]).astype(n_ref.dtype)
    else:
        o_ref = refs[-1]
    o_ref[...] = y.astype(o_ref.dtype)


def _matmul(a, w, *, tn, tm=MM_TM, residual=None, norm_g=None, out_dtype=F32, w_buffers=2, name="matmul"):
    M, K = a.shape
    N = w.shape[1]
    in_specs = [pl.BlockSpec((tm, K), lambda j, i: (i, 0)),
                pl.BlockSpec((K, tn), lambda j, i: (0, j), pipeline_mode=pl.Buffered(w_buffers))]
    args = [a, w]
    out_shape = jax.ShapeDtypeStruct((M, N), out_dtype)
    out_specs = pl.BlockSpec((tm, tn), lambda j, i: (i, j))
    if residual is not None:
        in_specs.append(pl.BlockSpec((tm, tn), lambda j, i: (i, j)))
        args.append(residual)
    if norm_g is not None:
        assert tn == N
        in_specs.append(pl.BlockSpec((1, N), lambda j, i: (0, 0)))
        args.append(norm_g.reshape(1, N).astype(F32))
        out_shape = (out_shape, jax.ShapeDtypeStruct((M, N), BF16))
        out_specs = (out_specs, pl.BlockSpec((tm, tn), lambda j, i: (i, j)))
    return pl.pallas_call(
        functools.partial(_mm_kernel, has_res=residual is not None, has_norm=norm_g is not None),
        out_shape=out_shape,
        grid=(N // tn, M // tm),
        in_specs=in_specs,
        out_specs=out_specs,
        compiler_params=_cparams(("parallel", "parallel")),
        name=name,
    )(*args)


def _swiglu_kernel(a_ref, wg_ref, wu_ref, o_ref):
    a = a_ref[...]
    g = _dot(a, wg_ref[...])
    u = _dot(a, wu_ref[...])
    o_ref[...] = (_silu(g) * u).astype(o_ref.dtype)


def _swiglu_up(a, wg, wu, *, tn, tm=MM_TM):
    M, K = a.shape
    F = wg.shape[1]
    return pl.pallas_call(
        _swiglu_kernel,
        out_shape=jax.ShapeDtypeStruct((M, F), BF16),
        grid=(F // tn, M // tm),
        in_specs=[pl.BlockSpec((tm, K), lambda j, i: (i, 0)),
                  pl.BlockSpec((K, tn), lambda j, i: (0, j)),
                  pl.BlockSpec((K, tn), lambda j, i: (0, j))],
        out_specs=pl.BlockSpec((tm, tn), lambda j, i: (i, j)),
        compiler_params=_cparams(("parallel", "parallel")),
        name="swiglu_up",
    )(a, wg, wu)


def _dispatch_kernel(lo_ref, nb_ref, d1_ref, d2_ref, rec_ref, hn_ref, o_ref, p_ref, acc_ref, pacc_ref, *, wb):
    i = pl.program_id(1)
    idx = pl.program_id(0) * pl.num_programs(1) + i
    tm = o_ref.shape[0]
    rows = i * tm + lax.broadcasted_iota(jnp.int32, (tm, wb), 0)
    lane = lax.broadcasted_iota(jnp.int32, (wb, LANES), 1)
    acc_ref[...] = jnp.zeros_like(acc_ref)
    pacc_ref[...] = jnp.zeros_like(pacc_ref)

    def hi_lo(p):
        hi = p.astype(BF16).astype(F32)
        return jnp.where(lane == 0, hi, jnp.where(lane == 1, p - hi, 0.0)).astype(BF16)

    def body(k, carry):
        start = pl.multiple_of((lo_ref[idx] + k) * wb, wb)
        hit1 = jnp.where(d1_ref[0, :, pl.ds(start, wb)] == rows, 1.0, 0.0)
        hit2 = jnp.where(d2_ref[0, :, pl.ds(start, wb)] == rows, 1.0, 0.0)
        acc_ref[...] += _dot((hit1 + hit2).astype(BF16), hn_ref[0, pl.ds(start, wb), :]).astype(BF16)
        rec = rec_ref[0, pl.ds(start, wb), :]
        pacc_ref[...] += (_dot(hit1.astype(BF16), hi_lo(rec[:, R_P1:R_P1 + 1]))
                          + _dot(hit2.astype(BF16), hi_lo(rec[:, R_P2:R_P2 + 1])))
        return carry

    lax.fori_loop(0, nb_ref[idx], body, 0)
    o_ref[...] = acc_ref[...].astype(o_ref.dtype)
    p_ref[...] = pacc_ref[...]


def _dispatch(hn3, rec3, d1, d2, lo_blk, n_blk, nti, *, tm, wb=ROW_TILE):
    B, T, D = hn3.shape
    dspec = pl.BlockSpec((1, 1, T), lambda b, i, lo, nb: (b, 0, 0))
    row_spec = lambda w: pl.BlockSpec((tm, w), lambda b, i, lo, nb: (b * nti + i, 0))
    grid_spec = pltpu.PrefetchScalarGridSpec(
        num_scalar_prefetch=2,
        grid=(B, nti),
        in_specs=[dspec, dspec,
                  pl.BlockSpec((1, T, LANES), lambda b, i, lo, nb: (b, 0, 0)),
                  pl.BlockSpec((1, T, D), lambda b, i, lo, nb: (b, 0, 0), pipeline_mode=pl.Buffered(1))],
        out_specs=(row_spec(D), row_spec(LANES)),
        scratch_shapes=[pltpu.VMEM((tm, D), BF16), pltpu.VMEM((tm, LANES), F32)])
    return pl.pallas_call(
        functools.partial(_dispatch_kernel, wb=wb),
        out_shape=(jax.ShapeDtypeStruct((B * nti * tm, D), BF16),
                   jax.ShapeDtypeStruct((B * nti * tm, LANES), F32)),
        grid_spec=grid_spec,
        compiler_params=_cparams(("arbitrary", "arbitrary")),
        name="moe_dispatch",
    )(lo_blk, n_blk, d1.reshape(B, 1, T), d2.reshape(B, 1, T), rec3, hn3)


def _expert_ffn_kernel(te_ref, nb_ref, x_ref, wg_ref, wu_ref, wd_ref, p_ref, o_ref):
    i = pl.program_id(0)

    @pl.when(nb_ref[i] > 0)
    def _():
        x = x_ref[...]
        p = p_ref[:, 0:1] + p_ref[:, 1:2]
        act = (_silu(_dot(x, wg_ref[0])) * _dot(x, wu_ref[0]) * p).astype(BF16)
        o_ref[...] = _dot(act, wd_ref[0]).astype(o_ref.dtype)

    @pl.when(nb_ref[i] == 0)
    def _():
        o_ref[...] = jnp.zeros_like(o_ref)


def _expert_ffn(x, p_rows, tile_expert, n_blk, wg, wu, wd, *, tm):
    R, D = x.shape
    E, _, F = wg.shape
    grid_spec = pltpu.PrefetchScalarGridSpec(
        num_scalar_prefetch=2,
        grid=(R // tm,),
        in_specs=[pl.BlockSpec((tm, D), lambda i, te, nb: (i, 0)),
                  pl.BlockSpec((1, D, F), lambda i, te, nb: (te[i], 0, 0)),
                  pl.BlockSpec((1, D, F), lambda i, te, nb: (te[i], 0, 0)),
                  pl.BlockSpec((1, F, D), lambda i, te, nb: (te[i], 0, 0)),
                  pl.BlockSpec((tm, LANES), lambda i, te, nb: (i, 0))],
        out_specs=pl.BlockSpec((tm, D), lambda i, te, nb: (i, 0)))
    return pl.pallas_call(
        _expert_ffn_kernel,
        out_shape=jax.ShapeDtypeStruct((R, D), BF16),
        grid_spec=grid_spec,
        compiler_params=_cparams(("arbitrary",)),
        name="moe_ffn",
    )(tile_expert, n_blk, x, wg, wu, wd, p_rows)


def _combine_kernel(yb_ref, yv_ref, h_ref, d1_ref, d2_ref, *refs, tm):
    y_refs, o_ref = refs[:-1], refs[-1]
    n = o_ref.shape[0]
    slots = len(y_refs) // N_EXPERTS
    base = (pl.program_id(0) * pl.num_programs(1) + pl.program_id(1)) * len(y_refs)
    d1, d2 = d1_ref[...], d2_ref[...]
    lane = lax.broadcasted_iota(jnp.int32, (n, tm), 1)

    def picked(k, weight):
        rows = yb_ref[base + k] * tm + lane
        hit = jnp.where(d1 == rows, weight, 0.0) + jnp.where(d2 == rows, weight, 0.0)
        return _dot(hit.astype(BF16), y_refs[k][...])

    acc = h_ref[...]
    for k in range(0, len(y_refs), slots):
        acc = acc + picked(k, jnp.where(yv_ref[base + k] > 0, 1.0, 0.0))
    o_ref[...] = acc
    for k in range(len(y_refs)):
        if k % slots:
            @pl.when(yv_ref[base + k] > 0)
            def _():
                o_ref[...] += picked(k, 1.0)


def _combine(h, y, d1, d2, y_blk, y_valid, B, T, *, tm):
    M, D = h.shape
    n = ROW_TILE
    NT = T // n
    slots = y_blk.shape[0] // (B * NT)
    y_specs = [pl.BlockSpec((tm, D), lambda b, t, yb, yv, k=k: (yb[(b * NT + t) * slots + k], 0))
               for k in range(slots)]
    tok = lambda w: pl.BlockSpec((n, w), lambda b, t, yb, yv: (b * NT + t, 0))
    grid_spec = pltpu.PrefetchScalarGridSpec(
        num_scalar_prefetch=2,
        grid=(B, NT),
        in_specs=[tok(D), tok(1), tok(1)] + y_specs,
        out_specs=tok(D))
    return pl.pallas_call(
        functools.partial(_combine_kernel, tm=tm),
        out_shape=jax.ShapeDtypeStruct((M, D), F32),
        grid_spec=grid_spec,
        compiler_params=_cparams(("arbitrary", "arbitrary")),
        name="moe_combine",
    )(y_blk, y_valid, h, d1, d2, *([y] * slots))


def _moe(h, hn, rec, cnt_before, wg, wu, wd, B, T):
    M, D = h.shape
    E, tm, n = N_EXPERTS, MOE_TM, ROW_TILE
    NT = T // n
    rb = TOP_K * T + E * tm
    i32 = jnp.int32
    col = lambda ln: rec[:, ln].reshape(B, T)
    e1, e2 = col(R_E1).astype(i32), col(R_E2).astype(i32)
    r1, r2 = col(R_R1).astype(i32), col(R_R2).astype(i32)
    cntb = cnt_before.reshape(B, NT, 8, LANES)[:, :, 0, :E].astype(i32)
    oh1, oh2 = jax.nn.one_hot(e1, E, dtype=i32), jax.nn.one_hot(e2, E, dtype=i32)
    counts = (oh1 + oh2).sum(axis=1)
    padded = (counts + tm - 1) // tm * tm
    ends = jnp.cumsum(padded, axis=1)
    off = ends - padded
    d1 = (oh1 * off[:, None, :]).sum(-1) + r1
    d2 = (oh2 * off[:, None, :]).sum(-1) + r2
    cnt_end = jnp.concatenate([cntb[:, 1:], counts[:, None, :]], axis=1)
    def tiles(rows):
        start = jnp.arange(rb // rows, dtype=i32)[None, :] * rows
        expert = jnp.minimum((start[:, :, None] >= ends[:, None, :]).sum(-1), E - 1).astype(i32)
        e_oh = jax.nn.one_hot(expert, E, dtype=i32)
        pick = lambda per_expert: (e_oh * per_expert[:, None, :]).sum(-1)
        rank_lo = start - pick(off)
        rank_hi = jnp.minimum(rank_lo + rows, pick(counts)) - 1
        through = (e_oh[:, :, None, :] * cnt_end[:, None, :, :]).sum(-1)
        lo_blk = (through <= rank_lo[..., None]).sum(-1)
        hi_blk = (through <= rank_hi[..., None]).sum(-1)
        nonempty = (start < pick(ends)) & (rank_hi >= rank_lo)
        return (expert.reshape(-1), jnp.where(nonempty, lo_blk, 0).astype(i32).reshape(-1),
                jnp.where(nonempty, hi_blk - lo_blk + 1, 0).astype(i32).reshape(-1))

    pr = MOE_DISPATCH_ROWS
    _, lo_blk, n_blk = tiles(pr)
    x, p_rows = _dispatch(hn.reshape(B, T, D), rec.reshape(B, T, LANES), d1, d2, lo_blk, n_blk, rb // pr, tm=pr)
    tile_expert, _, tile_used = tiles(tm)
    y = _expert_ffn(x, p_rows, tile_expert, tile_used, wg, wu, wd, tm=tm)
    first = off[:, None, :] + cntb
    last = off[:, None, :] + cnt_end - 1
    used = cnt_end > cntb
    span = jnp.arange(n // tm + 1, dtype=i32)
    blk = first[..., None] // tm + span
    valid = used[..., None] & (blk <= last[..., None] // tm)
    blk = jnp.minimum(blk, rb // tm - 1) + (jnp.arange(B, dtype=i32) * (rb // tm))[:, None, None, None]
    row0 = (jnp.arange(B, dtype=i32) * rb)[:, None]
    return _combine(h, y, (d1 + row0).reshape(M, 1), (d2 + row0).reshape(M, 1),
                    blk.reshape(-1).astype(i32), valid.reshape(-1).astype(i32), B, T, tm=tm)


def _merge_kernel(hn_ref, b0_ref, b1_ref, b2_ref, b3_ref, wg_ref, wb_ref, o_ref):
    hn = hn_ref[...]
    acc = None
    for b, br_ref in enumerate((b0_ref, b1_ref, b2_ref, b3_ref)):
        gate = jax.nn.sigmoid(_dot(hn, wg_ref[b]))
        term = gate * _dot(br_ref[...], wb_ref[b])
        acc = term if acc is None else acc + term
    o_ref[...] = acc.astype(o_ref.dtype)


def _merge(hn, branches, wg, wb, *, tn=512, tm=512):
    M, D = hn.shape
    N = wg.shape[2]
    bspec = pl.BlockSpec((tm, BRANCH_W), lambda j, i: (i, 0))
    return pl.pallas_call(
        _merge_kernel,
        out_shape=jax.ShapeDtypeStruct((M, N), BF16),
        grid=(N // tn, M // tm),
        in_specs=[pl.BlockSpec((tm, D), lambda j, i: (i, 0)), bspec, bspec, bspec, bspec,
                  pl.BlockSpec((N_BRANCH, D, tn), lambda j, i: (0, 0, j)),
                  pl.BlockSpec((N_BRANCH, BRANCH_W, tn), lambda j, i: (0, 0, j))],
        out_specs=pl.BlockSpec((tm, tn), lambda j, i: (i, j)),
        compiler_params=_cparams(("parallel", "parallel")),
        name="gate_merge",
    )(hn, *branches, wg, wb)


def _causal_conv(x, carry_ref, cw):
    n = x.shape[0]
    xext = jnp.concatenate([carry_ref[...], x], axis=0)
    y = cw[0:1] * xext[5:5 + n]
    for i in range(1, CONV_K):
        y = y + cw[i:i + 1] * xext[5 + i:5 + i + n]
    carry_ref[...] = x[n - 8:n]
    return y


def _causal_mask(n):
    row = lax.broadcasted_iota(jnp.int32, (n, n), 0)
    col = lax.broadcasted_iota(jnp.int32, (n, n), 1)
    return col <= row


def _gdn_kernel(qkv_ref, z_ref, sm_ref, cw_ref, alog_ref, dtb_ref, ng_ref, o_ref, s_ref, carry_ref):
    @pl.when(pl.program_id(1) == 0)
    def _():
        s_ref[...] = jnp.zeros_like(s_ref)
        carry_ref[...] = jnp.zeros_like(carry_ref)

    n = ROW_TILE
    y = _silu(_causal_conv(qkv_ref[0], carry_ref, cw_ref[...]))
    sm = sm_ref[0]
    z = z_ref[0]
    beta_all = jax.nn.sigmoid(sm)
    g_all = -jnp.exp(alog_ref[...]) * _softplus(sm + dtb_ref[...])
    row = lax.broadcasted_iota(jnp.int32, (n, n), 0)
    col = lax.broadcasted_iota(jnp.int32, (n, n), 1)
    causal, strict = col <= row, col < row
    diff_bits = row ^ col
    levels = int(math.log2(n))
    level_masks = [(diff_bits >= (1 << l)) & (diff_bits < (2 << l)) for l in range(levels)]
    eye = jnp.where(row == col, 1.0, 0.0)
    gcs_all = _dot(causal.astype(F32), g_all, HI)
    gcs_t = gcs_all.T
    heads = []
    for h in range(GDN_HEADS):
        q = y[:, h * GDN_DK:(h + 1) * GDN_DK]
        k = y[:, GDN_HEADS * GDN_DK + h * GDN_DK:GDN_HEADS * GDN_DK + (h + 1) * GDN_DK]
        v = y[:, 2 * GDN_HEADS * GDN_DK + h * GDN_DV:2 * GDN_HEADS * GDN_DK + (h + 1) * GDN_DV]
        q = q * lax.rsqrt(jnp.sum(q * q, axis=-1, keepdims=True) + EPS) * GDN_DK ** -0.5
        k = k * lax.rsqrt(jnp.sum(k * k, axis=-1, keepdims=True) + EPS)
        beta = beta_all[:, L_BETA + h:L_BETA + h + 1]
        gc = gcs_all[:, L_GA + h:L_GA + h + 1]
        gr = gcs_t[L_GA + h:L_GA + h + 1, :]
        decay = jnp.exp(jnp.where(causal, gc - gr, NEG))
        kb = k * beta
        k16 = k.astype(BF16)
        a = jnp.where(strict, _dot_nt(kb.astype(BF16), k16) * decay, 0.0)
        heads.append(dict(q=q, k=k, v=v, beta=beta, gc=gc, decay=decay, kb=kb, k16=k16, a=a,
                          t=eye - jnp.where(level_masks[0], a, 0.0)))
    for l in range(1, levels):
        for hd in heads:
            t16 = hd["t"].astype(BF16)
            a_l = jnp.where(level_masks[l], hd["a"], 0.0).astype(BF16)
            hd["t"] = hd["t"] - _dot(_dot(t16, a_l).astype(BF16), t16)
    outs = []
    for h, hd in enumerate(heads):
        q, k, gc = hd["q"], hd["k"], hd["gc"]
        egc = jnp.exp(gc)
        rhs = jnp.concatenate([hd["v"] * hd["beta"], hd["kb"] * egc], axis=1)
        uw = _dot(hd["t"].astype(BF16), rhs.astype(BF16))
        u, w = uw[:, :GDN_DV], uw[:, GDN_DV:]
        att = _dot_nt(q.astype(BF16), hd["k16"]) * hd["decay"]
        g_last = gc[n - 1:n, :]
        kd = k * jnp.exp(g_last - gc)
        s = s_ref[h]
        ws = _dot(jnp.concatenate([w, q * egc], axis=0).astype(BF16), s.astype(BF16))
        v_new = (u - ws[:n]).astype(BF16)
        o = ws[n:] + _dot(att.astype(BF16), v_new)
        s_ref[h] = s * jnp.exp(g_last) + _dot(kd.T.astype(BF16), v_new)
        o = o * lax.rsqrt(jnp.mean(o * o, axis=-1, keepdims=True) + EPS) * ng_ref[...]
        outs.append(o * _silu(z[:, h * GDN_DV:(h + 1) * GDN_DV]))
    o_ref[0] = jnp.concatenate(outs, axis=1).astype(o_ref.dtype)


def _lane_vec(vals, lane0):
    v = jnp.zeros((1, LANES), F32)
    return v.at[0, lane0:lane0 + vals.shape[0]].set(vals.astype(F32))


def _gdn(proj, conv_w, a_log, dt_bias, norm_g):
    B, T, _ = proj.shape
    n = ROW_TILE
    W = 2 * GDN_HEADS * GDN_DK + GDN_HEADS * GDN_DV
    ZW = GDN_HEADS * GDN_DV
    return pl.pallas_call(
        _gdn_kernel,
        out_shape=jax.ShapeDtypeStruct((B, T, ZW), BF16),
        grid=(B, T // n),
        in_specs=[pl.BlockSpec((1, n, W), lambda b, t: (b, t, COL_GDN_QKV // W)),
                  pl.BlockSpec((1, n, ZW), lambda b, t: (b, t, COL_GDN_Z // ZW)),
                  pl.BlockSpec((1, n, LANES), lambda b, t: (b, t, COL_SMALL // LANES)),
                  pl.BlockSpec((CONV_K, W), lambda b, t: (0, 0)),
                  pl.BlockSpec((1, LANES), lambda b, t: (0, 0)),
                  pl.BlockSpec((1, LANES), lambda b, t: (0, 0)),
                  pl.BlockSpec((1, GDN_DV), lambda b, t: (0, 0))],
        out_specs=pl.BlockSpec((1, n, ZW), lambda b, t: (b, t, 0)),
        scratch_shapes=[pltpu.VMEM((GDN_HEADS, GDN_DK, GDN_DV), F32),
                        pltpu.VMEM((8, W), F32)],
        compiler_params=_cparams(("arbitrary", "arbitrary")),
        name="gdn",
    )(proj, proj, proj, conv_w.astype(F32), _lane_vec(a_log, L_GA), _lane_vec(dt_bias, L_GA),
      norm_g.reshape(1, GDN_DV).astype(F32))


def _ssd_kernel(xbc_ref, z_ref, sm_ref, cw_ref, cb_ref, alog_ref, dtb_ref, dvec_ref, ng_ref,
                o_ref, hs_ref, carry_ref):
    @pl.when(pl.program_id(1) == 0)
    def _():
        hs_ref[...] = jnp.zeros_like(hs_ref)
        carry_ref[...] = jnp.zeros_like(carry_ref)

    n, P, N = ROW_TILE, SSD_HEADDIM, SSD_STATE
    y = _silu(_causal_conv(xbc_ref[0], carry_ref, cw_ref[...]) + cb_ref[...])
    xs = y[:, :SSD_INNER]
    bm = y[:, SSD_INNER:SSD_INNER + SSD_GROUPS * N]
    cm = y[:, SSD_INNER + SSD_GROUPS * N:]
    sm = sm_ref[0]
    dt_all = _softplus(sm + dtb_ref[...])
    a_all = dt_all * (-jnp.exp(alog_ref[...]))
    causal = _causal_mask(n)
    acs_all = _dot(causal.astype(F32), a_all, HI)
    acs_t = acs_all.T
    ys = []
    for g in range(SSD_GROUPS):
        bg = bm[:, g * N:(g + 1) * N]
        cg16 = cm[:, g * N:(g + 1) * N].astype(BF16)
        cb = _dot_nt(cg16, bg.astype(BF16))
        bgt16 = bg.T.astype(BF16)
        for j in range(SSD_HG):
            hh = g * SSD_HG + j
            ac = acs_all[:, L_DT + hh:L_DT + hh + 1]
            ar = acs_t[L_DT + hh:L_DT + hh + 1, :]
            lmat = jnp.exp(jnp.where(causal, ac - ar, NEG))
            xdt = xs[:, hh * P:(hh + 1) * P] * dt_all[:, L_DT + hh:L_DT + hh + 1]
            y_diag = _dot((cb * lmat).astype(BF16), xdt.astype(BF16))
            a_last = ac[n - 1:n, :]
            st = _dot(bgt16, (xdt * jnp.exp(a_last - ac)).astype(BF16))
            h_prev = hs_ref[hh]
            y_off = _dot(cg16, h_prev.astype(BF16)) * jnp.exp(ac)
            hs_ref[hh] = h_prev * jnp.exp(a_last) + st
            ys.append(y_diag + y_off)
    yy = jnp.concatenate(ys, axis=1) + xs * dvec_ref[...]
    yy = yy * _silu(z_ref[0])
    gw = SSD_HG * P
    outs = []
    for g in range(SSD_GROUPS):
        seg = yy[:, g * gw:(g + 1) * gw]
        outs.append(seg * lax.rsqrt(jnp.mean(seg * seg, axis=-1, keepdims=True) + EPS)
                    * ng_ref[:, g * gw:(g + 1) * gw])
    o_ref[0] = jnp.concatenate(outs, axis=1).astype(o_ref.dtype)


def _ssd(proj, conv_w, conv_b, dt_bias, a_log, d_skip, norm_g):
    B, T, _ = proj.shape
    n = ROW_TILE
    W = SSD_INNER + 2 * SSD_GROUPS * SSD_STATE
    dvec = jnp.repeat(d_skip.astype(F32), SSD_HEADDIM).reshape(1, SSD_INNER)
    return pl.pallas_call(
        _ssd_kernel,
        out_shape=jax.ShapeDtypeStruct((B, T, SSD_INNER), BF16),
        grid=(B, T // n),
        in_specs=[pl.BlockSpec((1, n, W), lambda b, t: (b, t, COL_SSD_XBC // W)),
                  pl.BlockSpec((1, n, SSD_INNER), lambda b, t: (b, t, COL_SSD_Z // SSD_INNER)),
                  pl.BlockSpec((1, n, LANES), lambda b, t: (b, t, COL_SMALL // LANES)),
                  pl.BlockSpec((CONV_K, W), lambda b, t: (0, 0)),
                  pl.BlockSpec((1, W), lambda b, t: (0, 0)),
                  pl.BlockSpec((1, LANES), lambda b, t: (0, 0)),
                  pl.BlockSpec((1, LANES), lambda b, t: (0, 0)),
                  pl.BlockSpec((1, SSD_INNER), lambda b, t: (0, 0)),
                  pl.BlockSpec((1, SSD_INNER), lambda b, t: (0, 0))],
        out_specs=pl.BlockSpec((1, n, SSD_INNER), lambda b, t: (b, t, 0)),
        scratch_shapes=[pltpu.VMEM((SSD_HEADS, SSD_STATE, SSD_HEADDIM), F32),
                        pltpu.VMEM((8, W), F32)],
        compiler_params=_cparams(("arbitrary", "arbitrary")),
        name="ssd",
    )(proj, proj, proj, conv_w.astype(F32), conv_b.reshape(1, W).astype(F32),
      _lane_vec(a_log, L_DT), _lane_vec(dt_bias, L_DT), dvec,
      norm_g.reshape(1, SSD_INNER).astype(F32))


def _ones_lane0(n):
    lane = lax.broadcasted_iota(jnp.int32, (n, LANES), 1)
    return jnp.where(lane == 0, 1.0, 0.0)


def _fox_prep_kernel(qkv_ref, sm_ref, qg_ref, kg_ref, bf_ref, q_ref, k_ref, v_ref, run_ref):
    @pl.when(pl.program_id(1) == 0)
    def _():
        run_ref[...] = jnp.zeros_like(run_ref)

    n, dh, H = ROW_TILE, FOX_DH, FOX_HEADS
    x = qkv_ref[0]
    log_f = -_softplus(-(sm_ref[0] + bf_ref[...]))
    causal = _causal_mask(n)
    cum = _dot(causal.astype(F32), log_f, HI) + run_ref[0:1, :]
    run_ref[...] = jnp.broadcast_to(cum[n - 1:n, :], run_ref.shape)
    lane = lax.broadcasted_iota(jnp.int32, (n, LANES), 1)
    ones = _ones_lane0(n)
    q_bias = jnp.where(lane < 3, 1.0, 0.0)
    qs, ks, vs = [], [], []
    for h in range(H):
        q = x[:, h * dh:(h + 1) * dh]
        k = x[:, H * dh + h * dh:H * dh + (h + 1) * dh]
        qs.append(q * lax.rsqrt(jnp.mean(q * q, axis=-1, keepdims=True) + EPS)
                  * (qg_ref[...] * (dh ** -0.5 * LOG2E)))
        qs.append(q_bias)
        ks.append(k * lax.rsqrt(jnp.mean(k * k, axis=-1, keepdims=True) + EPS) * kg_ref[...])
        c = cum[:, L_FF + h:L_FF + h + 1] * (-LOG2E)
        c_hi = c.astype(BF16).astype(F32)
        c_mid = (c - c_hi).astype(BF16).astype(F32)
        c_lo = c - c_hi - c_mid
        ks.append(jnp.where(lane == 0, c_hi, jnp.where(lane == 1, c_mid, jnp.where(lane == 2, c_lo, 0.0))))
        vs.append(x[:, 2 * H * dh + h * dh:2 * H * dh + (h + 1) * dh])
        vs.append(ones)
    q_ref[0] = jnp.concatenate(qs, axis=1).astype(q_ref.dtype)
    k_ref[0] = jnp.concatenate(ks, axis=1).astype(k_ref.dtype)
    v_ref[0] = jnp.concatenate(vs, axis=1).astype(v_ref.dtype)


def _fox_prep(proj, qn_g, kn_g, b_f):
    B, T, _ = proj.shape
    n = ROW_TILE
    W = 3 * FOX_HEADS * FOX_DH
    HW = FOX_HEADS * 2 * LANES
    ospec = pl.BlockSpec((1, n, HW), lambda b, t: (b, t, 0))
    return pl.pallas_call(
        _fox_prep_kernel,
        out_shape=(jax.ShapeDtypeStruct((B, T, HW), BF16),) * 3,
        grid=(B, T // n),
        in_specs=[pl.BlockSpec((1, n, W), lambda b, t: (b, t, COL_FOX_QKV // W)),
                  pl.BlockSpec((1, n, LANES), lambda b, t: (b, t, COL_SMALL // LANES)),
                  pl.BlockSpec((1, FOX_DH), lambda b, t: (0, 0)),
                  pl.BlockSpec((1, FOX_DH), lambda b, t: (0, 0)),
                  pl.BlockSpec((1, LANES), lambda b, t: (0, 0))],
        out_specs=(ospec, ospec, ospec),
        scratch_shapes=[pltpu.VMEM((8, LANES), F32)],
        compiler_params=_cparams(("arbitrary", "arbitrary")),
        name="fox_prep",
    )(proj, proj, qn_g.reshape(1, FOX_DH).astype(F32), kn_g.reshape(1, FOX_DH).astype(F32),
      _lane_vec(b_f, L_FF))


def _mla_prep_kernel(qa_ref, kva_ref, sm_ref, cos_ref, sin_ref, qag_ref, wq_ref, kvag_ref, wkv_ref,
                     qgn_ref, qgr_ref, kgn_ref, kgr_ref, rot_ref, exp_ref, q_ref, k_ref, v_ref):
    n, H, dn, dr = ROW_TILE, MLA_HEADS, MLA_NOPE, MLA_ROPE
    qa = qa_ref[0]
    qa = qa * lax.rsqrt(jnp.mean(qa * qa, axis=-1, keepdims=True) + EPS) * qag_ref[...]
    qq = _dot(qa.astype(BF16), wq_ref[...])
    kva = kva_ref[0]
    kva = kva * lax.rsqrt(jnp.mean(kva * kva, axis=-1, keepdims=True) + EPS) * kvag_ref[...]
    kv = _dot(kva.astype(BF16), wkv_ref[...])
    sm = sm_ref[0]
    lane = lax.broadcasted_iota(jnp.int32, (n, LANES), 1)
    is_kpe = (lane >= L_KPE) & (lane < L_KPE + dr)
    kpe_ss = jnp.sum(jnp.where(is_kpe, sm * sm, 0.0), axis=-1, keepdims=True)
    kpe4 = _dot(sm, exp_ref[...], HI)
    qr = qq[:, H * dn:]
    grp = lax.broadcasted_iota(jnp.int32, (n, H * dr), 1) // dr
    q_rs, k_rs = [], []
    for h in range(H):
        qn = qq[:, h * dn:(h + 1) * dn]
        ssr = jnp.sum(jnp.where(grp == h, qr * qr, 0.0), axis=-1, keepdims=True)
        q_rs.append(lax.rsqrt((jnp.sum(qn * qn, axis=-1, keepdims=True) + ssr) / MLA_DQK + EPS))
        kn = kv[:, h * dn:(h + 1) * dn]
        k_rs.append(lax.rsqrt((jnp.sum(kn * kn, axis=-1, keepdims=True) + kpe_ss) / MLA_DQK + EPS))

    def per_group(vals):
        out = vals[H - 1]
        for h in range(H - 2, -1, -1):
            out = jnp.where(grp == h, vals[h], out)
        return out

    cos, sin, rot = cos_ref[...], sin_ref[...], rot_ref[...]
    tq = qr * per_group(q_rs) * qgr_ref[...]
    tq = tq * cos + _dot(tq, rot, HI) * sin
    tk = kpe4 * per_group(k_rs) * kgr_ref[...]
    tk = tk * cos + _dot(tk, rot, HI) * sin
    scale = MLA_DQK ** -0.5 * LOG2E
    half_id = lane // dr
    ones = _ones_lane0(n)
    q_parts, k_parts, v_parts = [], [], []
    for h in range(H):
        blk = slice((h // 2) * LANES, (h // 2 + 1) * LANES)
        q_parts.append(qq[:, h * dn:(h + 1) * dn] * q_rs[h] * (qgn_ref[...] * scale))
        q_parts.append(jnp.where(half_id == h % 2, tq[:, blk] * scale, 0.0))
        k_parts.append(kv[:, h * dn:(h + 1) * dn] * k_rs[h] * kgn_ref[...])
        k_parts.append(tk[:, blk])
        v_parts.append(kv[:, H * dn + h * MLA_V:H * dn + (h + 1) * MLA_V])
        v_parts.append(ones)
    q_ref[0] = jnp.concatenate(q_parts, axis=1).astype(q_ref.dtype)
    k_ref[0] = jnp.concatenate(k_parts, axis=1).astype(k_ref.dtype)
    v_ref[0] = jnp.concatenate(v_parts, axis=1).astype(v_ref.dtype)


def _rope_consts(T):
    H, dr = MLA_HEADS, MLA_ROPE
    inv = 1.0 / (ROPE_BASE ** (jnp.arange(0, dr, 2, dtype=F32) / dr))
    ang = jnp.arange(T, dtype=F32)[:, None] * inv[None, :]
    ang = jnp.concatenate([ang, ang], axis=-1)
    cos4 = jnp.tile(jnp.cos(ang), (1, H))
    sin4 = jnp.tile(jnp.sin(ang), (1, H))
    rot = np.zeros((H * dr, H * dr), np.float32)
    for h in range(H):
        for c in range(dr // 2):
            rot[h * dr + c + dr // 2, h * dr + c] = -1.0
            rot[h * dr + c, h * dr + c + dr // 2] = 1.0
    expand = np.zeros((LANES, H * dr), np.float32)
    for h in range(H):
        for c in range(dr):
            expand[L_KPE + c, h * dr + c] = 1.0
    return cos4, sin4, jnp.asarray(rot), jnp.asarray(expand)


def _mla_prep(proj, rope, qa_g, wq_b, kva_g, wkv_b, qn_g, kn_g):
    B, T, _ = proj.shape
    n, H, dn, dr, dv = ROW_TILE, MLA_HEADS, MLA_NOPE, MLA_ROPE, MLA_V
    cos4, sin4, rot, expand = rope
    wq = wq_b.reshape(MLA_Q_RANK, H, MLA_DQK)
    wq = jnp.concatenate([wq[:, :, :dn].reshape(MLA_Q_RANK, H * dn),
                          wq[:, :, dn:].reshape(MLA_Q_RANK, H * dr)], axis=1).astype(BF16)
    wkv = wkv_b.reshape(MLA_KV_RANK, H, dn + dv)
    wkv = jnp.concatenate([wkv[:, :, :dn].reshape(MLA_KV_RANK, H * dn),
                           wkv[:, :, dn:].reshape(MLA_KV_RANK, H * dv)], axis=1).astype(BF16)
    qg, kg = qn_g.astype(F32), kn_g.astype(F32)
    const = lambda shape: pl.BlockSpec(shape, lambda b, t: (0,) * len(shape))
    QW = H * 2 * LANES
    return pl.pallas_call(
        _mla_prep_kernel,
        out_shape=(jax.ShapeDtypeStruct((B, T, QW), BF16), jax.ShapeDtypeStruct((B, T, QW), BF16),
                   jax.ShapeDtypeStruct((B, T, QW), BF16)),
        grid=(B, T // n),
        in_specs=[pl.BlockSpec((1, n, MLA_Q_RANK), lambda b, t: (b, t, COL_MLA_QA // MLA_Q_RANK)),
                  pl.BlockSpec((1, n, MLA_KV_RANK), lambda b, t: (b, t, COL_MLA_KVA // MLA_KV_RANK)),
                  pl.BlockSpec((1, n, LANES), lambda b, t: (b, t, COL_SMALL // LANES)),
                  pl.BlockSpec((n, H * dr), lambda b, t: (t, 0)),
                  pl.BlockSpec((n, H * dr), lambda b, t: (t, 0)),
                  const((1, MLA_Q_RANK)), const(wq.shape), const((1, MLA_KV_RANK)), const(wkv.shape),
                  const((1, dn)), const((1, H * dr)), const((1, dn)), const((1, H * dr)),
                  const(rot.shape), const(expand.shape)],
        out_specs=(pl.BlockSpec((1, n, QW), lambda b, t: (b, t, 0)),
                   pl.BlockSpec((1, n, QW), lambda b, t: (b, t, 0)),
                   pl.BlockSpec((1, n, QW), lambda b, t: (b, t, 0))),
        compiler_params=_cparams(("parallel", "parallel")),
        name="mla_prep",
    )(proj, proj, proj, cos4, sin4,
      qa_g.reshape(1, -1).astype(F32), wq, kva_g.reshape(1, -1).astype(F32), wkv,
      qg[:dn].reshape(1, dn), jnp.tile(qg[dn:], H).reshape(1, H * dr),
      kg[:dn].reshape(1, dn), jnp.tile(kg[dn:], H).reshape(1, H * dr), rot, expand)


def _flash_kernel(q_ref, k_ref, v_ref, o_ref, m_ref, acc_ref, *, tq):
    qi = pl.program_id(2)
    nh = m_ref.shape[0]
    dqk, dv2 = q_ref.shape[-1] // nh, v_ref.shape[-1] // nh
    dv = dv2 // 2
    rb = ROW_TILE
    m_ref[...] = jnp.full(m_ref.shape, NEG, F32)
    acc_ref[...] = jnp.zeros(acc_ref.shape, F32)

    def step(j, masked):
        start = pl.multiple_of(j * tq, tq)
        chains = [(h, r) for h in range(nh) for r in range(0, tq, rb)]
        logits, probs = {}, {}

        def qk(c):
            h, r = chains[c]
            logits[c] = _dot_nt(q_ref[0, r:r + rb, h * dqk:(h + 1) * dqk],
                                k_ref[0, pl.ds(start, tq), h * dqk:(h + 1) * dqk])

        def softmax(c):
            h, r = chains[c]
            s = logits.pop(c)
            if masked:
                row = lax.broadcasted_iota(jnp.int32, s.shape, 0) + r
                col = lax.broadcasted_iota(jnp.int32, s.shape, 1)
                s = jnp.where(col <= row, s, NEG)
            lane_max = s[:, :LANES]
            for cb in range(1, tq // LANES):
                lane_max = jnp.maximum(lane_max, s[:, cb * LANES:(cb + 1) * LANES])
            m_old = m_ref[h, r:r + rb]
            m_new = jnp.maximum(m_old, jnp.max(lane_max, axis=-1, keepdims=True))
            m_ref[h, r:r + rb] = m_new
            probs[c] = (jnp.exp2(s - m_new).astype(BF16), jnp.exp2(m_old - m_new))

        def pv(c):
            h, r = chains[c]
            p, alpha = probs.pop(c)
            acc_ref[h, r:r + rb] = (alpha * acc_ref[h, r:r + rb]
                                    + _dot(p, v_ref[0, pl.ds(start, tq), h * dv2:(h + 1) * dv2]))

        for c in range(len(chains)):
            qk(c)
            softmax(c)
            pv(c)

    def body(jj, carry):
        step(2 * jj, False)
        step(2 * jj + 1, False)
        return carry

    lax.fori_loop(0, qi // 2, body, 0)

    @pl.when(qi % 2 == 1)
    def _():
        step(qi - 1, False)

    step(qi, True)
    outs = []
    for h in range(nh):
        acc = acc_ref[h]
        outs.append(acc[:, :dv] / acc[:, dv:dv + 1])
    o_ref[0] = jnp.concatenate(outs, axis=1).astype(o_ref.dtype)


def _flash(q, k, v):
    B, T, QW = q.shape
    H, nh = MLA_HEADS, FLASH_HEADS_PER_STEP
    dqk, dv2 = QW // H, v.shape[-1] // H
    dv = dv2 // 2
    tq = next(t for t in FLASH_TILES if T % t == 0)
    return pl.pallas_call(
        functools.partial(_flash_kernel, tq=tq),
        out_shape=jax.ShapeDtypeStruct((B, T, H * dv), BF16),
        grid=(B, H // nh, T // tq),
        in_specs=[pl.BlockSpec((1, tq, nh * dqk), lambda b, h, i: (b, i, h)),
                  pl.BlockSpec((1, T, nh * dqk), lambda b, h, i: (b, 0, h)),
                  pl.BlockSpec((1, T, nh * dv2), lambda b, h, i: (b, 0, h))],
        out_specs=pl.BlockSpec((1, tq, nh * dv), lambda b, h, i: (b, i, h)),
        scratch_shapes=[pltpu.VMEM((nh, tq, 1), F32), pltpu.VMEM((nh, tq, dv2), F32)],
        compiler_params=_cparams(("parallel", "parallel", "arbitrary")),
        name="flash",
    )(q, k, v)


def _reorder_kernel(w_ref, o_ref):
    w = w_ref[...]
    n = w.shape[0]
    z = lambda width: jnp.zeros((n, width), w.dtype)
    seg = lambda *names: w[:, IN_OFFSET[names[0]]:IN_OFFSET[names[-1]] + IN_WIDTH[names[-1]]]
    small = [seg("gdn_b"), seg("gdn_a"), seg("fox_f"), seg("ssd_dt"), seg("mla_kpe")]
    n_small = sum(s.shape[1] for s in small)
    cols = [seg("gdn_q", "gdn_k", "gdn_v", "gdn_z"),
            seg("ssd_x", "ssd_b", "ssd_c"),
            seg("ssd_z"),
            seg("mla_qa", "mla_kva"),
            *small, z(LANES - n_small), z(COL_FOX_QKV - COL_SMALL - LANES),
            seg("fox_q", "fox_k", "fox_v")]
    o_ref[...] = jnp.concatenate(cols, axis=1).astype(o_ref.dtype)


def _reorder_w_in(w_in):
    D, W = w_in.shape
    rows = ROW_TILE
    return pl.pallas_call(
        _reorder_kernel,
        out_shape=jax.ShapeDtypeStruct((D, PROJ_W), BF16),
        grid=(D // rows,),
        in_specs=[pl.BlockSpec((rows, W), lambda i: (i, 0))],
        out_specs=pl.BlockSpec((rows, PROJ_W), lambda i: (i, 0)),
        compiler_params=_cparams(("parallel",)),
        name="reorder_w_in",
    )(w_in)


def _mixer(h, hn, B, T, rope, w_in_cols, gdn_conv_w, gdn_A_log, gdn_dt_bias, gdn_norm_g,
           mla_qa_g, mla_wq_b, mla_kva_g, mla_wkv_b, mla_qn_g, mla_kn_g,
           fox_qn_g, fox_kn_g, fox_b_f,
           ssd_conv_w, ssd_conv_b, ssd_dt_bias, ssd_A_log, ssd_D, ssd_norm_g,
           w_gate, w_branch, w_o, next_norm_g=None):
    M = B * T
    proj = _matmul(hn, w_in_cols, tn=2048, name="in_proj").reshape(B, T, PROJ_W)
    o_gdn = _gdn(proj, gdn_conv_w, gdn_A_log, gdn_dt_bias, gdn_norm_g)
    mq, mk, mv = _mla_prep(proj, rope, mla_qa_g, mla_wq_b, mla_kva_g, mla_wkv_b, mla_qn_g, mla_kn_g)
    o_mla = _flash(mq, mk, mv)
    o_fox = _flash(*_fox_prep(proj, fox_qn_g, fox_kn_g, fox_b_f))
    o_ssd = _ssd(proj, ssd_conv_w, ssd_conv_b, ssd_dt_bias, ssd_A_log, ssd_D, ssd_norm_g)
    branches = [o.reshape(M, BRANCH_W) for o in (o_gdn, o_mla, o_fox, o_ssd)]
    merged = _merge(hn, branches, w_gate.astype(BF16), w_branch.astype(BF16))
    return _matmul(merged, w_o.astype(BF16), tn=w_o.shape[1], tm=512, residual=h, norm_g=next_norm_g,
                   name="out_proj")


def kernel(x, meta_tokens, mix_norm_g, w_in, gdn_conv_w, gdn_A_log, gdn_dt_bias, gdn_norm_g, mla_qa_g, mla_wq_b, mla_kva_g, mla_wkv_b, mla_qn_g, mla_kn_g, fox_qn_g, fox_kn_g, fox_b_f, ssd_conv_w, ssd_conv_b, ssd_dt_bias, ssd_A_log, ssd_D, ssd_norm_g, w_gate, w_branch, w_o, ffn_norm_g, dense_w_gate, dense_w_up, dense_w_down, router_w, moe_w_gate, moe_w_up, moe_w_down):
    B, S, D = x.shape
    L = N_META + S
    T = -(-L // ROW_TILE) * ROW_TILE
    assert (B * T) % MM_TM == 0
    depth = w_in.shape[0]
    rope = _rope_consts(T)
    h, hn = _frame_and_norm(x, meta_tokens.astype(x.dtype), mix_norm_g[0], T)
    for layer in range(depth):
        dense = layer % 2 == 0
        next_mix_g = mix_norm_g[layer + 1] if layer + 1 < depth else None
        mixed = _mixer(h, hn, B, T, rope, _reorder_w_in(w_in[layer]),
                       gdn_conv_w[layer], gdn_A_log[layer], gdn_dt_bias[layer], gdn_norm_g[layer],
                       mla_qa_g[layer], mla_wq_b[layer], mla_kva_g[layer], mla_wkv_b[layer],
                       mla_qn_g[layer], mla_kn_g[layer],
                       fox_qn_g[layer], fox_kn_g[layer], fox_b_f[layer],
                       ssd_conv_w[layer], ssd_conv_b[layer], ssd_dt_bias[layer], ssd_A_log[layer],
                       ssd_D[layer], ssd_norm_g[layer],
                       w_gate[layer], w_branch[layer], w_o[layer],
                       next_norm_g=ffn_norm_g[layer] if dense else None)
        i = layer // 2
        if dense:
            h, hn = mixed
            act = _swiglu_up(hn, dense_w_gate[i].astype(BF16), dense_w_up[i].astype(BF16), tn=512)
            out = _matmul(act, dense_w_down[i].astype(BF16), tn=D, tm=FFN_DOWN_TM, residual=h,
                          norm_g=next_mix_g, w_buffers=1, name="ffn_down")
            h, hn = out if next_mix_g is not None else (out, None)
        else:
            h = mixed
            hn, rec, cnt_before = _rmsnorm_router(h, ffn_norm_g[layer], router_w[i], B, T)
            h = _moe(h, hn, rec, cnt_before, moe_w_gate[i].astype(BF16), moe_w_up[i].astype(BF16),
                     moe_w_down[i].astype(BF16), B, T)
            hn = _rmsnorm(h, next_mix_g) if next_mix_g is not None else None
    return h.reshape(B, T, D)[:, N_META:L].astype(x.dtype)
```

```python
import functools
import math

import numpy as np
import jax
import jax.numpy as jnp
from jax import lax
from jax.experimental import pallas as pl
from jax.experimental.pallas import tpu as pltpu

F32 = jnp.float32
BF16 = jnp.bfloat16
HI = lax.Precision.HIGHEST
NT_DIMS = (((1,), (1,)), ((), ()))

N_META = 16
EPS = 1e-6
NEG = -1e30
CONV_K = 4

GDN_HEADS, GDN_DK, GDN_DV = 4, 128, 128
MLA_HEADS, MLA_Q_RANK, MLA_KV_RANK, MLA_NOPE, MLA_ROPE, MLA_V = 4, 512, 256, 128, 64, 128
MLA_DQK = MLA_NOPE + MLA_ROPE
ROPE_BASE = 10000.0
FOX_HEADS, FOX_DH = 4, 128
SSD_HEADS, SSD_HEADDIM, SSD_GROUPS, SSD_STATE = 8, 64, 2, 128
SSD_HG = SSD_HEADS // SSD_GROUPS
SSD_INNER = SSD_HEADS * SSD_HEADDIM
N_BRANCH, BRANCH_W = 4, 512
N_EXPERTS, TOP_K = 8, 2

LANES = 128
ROW_TILE = 256
FLASH_TILES = (768, 256)
FLASH_HEADS_PER_STEP = 2
LOG2E = 1.4426950408889634
MM_TM = 768
FFN_DOWN_TM = 384
MOE_TM = 256
MOE_DISPATCH_ROWS = 128
R_E1, R_E2, R_P1, R_P2, R_R1, R_R2 = range(6)
VMEM_LIMIT = 56 * 1024 * 1024

_GDN_W, _FOX_W, _SSD_BC = GDN_HEADS * GDN_DK, FOX_HEADS * FOX_DH, SSD_GROUPS * SSD_STATE
IN_WIDTH = dict(gdn_q=_GDN_W, gdn_k=_GDN_W, gdn_v=GDN_HEADS * GDN_DV, gdn_z=GDN_HEADS * GDN_DV,
                gdn_b=GDN_HEADS, gdn_a=GDN_HEADS,
                mla_qa=MLA_Q_RANK, mla_kva=MLA_KV_RANK, mla_kpe=MLA_ROPE,
                fox_q=_FOX_W, fox_k=_FOX_W, fox_v=_FOX_W, fox_f=FOX_HEADS,
                ssd_z=SSD_INNER, ssd_x=SSD_INNER, ssd_b=_SSD_BC, ssd_c=_SSD_BC, ssd_dt=SSD_HEADS)
IN_OFFSET = dict(zip(IN_WIDTH, np.cumsum([0] + list(IN_WIDTH.values())[:-1]).tolist()))

PROJ_W = 6144
COL_GDN_QKV, COL_GDN_Z = 0, 1536
COL_SSD_XBC, COL_SSD_Z = 2048, 3072
COL_MLA_QA, COL_MLA_KVA = 3584, 4096
COL_SMALL = 4352
COL_FOX_QKV = 4608
L_BETA, L_GA, L_FF, L_DT, L_KPE = 0, 4, 8, 12, 20


def _cparams(sem, vmem=VMEM_LIMIT):
    return pltpu.CompilerParams(dimension_semantics=sem, vmem_limit_bytes=vmem)


def _softplus(x):
    return jnp.maximum(x, 0.0) + jnp.log1p(jnp.exp(-jnp.abs(x)))


def _silu(x):
    return x * jax.nn.sigmoid(x)


def _dot(a, b, precision=None):
    return jnp.dot(a, b, preferred_element_type=F32, precision=precision)


def _dot_nt(a, b):
    return lax.dot_general(a, b, NT_DIMS, preferred_element_type=F32)


def _rmsnorm_kernel(h_ref, g_ref, o_ref):
    x = h_ref[...]
    y = x * lax.rsqrt(jnp.mean(x * x, axis=-1, keepdims=True) + EPS) * g_ref[...]
    o_ref[...] = y.astype(o_ref.dtype)


def _rmsnorm(h, g):
    M, D = h.shape
    return pl.pallas_call(
        _rmsnorm_kernel,
        out_shape=jax.ShapeDtypeStruct((M, D), BF16),
        grid=(M // MM_TM,),
        in_specs=[pl.BlockSpec((MM_TM, D), lambda i: (i, 0)),
                  pl.BlockSpec((1, D), lambda i: (0, 0))],
        out_specs=pl.BlockSpec((MM_TM, D), lambda i: (i, 0)),
        compiler_params=_cparams(("parallel",)),
        name="rmsnorm",
    )(h, g.reshape(1, D).astype(F32))


def _frame_norm_kernel(meta_ref, xa_ref, xb_ref, g_ref, h_ref, hn_ref):
    t = pl.program_id(1)
    n = h_ref.shape[0]
    head = jnp.where(t == 0, meta_ref[...], xa_ref[0])
    body = jnp.where(t < pl.num_programs(1) - 1, xb_ref[0, :n - N_META], 0.0)
    x = jnp.concatenate([head, body], axis=0)
    h_ref[...] = x
    hn_ref[...] = (x * lax.rsqrt(jnp.mean(x * x, axis=-1, keepdims=True) + EPS) * g_ref[...]).astype(hn_ref.dtype)


def _frame_and_norm(x, meta, g, T):
    B, S, D = x.shape
    n = ROW_TILE
    assert S % n == 0 and T == S + n and meta.shape[0] == N_META and n % N_META == 0
    NT, per = T // n, n // N_META
    return pl.pallas_call(
        _frame_norm_kernel,
        out_shape=(jax.ShapeDtypeStruct((B * T, D), x.dtype), jax.ShapeDtypeStruct((B * T, D), BF16)),
        grid=(B, NT),
        in_specs=[pl.BlockSpec((N_META, D), lambda b, t: (0, 0)),
                  pl.BlockSpec((1, N_META, D), lambda b, t: (b, jnp.maximum(t * per - 1, 0), 0)),
                  pl.BlockSpec((1, n, D), lambda b, t: (b, jnp.minimum(t, S // n - 1), 0)),
                  pl.BlockSpec((1, D), lambda b, t: (0, 0))],
        out_specs=(pl.BlockSpec((n, D), lambda b, t: (b * NT + t, 0)),
                   pl.BlockSpec((n, D), lambda b, t: (b * NT + t, 0))),
        compiler_params=_cparams(("parallel", "parallel")),
        name="frame_norm",
    )(meta, x, x, g.reshape(1, D).astype(F32))


def _rmsnorm_router_kernel(h_ref, g_ref, rw_ref, o_ref, rec_ref, cnt_ref, run_ref):
    @pl.when(pl.program_id(1) == 0)
    def _():
        run_ref[...] = jnp.zeros_like(run_ref)

    x = h_ref[...]
    n = x.shape[0]
    y = x * lax.rsqrt(jnp.mean(x * x, axis=-1, keepdims=True) + EPS) * g_ref[...]
    y16 = y.astype(BF16)
    o_ref[...] = y16
    y_lo = (y - y16.astype(F32)).astype(BF16)
    both = _dot(y16, rw_ref[...])
    logits = both[:, :LANES] + both[:, LANES:] + _dot(y_lo, rw_ref[:, :LANES])
    lane = lax.broadcasted_iota(jnp.int32, logits.shape, 1)
    logits = jnp.where(lane < N_EXPERTS, logits, NEG)
    m1 = jnp.max(logits, axis=-1, keepdims=True)
    i1 = jnp.min(jnp.where(logits == m1, lane, LANES), axis=-1, keepdims=True)
    rest = jnp.where(lane == i1, NEG, logits)
    m2 = jnp.max(rest, axis=-1, keepdims=True)
    i2 = jnp.min(jnp.where(rest == m2, lane, LANES), axis=-1, keepdims=True)
    e2 = jnp.exp(m2 - m1)
    p1 = 1.0 / (1.0 + e2)
    p2 = e2 * p1
    sel = jnp.where(lane == i1, 1.0, 0.0) + jnp.where(lane == i2, 1.0, 0.0)
    row = lax.broadcasted_iota(jnp.int32, (n, n), 0)
    col = lax.broadcasted_iota(jnp.int32, (n, n), 1)
    earlier = jnp.where(col < row, 1.0, 0.0).astype(BF16)
    run = run_ref[0:1, :]
    before = _dot(earlier, sel.astype(BF16)) + run
    r1 = jnp.sum(jnp.where(lane == i1, before, 0.0), axis=-1, keepdims=True)
    r2 = jnp.sum(jnp.where(lane == i2, before, 0.0), axis=-1, keepdims=True)
    cnt_ref[...] = jnp.broadcast_to(run, cnt_ref.shape)
    run_ref[...] = jnp.broadcast_to(before[n - 1:n] + sel[n - 1:n], run_ref.shape)
    rec = jnp.zeros_like(logits)
    for ln, val in ((R_E1, i1.astype(F32)), (R_E2, i2.astype(F32)), (R_P1, p1), (R_P2, p2), (R_R1, r1), (R_R2, r2)):
        rec = jnp.where(lane == ln, val, rec)
    rec_ref[...] = rec


def _rmsnorm_router(h, g, router_w, B, T):
    M, D = h.shape
    n = ROW_TILE
    NT = T // n
    rw = jnp.zeros((D, LANES), F32).at[:, :N_EXPERTS].set(router_w.astype(F32))
    rw_hi = rw.astype(BF16)
    rw = jnp.concatenate([rw_hi, (rw - rw_hi.astype(F32)).astype(BF16)], axis=1)
    return pl.pallas_call(
        _rmsnorm_router_kernel,
        out_shape=(jax.ShapeDtypeStruct((M, D), BF16), jax.ShapeDtypeStruct((M, LANES), F32),
                   jax.ShapeDtypeStruct((B * NT * 8, LANES), F32)),
        grid=(B, NT),
        in_specs=[pl.BlockSpec((n, D), lambda b, t: (b * NT + t, 0)),
                  pl.BlockSpec((1, D), lambda b, t: (0, 0)),
                  pl.BlockSpec((D, 2 * LANES), lambda b, t: (0, 0))],
        out_specs=(pl.BlockSpec((n, D), lambda b, t: (b * NT + t, 0)),
                   pl.BlockSpec((n, LANES), lambda b, t: (b * NT + t, 0)),
                   pl.BlockSpec((8, LANES), lambda b, t: (b * NT + t, 0))),
        scratch_shapes=[pltpu.VMEM((8, LANES), F32)],
        compiler_params=_cparams(("arbitrary", "arbitrary")),
        name="rmsnorm_router",
    )(h, g.reshape(1, D).astype(F32), rw)


def _mm_kernel(*refs, has_res, has_norm):
    a_ref, w_ref = refs[:2]
    y = _dot(a_ref[...], w_ref[...])
    if has_res:
        y = y + refs[2][...]
    if has_norm:
        g_ref, o_ref, n_ref = refs[-3:]
        n_ref[...] = (y * lax.rsqrt(jnp.mean(y * y, axis=-1, keepdims=True) + EPS) * g_ref[...]).astype(n_ref.dtype)
    else:
        o_ref = refs[-1]
    o_ref[...] = y.astype(o_ref.dtype)


def _matmul(a, w, *, tn, tm=MM_TM, residual=None, norm_g=None, out_dtype=F32, w_buffers=2, name="matmul"):
    M, K = a.shape
    N = w.shape[1]
    in_specs = [pl.BlockSpec((tm, K), lambda j, i: (i, 0)),
                pl.BlockSpec((K, tn), lambda j, i: (0, j), pipeline_mode=pl.Buffered(w_buffers))]
    args = [a, w]
    out_shape = jax.ShapeDtypeStruct((M, N), out_dtype)
    out_specs = pl.BlockSpec((tm, tn), lambda j, i: (i, j))
    if residual is not None:
        in_specs.append(pl.BlockSpec((tm, tn), lambda j, i: (i, j)))
        args.append(residual)
    if norm_g is not None:
        assert tn == N
        in_specs.append(pl.BlockSpec((1, N), lambda j, i: (0, 0)))
        args.append(norm_g.reshape(1, N).astype(F32))
        out_shape = (out_shape, jax.ShapeDtypeStruct((M, N), BF16))
        out_specs = (out_specs, pl.BlockSpec((tm, tn), lambda j, i: (i, j)))
    return pl.pallas_call(
        functools.partial(_mm_kernel, has_res=residual is not None, has_norm=norm_g is not None),
        out_shape=out_shape,
        grid=(N // tn, M // tm),
        in_specs=in_specs,
        out_specs=out_specs,
        compiler_params=_cparams(("parallel", "parallel")),
        name=name,
    )(*args)


def _swiglu_kernel(a_ref, wg_ref, wu_ref, o_ref):
    a = a_ref[...]
    g = _dot(a, wg_ref[...])
    u = _dot(a, wu_ref[...])
    o_ref[...] = (_silu(g) * u).astype(o_ref.dtype)


def _swiglu_up(a, wg, wu, *, tn, tm=MM_TM):
    M, K = a.shape
    F = wg.shape[1]
    return pl.pallas_call(
        _swiglu_kernel,
        out_shape=jax.ShapeDtypeStruct((M, F), BF16),
        grid=(F // tn, M // tm),
        in_specs=[pl.BlockSpec((tm, K), lambda j, i: (i, 0)),
                  pl.BlockSpec((K, tn), lambda j, i: (0, j)),
                  pl.BlockSpec((K, tn), lambda j, i: (0, j))],
        out_specs=pl.BlockSpec((tm, tn), lambda j, i: (i, j)),
        compiler_params=_cparams(("parallel", "parallel")),
        name="swiglu_up",
    )(a, wg, wu)


def _dispatch_kernel(lo_ref, nb_ref, d1_ref, d2_ref, rec_ref, hn_ref, o_ref, p_ref, acc_ref, pacc_ref, *, wb):
    i = pl.program_id(1)
    idx = pl.program_id(0) * pl.num_programs(1) + i
    tm = o_ref.shape[0]
    rows = i * tm + lax.broadcasted_iota(jnp.int32, (tm, wb), 0)
    lane = lax.broadcasted_iota(jnp.int32, (wb, LANES), 1)
    acc_ref[...] = jnp.zeros_like(acc_ref)
    pacc_ref[...] = jnp.zeros_like(pacc_ref)

    def hi_lo(p):
        hi = p.astype(BF16).astype(F32)
        return jnp.where(lane == 0, hi, jnp.where(lane == 1, p - hi, 0.0)).astype(BF16)

    def body(k, carry):
        start = pl.multiple_of((lo_ref[idx] + k) * wb, wb)
        hit1 = jnp.where(d1_ref[0, :, pl.ds(start, wb)] == rows, 1.0, 0.0)
        hit2 = jnp.where(d2_ref[0, :, pl.ds(start, wb)] == rows, 1.0, 0.0)
        acc_ref[...] += _dot((hit1 + hit2).astype(BF16), hn_ref[0, pl.ds(start, wb), :]).astype(BF16)
        rec = rec_ref[0, pl.ds(start, wb), :]
        pacc_ref[...] += (_dot(hit1.astype(BF16), hi_lo(rec[:, R_P1:R_P1 + 1]))
                          + _dot(hit2.astype(BF16), hi_lo(rec[:, R_P2:R_P2 + 1])))
        return carry

    lax.fori_loop(0, nb_ref[idx], body, 0)
    o_ref[...] = acc_ref[...].astype(o_ref.dtype)
    p_ref[...] = pacc_ref[...]


def _dispatch(hn3, rec3, d1, d2, lo_blk, n_blk, nti, *, tm, wb=ROW_TILE):
    B, T, D = hn3.shape
    dspec = pl.BlockSpec((1, 1, T), lambda b, i, lo, nb: (b, 0, 0))
    row_spec = lambda w: pl.BlockSpec((tm, w), lambda b, i, lo, nb: (b * nti + i, 0))
    grid_spec = pltpu.PrefetchScalarGridSpec(
        num_scalar_prefetch=2,
        grid=(B, nti),
        in_specs=[dspec, dspec,
                  pl.BlockSpec((1, T, LANES), lambda b, i, lo, nb: (b, 0, 0)),
                  pl.BlockSpec((1, T, D), lambda b, i, lo, nb: (b, 0, 0), pipeline_mode=pl.Buffered(1))],
        out_specs=(row_spec(D), row_spec(LANES)),
        scratch_shapes=[pltpu.VMEM((tm, D), BF16), pltpu.VMEM((tm, LANES), F32)])
    return pl.pallas_call(
        functools.partial(_dispatch_kernel, wb=wb),
        out_shape=(jax.ShapeDtypeStruct((B * nti * tm, D), BF16),
                   jax.ShapeDtypeStruct((B * nti * tm, LANES), F32)),
        grid_spec=grid_spec,
        compiler_params=_cparams(("arbitrary", "arbitrary")),
        name="moe_dispatch",
    )(lo_blk, n_blk, d1.reshape(B, 1, T), d2.reshape(B, 1, T), rec3, hn3)


def _expert_ffn_kernel(te_ref, nb_ref, x_ref, wg_ref, wu_ref, wd_ref, p_ref, o_ref):
    i = pl.program_id(0)

    @pl.when(nb_ref[i] > 0)
    def _():
        x = x_ref[...]
        p = p_ref[:, 0:1] + p_ref[:, 1:2]
        act = (_silu(_dot(x, wg_ref[0])) * _dot(x, wu_ref[0]) * p).astype(BF16)
        o_ref[...] = _dot(act, wd_ref[0]).astype(o_ref.dtype)

    @pl.when(nb_ref[i] == 0)
    def _():
        o_ref[...] = jnp.zeros_like(o_ref)


def _expert_ffn(x, p_rows, tile_expert, n_blk, wg, wu, wd, *, tm):
    R, D = x.shape
    E, _, F = wg.shape
    grid_spec = pltpu.PrefetchScalarGridSpec(
        num_scalar_prefetch=2,
        grid=(R // tm,),
        in_specs=[pl.BlockSpec((tm, D), lambda i, te, nb: (i, 0)),
                  pl.BlockSpec((1, D, F), lambda i, te, nb: (te[i], 0, 0)),
                  pl.BlockSpec((1, D, F), lambda i, te, nb: (te[i], 0, 0)),
                  pl.BlockSpec((1, F, D), lambda i, te, nb: (te[i], 0, 0)),
                  pl.BlockSpec((tm, LANES), lambda i, te, nb: (i, 0))],
        out_specs=pl.BlockSpec((tm, D), lambda i, te, nb: (i, 0)))
    return pl.pallas_call(
        _expert_ffn_kernel,
        out_shape=jax.ShapeDtypeStruct((R, D), BF16),
        grid_spec=grid_spec,
        compiler_params=_cparams(("arbitrary",)),
        name="moe_ffn",
    )(tile_expert, n_blk, x, wg, wu, wd, p_rows)


def _combine_kernel(yb_ref, yv_ref, h_ref, d1_ref, d2_ref, *refs, tm):
    y_refs, o_ref = refs[:-1], refs[-1]
    n = o_ref.shape[0]
    slots = len(y_refs) // N_EXPERTS
    base = (pl.program_id(0) * pl.num_programs(1) + pl.program_id(1)) * len(y_refs)
    d1, d2 = d1_ref[...], d2_ref[...]
    lane = lax.broadcasted_iota(jnp.int32, (n, tm), 1)

    def picked(k, weight):
        rows = yb_ref[base + k] * tm + lane
        hit = jnp.where(d1 == rows, weight, 0.0) + jnp.where(d2 == rows, weight, 0.0)
        return _dot(hit.astype(BF16), y_refs[k][...])

    acc = h_ref[...]
    for k in range(0, len(y_refs), slots):
        acc = acc + picked(k, jnp.where(yv_ref[base + k] > 0, 1.0, 0.0))
    o_ref[...] = acc
    for k in range(len(y_refs)):
        if k % slots:
            @pl.when(yv_ref[base + k] > 0)
            def _():
                o_ref[...] += picked(k, 1.0)


def _combine(h, y, d1, d2, y_blk, y_valid, B, T, *, tm):
    M, D = h.shape
    n = ROW_TILE
    NT = T // n
    slots = y_blk.shape[0] // (B * NT)
    y_specs = [pl.BlockSpec((tm, D), lambda b, t, yb, yv, k=k: (yb[(b * NT + t) * slots + k], 0))
               for k in range(slots)]
    tok = lambda w: pl.BlockSpec((n, w), lambda b, t, yb, yv: (b * NT + t, 0))
    grid_spec = pltpu.PrefetchScalarGridSpec(
        num_scalar_prefetch=2,
        grid=(B, NT),
        in_specs=[tok(D), tok(1), tok(1)] + y_specs,
        out_specs=tok(D))
    return pl.pallas_call(
        functools.partial(_combine_kernel, tm=tm),
        out_shape=jax.ShapeDtypeStruct((M, D), F32),
        grid_spec=grid_spec,
        compiler_params=_cparams(("arbitrary", "arbitrary")),
        name="moe_combine",
    )(y_blk, y_valid, h, d1, d2, *([y] * slots))


def _moe(h, hn, rec, cnt_before, wg, wu, wd, B, T):
    M, D = h.shape
    E, tm, n = N_EXPERTS, MOE_TM, ROW_TILE
    NT = T // n
    rb = TOP_K * T + E * tm
    i32 = jnp.int32
    col = lambda ln: rec[:, ln].reshape(B, T)
    e1, e2 = col(R_E1).astype(i32), col(R_E2).astype(i32)
    r1, r2 = col(R_R1).astype(i32), col(R_R2).astype(i32)
    cntb = cnt_before.reshape(B, NT, 8, LANES)[:, :, 0, :E].astype(i32)
    oh1, oh2 = jax.nn.one_hot(e1, E, dtype=i32), jax.nn.one_hot(e2, E, dtype=i32)
    counts = (oh1 + oh2).sum(axis=1)
    padded = (counts + tm - 1) // tm * tm
    ends = jnp.cumsum(padded, axis=1)
    off = ends - padded
    d1 = (oh1 * off[:, None, :]).sum(-1) + r1
    d2 = (oh2 * off[:, None, :]).sum(-1) + r2
    cnt_end = jnp.concatenate([cntb[:, 1:], counts[:, None, :]], axis=1)
    def tiles(rows):
        start = jnp.arange(rb // rows, dtype=i32)[None, :] * rows
        expert = jnp.minimum((start[:, :, None] >= ends[:, None, :]).sum(-1), E - 1).astype(i32)
        e_oh = jax.nn.one_hot(expert, E, dtype=i32)
        pick = lambda per_expert: (e_oh * per_expert[:, None, :]).sum(-1)
        rank_lo = start - pick(off)
        rank_hi = jnp.minimum(rank_lo + rows, pick(counts)) - 1
        through = (e_oh[:, :, None, :] * cnt_end[:, None, :, :]).sum(-1)
        lo_blk = (through <= rank_lo[..., None]).sum(-1)
        hi_blk = (through <= rank_hi[..., None]).sum(-1)
        nonempty = (start < pick(ends)) & (rank_hi >= rank_lo)
        return (expert.reshape(-1), jnp.where(nonempty, lo_blk, 0).astype(i32).reshape(-1),
                jnp.where(nonempty, hi_blk - lo_blk + 1, 0).astype(i32).reshape(-1))

    pr = MOE_DISPATCH_ROWS
    _, lo_blk, n_blk = tiles(pr)
    x, p_rows = _dispatch(hn.reshape(B, T, D), rec.reshape(B, T, LANES), d1, d2, lo_blk, n_blk, rb // pr, tm=pr)
    tile_expert, _, tile_used = tiles(tm)
    y = _expert_ffn(x, p_rows, tile_expert, tile_used, wg, wu, wd, tm=tm)
    first = off[:, None, :] + cntb
    last = off[:, None, :] + cnt_end - 1
    used = cnt_end > cntb
    span = jnp.arange(n // tm + 1, dtype=i32)
    blk = first[..., None] // tm + span
    valid = used[..., None] & (blk <= last[..., None] // tm)
    blk = jnp.minimum(blk, rb // tm - 1) + (jnp.arange(B, dtype=i32) * (rb // tm))[:, None, None, None]
    row0 = (jnp.arange(B, dtype=i32) * rb)[:, None]
    return _combine(h, y, (d1 + row0).reshape(M, 1), (d2 + row0).reshape(M, 1),
                    blk.reshape(-1).astype(i32), valid.reshape(-1).astype(i32), B, T, tm=tm)


def _merge_kernel(hn_ref, b0_ref, b1_ref, b2_ref, b3_ref, wg_ref, wb_ref, o_ref):
    hn = hn_ref[...]
    acc = None
    for b, br_ref in enumerate((b0_ref, b1_ref, b2_ref, b3_ref)):
        gate = jax.nn.sigmoid(_dot(hn, wg_ref[b]))
        term = gate * _dot(br_ref[...], wb_ref[b])
        acc = term if acc is None else acc + term
    o_ref[...] = acc.astype(o_ref.dtype)


def _merge(hn, branches, wg, wb, *, tn=512, tm=512):
    M, D = hn.shape
    N = wg.shape[2]
    bspec = pl.BlockSpec((tm, BRANCH_W), lambda j, i: (i, 0))
    return pl.pallas_call(
        _merge_kernel,
        out_shape=jax.ShapeDtypeStruct((M, N), BF16),
        grid=(N // tn, M // tm),
        in_specs=[pl.BlockSpec((tm, D), lambda j, i: (i, 0)), bspec, bspec, bspec, bspec,
                  pl.BlockSpec((N_BRANCH, D, tn), lambda j, i: (0, 0, j)),
                  pl.BlockSpec((N_BRANCH, BRANCH_W, tn), lambda j, i: (0, 0, j))],
        out_specs=pl.BlockSpec((tm, tn), lambda j, i: (i, j)),
        compiler_params=_cparams(("parallel", "parallel")),
        name="gate_merge",
    )(hn, *branches, wg, wb)


def _causal_conv(x, carry_ref, cw):
    n = x.shape[0]
    xext = jnp.concatenate([carry_ref[...], x], axis=0)
    y = cw[0:1] * xext[5:5 + n]
    for i in range(1, CONV_K):
        y = y + cw[i:i + 1] * xext[5 + i:5 + i + n]
    carry_ref[...] = x[n - 8:n]
    return y


def _causal_mask(n):
    row = lax.broadcasted_iota(jnp.int32, (n, n), 0)
    col = lax.broadcasted_iota(jnp.int32, (n, n), 1)
    return col <= row


def _gdn_kernel(qkv_ref, z_ref, sm_ref, cw_ref, alog_ref, dtb_ref, ng_ref, o_ref, s_ref, carry_ref):
    @pl.when(pl.program_id(1) == 0)
    def _():
        s_ref[...] = jnp.zeros_like(s_ref)
        carry_ref[...] = jnp.zeros_like(carry_ref)

    n = ROW_TILE
    y = _silu(_causal_conv(qkv_ref[0], carry_ref, cw_ref[...]))
    sm = sm_ref[0]
    z = z_ref[0]
    beta_all = jax.nn.sigmoid(sm)
    g_all = -jnp.exp(alog_ref[...]) * _softplus(sm + dtb_ref[...])
    row = lax.broadcasted_iota(jnp.int32, (n, n), 0)
    col = lax.broadcasted_iota(jnp.int32, (n, n), 1)
    causal, strict = col <= row, col < row
    diff_bits = row ^ col
    levels = int(math.log2(n))
    level_masks = [(diff_bits >= (1 << l)) & (diff_bits < (2 << l)) for l in range(levels)]
    eye = jnp.where(row == col, 1.0, 0.0)
    gcs_all = _dot(causal.astype(F32), g_all, HI)
    gcs_t = gcs_all.T
    heads = []
    for h in range(GDN_HEADS):
        q = y[:, h * GDN_DK:(h + 1) * GDN_DK]
        k = y[:, GDN_HEADS * GDN_DK + h * GDN_DK:GDN_HEADS * GDN_DK + (h + 1) * GDN_DK]
        v = y[:, 2 * GDN_HEADS * GDN_DK + h * GDN_DV:2 * GDN_HEADS * GDN_DK + (h + 1) * GDN_DV]
        q = q * lax.rsqrt(jnp.sum(q * q, axis=-1, keepdims=True) + EPS) * GDN_DK ** -0.5
        k = k * lax.rsqrt(jnp.sum(k * k, axis=-1, keepdims=True) + EPS)
        beta = beta_all[:, L_BETA + h:L_BETA + h + 1]
        gc = gcs_all[:, L_GA + h:L_GA + h + 1]
        gr = gcs_t[L_GA + h:L_GA + h + 1, :]
        decay = jnp.exp(jnp.where(causal, gc - gr, NEG))
        kb = k * beta
        k16 = k.astype(BF16)
        a = jnp.where(strict, _dot_nt(kb.astype(BF16), k16) * decay, 0.0)
        heads.append(dict(q=q, k=k, v=v, beta=beta, gc=gc, decay=decay, kb=kb, k16=k16, a=a,
                          t=eye - jnp.where(level_masks[0], a, 0.0)))
    for l in range(1, levels):
        for hd in heads:
            t16 = hd["t"].astype(BF16)
            a_l = jnp.where(level_masks[l], hd["a"], 0.0).astype(BF16)
            hd["t"] = hd["t"] - _dot(_dot(t16, a_l).astype(BF16), t16)
    outs = []
    for h, hd in enumerate(heads):
        q, k, gc = hd["q"], hd["k"], hd["gc"]
        egc = jnp.exp(gc)
        rhs = jnp.concatenate([hd["v"] * hd["beta"], hd["kb"] * egc], axis=1)
        uw = _dot(hd["t"].astype(BF16), rhs.astype(BF16))
        u, w = uw[:, :GDN_DV], uw[:, GDN_DV:]
        att = _dot_nt(q.astype(BF16), hd["k16"]) * hd["decay"]
        g_last = gc[n - 1:n, :]
        kd = k * jnp.exp(g_last - gc)
        s = s_ref[h]
        ws = _dot(jnp.concatenate([w, q * egc], axis=0).astype(BF16), s.astype(BF16))
        v_new = (u - ws[:n]).astype(BF16)
        o = ws[n:] + _dot(att.astype(BF16), v_new)
        s_ref[h] = s * jnp.exp(g_last) + _dot(kd.T.astype(BF16), v_new)
        o = o * lax.rsqrt(jnp.mean(o * o, axis=-1, keepdims=True) + EPS) * ng_ref[...]
        outs.append(o * _silu(z[:, h * GDN_DV:(h + 1) * GDN_DV]))
    o_ref[0] = jnp.concatenate(outs, axis=1).astype(o_ref.dtype)


def _lane_vec(vals, lane0):
    v = jnp.zeros((1, LANES), F32)
    return v.at[0, lane0:lane0 + vals.shape[0]].set(vals.astype(F32))


def _gdn(proj, conv_w, a_log, dt_bias, norm_g):
    B, T, _ = proj.shape
    n = ROW_TILE
    W = 2 * GDN_HEADS * GDN_DK + GDN_HEADS * GDN_DV
    ZW = GDN_HEADS * GDN_DV
    return pl.pallas_call(
        _gdn_kernel,
        out_shape=jax.ShapeDtypeStruct((B, T, ZW), BF16),
        grid=(B, T // n),
        in_specs=[pl.BlockSpec((1, n, W), lambda b, t: (b, t, COL_GDN_QKV // W)),
                  pl.BlockSpec((1, n, ZW), lambda b, t: (b, t, COL_GDN_Z // ZW)),
                  pl.BlockSpec((1, n, LANES), lambda b, t: (b, t, COL_SMALL // LANES)),
                  pl.BlockSpec((CONV_K, W), lambda b, t: (0, 0)),
                  pl.BlockSpec((1, LANES), lambda b, t: (0, 0)),
                  pl.BlockSpec((1, LANES), lambda b, t: (0, 0)),
                  pl.BlockSpec((1, GDN_DV), lambda b, t: (0, 0))],
        out_specs=pl.BlockSpec((1, n, ZW), lambda b, t: (b, t, 0)),
        scratch_shapes=[pltpu.VMEM((GDN_HEADS, GDN_DK, GDN_DV), F32),
                        pltpu.VMEM((8, W), F32)],
        compiler_params=_cparams(("arbitrary", "arbitrary")),
        name="gdn",
    )(proj, proj, proj, conv_w.astype(F32), _lane_vec(a_log, L_GA), _lane_vec(dt_bias, L_GA),
      norm_g.reshape(1, GDN_DV).astype(F32))


def _ssd_kernel(xbc_ref, z_ref, sm_ref, cw_ref, cb_ref, alog_ref, dtb_ref, dvec_ref, ng_ref,
                o_ref, hs_ref, carry_ref):
    @pl.when(pl.program_id(1) == 0)
    def _():
        hs_ref[...] = jnp.zeros_like(hs_ref)
        carry_ref[...] = jnp.zeros_like(carry_ref)

    n, P, N = ROW_TILE, SSD_HEADDIM, SSD_STATE
    y = _silu(_causal_conv(xbc_ref[0], carry_ref, cw_ref[...]) + cb_ref[...])
    xs = y[:, :SSD_INNER]
    bm = y[:, SSD_INNER:SSD_INNER + SSD_GROUPS * N]
    cm = y[:, SSD_INNER + SSD_GROUPS * N:]
    sm = sm_ref[0]
    dt_all = _softplus(sm + dtb_ref[...])
    a_all = dt_all * (-jnp.exp(alog_ref[...]))
    causal = _causal_mask(n)
    acs_all = _dot(causal.astype(F32), a_all, HI)
    acs_t = acs_all.T
    ys = []
    for g in range(SSD_GROUPS):
        bg = bm[:, g * N:(g + 1) * N]
        cg16 = cm[:, g * N:(g + 1) * N].astype(BF16)
        cb = _dot_nt(cg16, bg.astype(BF16))
        bgt16 = bg.T.astype(BF16)
        for j in range(SSD_HG):
            hh = g * SSD_HG + j
            ac = acs_all[:, L_DT + hh:L_DT + hh + 1]
            ar = acs_t[L_DT + hh:L_DT + hh + 1, :]
            lmat = jnp.exp(jnp.where(causal, ac - ar, NEG))
            xdt = xs[:, hh * P:(hh + 1) * P] * dt_all[:, L_DT + hh:L_DT + hh + 1]
            y_diag = _dot((cb * lmat).astype(BF16), xdt.astype(BF16))
            a_last = ac[n - 1:n, :]
            st = _dot(bgt16, (xdt * jnp.exp(a_last - ac)).astype(BF16))
            h_prev = hs_ref[hh]
            y_off = _dot(cg16, h_prev.astype(BF16)) * jnp.exp(ac)
            hs_ref[hh] = h_prev * jnp.exp(a_last) + st
            ys.append(y_diag + y_off)
    yy = jnp.concatenate(ys, axis=1) + xs * dvec_ref[...]
    yy = yy * _silu(z_ref[0])
    gw = SSD_HG * P
    outs = []
    for g in range(SSD_GROUPS):
        seg = yy[:, g * gw:(g + 1) * gw]
        outs.append(seg * lax.rsqrt(jnp.mean(seg * seg, axis=-1, keepdims=True) + EPS)
                    * ng_ref[:, g * gw:(g + 1) * gw])
    o_ref[0] = jnp.concatenate(outs, axis=1).astype(o_ref.dtype)


def _ssd(proj, conv_w, conv_b, dt_bias, a_log, d_skip, norm_g):
    B, T, _ = proj.shape
    n = ROW_TILE
    W = SSD_INNER + 2 * SSD_GROUPS * SSD_STATE
    dvec = jnp.repeat(d_skip.astype(F32), SSD_HEADDIM).reshape(1, SSD_INNER)
    return pl.pallas_call(
        _ssd_kernel,
        out_shape=jax.ShapeDtypeStruct((B, T, SSD_INNER), BF16),
        grid=(B, T // n),
        in_specs=[pl.BlockSpec((1, n, W), lambda b, t: (b, t, COL_SSD_XBC // W)),
                  pl.BlockSpec((1, n, SSD_INNER), lambda b, t: (b, t, COL_SSD_Z // SSD_INNER)),
                  pl.BlockSpec((1, n, LANES), lambda b, t: (b, t, COL_SMALL // LANES)),
                  pl.BlockSpec((CONV_K, W), lambda b, t: (0, 0)),
                  pl.BlockSpec((1, W), lambda b, t: (0, 0)),
                  pl.BlockSpec((1, LANES), lambda b, t: (0, 0)),
                  pl.BlockSpec((1, LANES), lambda b, t: (0, 0)),
                  pl.BlockSpec((1, SSD_INNER), lambda b, t: (0, 0)),
                  pl.BlockSpec((1, SSD_INNER), lambda b, t: (0, 0))],
        out_specs=pl.BlockSpec((1, n, SSD_INNER), lambda b, t: (b, t, 0)),
        scratch_shapes=[pltpu.VMEM((SSD_HEADS, SSD_STATE, SSD_HEADDIM), F32),
                        pltpu.VMEM((8, W), F32)],
        compiler_params=_cparams(("arbitrary", "arbitrary")),
        name="ssd",
    )(proj, proj, proj, conv_w.astype(F32), conv_b.reshape(1, W).astype(F32),
      _lane_vec(a_log, L_DT), _lane_vec(dt_bias, L_DT), dvec,
      norm_g.reshape(1, SSD_INNER).astype(F32))


def _ones_lane0(n):
    lane = lax.broadcasted_iota(jnp.int32, (n, LANES), 1)
    return jnp.where(lane == 0, 1.0, 0.0)


def _fox_prep_kernel(qkv_ref, sm_ref, qg_ref, kg_ref, bf_ref, q_ref, k_ref, v_ref, run_ref):
    @pl.when(pl.program_id(1) == 0)
    def _():
        run_ref[...] = jnp.zeros_like(run_ref)

    n, dh, H = ROW_TILE, FOX_DH, FOX_HEADS
    x = qkv_ref[0]
    log_f = -_softplus(-(sm_ref[0] + bf_ref[...]))
    causal = _causal_mask(n)
    cum = _dot(causal.astype(F32), log_f, HI) + run_ref[0:1, :]
    run_ref[...] = jnp.broadcast_to(cum[n - 1:n, :], run_ref.shape)
    lane = lax.broadcasted_iota(jnp.int32, (n, LANES), 1)
    ones = _ones_lane0(n)
    q_bias = jnp.where(lane < 3, 1.0, 0.0)
    qs, ks, vs = [], [], []
    for h in range(H):
        q = x[:, h * dh:(h + 1) * dh]
        k = x[:, H * dh + h * dh:H * dh + (h + 1) * dh]
        qs.append(q * lax.rsqrt(jnp.mean(q * q, axis=-1, keepdims=True) + EPS)
                  * (qg_ref[...] * (dh ** -0.5 * LOG2E)))
        qs.append(q_bias)
        ks.append(k * lax.rsqrt(jnp.mean(k * k, axis=-1, keepdims=True) + EPS) * kg_ref[...])
        c = cum[:, L_FF + h:L_FF + h + 1] * (-LOG2E)
        c_hi = c.astype(BF16).astype(F32)
        c_mid = (c - c_hi).astype(BF16).astype(F32)
        c_lo = c - c_hi - c_mid
        ks.append(jnp.where(lane == 0, c_hi, jnp.where(lane == 1, c_mid, jnp.where(lane == 2, c_lo, 0.0))))
        vs.append(x[:, 2 * H * dh + h * dh:2 * H * dh + (h + 1) * dh])
        vs.append(ones)
    q_ref[0] = jnp.concatenate(qs, axis=1).astype(q_ref.dtype)
    k_ref[0] = jnp.concatenate(ks, axis=1).astype(k_ref.dtype)
    v_ref[0] = jnp.concatenate(vs, axis=1).astype(v_ref.dtype)


def _fox_prep(proj, qn_g, kn_g, b_f):
    B, T, _ = proj.shape
    n = ROW_TILE
    W = 3 * FOX_HEADS * FOX_DH
    HW = FOX_HEADS * 2 * LANES
    ospec = pl.BlockSpec((1, n, HW), lambda b, t: (b, t, 0))
    return pl.pallas_call(
        _fox_prep_kernel,
        out_shape=(jax.ShapeDtypeStruct((B, T, HW), BF16),) * 3,
        grid=(B, T // n),
        in_specs=[pl.BlockSpec((1, n, W), lambda b, t: (b, t, COL_FOX_QKV // W)),
                  pl.BlockSpec((1, n, LANES), lambda b, t: (b, t, COL_SMALL // LANES)),
                  pl.BlockSpec((1, FOX_DH), lambda b, t: (0, 0)),
                  pl.BlockSpec((1, FOX_DH), lambda b, t: (0, 0)),
                  pl.BlockSpec((1, LANES), lambda b, t: (0, 0))],
        out_specs=(ospec, ospec, ospec),
        scratch_shapes=[pltpu.VMEM((8, LANES), F32)],
        compiler_params=_cparams(("arbitrary", "arbitrary")),
        name="fox_prep",
    )(proj, proj, qn_g.reshape(1, FOX_DH).astype(F32), kn_g.reshape(1, FOX_DH).astype(F32),
      _lane_vec(b_f, L_FF))


def _mla_prep_kernel(qa_ref, kva_ref, sm_ref, cos_ref, sin_ref, qag_ref, wq_ref, kvag_ref, wkv_ref,
                     qgn_ref, qgr_ref, kgn_ref, kgr_ref, rot_ref, exp_ref, q_ref, k_ref, v_ref):
    n, H, dn, dr = ROW_TILE, MLA_HEADS, MLA_NOPE, MLA_ROPE
    qa = qa_ref[0]
    qa = qa * lax.rsqrt(jnp.mean(qa * qa, axis=-1, keepdims=True) + EPS) * qag_ref[...]
    qq = _dot(qa.astype(BF16), wq_ref[...])
    kva = kva_ref[0]
    kva = kva * lax.rsqrt(jnp.mean(kva * kva, axis=-1, keepdims=True) + EPS) * kvag_ref[...]
    kv = _dot(kva.astype(BF16), wkv_ref[...])
    sm = sm_ref[0]
    lane = lax.broadcasted_iota(jnp.int32, (n, LANES), 1)
    is_kpe = (lane >= L_KPE) & (lane < L_KPE + dr)
    kpe_ss = jnp.sum(jnp.where(is_kpe, sm * sm, 0.0), axis=-1, keepdims=True)
    kpe4 = _dot(sm, exp_ref[...], HI)
    qr = qq[:, H * dn:]
    grp = lax.broadcasted_iota(jnp.int32, (n, H * dr), 1) // dr
    q_rs, k_rs = [], []
    for h in range(H):
        qn = qq[:, h * dn:(h + 1) * dn]
        ssr = jnp.sum(jnp.where(grp == h, qr * qr, 0.0), axis=-1, keepdims=True)
        q_rs.append(lax.rsqrt((jnp.sum(qn * qn, axis=-1, keepdims=True) + ssr) / MLA_DQK + EPS))
        kn = kv[:, h * dn:(h + 1) * dn]
        k_rs.append(lax.rsqrt((jnp.sum(kn * kn, axis=-1, keepdims=True) + kpe_ss) / MLA_DQK + EPS))

    def per_group(vals):
        out = vals[H - 1]
        for h in range(H - 2, -1, -1):
            out = jnp.where(grp == h, vals[h], out)
        return out

    cos, sin, rot = cos_ref[...], sin_ref[...], rot_ref[...]
    tq = qr * per_group(q_rs) * qgr_ref[...]
    tq = tq * cos + _dot(tq, rot, HI) * sin
    tk = kpe4 * per_group(k_rs) * kgr_ref[...]
    tk = tk * cos + _dot(tk, rot, HI) * sin
    scale = MLA_DQK ** -0.5 * LOG2E
    half_id = lane // dr
    ones = _ones_lane0(n)
    q_parts, k_parts, v_parts = [], [], []
    for h in range(H):
        blk = slice((h // 2) * LANES, (h // 2 + 1) * LANES)
        q_parts.append(qq[:, h * dn:(h + 1) * dn] * q_rs[h] * (qgn_ref[...] * scale))
        q_parts.append(jnp.where(half_id == h % 2, tq[:, blk] * scale, 0.0))
        k_parts.append(kv[:, h * dn:(h + 1) * dn] * k_rs[h] * kgn_ref[...])
        k_parts.append(tk[:, blk])
        v_parts.append(kv[:, H * dn + h * MLA_V:H * dn + (h + 1) * MLA_V])
        v_parts.append(ones)
    q_ref[0] = jnp.concatenate(q_parts, axis=1).astype(q_ref.dtype)
    k_ref[0] = jnp.concatenate(k_parts, axis=1).astype(k_ref.dtype)
    v_ref[0] = jnp.concatenate(v_parts, axis=1).astype(v_ref.dtype)


def _rope_consts(T):
    H, dr = MLA_HEADS, MLA_ROPE
    inv = 1.0 / (ROPE_BASE ** (jnp.arange(0, dr, 2, dtype=F32) / dr))
    ang = jnp.arange(T, dtype=F32)[:, None] * inv[None, :]
    ang = jnp.concatenate([ang, ang], axis=-1)
    cos4 = jnp.tile(jnp.cos(ang), (1, H))
    sin4 = jnp.tile(jnp.sin(ang), (1, H))
    rot = np.zeros((H * dr, H * dr), np.float32)
    for h in range(H):
        for c in range(dr // 2):
            rot[h * dr + c + dr // 2, h * dr + c] = -1.0
            rot[h * dr + c, h * dr + c + dr // 2] = 1.0
    expand = np.zeros((LANES, H * dr), np.float32)
    for h in range(H):
        for c in range(dr):
            expand[L_KPE + c, h * dr + c] = 1.0
    return cos4, sin4, jnp.asarray(rot), jnp.asarray(expand)


def _mla_prep(proj, rope, qa_g, wq_b, kva_g, wkv_b, qn_g, kn_g):
    B, T, _ = proj.shape
    n, H, dn, dr, dv = ROW_TILE, MLA_HEADS, MLA_NOPE, MLA_ROPE, MLA_V
    cos4, sin4, rot, expand = rope
    wq = wq_b.reshape(MLA_Q_RANK, H, MLA_DQK)
    wq = jnp.concatenate([wq[:, :, :dn].reshape(MLA_Q_RANK, H * dn),
                          wq[:, :, dn:].reshape(MLA_Q_RANK, H * dr)], axis=1).astype(BF16)
    wkv = wkv_b.reshape(MLA_KV_RANK, H, dn + dv)
    wkv = jnp.concatenate([wkv[:, :, :dn].reshape(MLA_KV_RANK, H * dn),
                           wkv[:, :, dn:].reshape(MLA_KV_RANK, H * dv)], axis=1).astype(BF16)
    qg, kg = qn_g.astype(F32), kn_g.astype(F32)
    const = lambda shape: pl.BlockSpec(shape, lambda b, t: (0,) * len(shape))
    QW = H * 2 * LANES
    return pl.pallas_call(
        _mla_prep_kernel,
        out_shape=(jax.ShapeDtypeStruct((B, T, QW), BF16), jax.ShapeDtypeStruct((B, T, QW), BF16),
                   jax.ShapeDtypeStruct((B, T, QW), BF16)),
        grid=(B, T // n),
        in_specs=[pl.BlockSpec((1, n, MLA_Q_RANK), lambda b, t: (b, t, COL_MLA_QA // MLA_Q_RANK)),
                  pl.BlockSpec((1, n, MLA_KV_RANK), lambda b, t: (b, t, COL_MLA_KVA // MLA_KV_RANK)),
                  pl.BlockSpec((1, n, LANES), lambda b, t: (b, t, COL_SMALL // LANES)),
                  pl.BlockSpec((n, H * dr), lambda b, t: (t, 0)),
                  pl.BlockSpec((n, H * dr), lambda b, t: (t, 0)),
                  const((1, MLA_Q_RANK)), const(wq.shape), const((1, MLA_KV_RANK)), const(wkv.shape),
                  const((1, dn)), const((1, H * dr)), const((1, dn)), const((1, H * dr)),
                  const(rot.shape), const(expand.shape)],
        out_specs=(pl.BlockSpec((1, n, QW), lambda b, t: (b, t, 0)),
                   pl.BlockSpec((1, n, QW), lambda b, t: (b, t, 0)),
                   pl.BlockSpec((1, n, QW), lambda b, t: (b, t, 0))),
        compiler_params=_cparams(("parallel", "parallel")),
        name="mla_prep",
    )(proj, proj, proj, cos4, sin4,
      qa_g.reshape(1, -1).astype(F32), wq, kva_g.reshape(1, -1).astype(F32), wkv,
      qg[:dn].reshape(1, dn), jnp.tile(qg[dn:], H).reshape(1, H * dr),
      kg[:dn].reshape(1, dn), jnp.tile(kg[dn:], H).reshape(1, H * dr), rot, expand)


def _flash_kernel(q_ref, k_ref, v_ref, o_ref, m_ref, acc_ref, *, tq):
    qi = pl.program_id(2)
    nh = m_ref.shape[0]
    dqk, dv2 = q_ref.shape[-1] // nh, v_ref.shape[-1] // nh
    dv = dv2 // 2
    rb = ROW_TILE
    m_ref[...] = jnp.full(m_ref.shape, NEG, F32)
    acc_ref[...] = jnp.zeros(acc_ref.shape, F32)

    def step(j, masked):
        start = pl.multiple_of(j * tq, tq)
        chains = [(h, r) for h in range(nh) for r in range(0, tq, rb)]
        logits, probs = {}, {}

        def qk(c):
            h, r = chains[c]
            logits[c] = _dot_nt(q_ref[0, r:r + rb, h * dqk:(h + 1) * dqk],
                                k_ref[0, pl.ds(start, tq), h * dqk:(h + 1) * dqk])

        def softmax(c):
            h, r = chains[c]
            s = logits.pop(c)
            if masked:
                row = lax.broadcasted_iota(jnp.int32, s.shape, 0) + r
                col = lax.broadcasted_iota(jnp.int32, s.shape, 1)
                s = jnp.where(col <= row, s, NEG)
            lane_max = s[:, :LANES]
            for cb in range(1, tq // LANES):
                lane_max = jnp.maximum(lane_max, s[:, cb * LANES:(cb + 1) * LANES])
            m_old = m_ref[h, r:r + rb]
            m_new = jnp.maximum(m_old, jnp.max(lane_max, axis=-1, keepdims=True))
            m_ref[h, r:r + rb] = m_new
            probs[c] = (jnp.exp2(s - m_new).astype(BF16), jnp.exp2(m_old - m_new))

        def pv(c):
            h, r = chains[c]
            p, alpha = probs.pop(c)
            acc_ref[h, r:r + rb] = (alpha * acc_ref[h, r:r + rb]
                                    + _dot(p, v_ref[0, pl.ds(start, tq), h * dv2:(h + 1) * dv2]))

        for c in range(len(chains)):
            qk(c)
            softmax(c)
            pv(c)

    def body(jj, carry):
        step(2 * jj, False)
        step(2 * jj + 1, False)
        return carry

    lax.fori_loop(0, qi // 2, body, 0)

    @pl.when(qi % 2 == 1)
    def _():
        step(qi - 1, False)

    step(qi, True)
    outs = []
    for h in range(nh):
        acc = acc_ref[h]
        outs.append(acc[:, :dv] / acc[:, dv:dv + 1])
    o_ref[0] = jnp.concatenate(outs, axis=1).astype(o_ref.dtype)


def _flash(q, k, v):
    B, T, QW = q.shape
    H, nh = MLA_HEADS, FLASH_HEADS_PER_STEP
    dqk, dv2 = QW // H, v.shape[-1] // H
    dv = dv2 // 2
    tq = next(t for t in FLASH_TILES if T % t == 0)
    return pl.pallas_call(
        functools.partial(_flash_kernel, tq=tq),
        out_shape=jax.ShapeDtypeStruct((B, T, H * dv), BF16),
        grid=(B, H // nh, T // tq),
        in_specs=[pl.BlockSpec((1, tq, nh * dqk), lambda b, h, i: (b, i, h)),
                  pl.BlockSpec((1, T, nh * dqk), lambda b, h, i: (b, 0, h)),
                  pl.BlockSpec((1, T, nh * dv2), lambda b, h, i: (b, 0, h))],
        out_specs=pl.BlockSpec((1, tq, nh * dv), lambda b, h, i: (b, i, h)),
        scratch_shapes=[pltpu.VMEM((nh, tq, 1), F32), pltpu.VMEM((nh, tq, dv2), F32)],
        compiler_params=_cparams(("parallel", "parallel", "arbitrary")),
        name="flash",
    )(q, k, v)


def _reorder_kernel(w_ref, o_ref):
    w = w_ref[...]
    n = w.shape[0]
    z = lambda width: jnp.zeros((n, width), w.dtype)
    seg = lambda *names: w[:, IN_OFFSET[names[0]]:IN_OFFSET[names[-1]] + IN_WIDTH[names[-1]]]
    small = [seg("gdn_b"), seg("gdn_a"), seg("fox_f"), seg("ssd_dt"), seg("mla_kpe")]
    n_small = sum(s.shape[1] for s in small)
    cols = [seg("gdn_q", "gdn_k", "gdn_v", "gdn_z"),
            seg("ssd_x", "ssd_b", "ssd_c"),
            seg("ssd_z"),
            seg("mla_qa", "mla_kva"),
            *small, z(LANES - n_small), z(COL_FOX_QKV - COL_SMALL - LANES),
            seg("fox_q", "fox_k", "fox_v")]
    o_ref[...] = jnp.concatenate(cols, axis=1).astype(o_ref.dtype)


def _reorder_w_in(w_in, layer):
    depth, D, W = w_in.shape
    rows = ROW_TILE
    return pl.pallas_call(
        _reorder_kernel,
        out_shape=jax.ShapeDtypeStruct((D, PROJ_W), BF16),
        grid=(D // rows,),
        in_specs=[pl.BlockSpec((rows, W), lambda i: (layer * (D // rows) + i, 0))],
        out_specs=pl.BlockSpec((rows, PROJ_W), lambda i: (i, 0)),
        compiler_params=_cparams(("parallel",)),
        name="reorder_w_in",
    )(w_in.reshape(depth * D, W))


def _mixer(h, hn, B, T, rope, w_in_cols, gdn_conv_w, gdn_A_log, gdn_dt_bias, gdn_norm_g,
           mla_qa_g, mla_wq_b, mla_kva_g, mla_wkv_b, mla_qn_g, mla_kn_g,
           fox_qn_g, fox_kn_g, fox_b_f,
           ssd_conv_w, ssd_conv_b, ssd_dt_bias, ssd_A_log, ssd_D, ssd_norm_g,
           w_gate, w_branch, w_o, next_norm_g=None):
    M = B * T
    proj = _matmul(hn, w_in_cols, tn=2048, name="in_proj").reshape(B, T, PROJ_W)
    o_gdn = _gdn(proj, gdn_conv_w, gdn_A_log, gdn_dt_bias, gdn_norm_g)
    mq, mk, mv = _mla_prep(proj, rope, mla_qa_g, mla_wq_b, mla_kva_g, mla_wkv_b, mla_qn_g, mla_kn_g)
    o_mla = _flash(mq, mk, mv)
    o_fox = _flash(*_fox_prep(proj, fox_qn_g, fox_kn_g, fox_b_f))
    o_ssd = _ssd(proj, ssd_conv_w, ssd_conv_b, ssd_dt_bias, ssd_A_log, ssd_D, ssd_norm_g)
    branches = [o.reshape(M, BRANCH_W) for o in (o_gdn, o_mla, o_fox, o_ssd)]
    merged = _merge(hn, branches, w_gate.astype(BF16), w_branch.astype(BF16))
    return _matmul(merged, w_o.astype(BF16), tn=w_o.shape[1], tm=512, residual=h, norm_g=next_norm_g,
                   name="out_proj")


def kernel(x, meta_tokens, mix_norm_g, w_in, gdn_conv_w, gdn_A_log, gdn_dt_bias, gdn_norm_g, mla_qa_g, mla_wq_b, mla_kva_g, mla_wkv_b, mla_qn_g, mla_kn_g, fox_qn_g, fox_kn_g, fox_b_f, ssd_conv_w, ssd_conv_b, ssd_dt_bias, ssd_A_log, ssd_D, ssd_norm_g, w_gate, w_branch, w_o, ffn_norm_g, dense_w_gate, dense_w_up, dense_w_down, router_w, moe_w_gate, moe_w_up, moe_w_down):
    B, S, D = x.shape
    L = N_META + S
    T = -(-L // ROW_TILE) * ROW_TILE
    assert (B * T) % MM_TM == 0
    depth = w_in.shape[0]
    rope = _rope_consts(T)
    h, hn = _frame_and_norm(x, meta_tokens.astype(x.dtype), mix_norm_g[0], T)
    for layer in range(depth):
        dense = layer % 2 == 0
        next_mix_g = mix_norm_g[layer + 1] if layer + 1 < depth else None
        mixed = _mixer(h, hn, B, T, rope, _reorder_w_in(w_in, layer),
                       gdn_conv_w[layer], gdn_A_log[layer], gdn_dt_bias[layer], gdn_norm_g[layer],
                       mla_qa_g[layer], mla_wq_b[layer], mla_kva_g[layer], mla_wkv_b[layer],
                       mla_qn_g[layer], mla_kn_g[layer],
                       fox_qn_g[layer], fox_kn_g[layer], fox_b_f[layer],
                       ssd_conv_w[layer], ssd_conv_b[layer], ssd_dt_bias[layer], ssd_A_log[layer],
                       ssd_D[layer], ssd_norm_g[layer],
                       w_gate[layer], w_branch[layer], w_o[layer],
                       next_norm_g=ffn_norm_g[layer] if dense else None)
        i = layer // 2
        if dense:
            h, hn = mixed
            act = _swiglu_up(hn, dense_w_gate[i].astype(BF16), dense_w_up[i].astype(BF16), tn=512)
            out = _matmul(act, dense_w_down[i].astype(BF16), tn=D, tm=FFN_DOWN_TM, residual=h,
                          norm_g=next_mix_g, w_buffers=1, name="ffn_down")
            h, hn = out if next_mix_g is not None else (out, None)
        else:
            h = mixed
            hn, rec, cnt_before = _rmsnorm_router(h, ffn_norm_g[layer], router_w[i], B, T)
            h = _moe(h, hn, rec, cnt_before, moe_w_gate[i].astype(BF16), moe_w_up[i].astype(BF16),
                     moe_w_down[i].astype(BF16), B, T)
            hn = _rmsnorm(h, next_mix_g) if next_mix_g is not None else None
    return h.reshape(B, T, D)[:, N_META:L].astype(x.dtype)
```
